```python
import math
import jax, jax.numpy as jnp
from jax import lax
import numpy as np

D_MODEL = 1024
BATCH = 8
SEQ = 4096
DEPTH = 2

DN_HEADS = 8
DN_HEAD_DIM = 128
DN_WIDTH = DN_HEADS * DN_HEAD_DIM
DN_CONV = 4
DN_CHUNK = 64
DA_HEADS = 12
DA_HEAD_DIM = 64
DA_WIDTH = DA_HEADS * DA_HEAD_DIM
DA_PATTERNS = ((128, 1), (512, 4), (2048, 16))
DA_BLOCK = 128
ALIBI_MAX_EXP = 8.0
D_FF = 2816
MACARON_WEIGHT = 0.5
NORM_EPS = 1e-6
N_ADA = 9
IN_SPLITS = (3 * DN_WIDTH, DN_WIDTH, DN_HEADS, DN_HEADS,
             DA_WIDTH, DA_WIDTH, DA_WIDTH, D_MODEL, D_MODEL)
IN_COLS = 3 * DN_WIDTH + DN_WIDTH + 2 * DN_HEADS + 3 * DA_WIDTH + 2 * D_MODEL

kernel_name = "hybrid_deltanet_dilated_attn_macaron_adaln"


def rmsnorm(x, g):
    xf = x.astype(jnp.float32)
    y = xf * lax.rsqrt(jnp.mean(xf * xf, axis=-1, keepdims=True) + NORM_EPS)
    return (y * g.astype(jnp.float32)).astype(x.dtype)


def l2norm(x):
    xf = x.astype(jnp.float32)
    return xf * lax.rsqrt(jnp.sum(xf * xf, axis=-1, keepdims=True) + NORM_EPS)


def modulate(x, shift, scale):
    return x * (1.0 + scale[:, None, :]) + shift[:, None, :]


def swiglu(x, w_gate, w_up, w_down):
    return (jax.nn.silu(x @ w_gate) * (x @ w_up)) @ w_down


def causal_depthwise_conv(x, w):
    K = w.shape[0]
    S = x.shape[1]
    xp = jnp.pad(x, ((0, 0), (K - 1, 0), (0, 0)))
    y = xp[:, 0:S] * w[0]
    for j in range(1, K):
        y = y + xp[:, j:j + S] * w[j]
    return y


def gated_delta_rule(q, k, v, g, beta):
    B, S, H, dk = q.shape
    dv = v.shape[-1]
    C = DN_CHUNK
    N = S // C
    f32 = jnp.float32

    def chunks(t):
        t = t.astype(f32).reshape((B, N, C, H) + t.shape[3:])
        return t.transpose((1, 0, 3, 2) + tuple(range(4, t.ndim)))

    qc, kc, vc = chunks(q), chunks(k), chunks(v)
    gc = jnp.cumsum(chunks(g), axis=-1)
    bc = chunks(beta)
    kb = kc * bc[..., None]
    vb = vc * bc[..., None]
    incl = jnp.tril(jnp.ones((C, C), dtype=bool))
    strict = jnp.tril(jnp.ones((C, C), dtype=bool), -1)
    decay = jnp.exp(jnp.where(incl, gc[..., :, None] - gc[..., None, :], -jnp.inf))
    m = jnp.where(strict, jnp.einsum('nbhid,nbhjd->nbhij', kb, kc) * decay, 0.0)
    a = m + jnp.eye(C, dtype=f32)
    u_c = lax.linalg.triangular_solve(a, vb, left_side=True, lower=True, unit_diagonal=True)
    w_c = lax.linalg.triangular_solve(a, kb * jnp.exp(gc)[..., None], left_side=True,
                                      lower=True, unit_diagonal=True)
    qk = jnp.einsum('nbhid,nbhjd->nbhij', qc, kc) * decay

    def step(state, xs):
        q_i, k_i, u_i, w_i, g_i, qk_i = xs
        v_new = u_i - jnp.einsum('bhcd,bhde->bhce', w_i, state)
        o_i = (jnp.einsum('bhcd,bhde->bhce', q_i * jnp.exp(g_i)[..., None], state)
               + jnp.einsum('bhij,bhje->bhie', qk_i, v_new))
        g_last = g_i[..., -1]
        state = (state * jnp.exp(g_last)[..., None, None]
                 + jnp.einsum('bhcd,bhce->bhde',
                              k_i * jnp.exp(g_last[..., None] - g_i)[..., None], v_new))
        return state, o_i

    state0 = jnp.zeros((B, H, dk, dv), f32)
    _, o = lax.scan(step, state0, (qc, kc, u_c, w_c, gc, qk))
    return o.transpose(1, 0, 3, 2, 4).reshape(B, S, H, dv)


def dilated_window_branch(q, k, v, slopes, window, dilation):
    B, S, H, dh = q.shape
    r = dilation
    n = S // r
    span = window // r
    nb = -(-n // DA_BLOCK)
    n_pad = nb * DA_BLOCK
    z = B * r

    def to_sub(t):
        t = t.reshape(B, n, r, H, dh).transpose(0, 2, 1, 3, 4).reshape(z, n, H, dh)
        return jnp.pad(t, ((0, 0), (0, n_pad - n), (0, 0), (0, 0)))

    def band(t):
        tp = jnp.pad(t, ((0, 0), (DA_BLOCK, 0), (0, 0), (0, 0)))
        prev = tp[:, :n_pad].reshape(z, nb, DA_BLOCK, H, dh)
        cur = t.reshape(z, nb, DA_BLOCK, H, dh)
        return jnp.concatenate([prev, cur], axis=2)

    qs, ks, vs = to_sub(q), to_sub(k), to_sub(v)
    qb = qs.reshape(z, nb, DA_BLOCK, H, dh)
    kb, vb = band(ks), band(vs)
    s = jnp.einsum('znqhd,znkhd->znhqk', qb, kb).astype(jnp.float32) * (dh ** -0.5)
    qi = jnp.arange(DA_BLOCK)[:, None]
    ki = jnp.arange(2 * DA_BLOCK)[None, :]
    dist = qi + DA_BLOCK - ki
    key_pos = jnp.arange(nb)[:, None, None] * DA_BLOCK + ki[None] - DA_BLOCK
    valid = (dist[None] >= 0) & (dist[None] <= span) & (key_pos >= 0)
    bias = -slopes[:, None, None] * (dist * r).astype(jnp.float32)[None]
    s = jnp.where(valid[None, :, None], s + bias[None, None], -jnp.inf)
    mx = jnp.max(s, axis=-1, keepdims=True)
    p = jnp.exp(s - mx)
    l = jnp.sum(p, axis=-1, keepdims=True)
    o = jnp.einsum('znhqk,znkhd->znqhd', (p / l).astype(v.dtype), vb).astype(jnp.float32)
    lse = (mx + jnp.log(l))[..., 0]
    o = o.reshape(z, n_pad, H, dh)[:, :n].reshape(B, r, n, H, dh)
    o = o.transpose(0, 2, 1, 3, 4).reshape(B, S, H, dh)
    lse = lse.transpose(0, 1, 3, 2).reshape(z, n_pad, H)[:, :n].reshape(B, r, n, H)
    lse = lse.transpose(0, 2, 1, 3).reshape(B, S, H)
    return o, lse


def hybrid_mixer(u, w_in, conv_w, a_log, dt_bias, dn_norm, w_a, w_b, w_o):
    B, S, _ = u.shape
    proj = u @ w_in
    idx = np.cumsum(IN_SPLITS)[:-1].tolist()
    dn_qkv, dn_z, dn_b, dn_a, da_q, da_k, da_v, gate_a, gate_b = jnp.split(proj, idx, axis=-1)

    qkv = jax.nn.silu(causal_depthwise_conv(dn_qkv, conv_w))
    q, k, v = jnp.split(qkv, 3, axis=-1)
    q = l2norm(q.reshape(B, S, DN_HEADS, DN_HEAD_DIM)) * (DN_HEAD_DIM ** -0.5)
    k = l2norm(k.reshape(B, S, DN_HEADS, DN_HEAD_DIM))
    v = v.reshape(B, S, DN_HEADS, DN_HEAD_DIM)
    beta = jax.nn.sigmoid(dn_b.astype(jnp.float32))
    g = -jnp.exp(a_log.astype(jnp.float32)) * jax.nn.softplus(
        dn_a.astype(jnp.float32) + dt_bias.astype(jnp.float32))
    o_a = gated_delta_rule(q, k, v, g, beta).astype(u.dtype)
    o_a = rmsnorm(o_a, dn_norm) * jax.nn.silu(dn_z.reshape(B, S, DN_HEADS, DN_HEAD_DIM))
    y_a = o_a.reshape(B, S, DN_WIDTH) @ w_a

    qd = da_q.reshape(B, S, DA_HEADS, DA_HEAD_DIM)
    kd = da_k.reshape(B, S, DA_HEADS, DA_HEAD_DIM)
    vd = da_v.reshape(B, S, DA_HEADS, DA_HEAD_DIM)
    slopes = 2.0 ** (-ALIBI_MAX_EXP * jnp.arange(1, DA_HEADS + 1, dtype=jnp.float32) / DA_HEADS)
    outs, lses = [], []
    for window, dilation in DA_PATTERNS:
        o_p, lse_p = dilated_window_branch(qd, kd, vd, slopes, window, dilation)
        outs.append(o_p)
        lses.append(lse_p)
    wts = jax.nn.softmax(jnp.stack(lses, axis=0), axis=0)
    o_b = jnp.sum(wts[..., None] * jnp.stack(outs, axis=0), axis=0).astype(u.dtype)
    y_b = o_b.reshape(B, S, DA_WIDTH) @ w_b

    merged = jax.nn.sigmoid(gate_a) * y_a + jax.nn.sigmoid(gate_b) * y_b
    return merged @ w_o


def _fwd_setup_inputs(seed: int = 0) -> dict:
    key = jax.random.key(seed)
    ks = jax.random.split(key, 24)
    f32 = jnp.float32
    D = D_MODEL

    def nrm(k, shape, scale):
        return jax.random.normal(k, shape, f32) * scale

    x = nrm(ks[0], (BATCH, SEQ, D), 1.0)
    c = nrm(ks[1], (BATCH, D), 1.0)
    ada_w = nrm(ks[2], (DEPTH, D, N_ADA * D), 0.5 * D ** -0.5)
    ada_b = nrm(ks[3], (DEPTH, N_ADA * D), 0.02)
    ln_ffn1 = 1.0 + nrm(ks[4], (DEPTH, D), 0.02)
    ln_mix = 1.0 + nrm(ks[5], (DEPTH, D), 0.02)
    ln_ffn2 = 1.0 + nrm(ks[6], (DEPTH, D), 0.02)
    ffn1_wg = nrm(ks[7], (DEPTH, D, D_FF), D ** -0.5)
    ffn1_wu = nrm(ks[8], (DEPTH, D, D_FF), D ** -0.5)
    ffn1_wd = nrm(ks[9], (DEPTH, D_FF, D), D_FF ** -0.5)
    w_in = nrm(ks[10], (DEPTH, D, IN_COLS), D ** -0.5)
    conv_w = nrm(ks[11], (DEPTH, DN_CONV, 3 * DN_WIDTH), DN_CONV ** -0.5)
    a_log = jnp.log(jax.random.uniform(ks[12], (DEPTH, DN_HEADS), f32, 1.0, 16.0))
    dt = jnp.exp(jax.random.uniform(ks[13], (DEPTH, DN_HEADS), f32,
                                    math.log(1e-3), math.log(1e-1)))
    dt_bias = dt + jnp.log(-jnp.expm1(-dt))
    dn_norm = 1.0 + nrm(ks[14], (DEPTH, DN_HEAD_DIM), 0.02)
    w_a = nrm(ks[15], (DEPTH, DN_WIDTH, D), DN_WIDTH ** -0.5)
    w_b = nrm(ks[16], (DEPTH, DA_WIDTH, D), DA_WIDTH ** -0.5)
    w_o = nrm(ks[17], (DEPTH, D, D), D ** -0.5)
    ffn2_wg = nrm(ks[18], (DEPTH, D, D_FF), D ** -0.5)
    ffn2_wu = nrm(ks[19], (DEPTH, D, D_FF), D ** -0.5)
    ffn2_wd = nrm(ks[20], (DEPTH, D_FF, D), D_FF ** -0.5)
    final_norm = 1.0 + nrm(ks[21], (D,), 0.02)
    return {"x": x, "c": c, "ada_w": ada_w, "ada_b": ada_b,
            "ln_ffn1": ln_ffn1, "ln_mix": ln_mix, "ln_ffn2": ln_ffn2,
            "ffn1_wg": ffn1_wg, "ffn1_wu": ffn1_wu, "ffn1_wd": ffn1_wd,
            "w_in": w_in, "conv_w": conv_w, "a_log": a_log, "dt_bias": dt_bias,
            "dn_norm": dn_norm, "w_a": w_a, "w_b": w_b, "w_o": w_o,
            "ffn2_wg": ffn2_wg, "ffn2_wu": ffn2_wu, "ffn2_wd": ffn2_wd,
            "final_norm": final_norm}


def _fwd_reference(x, c, ada_w, ada_b, ln_ffn1, ln_mix, ln_ffn2, ffn1_wg, ffn1_wu, ffn1_wd,
              w_in, conv_w, a_log, dt_bias, dn_norm, w_a, w_b, w_o,
              ffn2_wg, ffn2_wu, ffn2_wd, final_norm):
    h = x
    c_act = jax.nn.silu(c)
    for l in range(DEPTH):
        mod = c_act @ ada_w[l] + ada_b[l]
        (sh1, sc1, gt1, sh2, sc2, gt2, sh3, sc3, gt3) = jnp.split(mod, N_ADA, axis=-1)
        f = swiglu(modulate(rmsnorm(h, ln_ffn1[l]), sh1, sc1), ffn1_wg[l], ffn1_wu[l], ffn1_wd[l])
        h = h + MACARON_WEIGHT * gt1[:, None, :] * f
        u = modulate(rmsnorm(h, ln_mix[l]), sh2, sc2)
        m = hybrid_mixer(u, w_in[l], conv_w[l], a_log[l], dt_bias[l], dn_norm[l],
                         w_a[l], w_b[l], w_o[l])
        h = h + gt2[:, None, :] * m
        f = swiglu(modulate(rmsnorm(h, ln_ffn2[l]), sh3, sc3), ffn2_wg[l], ffn2_wu[l], ffn2_wd[l])
        h = h + MACARON_WEIGHT * gt3[:, None, :] * f
    return rmsnorm(h, final_norm)


import jax as _jax
import jax.numpy as _jnp

TWIN_FORMAT = 'train_step'
FWD_PARAMS = ['x', 'c', 'ada_w', 'ada_b', 'ln_ffn1', 'ln_mix', 'ln_ffn2', 'ffn1_wg', 'ffn1_wu', 'ffn1_wd', 'w_in', 'conv_w', 'a_log', 'dt_bias', 'dn_norm', 'w_a', 'w_b', 'w_o', 'ffn2_wg', 'ffn2_wu', 'ffn2_wd', 'final_norm']
TWIN_WEIGHTS = ['ada_w', 'ada_b', 'ln_ffn1', 'ln_mix', 'ln_ffn2', 'ffn1_wg', 'ffn1_wu', 'ffn1_wd', 'w_in', 'conv_w', 'a_log', 'dt_bias', 'dn_norm', 'w_a', 'w_b', 'w_o', 'ffn2_wg', 'ffn2_wu', 'ffn2_wd', 'final_norm']
TWIN_DIFF_INPUT = 'x'
TWIN_INPUTS = ['x', 'c', 'ada_w', 'ada_b', 'ln_ffn1', 'ln_mix', 'ln_ffn2', 'ffn1_wg', 'ffn1_wu', 'ffn1_wd', 'w_in', 'conv_w', 'a_log', 'dt_bias', 'dn_norm', 'w_a', 'w_b', 'w_o', 'ffn2_wg', 'ffn2_wu', 'ffn2_wd', 'final_norm', 'loss_target', 'm_ada_w', 'm_ada_b', 'm_ln_ffn1', 'm_ln_mix', 'm_ln_ffn2', 'm_ffn1_wg', 'm_ffn1_wu', 'm_ffn1_wd', 'm_w_in', 'm_conv_w', 'm_a_log', 'm_dt_bias', 'm_dn_norm', 'm_w_a', 'm_w_b', 'm_w_o', 'm_ffn2_wg', 'm_ffn2_wu', 'm_ffn2_wd', 'm_final_norm', 'v_ada_w', 'v_ada_b', 'v_ln_ffn1', 'v_ln_mix', 'v_ln_ffn2', 'v_ffn1_wg', 'v_ffn1_wu', 'v_ffn1_wd', 'v_w_in', 'v_conv_w', 'v_a_log', 'v_dt_bias', 'v_dn_norm', 'v_w_a', 'v_w_b', 'v_w_o', 'v_ffn2_wg', 'v_ffn2_wu', 'v_ffn2_wd', 'v_final_norm']
TWIN_OUTPUTS = ['loss', 'grad_x', 'grad_ada_w', 'grad_ada_b', 'grad_ln_ffn1', 'grad_ln_mix', 'grad_ln_ffn2', 'grad_ffn1_wg', 'grad_ffn1_wu', 'grad_ffn1_wd', 'grad_w_in', 'grad_conv_w', 'grad_a_log', 'grad_dt_bias', 'grad_dn_norm', 'grad_w_a', 'grad_w_b', 'grad_w_o', 'grad_ffn2_wg', 'grad_ffn2_wu', 'grad_ffn2_wd', 'grad_final_norm', 'delta_ada_w', 'delta_ada_b', 'delta_ln_ffn1', 'delta_ln_mix', 'delta_ln_ffn2', 'delta_ffn1_wg', 'delta_ffn1_wu', 'delta_ffn1_wd', 'delta_w_in', 'delta_conv_w', 'delta_a_log', 'delta_dt_bias', 'delta_dn_norm', 'delta_w_a', 'delta_w_b', 'delta_w_o', 'delta_ffn2_wg', 'delta_ffn2_wu', 'delta_ffn2_wd', 'delta_final_norm', 'new_m_ada_w', 'new_m_ada_b', 'new_m_ln_ffn1', 'new_m_ln_mix', 'new_m_ln_ffn2', 'new_m_ffn1_wg', 'new_m_ffn1_wu', 'new_m_ffn1_wd', 'new_m_w_in', 'new_m_conv_w', 'new_m_a_log', 'new_m_dt_bias', 'new_m_dn_norm', 'new_m_w_a', 'new_m_w_b', 'new_m_w_o', 'new_m_ffn2_wg', 'new_m_ffn2_wu', 'new_m_ffn2_wd', 'new_m_final_norm', 'new_v_ada_w', 'new_v_ada_b', 'new_v_ln_ffn1', 'new_v_ln_mix', 'new_v_ln_ffn2', 'new_v_ffn1_wg', 'new_v_ffn1_wu', 'new_v_ffn1_wd', 'new_v_w_in', 'new_v_conv_w', 'new_v_a_log', 'new_v_dt_bias', 'new_v_dn_norm', 'new_v_w_a', 'new_v_w_b', 'new_v_w_o', 'new_v_ffn2_wg', 'new_v_ffn2_wu', 'new_v_ffn2_wd', 'new_v_final_norm']
TWIN_LEAF_KINDS = {'loss': 'loss', 'grad_x': 'grad_x', 'grad_ada_w': 'grad_w', 'grad_ada_b': 'grad_w', 'grad_ln_ffn1': 'grad_w', 'grad_ln_mix': 'grad_w', 'grad_ln_ffn2': 'grad_w', 'grad_ffn1_wg': 'grad_w', 'grad_ffn1_wu': 'grad_w', 'grad_ffn1_wd': 'grad_w', 'grad_w_in': 'grad_w', 'grad_conv_w': 'grad_w', 'grad_a_log': 'grad_w', 'grad_dt_bias': 'grad_w', 'grad_dn_norm': 'grad_w', 'grad_w_a': 'grad_w', 'grad_w_b': 'grad_w', 'grad_w_o': 'grad_w', 'grad_ffn2_wg': 'grad_w', 'grad_ffn2_wu': 'grad_w', 'grad_ffn2_wd': 'grad_w', 'grad_final_norm': 'grad_w', 'delta_ada_w': 'delta_w', 'delta_ada_b': 'delta_w', 'delta_ln_ffn1': 'delta_w', 'delta_ln_mix': 'delta_w', 'delta_ln_ffn2': 'delta_w', 'delta_ffn1_wg': 'delta_w', 'delta_ffn1_wu': 'delta_w', 'delta_ffn1_wd': 'delta_w', 'delta_w_in': 'delta_w', 'delta_conv_w': 'delta_w', 'delta_a_log': 'delta_w', 'delta_dt_bias': 'delta_w', 'delta_dn_norm': 'delta_w', 'delta_w_a': 'delta_w', 'delta_w_b': 'delta_w', 'delta_w_o': 'delta_w', 'delta_ffn2_wg': 'delta_w', 'delta_ffn2_wu': 'delta_w', 'delta_ffn2_wd': 'delta_w', 'delta_final_norm': 'delta_w', 'new_m_ada_w': 'new_m', 'new_m_ada_b': 'new_m', 'new_m_ln_ffn1': 'new_m', 'new_m_ln_mix': 'new_m', 'new_m_ln_ffn2': 'new_m', 'new_m_ffn1_wg': 'new_m', 'new_m_ffn1_wu': 'new_m', 'new_m_ffn1_wd': 'new_m', 'new_m_w_in': 'new_m', 'new_m_conv_w': 'new_m', 'new_m_a_log': 'new_m', 'new_m_dt_bias': 'new_m', 'new_m_dn_norm': 'new_m', 'new_m_w_a': 'new_m', 'new_m_w_b': 'new_m', 'new_m_w_o': 'new_m', 'new_m_ffn2_wg': 'new_m', 'new_m_ffn2_wu': 'new_m', 'new_m_ffn2_wd': 'new_m', 'new_m_final_norm': 'new_m', 'new_v_ada_w': 'new_v', 'new_v_ada_b': 'new_v', 'new_v_ln_ffn1': 'new_v', 'new_v_ln_mix': 'new_v', 'new_v_ln_ffn2': 'new_v', 'new_v_ffn1_wg': 'new_v', 'new_v_ffn1_wu': 'new_v', 'new_v_ffn1_wd': 'new_v', 'new_v_w_in': 'new_v', 'new_v_conv_w': 'new_v', 'new_v_a_log': 'new_v', 'new_v_dt_bias': 'new_v', 'new_v_dn_norm': 'new_v', 'new_v_w_a': 'new_v', 'new_v_w_b': 'new_v', 'new_v_w_o': 'new_v', 'new_v_ffn2_wg': 'new_v', 'new_v_ffn2_wu': 'new_v', 'new_v_ffn2_wd': 'new_v', 'new_v_final_norm': 'new_v'}


def _forward(args):
    return _fwd_reference(*[args[k] for k in FWD_PARAMS])


def _output_shape():
    out = _jax.eval_shape(lambda: _forward(_fwd_setup_inputs(0)))
    return out.shape, out.dtype

N_MICROBATCH = 1
ADAM_LR = 0.001
ADAM_B1 = 0.9
ADAM_B2 = 0.999
ADAM_EPS = 1e-08
ADAM_WD = 0.01
ADAM_STEP = 10
PER_EXAMPLE_BATCH_AXIS = {'x': 0, 'c': 0, 'loss_target': 0}
SHARED_INPUTS = []
_WEIGHT_DTYPES = {'ada_w': _jnp.float32, 'ada_b': _jnp.float32, 'ln_ffn1': _jnp.float32, 'ln_mix': _jnp.float32, 'ln_ffn2': _jnp.float32, 'ffn1_wg': _jnp.float32, 'ffn1_wu': _jnp.float32, 'ffn1_wd': _jnp.float32, 'w_in': _jnp.float32, 'conv_w': _jnp.float32, 'a_log': _jnp.float32, 'dt_bias': _jnp.float32, 'dn_norm': _jnp.float32, 'w_a': _jnp.float32, 'w_b': _jnp.float32, 'w_o': _jnp.float32, 'ffn2_wg': _jnp.float32, 'ffn2_wu': _jnp.float32, 'ffn2_wd': _jnp.float32, 'final_norm': _jnp.float32}
MOMENT_SCALE = {'ada_w': 2.989504e-02, 'ada_b': 4.879101e-02, 'ln_ffn1': 2.765389e-02, 'ln_mix': 3.753031e-02, 'ln_ffn2': 2.720454e-02, 'ffn1_wg': 1.241446e-02, 'ffn1_wu': 1.203650e-02, 'ffn1_wd': 1.989783e-02, 'w_in': 1.364238e-02, 'conv_w': 1.460069e-02, 'a_log': 2.009078e-01, 'dt_bias': 1.840814e-01, 'dn_norm': 5.319262e-02, 'w_a': 1.901775e-02, 'w_b': 1.591119e-02, 'w_o': 2.468532e-02, 'ffn2_wg': 1.214272e-02, 'ffn2_wu': 1.173901e-02, 'ffn2_wd': 1.946038e-02, 'final_norm': 3.200042e+01}


def _to_microbatches(a, axis):
    t = _jnp.moveaxis(a, axis, 0)
    t = t.reshape((N_MICROBATCH, t.shape[0] // N_MICROBATCH) + t.shape[1:])
    return _jnp.moveaxis(t, 1, axis + 1)


def setup_inputs(seed: int = 0) -> dict:
    inp = _fwd_setup_inputs(seed)
    key = _jax.random.fold_in(_jax.random.key(seed), 7919)
    shape, _ = _output_shape()
    out = dict(inp)
    out["loss_target"] = _jax.random.normal(_jax.random.fold_in(key, 0), shape, _jnp.float32)
    for i, name in enumerate(TWIN_WEIGHTS):
        w = inp[name].astype(_jnp.float32)
        if MOMENT_SCALE is None:
            s = _jnp.sqrt(_jnp.mean(_jnp.square(w)) + 1e-30)
        else:
            s = MOMENT_SCALE[name]
        km, kv = _jax.random.split(_jax.random.fold_in(key, i + 1))
        out[name] = w
        out["m_" + name] = s * _jax.random.normal(km, w.shape, _jnp.float32)
        out["v_" + name] = (s * s) * _jax.random.uniform(kv, w.shape, _jnp.float32, 0.5, 1.5)
    if N_MICROBATCH > 1:
        for name, axis in PER_EXAMPLE_BATCH_AXIS.items():
            out[name] = _to_microbatches(out[name], axis)
    return {'x': out['x'], 'c': out['c'], 'ada_w': out['ada_w'], 'ada_b': out['ada_b'], 'ln_ffn1': out['ln_ffn1'], 'ln_mix': out['ln_mix'], 'ln_ffn2': out['ln_ffn2'], 'ffn1_wg': out['ffn1_wg'], 'ffn1_wu': out['ffn1_wu'], 'ffn1_wd': out['ffn1_wd'], 'w_in': out['w_in'], 'conv_w': out['conv_w'], 'a_log': out['a_log'], 'dt_bias': out['dt_bias'], 'dn_norm': out['dn_norm'], 'w_a': out['w_a'], 'w_b': out['w_b'], 'w_o': out['w_o'], 'ffn2_wg': out['ffn2_wg'], 'ffn2_wu': out['ffn2_wu'], 'ffn2_wd': out['ffn2_wd'], 'final_norm': out['final_norm'], 'loss_target': out['loss_target'], 'm_ada_w': out['m_ada_w'], 'm_ada_b': out['m_ada_b'], 'm_ln_ffn1': out['m_ln_ffn1'], 'm_ln_mix': out['m_ln_mix'], 'm_ln_ffn2': out['m_ln_ffn2'], 'm_ffn1_wg': out['m_ffn1_wg'], 'm_ffn1_wu': out['m_ffn1_wu'], 'm_ffn1_wd': out['m_ffn1_wd'], 'm_w_in': out['m_w_in'], 'm_conv_w': out['m_conv_w'], 'm_a_log': out['m_a_log'], 'm_dt_bias': out['m_dt_bias'], 'm_dn_norm': out['m_dn_norm'], 'm_w_a': out['m_w_a'], 'm_w_b': out['m_w_b'], 'm_w_o': out['m_w_o'], 'm_ffn2_wg': out['m_ffn2_wg'], 'm_ffn2_wu': out['m_ffn2_wu'], 'm_ffn2_wd': out['m_ffn2_wd'], 'm_final_norm': out['m_final_norm'], 'v_ada_w': out['v_ada_w'], 'v_ada_b': out['v_ada_b'], 'v_ln_ffn1': out['v_ln_ffn1'], 'v_ln_mix': out['v_ln_mix'], 'v_ln_ffn2': out['v_ln_ffn2'], 'v_ffn1_wg': out['v_ffn1_wg'], 'v_ffn1_wu': out['v_ffn1_wu'], 'v_ffn1_wd': out['v_ffn1_wd'], 'v_w_in': out['v_w_in'], 'v_conv_w': out['v_conv_w'], 'v_a_log': out['v_a_log'], 'v_dt_bias': out['v_dt_bias'], 'v_dn_norm': out['v_dn_norm'], 'v_w_a': out['v_w_a'], 'v_w_b': out['v_w_b'], 'v_w_o': out['v_w_o'], 'v_ffn2_wg': out['v_ffn2_wg'], 'v_ffn2_wu': out['v_ffn2_wu'], 'v_ffn2_wd': out['v_ffn2_wd'], 'v_final_norm': out['v_final_norm']}


def _loss(weights, diff, rest, loss_target):
    with _jax.named_scope("forward"):
        args = {**rest, TWIN_DIFF_INPUT: diff, **{k: w.astype(_WEIGHT_DTYPES[k]) for k, w in weights.items()}}
        y = _forward(args)
    with _jax.named_scope("loss_head"):
        err = _jnp.square(y.astype(_jnp.float32) - loss_target)
        return 0.5 * _jnp.sum(_jnp.mean(err, axis=-1)) if err.ndim else 0.5 * err


def _adamw(w, g, m, v):
    m = ADAM_B1 * m + (1.0 - ADAM_B1) * g
    v = ADAM_B2 * v + (1.0 - ADAM_B2) * _jnp.square(g)
    m_hat = m / (1.0 - ADAM_B1 ** ADAM_STEP)
    v_hat = v / (1.0 - ADAM_B2 ** ADAM_STEP)
    delta = -ADAM_LR * (m_hat / (_jnp.sqrt(v_hat) + ADAM_EPS) + ADAM_WD * w)
    return delta, m, v


def reference(x, c, ada_w, ada_b, ln_ffn1, ln_mix, ln_ffn2, ffn1_wg, ffn1_wu, ffn1_wd, w_in, conv_w, a_log, dt_bias, dn_norm, w_a, w_b, w_o, ffn2_wg, ffn2_wu, ffn2_wd, final_norm, loss_target, m_ada_w, m_ada_b, m_ln_ffn1, m_ln_mix, m_ln_ffn2, m_ffn1_wg, m_ffn1_wu, m_ffn1_wd, m_w_in, m_conv_w, m_a_log, m_dt_bias, m_dn_norm, m_w_a, m_w_b, m_w_o, m_ffn2_wg, m_ffn2_wu, m_ffn2_wd, m_final_norm, v_ada_w, v_ada_b, v_ln_ffn1, v_ln_mix, v_ln_ffn2, v_ffn1_wg, v_ffn1_wu, v_ffn1_wd, v_w_in, v_conv_w, v_a_log, v_dt_bias, v_dn_norm, v_w_a, v_w_b, v_w_o, v_ffn2_wg, v_ffn2_wu, v_ffn2_wd, v_final_norm):
    given = dict(x=x, c=c, ada_w=ada_w, ada_b=ada_b, ln_ffn1=ln_ffn1, ln_mix=ln_mix, ln_ffn2=ln_ffn2, ffn1_wg=ffn1_wg, ffn1_wu=ffn1_wu, ffn1_wd=ffn1_wd, w_in=w_in, conv_w=conv_w, a_log=a_log, dt_bias=dt_bias, dn_norm=dn_norm, w_a=w_a, w_b=w_b, w_o=w_o, ffn2_wg=ffn2_wg, ffn2_wu=ffn2_wu, ffn2_wd=ffn2_wd, final_norm=final_norm, loss_target=loss_target, m_ada_w=m_ada_w, m_ada_b=m_ada_b, m_ln_ffn1=m_ln_ffn1, m_ln_mix=m_ln_mix, m_ln_ffn2=m_ln_ffn2, m_ffn1_wg=m_ffn1_wg, m_ffn1_wu=m_ffn1_wu, m_ffn1_wd=m_ffn1_wd, m_w_in=m_w_in, m_conv_w=m_conv_w, m_a_log=m_a_log, m_dt_bias=m_dt_bias, m_dn_norm=m_dn_norm, m_w_a=m_w_a, m_w_b=m_w_b, m_w_o=m_w_o, m_ffn2_wg=m_ffn2_wg, m_ffn2_wu=m_ffn2_wu, m_ffn2_wd=m_ffn2_wd, m_final_norm=m_final_norm, v_ada_w=v_ada_w, v_ada_b=v_ada_b, v_ln_ffn1=v_ln_ffn1, v_ln_mix=v_ln_mix, v_ln_ffn2=v_ln_ffn2, v_ffn1_wg=v_ffn1_wg, v_ffn1_wu=v_ffn1_wu, v_ffn1_wd=v_ffn1_wd, v_w_in=v_w_in, v_conv_w=v_conv_w, v_a_log=v_a_log, v_dt_bias=v_dt_bias, v_dn_norm=v_dn_norm, v_w_a=v_w_a, v_w_b=v_w_b, v_w_o=v_w_o, v_ffn2_wg=v_ffn2_wg, v_ffn2_wu=v_ffn2_wu, v_ffn2_wd=v_ffn2_wd, v_final_norm=v_final_norm)
    weights = {n: given[n] for n in TWIN_WEIGHTS}
    shared = {n: given[n] for n in SHARED_INPUTS}
    per_example = {n: given[n] for n in ['x', 'c']}
    grad_fn = _jax.value_and_grad(_loss, argnums=(0, 1))

    def one_microbatch(ex, loss_target):
        ex = dict(ex)
        diff = ex.pop(TWIN_DIFF_INPUT)
        return grad_fn(weights, diff, {**shared, **ex}, loss_target)

    if N_MICROBATCH == 1:
        loss, (grad_w, grad_x) = one_microbatch(per_example, given["loss_target"])
    else:
        def body(carry, xs):
            loss_sum, grad_sum = carry
            l_k, (gw_k, gx_k) = one_microbatch(xs[0], xs[1])
            with _jax.named_scope("update"):
                return (loss_sum + l_k, _jax.tree.map(_jnp.add, grad_sum, gw_k)), gx_k

        init = (_jnp.zeros((), _jnp.float32), _jax.tree.map(_jnp.zeros_like, weights))
        (loss, grad_w), grad_x = _jax.lax.scan(body, init, (per_example, given["loss_target"]))
    with _jax.named_scope("update"):
        delta_w, new_m, new_v = {}, {}, {}
        for n in TWIN_WEIGHTS:
            delta_w[n], new_m[n], new_v[n] = _adamw(weights[n], grad_w[n], given["m_" + n], given["v_" + n])
    return (loss, grad_x, *[grad_w[n] for n in TWIN_WEIGHTS], *[delta_w[n] for n in TWIN_WEIGHTS],
            *[new_m[n] for n in TWIN_WEIGHTS], *[new_v[n] for n in TWIN_WEIGHTS])
```

```python
import functools
import math

import numpy as np
import jax
import jax.numpy as jnp
from jax import lax
from jax.experimental import pallas as pl
from jax.experimental.pallas import tpu as pltpu

F32 = jnp.float32
BF16 = jnp.bfloat16
HI = lax.Precision.HIGHEST

D = 1024
SEQ = 4096
DEPTH = 2
N_DEV = 8
DN_HEADS = 8
DN_DIM = 128
DN_CHUNK = 64
DN_CONV = 4
DA_HEADS = 12
DA_DIM = 64
DA_BLOCK = 128
DA_DILATIONS = (1, 4, 16)
ALIBI_MAX_EXP = 8.0
D_FF = 2816
N_ADA = 9
NORM_EPS = 1e-6
IN_COLS = 8464
IN_COLS_PAD = 8704
ADAM_LR, ADAM_B1, ADAM_B2, ADAM_EPS, ADAM_WD, ADAM_STEP = 0.001, 0.9, 0.999, 1e-08, 0.01, 10
NEG = -1e30

VMEM_LIMIT = 56 * 1024 * 1024
LANES = 128

MESH = pl.DeviceIdType.MESH


def _cparams(n_grid):
    return pltpu.CompilerParams(dimension_semantics=("arbitrary",) * n_grid,
                                vmem_limit_bytes=VMEM_LIMIT)


def blockwise(name, f, grid, ins, outs):
    n_in, n_out = len(ins), len(outs)
    diff = [i for i, (_, _, kind) in enumerate(ins) if kind != "const"]

    def apply(*vals):
        res = f(*vals)
        return tuple(r.astype(dt) for r, (_, dt, _, _) in zip(res, outs))

    def fwd_call(*arrays):
        def body(*refs):
            res = apply(*[r[...] for r in refs[:n_in]])
            for r, v in zip(refs[n_in:], res):
                r[...] = v

        return pl.pallas_call(
            body, name=name + "_fwd", grid=grid,
            in_specs=[pl.BlockSpec(b, im) for (b, im, _) in ins],
            out_specs=[pl.BlockSpec(b, im) for (_, _, b, im) in outs],
            out_shape=[jax.ShapeDtypeStruct(s, dt) for (s, dt, _, _) in outs],
            compiler_params=_cparams(len(grid)),
        )(*arrays)

    def bwd_call(arrays, cts):
        def body(*refs):
            in_refs, ct_refs = refs[:n_in], refs[n_in:n_in + n_out]
            g_refs = refs[n_in + n_out:]
            vals = [r[...] for r in in_refs]

            def fd(*dvals):
                full = list(vals)
                for i, v in zip(diff, dvals):
                    full[i] = v
                return apply(*full)

            _, vjp = jax.vjp(fd, *[vals[i] for i in diff])
            grads = vjp(tuple(r[...] for r in ct_refs))
            first = functools.reduce(jnp.logical_and,
                                     [pl.program_id(a) == 0 for a in range(len(grid))])
            for g_ref, g, i in zip(g_refs, grads, diff):
                if ins[i][2] == "acc":
                    @pl.when(first)
                    def _(g_ref=g_ref):
                        g_ref[...] = jnp.zeros_like(g_ref)
                    g_ref[...] += g.astype(F32)
                else:
                    g_ref[...] = g.astype(g_ref.dtype)

        g_shapes = [jax.ShapeDtypeStruct(arrays[i].shape,
                                         F32 if ins[i][2] == "acc" else arrays[i].dtype)
                    for i in diff]
        return pl.pallas_call(
            body, name=name + "_bwd", grid=grid,
            in_specs=([pl.BlockSpec(b, im) for (b, im, _) in ins]
                      + [pl.BlockSpec(b, im) for (_, _, b, im) in outs]),
            out_specs=[pl.BlockSpec(ins[i][0], ins[i][1]) for i in diff],
            out_shape=g_shapes,
            compiler_params=_cparams(len(grid)),
        )(*arrays, *cts)

    @jax.custom_vjp
    def op(*arrays):
        return tuple(fwd_call(*arrays))

    def op_fwd(*arrays):
        return tuple(fwd_call(*arrays)), arrays

    def op_bwd(arrays, cts):
        grads = bwd_call(arrays, cts)
        full = [None] * n_in
        for i, g in zip(diff, grads):
            full[i] = g.astype(arrays[i].dtype)
        return tuple(full)

    op.defvjp(op_fwd, op_bwd)
    return op


def _pick(n, cands):
    for c in cands:
        if n % c == 0:
            return c
    return n


def matmul(name, a, b, form, out_dtype):
    if form == "nn":
        (m, k), (_, n) = a.shape, b.shape
    elif form == "nt":
        (m, k), (n, _) = a.shape, b.shape
    else:
        (k, m), (_, n) = a.shape, b.shape
    tm = _pick(m, (1408, 1024, 512, 256, 128, 8))
    tn = _pick(n, (1408, 1024, 512, 384, 256, 128))
    tk = _pick(k, (1024, 1408, 512, 384, 256, 128, 8))
    nk = k // tk
    a_spec = (pl.BlockSpec((tk, tm), lambda i, j, kk: (kk, i)) if form == "tn"
              else pl.BlockSpec((tm, tk), lambda i, j, kk: (i, kk)))
    b_spec = (pl.BlockSpec((tn, tk), lambda i, j, kk: (j, kk)) if form == "nt"
              else pl.BlockSpec((tk, tn), lambda i, j, kk: (kk, j)))
    dims = {"nn": (((1,), (0,)), ((), ())), "nt": (((1,), (1,)), ((), ())),
            "tn": (((0,), (0,)), ((), ()))}[form]

    def body(a_ref, b_ref, o_ref, acc_ref):
        kk = pl.program_id(2)
        part = lax.dot_general(a_ref[...].astype(BF16), b_ref[...].astype(BF16), dims,
                               preferred_element_type=F32)

        @pl.when(kk == 0)
        def _():
            acc_ref[...] = part

        @pl.when(kk > 0)
        def _():
            acc_ref[...] += part

        @pl.when(kk == nk - 1)
        def _():
            o_ref[...] = acc_ref[...].astype(o_ref.dtype)

    return pl.pallas_call(
        body, name=name, grid=(m // tm, n // tn, nk),
        in_specs=[a_spec, b_spec],
        out_specs=pl.BlockSpec((tm, tn), lambda i, j, kk: (i, j)),
        out_shape=jax.ShapeDtypeStruct((m, n), out_dtype),
        scratch_shapes=[pltpu.VMEM((tm, tn), F32)],
        compiler_params=_cparams(3),
    )(a, b)


def linear(name, x, w, out_dtype):
    @jax.custom_vjp
    def op(x, w):
        return matmul(name + "_y", x, w, "nn", out_dtype)

    def op_fwd(x, w):
        return op(x, w), (x, w)

    def op_bwd(res, dy):
        x, w = res
        dx = matmul(name + "_dx", dy, w, "nt", x.dtype)
        dw = matmul(name + "_dw", x, dy, "tn", w.dtype)
        return dx, dw

    op.defvjp(op_fwd, op_bwd)
    return op(x, w)


def _sigmoid(x):
    return 1.0 / (1.0 + jnp.exp(-x))


def _silu(x):
    return x * _sigmoid(x)


def _softplus(x):
    return jnp.maximum(x, 0.0) + jnp.log(1.0 + jnp.exp(-jnp.abs(x)))


def _rms(x):
    return x * lax.rsqrt(jnp.mean(x * x, axis=-1, keepdims=True) + NORM_EPS)


ROW_TILE = 512


def _row(i):
    return (i, 0)


def _fixed(*_):
    return (0, 0)


def norm_mod(name, h, ln, shift, scale):
    s, d = h.shape

    def f(h, ln, sh, sc):
        return ((_rms(h) * ln) * (1.0 + sc) + sh,)

    op = blockwise(name, f, (s // ROW_TILE,),
                   [((ROW_TILE, d), _row, "tile")] + [((1, d), _fixed, "acc")] * 3,
                   [((s, d), BF16, (ROW_TILE, d), _row)])
    return op(h, ln, shift, scale)[0]


def swiglu_act(name, gu):
    s, f2 = gu.shape
    ff = f2 // 2
    tn = 1408
    nj = ff // tn

    def f(g, u):
        return (_silu(g.astype(F32)) * u.astype(F32),)

    op = blockwise(name, f, (s // ROW_TILE, nj),
                   [((ROW_TILE, tn), lambda i, j: (i, j), "tile"),
                    ((ROW_TILE, tn), lambda i, j: (i, j), "tile")],
                   [((s, ff), BF16, (ROW_TILE, tn), lambda i, j: (i, j))])
    return op(gu[:, :ff], gu[:, ff:])[0]


def residual(name, h, y, gate, weight):
    s, d = h.shape

    def f(h, y, gate):
        return (h + (weight * gate) * y,)

    op = blockwise(name, f, (s // ROW_TILE,),
                   [((ROW_TILE, d), _row, "tile"), ((ROW_TILE, d), _row, "tile"),
                    ((1, d), _fixed, "acc")],
                   [((s, d), F32, (ROW_TILE, d), _row)])
    return op(h, y, gate)[0]


def merge_gates(name, ga, gb, ya, yb):
    s, d = ya.shape

    def f(ga, gb, ya, yb):
        return (_sigmoid(ga) * ya + _sigmoid(gb) * yb,)

    op = blockwise(name, f, (s // ROW_TILE,), [((ROW_TILE, d), _row, "tile")] * 4,
                   [((s, d), BF16, (ROW_TILE, d), _row)])
    return op(ga, gb, ya, yb)[0]


def loss_rows(name, h, g, target):
    s, d = h.shape

    def f(h, g, t):
        err = _rms(h) * g - t
        return (jnp.mean(err * err, axis=-1, keepdims=True),)

    op = blockwise(name, f, (s // ROW_TILE,),
                   [((ROW_TILE, d), _row, "tile"), ((1, d), _fixed, "acc"),
                    ((ROW_TILE, d), _row, "const")],
                   [((s, 1), F32, (ROW_TILE, 1), _row)])
    return op(h, g, target)[0]


def decay_beta(name, ab, a_log_pad, dt_bias_pad):
    s, n = ab.shape

    def f(ab, a_log, dt_bias):
        lane = lax.broadcasted_iota(jnp.int32, ab.shape, 1)
        beta = _sigmoid(ab)
        g = -jnp.exp(a_log) * _softplus(ab + dt_bias)
        return (jnp.where(lane < DN_HEADS, beta, jnp.where(lane < 2 * DN_HEADS, g, 0.0)),)

    op = blockwise(name, f, (s // ROW_TILE,),
                   [((ROW_TILE, n), _row, "tile"), ((1, n), _fixed, "acc"), ((1, n), _fixed, "acc")],
                   [((s, n), F32, (ROW_TILE, n), _row)])
    return op(ab, a_log_pad, dt_bias_pad)[0]


def _shift_rows(x, k):
    n = x.shape[0]

    @jax.custom_vjp
    def shift(x):
        row = lax.broadcasted_iota(jnp.int32, x.shape, 0)
        return jnp.where(row >= k, pltpu.roll(x, k, 0), 0.0)

    def shift_fwd(x):
        return shift(x), None

    def shift_bwd(_, g):
        row = lax.broadcasted_iota(jnp.int32, g.shape, 0)
        return (jnp.where(row < n - k, pltpu.roll(g, n - k, 0), 0.0),)

    shift.defvjp(shift_fwd, shift_bwd)
    return shift(x)


def conv_heads(name, x, w, mode):
    s, width = x.shape
    nh = width // LANES

    def f(x, w):
        y = w[DN_CONV - 1] * x
        for j in range(DN_CONV - 1):
            y = y + w[j] * _shift_rows(x, DN_CONV - 1 - j)
        y = _silu(y)
        if mode != "v":
            y = y * lax.rsqrt(jnp.sum(y * y, axis=-1, keepdims=True) + NORM_EPS)
        if mode == "q":
            y = y * (DN_DIM ** -0.5)
        return (y[None],)

    op = blockwise(name, f, (nh,),
                   [((s, LANES), lambda j: (0, j), "tile"),
                    ((DN_CONV, 1, LANES), lambda j: (0, 0, j), "tile")],
                   [((nh, s, LANES), F32, (1, s, LANES), lambda j: (j, 0, 0))])
    return op(x, w)[0]


def gated_head_norm(name, o, z, w):
    nh, s, dh = o.shape

    def f(o, z, w):
        return (_rms(o[0]) * w * _silu(z),)

    op = blockwise(name, f, (s // ROW_TILE, nh),
                   [((1, ROW_TILE, dh), lambda i, h: (h, i, 0), "tile"),
                    ((ROW_TILE, dh), lambda i, h: (i, h), "tile"),
                    ((1, dh), lambda i, h: (0, 0), "acc")],
                   [((s, nh * dh), BF16, (ROW_TILE, dh), lambda i, h: (i, h))])
    return op(o, z, w)[0]


def _attn_bias():
    slopes = 2.0 ** (-ALIBI_MAX_EXP * np.arange(1, DA_HEADS + 1, dtype=np.float64) / DA_HEADS)
    qi = np.arange(DA_BLOCK)[:, None]
    ki = np.arange(DA_BLOCK)[None, :]
    prev = np.zeros((3, DA_HEADS, 2, DA_BLOCK, DA_BLOCK), np.float32)
    cur = np.zeros((3, DA_HEADS, DA_BLOCK, DA_BLOCK), np.float32)
    for p, r in enumerate(DA_DILATIONS):
        dist_prev = qi + DA_BLOCK - ki
        dist_cur = qi - ki
        for h in range(DA_HEADS):
            prev[p, h, 0] = np.where(dist_prev <= DA_BLOCK, -slopes[h] * dist_prev * r, NEG)
            prev[p, h, 1] = NEG
            cur[p, h] = np.where(dist_cur >= 0, -slopes[h] * dist_cur * r, NEG)
    return jnp.asarray(prev), jnp.asarray(cur)


def dilated_attention(name, q, k, v):
    npat, nh, s, dh = q.shape
    nblk = s // DA_BLOCK
    bias_prev, bias_cur = _attn_bias()

    def f(q, kp, kc, vp, vc, bp, bc):
        q = q[0, 0].astype(BF16)
        nt = (((1,), (1,)), ((), ()))
        sp = lax.dot_general(q, kp[0, 0].astype(BF16), nt, preferred_element_type=F32)
        sc = lax.dot_general(q, kc[0, 0].astype(BF16), nt, preferred_element_type=F32)
        sp = sp * (dh ** -0.5) + bp[0, 0, 0]
        sc = sc * (dh ** -0.5) + bc[0, 0]
        mx = jnp.maximum(jnp.max(sp, axis=-1, keepdims=True), jnp.max(sc, axis=-1, keepdims=True))
        pp = jnp.exp(sp - mx)
        pc = jnp.exp(sc - mx)
        l = jnp.sum(pp, axis=-1, keepdims=True) + jnp.sum(pc, axis=-1, keepdims=True)
        o = (jnp.dot((pp / l).astype(BF16), vp[0, 0].astype(BF16), preferred_element_type=F32)
             + jnp.dot((pc / l).astype(BF16), vc[0, 0].astype(BF16), preferred_element_type=F32))
        lse = mx + jnp.log(l)
        return o[None, None], jnp.broadcast_to(lse, o.shape)[None, None]

    def is_first(p, j):
        return ((j & ((nblk >> (2 * p)) - 1)) == 0).astype(jnp.int32)

    blk = (1, 1, DA_BLOCK, dh)
    cur_map = lambda p, h, j: (p, h, j, 0)
    prev_map = lambda p, h, j: (p, h, jnp.maximum(j - 1, 0), 0)
    op = blockwise(
        name, f, (npat, nh, nblk),
        [(blk, cur_map, "tile"), (blk, prev_map, "tile"), (blk, cur_map, "tile"),
         (blk, prev_map, "tile"), (blk, cur_map, "tile"),
         ((1, 1, 1, DA_BLOCK, DA_BLOCK), lambda p, h, j: (p, h, is_first(p, j), 0, 0), "const"),
         ((1, 1, DA_BLOCK, DA_BLOCK), lambda p, h, j: (p, h, 0, 0), "const")],
        [((npat, nh, s, dh), F32, blk, cur_map), ((npat, nh, s, dh), F32, blk, cur_map)])

    @jax.custom_vjp
    def attn(q, k, v):
        return op(q, k, k, v, v, bias_prev, bias_cur)

    def attn_fwd(q, k, v):
        return attn(q, k, v), (q, k, v)

    def attn_bwd(res, cts):
        q, k, v = res
        _, vjp = jax.vjp(lambda q, kp, kc, vp, vc: op(q, kp, kc, vp, vc, bias_prev, bias_cur),
                         q, k, k, v, v)
        dq, dkp, dkc, dvp, dvc = vjp(cts)
        fill = lambda t: t.at[:, :, s - DA_BLOCK:].set(0.0)
        return dq, dkc + fill(dkp), dvc + fill(dvp)

    attn.defvjp(attn_fwd, attn_bwd)
    return attn(q, k, v)


def combine_patterns(name, o, lse):
    npat, nh, s, dh = o.shape

    def f(o, lse):
        lse = lse[:, 0]
        mx = jnp.max(lse, axis=0, keepdims=True)
        e = jnp.exp(lse - mx)
        w = e / jnp.sum(e, axis=0, keepdims=True)
        return (jnp.sum(w * o[:, 0], axis=0)[None],)

    blk = (npat, 1, ROW_TILE, dh)
    imap = lambda h, i: (0, h, i, 0)
    op = blockwise(name, f, (nh, s // ROW_TILE), [(blk, imap, "tile"), (blk, imap, "tile")],
                   [((nh, s, dh), F32, (1, ROW_TILE, dh), lambda h, i: (h, i, 0))])
    return op(o, lse)[0]


def _to_sub(t):
    s = t.shape[0]
    t = t.reshape(s, DA_HEADS, DA_DIM)
    outs = []
    for r in DA_DILATIONS:
        outs.append(t.reshape(s // r, r, DA_HEADS, DA_DIM).transpose(2, 1, 0, 3)
                    .reshape(DA_HEADS, s, DA_DIM))
    return jnp.stack(outs)


def _from_sub(t):
    _, nh, s, dh = t.shape
    outs = []
    for p, r in enumerate(DA_DILATIONS):
        outs.append(t[p].reshape(nh, r, s // r, dh).transpose(0, 2, 1, 3).reshape(nh, s, dh))
    return jnp.stack(outs)


def _bmm(a, b, ca, cb):
    return lax.dot_general(a, b, (((ca,), (cb,)), ((0,), (0,))), precision=HI,
                           preferred_element_type=F32)


def _delta_chunk(q, k, v, gcol, grow, bcol, state):
    c = q.shape[1]
    ii = lax.broadcasted_iota(jnp.int32, (1, c, c), 1)
    jj = lax.broadcasted_iota(jnp.int32, (1, c, c), 2)
    incl, strict = ii >= jj, ii > jj
    gc_col = jnp.sum(jnp.where(incl, grow, 0.0), axis=2, keepdims=True)
    gc_row = jnp.sum(jnp.where(ii <= jj, gcol, 0.0), axis=1, keepdims=True)
    decay = jnp.where(incl, jnp.exp(jnp.where(incl, gc_col - gc_row, 0.0)), 0.0)
    kb, vb = k * bcol, v * bcol
    m = jnp.where(strict, _bmm(kb, k, 2, 2) * decay, 0.0)
    eye = (ii == jj).astype(F32)
    p = -m
    inv = eye + p
    for _ in range(int(math.log2(c)) - 1):
        p = _bmm(p, p, 2, 1)
        inv = inv + _bmm(inv, p, 2, 1)
    e_col = jnp.exp(gc_col)
    u = _bmm(inv, vb, 2, 1)
    w = _bmm(inv, kb * e_col, 2, 1)
    qk = _bmm(q, k, 2, 2) * decay
    v_new = u - _bmm(w, state, 2, 1)
    o = _bmm(q * e_col, state, 2, 1) + _bmm(qk, v_new, 2, 1)
    g_last = jnp.sum(grow, axis=2, keepdims=True)
    new_state = state * jnp.exp(g_last) + _bmm(k * jnp.exp(g_last - gc_col), v_new, 1, 1)
    return o, new_state


def delta_rule(name, q, k, v, gcol, grow, bcol):
    nh, s, dh = q.shape
    c = DN_CHUNK
    n = s // c

    def specs(rev):
        t = (lambda i: n - 1 - i) if rev else (lambda i: i)
        seq = pl.BlockSpec((nh, c, dh), lambda i: (0, t(i), 0))
        col = pl.BlockSpec((nh, c, 1), lambda i: (0, t(i), 0))
        row = pl.BlockSpec((nh, 1, 1, c), lambda i: (0, t(i), 0, 0))
        st = pl.BlockSpec((nh, 1, dh, dh), lambda i: (0, t(i), 0, 0))
        return seq, col, row, st

    def fwd_call(q, k, v, gcol, grow, bcol):
        seq, col, row, st = specs(False)

        def body(q_ref, k_ref, v_ref, gc_ref, gr_ref, b_ref, o_ref, st_ref, state):
            @pl.when(pl.program_id(0) == 0)
            def _():
                state[...] = jnp.zeros_like(state)

            st_ref[:, 0] = state[...]
            o, new_state = _delta_chunk(q_ref[...], k_ref[...], v_ref[...], gc_ref[...],
                                        gr_ref[:, 0], b_ref[...], state[...])
            o_ref[...] = o
            state[...] = new_state

        return pl.pallas_call(
            body, name=name + "_fwd", grid=(n,),
            in_specs=[seq, seq, seq, col, row, col],
            out_specs=[seq, st],
            out_shape=[jax.ShapeDtypeStruct((nh, s, dh), F32),
                       jax.ShapeDtypeStruct((nh, n, dh, dh), F32)],
            scratch_shapes=[pltpu.VMEM((nh, dh, dh), F32)],
            compiler_params=_cparams(1),
        )(q, k, v, gcol, grow, bcol)

    def bwd_call(q, k, v, gcol, grow, bcol, states, do):
        seq, col, row, st = specs(True)

        def body(q_ref, k_ref, v_ref, gc_ref, gr_ref, b_ref, st_ref, do_ref,
                 dq_ref, dk_ref, dv_ref, dgc_ref, dgr_ref, db_ref, dstate):
            @pl.when(pl.program_id(0) == 0)
            def _():
                dstate[...] = jnp.zeros_like(dstate)

            _, vjp = jax.vjp(_delta_chunk, q_ref[...], k_ref[...], v_ref[...], gc_ref[...],
                             gr_ref[:, 0], b_ref[...], st_ref[:, 0])
            dq, dk, dv, dgc, dgr, db, dst = vjp((do_ref[...], dstate[...]))
            dq_ref[...] = dq
            dk_ref[...] = dk
            dv_ref[...] = dv
            dgc_ref[...] = dgc
            dgr_ref[:, 0] = dgr
            db_ref[...] = db
            dstate[...] = dst

        return pl.pallas_call(
            body, name=name + "_bwd", grid=(n,),
            in_specs=[seq, seq, seq, col, row, col, st, seq],
            out_specs=[seq, seq, seq, col, row, col],
            out_shape=[jax.ShapeDtypeStruct((nh, s, dh), F32)] * 3
            + [jax.ShapeDtypeStruct((nh, s, 1), F32), jax.ShapeDtypeStruct((nh, n, 1, c), F32),
               jax.ShapeDtypeStruct((nh, s, 1), F32)],
            scratch_shapes=[pltpu.VMEM((nh, dh, dh), F32)],
            compiler_params=_cparams(1),
        )(q, k, v, gcol, grow, bcol, states, do)

    @jax.custom_vjp
    def op(q, k, v, gcol, grow, bcol):
        return fwd_call(q, k, v, gcol, grow, bcol)[0]

    def op_fwd(q, k, v, gcol, grow, bcol):
        o, states = fwd_call(q, k, v, gcol, grow, bcol)
        return o, (q, k, v, gcol, grow, bcol, states)

    def op_bwd(res, do):
        return tuple(bwd_call(*res, do))

    op.defvjp(op_fwd, op_bwd)
    return op(q, k, v, gcol, grow, bcol)


def _my_place():
    return lax.axis_index("x"), lax.axis_index("y"), lax.axis_index("c")


def all_gather(name, shard):
    r, cc = shard.shape

    def body(x_ref, out_ref, send_sems, recv_sems, local_sem):
        x, y, c = _my_place()
        me, sibling = (x, y, c), (x, y, 1 - c)
        chips = [(1 - x, y), (x, 1 - y), (1 - x, 1 - y)]

        def slot(px, py, pc):
            return out_ref.at[4 * px + 2 * py + pc]

        def copy(k, block, to, src=None):
            return pltpu.make_async_remote_copy(
                src_ref=slot(*block) if src is None else src, dst_ref=slot(*block),
                send_sem=send_sems.at[k], recv_sem=recv_sems.at[k],
                device_id=to, device_id_type=MESH)

        mine = pltpu.make_async_copy(x_ref, slot(*me), local_sem)
        mine.start()
        first = [copy(0, me, sibling, src=x_ref)]
        first += [copy(1 + j, me, (*chip, c), src=x_ref) for j, chip in enumerate(chips)]
        for cp in first:
            cp.start()
        passed = [copy(4 + j, (*chip, c), sibling) for j, chip in enumerate(chips)]
        for j, chip in enumerate(chips):
            copy(1 + j, (*chip, c), me).wait_recv()
            passed[j].start()
        copy(0, sibling, me).wait_recv()
        for j, chip in enumerate(chips):
            copy(4 + j, (*chip, 1 - c), me).wait_recv()
        for cp in first + passed:
            cp.wait_send()
        mine.wait()

    return pl.pallas_call(
        body, name=name,
        out_shape=jax.ShapeDtypeStruct((N_DEV, r, cc), shard.dtype),
        in_specs=[pl.BlockSpec(memory_space=pl.ANY)],
        out_specs=pl.BlockSpec(memory_space=pl.ANY),
        scratch_shapes=[pltpu.SemaphoreType.DMA((7,)), pltpu.SemaphoreType.DMA((7,)),
                        pltpu.SemaphoreType.DMA],
    )(shard)


def all_to_all(name, parts):
    _, r, cc = parts.shape

    def body(p_ref, out_ref, send_sems, recv_sems, local_sem):
        x, y, c = _my_place()
        me = 4 * x + 2 * y + c
        flips = [(fx, fy, fc) for fx in (0, 1) for fy in (0, 1) for fc in (0, 1)][1:]
        peers = [((1 - x) if fx else x, (1 - y) if fy else y, (1 - c) if fc else c)
                 for fx, fy, fc in flips]

        def copy(k):
            px, py, pc = peers[k]
            idx = 4 * px + 2 * py + pc
            return pltpu.make_async_remote_copy(
                src_ref=p_ref.at[idx], dst_ref=out_ref.at[me],
                send_sem=send_sems.at[k], recv_sem=recv_sems.at[k],
                device_id=peers[k], device_id_type=MESH)

        def landing(k):
            px, py, pc = peers[k]
            idx = 4 * px + 2 * py + pc
            return pltpu.make_async_remote_copy(
                src_ref=p_ref.at[idx], dst_ref=out_ref.at[idx],
                send_sem=send_sems.at[k], recv_sem=recv_sems.at[k],
                device_id=peers[k], device_id_type=MESH)

        mine = pltpu.make_async_copy(p_ref.at[me], out_ref.at[me], local_sem)
        mine.start()
        copies = [copy(k) for k in range(7)]
        for cp in copies:
            cp.start()
        for k in range(7):
            landing(k).wait_recv()
        for cp in copies:
            cp.wait_send()
        mine.wait()

    return pl.pallas_call(
        body, name=name,
        out_shape=jax.ShapeDtypeStruct(parts.shape, parts.dtype),
        in_specs=[pl.BlockSpec(memory_space=pl.ANY)],
        out_specs=pl.BlockSpec(memory_space=pl.ANY),
        scratch_shapes=[pltpu.SemaphoreType.DMA((7,)), pltpu.SemaphoreType.DMA((7,)),
                        pltpu.SemaphoreType.DMA],
    )(parts)


def adamw(name, grad, w, m, v, tile):
    rows, cols = w.shape
    stacked = grad.ndim == 3

    def body(g_ref, w_ref, m_ref, v_ref, go_ref, d_ref, mo_ref, vo_ref):
        if stacked:
            g = g_ref[0].astype(F32)
            for s in range(1, N_DEV):
                g = g + g_ref[s].astype(F32)
        else:
            g = g_ref[...]
        m = ADAM_B1 * m_ref[...] + (1.0 - ADAM_B1) * g
        v = ADAM_B2 * v_ref[...] + (1.0 - ADAM_B2) * jnp.square(g)
        m_hat = m / (1.0 - ADAM_B1 ** ADAM_STEP)
        v_hat = v / (1.0 - ADAM_B2 ** ADAM_STEP)
        go_ref[...] = g
        d_ref[...] = -ADAM_LR * (m_hat / (jnp.sqrt(v_hat) + ADAM_EPS) + ADAM_WD * w_ref[...])
        mo_ref[...] = m
        vo_ref[...] = v

    flat = pl.BlockSpec((tile, cols), lambda i: (i, 0))
    g_spec = pl.BlockSpec((N_DEV, tile, cols), lambda i: (0, i, 0)) if stacked else flat
    return pl.pallas_call(
        body, name=name, grid=(rows // tile,),
        in_specs=[g_spec, flat, flat, flat], out_specs=[flat] * 4,
        out_shape=[jax.ShapeDtypeStruct((rows, cols), F32)] * 4,
        compiler_params=_cparams(1),
    )(grad, w, m, v)


def silu_rows(name, x):
    def body(x_ref, o_ref):
        o_ref[...] = _silu(x_ref[...])

    return pl.pallas_call(body, name=name, out_shape=jax.ShapeDtypeStruct(x.shape, F32))(x)


BIG = (("ffn1_wg", (D, D_FF // N_DEV), 1), ("ffn1_wu", (D, D_FF // N_DEV), 1),
       ("ffn1_wd", (D_FF // N_DEV, D), 0), ("w_in", (D, IN_COLS // N_DEV), 1),
       ("conv_w", (DN_CONV, 3 * D // N_DEV), 1), ("w_a", (D // N_DEV, D), 0),
       ("w_b", (DA_HEADS * DA_DIM, D // N_DEV), 1), ("w_o", (D // N_DEV, D), 0),
       ("ffn2_wg", (D, D_FF // N_DEV), 1), ("ffn2_wu", (D, D_FF // N_DEV), 1),
       ("ffn2_wd", (D_FF // N_DEV, D), 0))
BIG_ROWS = 7168
SMALL = (("ada_b", (DEPTH, N_ADA * D)), ("ln_ffn1", (DEPTH, D)), ("ln_mix", (DEPTH, D)),
         ("ln_ffn2", (DEPTH, D)), ("a_log", (DEPTH, DN_HEADS)), ("dt_bias", (DEPTH, DN_HEADS)),
         ("dn_norm", (DEPTH, DN_DIM)), ("final_norm", (D,)))
SMALL_ROWS = 32


def _pack(arrays, rows):
    flat = jnp.concatenate([a.reshape(-1) for a in arrays])
    return jnp.pad(flat, (0, rows * D - flat.shape[0])).reshape(rows, D)


def _unpack(buf, shapes):
    flat = buf.reshape(-1)
    out, off = [], 0
    for shp in shapes:
        n = int(np.prod(shp))
        out.append(flat[off:off + n].reshape(shp))
        off += n
    return out


def _full_weights(gathered):
    flat = gathered.reshape(N_DEV, -1)
    full, off = {}, 0
    for name, (a, b), axis in BIG:
        n = DEPTH * a * b
        t = flat[:, off:off + n].reshape(N_DEV, DEPTH, a, b)
        off += n
        if axis == 1:
            full[name] = t.transpose(1, 2, 0, 3).reshape(DEPTH, a, N_DEV * b)
        else:
            full[name] = t.transpose(1, 0, 2, 3).reshape(DEPTH, N_DEV * a, b)
    return full


_Z0, _B0, _A0, _DQ0, _GA0 = 3072, 4096, 4104, 4112, 6416


def _reorder_in_proj(w):
    pad = jnp.zeros((w.shape[0], IN_COLS_PAD - IN_COLS), w.dtype)
    return jnp.concatenate([w[:, :_B0], w[:, _DQ0:], w[:, _B0:_DQ0], pad], axis=1)


def _ffn(tag, h, ln, shift, scale, gate, w_gu, w_d):
    n = norm_mod(tag + "_norm", h, ln, shift, scale)
    gu = linear(tag + "_gu", n, w_gu, BF16)
    a = swiglu_act(tag + "_act", gu)
    f = linear(tag + "_down", a, w_d, F32)
    return residual(tag + "_res", h, f, gate, 0.5)


def _mixer(tag, u, w_in, conv_w, a_log, dt_bias, dn_norm, w_a, w_b, w_o):
    s = u.shape[0]
    proj = linear(tag + "_in", u, w_in, F32)
    c0 = 0
    qkv_pre = [proj[:, i * D:(i + 1) * D] for i in range(3)]
    z = proj[:, 3072:4096]
    da = [proj[:, 4096 + i * 768:4096 + (i + 1) * 768] for i in range(3)]
    gate_a, gate_b = proj[:, 6400:7424], proj[:, 7424:8448]
    ab = proj[:, 8448:8576]

    cw = conv_w.astype(F32).reshape(DN_CONV, 1, 3 * D)
    q, k, v = [conv_heads(f"{tag}_conv_{m}", qkv_pre[i], cw[:, :, i * D:(i + 1) * D], m)
               for i, m in enumerate("qkv")]
    pad = lambda t: jnp.pad(t, (DN_HEADS, LANES - 2 * DN_HEADS))[None]
    gb = decay_beta(tag + "_decay", ab, pad(a_log), pad(dt_bias))
    beta = gb[:, :DN_HEADS].T[:, :, None]
    g = gb[:, DN_HEADS:2 * DN_HEADS].T
    o = delta_rule(tag + "_delta", q, k, v, g[:, :, None],
                   g.reshape(DN_HEADS, s // DN_CHUNK, 1, DN_CHUNK), beta)
    o_a = gated_head_norm(tag + "_gnorm", o, z, dn_norm[None])
    y_a = linear(tag + "_wa", o_a, w_a, F32)

    o_s, lse_s = dilated_attention(tag + "_attn", *[_to_sub(t) for t in da])
    o_b = combine_patterns(tag + "_comb", _from_sub(o_s), _from_sub(lse_s))
    o_b = o_b.transpose(1, 0, 2).reshape(s, DA_HEADS * DA_DIM).astype(BF16)
    y_b = linear(tag + "_wb", o_b, w_b, F32)

    merged = merge_gates(tag + "_merge", gate_a, gate_b, y_a, y_b)
    return linear(tag + "_wo", merged, w_o, F32)


def _local_loss(x, gathered, small, mod, target):
    w = _full_weights(gathered)
    h = x
    for l in range(DEPTH):
        tag = f"l{l}"
        sh1, sc1, gt1, sh2, sc2, gt2, sh3, sc3, gt3 = [mod[l, i * D:(i + 1) * D][None]
                                                       for i in range(N_ADA)]
        w_gu1 = jnp.concatenate([w["ffn1_wg"][l], w["ffn1_wu"][l]], axis=1)
        w_gu2 = jnp.concatenate([w["ffn2_wg"][l], w["ffn2_wu"][l]], axis=1)
        h = _ffn(tag + "_ffn1", h, small["ln_ffn1"][l][None], sh1, sc1, gt1, w_gu1, w["ffn1_wd"][l])
        u = norm_mod(tag + "_mixnorm", h, small["ln_mix"][l][None], sh2, sc2)
        m = _mixer(tag + "_mix", u, _reorder_in_proj(w["w_in"][l]), w["conv_w"][l],
                   small["a_log"][l], small["dt_bias"][l], small["dn_norm"][l],
                   w["w_a"][l], w["w_b"][l], w["w_o"][l])
        h = residual(tag + "_mixres", h, m, gt2, 1.0)
        h = _ffn(tag + "_ffn2", h, small["ln_ffn2"][l][None], sh3, sc3, gt3, w_gu2, w["ffn2_wd"][l])
    rows = loss_rows("loss", h, small["final_norm"][None], target)
    return 0.5 * jnp.sum(rows)


def kernel(x, c, ada_w, ada_b, ln_ffn1, ln_mix, ln_ffn2, ffn1_wg, ffn1_wu, ffn1_wd, w_in, conv_w, a_log, dt_bias, dn_norm, w_a, w_b, w_o, ffn2_wg, ffn2_wu, ffn2_wd, final_norm, loss_target, m_ada_w, m_ada_b, m_ln_ffn1, m_ln_mix, m_ln_ffn2, m_ffn1_wg, m_ffn1_wu, m_ffn1_wd, m_w_in, m_conv_w, m_a_log, m_dt_bias, m_dn_norm, m_w_a, m_w_b, m_w_o, m_ffn2_wg, m_ffn2_wu, m_ffn2_wd, m_final_norm, v_ada_w, v_ada_b, v_ln_ffn1, v_ln_mix, v_ln_ffn2, v_ffn1_wg, v_ffn1_wu, v_ffn1_wd, v_w_in, v_conv_w, v_a_log, v_dt_bias, v_dn_norm, v_w_a, v_w_b, v_w_o, v_ffn2_wg, v_ffn2_wu, v_ffn2_wd, v_final_norm):
    args = dict(locals())
    big_names = [n for n, _, _ in BIG]
    small_names = [n for n, _ in SMALL]
    me = 4 * lax.axis_index("x") + 2 * lax.axis_index("y") + lax.axis_index("c")
    cols = N_ADA * D // N_DEV

    c_all = all_gather("gather_c", jnp.pad(silu_rows("silu_c", c), ((0, 7), (0, 0))))[:, 0]
    mod_cols = jnp.stack([matmul(f"ada{l}", c_all, ada_w[l], "nn", F32) for l in range(DEPTH)])
    mod_cols = mod_cols + lax.dynamic_slice_in_dim(ada_b, me * cols, cols, axis=1)[:, None, :]
    mod_all = all_gather("gather_mod", mod_cols.reshape(DEPTH * N_DEV, cols))
    mod_all = mod_all.reshape(N_DEV, DEPTH, N_DEV, cols)
    mod = lax.dynamic_index_in_dim(mod_all, me, axis=2, keepdims=False)
    mod = mod.transpose(1, 0, 2).reshape(DEPTH, N_ADA * D)

    w_pack = _pack([args[n] for n in big_names], BIG_ROWS)
    gathered = all_gather("gather_w", w_pack.astype(BF16))

    small = {n: args[n] for n in small_names if n != "ada_b"}
    loss, (dx, dgathered, dsmall, dmod) = jax.value_and_grad(_local_loss, argnums=(0, 1, 2, 3))(
        x[0], gathered, small, mod, loss_target[0])

    part = _pack([dmod] + [dsmall[n] for n in small_names[1:]], SMALL_ROWS)
    parts = all_gather("gather_small", part)
    sm_out = adamw("adamw_small", parts, _pack([args[n] for n in small_names], SMALL_ROWS),
                   _pack([args["m_" + n] for n in small_names], SMALL_ROWS),
                   _pack([args["v_" + n] for n in small_names], SMALL_ROWS), SMALL_ROWS)

    dmod_all = parts.reshape(N_DEV, -1)[:, :DEPTH * N_ADA * D].reshape(N_DEV, DEPTH, N_ADA * D)
    dmod_mine = lax.dynamic_slice_in_dim(dmod_all, me * cols, cols, axis=2)
    g_ada = jnp.stack([matmul(f"ada{l}_dw", c_all, dmod_mine[:, l], "tn", F32) for l in range(DEPTH)])
    ada_rows = DEPTH * D * cols // D
    ada_out = adamw("adamw_ada", g_ada.reshape(ada_rows, D), ada_w.reshape(ada_rows, D),
                    m_ada_w.reshape(ada_rows, D), v_ada_w.reshape(ada_rows, D), 256)

    landed = all_to_all("scatter_grads", dgathered)
    big_out = adamw("adamw_big", landed, _pack([args[n] for n in big_names], BIG_ROWS),
                    _pack([args["m_" + n] for n in big_names], BIG_ROWS),
                    _pack([args["v_" + n] for n in big_names], BIG_ROWS), 256)

    big_shapes = [(DEPTH,) + shp for _, shp, _ in BIG]
    small_shapes = [shp for _, shp in SMALL]
    names = ["ada_w", "ada_b", "ln_ffn1", "ln_mix", "ln_ffn2", "ffn1_wg", "ffn1_wu", "ffn1_wd", "w_in",
             "conv_w", "a_log", "dt_bias", "dn_norm", "w_a", "w_b", "w_o", "ffn2_wg", "ffn2_wu",
             "ffn2_wd", "final_norm"]
    outs = [lax.psum(loss, ("x", "y", "c")), dx[None]]
    for kind in range(4):
        table = dict(zip(big_names, _unpack(big_out[kind], big_shapes)))
        table.update(zip(small_names, _unpack(sm_out[kind], small_shapes)))
        table["ada_w"] = ada_out[kind].reshape(ada_w.shape)
        outs += [table[n] for n in names]
    return tuple(outs)
```

```python
import functools
import math

import numpy as np
import jax
import jax.numpy as jnp
from jax import lax
from jax.experimental import pallas as pl
from jax.experimental.pallas import tpu as pltpu

F32 = jnp.float32
BF16 = jnp.bfloat16
HI = lax.Precision.HIGHEST

D = 1024
SEQ = 4096
DEPTH = 2
N_DEV = 8
DN_HEADS = 8
DN_DIM = 128
DN_CHUNK = 64
DN_CONV = 4
DA_HEADS = 12
DA_DIM = 64
DA_BLOCK = 128
DA_DILATIONS = (1, 4, 16)
ALIBI_MAX_EXP = 8.0
D_FF = 2816
N_ADA = 9
NORM_EPS = 1e-6
IN_COLS = 8464
IN_COLS_PAD = 8704
ADAM_LR, ADAM_B1, ADAM_B2, ADAM_EPS, ADAM_WD, ADAM_STEP = 0.001, 0.9, 0.999, 1e-08, 0.01, 10
NEG = -1e30

VMEM_LIMIT = 56 * 1024 * 1024
LANES = 128

MESH = pl.DeviceIdType.MESH


def _cparams(n_grid):
    return pltpu.CompilerParams(dimension_semantics=("arbitrary",) * n_grid,
                                vmem_limit_bytes=VMEM_LIMIT)


def blockwise(name, f, grid, ins, outs):
    n_in, n_out = len(ins), len(outs)
    diff = [i for i, (_, _, kind) in enumerate(ins) if kind != "const"]

    def apply(*vals):
        res = f(*vals)
        return tuple(r.astype(dt) for r, (_, dt, _, _) in zip(res, outs))

    def fwd_call(*arrays):
        def body(*refs):
            res = apply(*[r[...] for r in refs[:n_in]])
            for r, v in zip(refs[n_in:], res):
                r[...] = v

        return pl.pallas_call(
            body, name=name + "_fwd", grid=grid,
            in_specs=[pl.BlockSpec(b, im) for (b, im, _) in ins],
            out_specs=[pl.BlockSpec(b, im) for (_, _, b, im) in outs],
            out_shape=[jax.ShapeDtypeStruct(s, dt) for (s, dt, _, _) in outs],
            compiler_params=_cparams(len(grid)),
        )(*arrays)

    def bwd_call(arrays, cts):
        def body(*refs):
            in_refs, ct_refs = refs[:n_in], refs[n_in:n_in + n_out]
            g_refs = refs[n_in + n_out:]
            vals = [r[...] for r in in_refs]

            def fd(*dvals):
                full = list(vals)
                for i, v in zip(diff, dvals):
                    full[i] = v
                return apply(*full)

            _, vjp = jax.vjp(fd, *[vals[i] for i in diff])
            grads = vjp(tuple(r[...] for r in ct_refs))
            first = functools.reduce(jnp.logical_and,
                                     [pl.program_id(a) == 0 for a in range(len(grid))])
            for g_ref, g, i in zip(g_refs, grads, diff):
                if ins[i][2] == "acc":
                    @pl.when(first)
                    def _(g_ref=g_ref):
                        g_ref[...] = jnp.zeros_like(g_ref)
                    g_ref[...] += g.astype(F32)
                else:
                    g_ref[...] = g.astype(g_ref.dtype)

        g_shapes = [jax.ShapeDtypeStruct(arrays[i].shape,
                                         F32 if ins[i][2] == "acc" else arrays[i].dtype)
                    for i in diff]
        return pl.pallas_call(
            body, name=name + "_bwd", grid=grid,
            in_specs=([pl.BlockSpec(b, im) for (b, im, _) in ins]
                      + [pl.BlockSpec(b, im) for (_, _, b, im) in outs]),
            out_specs=[pl.BlockSpec(ins[i][0], ins[i][1]) for i in diff],
            out_shape=g_shapes,
            compiler_params=_cparams(len(grid)),
        )(*arrays, *cts)

    @jax.custom_vjp
    def op(*arrays):
        return tuple(fwd_call(*arrays))

    def op_fwd(*arrays):
        return tuple(fwd_call(*arrays)), arrays

    def op_bwd(arrays, cts):
        grads = bwd_call(arrays, cts)
        full = [None] * n_in
        for i, g in zip(diff, grads):
            full[i] = g.astype(arrays[i].dtype)
        return tuple(full)

    op.defvjp(op_fwd, op_bwd)
    return op


def _pick(n, cands):
    for c in cands:
        if n % c == 0:
            return c
    return n


def matmul(name, a, b, form, out_dtype):
    if form == "nn":
        (m, k), (_, n) = a.shape, b.shape
    elif form == "nt":
        (m, k), (n, _) = a.shape, b.shape
    else:
        (k, m), (_, n) = a.shape, b.shape
    tm = _pick(m, (1408, 1024, 512, 256, 128, 8))
    tn = _pick(n, (1408, 1024, 512, 384, 256, 128))
    tk = _pick(k, (1024, 1408, 512, 384, 256, 128, 8))
    nk = k // tk
    a_spec = (pl.BlockSpec((tk, tm), lambda i, j, kk: (kk, i)) if form == "tn"
              else pl.BlockSpec((tm, tk), lambda i, j, kk: (i, kk)))
    b_spec = (pl.BlockSpec((tn, tk), lambda i, j, kk: (j, kk)) if form == "nt"
              else pl.BlockSpec((tk, tn), lambda i, j, kk: (kk, j)))
    dims = {"nn": (((1,), (0,)), ((), ())), "nt": (((1,), (1,)), ((), ())),
            "tn": (((0,), (0,)), ((), ()))}[form]

    def body(a_ref, b_ref, o_ref, acc_ref):
        kk = pl.program_id(2)
        part = lax.dot_general(a_ref[...].astype(BF16), b_ref[...].astype(BF16), dims,
                               preferred_element_type=F32)

        @pl.when(kk == 0)
        def _():
            acc_ref[...] = part

        @pl.when(kk > 0)
        def _():
            acc_ref[...] += part

        @pl.when(kk == nk - 1)
        def _():
            o_ref[...] = acc_ref[...].astype(o_ref.dtype)

    return pl.pallas_call(
        body, name=name, grid=(m // tm, n // tn, nk),
        in_specs=[a_spec, b_spec],
        out_specs=pl.BlockSpec((tm, tn), lambda i, j, kk: (i, j)),
        out_shape=jax.ShapeDtypeStruct((m, n), out_dtype),
        scratch_shapes=[pltpu.VMEM((tm, tn), F32)],
        compiler_params=_cparams(3),
    )(a, b)


def linear(name, x, w, out_dtype):
    @jax.custom_vjp
    def op(x, w):
        return matmul(name + "_y", x, w, "nn", out_dtype)

    def op_fwd(x, w):
        return op(x, w), (x, w)

    def op_bwd(res, dy):
        x, w = res
        dx = matmul(name + "_dx", dy, w, "nt", x.dtype)
        dw = matmul(name + "_dw", x, dy, "tn", w.dtype)
        return dx, dw

    op.defvjp(op_fwd, op_bwd)
    return op(x, w)


def _sigmoid(x):
    return 1.0 / (1.0 + jnp.exp(-x))


def _silu(x):
    return x * _sigmoid(x)


def _softplus(x):
    return jnp.maximum(x, 0.0) + jnp.log(1.0 + jnp.exp(-jnp.abs(x)))


def _rms(x):
    return x * lax.rsqrt(jnp.mean(x * x, axis=-1, keepdims=True) + NORM_EPS)


ROW_TILE = 512


def _row(i):
    return (i, 0)


def _fixed(*_):
    return (0, 0)


def norm_mod(name, h, ln, shift, scale):
    s, d = h.shape

    def f(h, ln, sh, sc):
        return ((_rms(h) * ln) * (1.0 + sc) + sh,)

    op = blockwise(name, f, (s // ROW_TILE,),
                   [((ROW_TILE, d), _row, "tile")] + [((1, d), _fixed, "acc")] * 3,
                   [((s, d), BF16, (ROW_TILE, d), _row)])
    return op(h, ln, shift, scale)[0]


FF_TILE = 1408


def _interleave_gate_up(wg, wu):
    parts = []
    for j in range(wg.shape[1] // FF_TILE):
        parts += [wg[:, j * FF_TILE:(j + 1) * FF_TILE], wu[:, j * FF_TILE:(j + 1) * FF_TILE]]
    return jnp.concatenate(parts, axis=1)


def swiglu_act(name, gu):
    s, f2 = gu.shape
    ff = f2 // 2

    def f(gu):
        g, u = gu[:, :FF_TILE].astype(F32), gu[:, FF_TILE:].astype(F32)
        return (_silu(g) * u,)

    op = blockwise(name, f, (s // ROW_TILE, ff // FF_TILE),
                   [((ROW_TILE, 2 * FF_TILE), lambda i, j: (i, j), "tile")],
                   [((s, ff), BF16, (ROW_TILE, FF_TILE), lambda i, j: (i, j))])
    return op(gu)[0]


def residual(name, h, y, gate, weight):
    s, d = h.shape

    def f(h, y, gate):
        return (h + (weight * gate) * y,)

    op = blockwise(name, f, (s // ROW_TILE,),
                   [((ROW_TILE, d), _row, "tile"), ((ROW_TILE, d), _row, "tile"),
                    ((1, d), _fixed, "acc")],
                   [((s, d), F32, (ROW_TILE, d), _row)])
    return op(h, y, gate)[0]


def merge_gates(name, ga, gb, ya, yb):
    s, d = ya.shape

    def f(ga, gb, ya, yb):
        return (_sigmoid(ga) * ya + _sigmoid(gb) * yb,)

    op = blockwise(name, f, (s // ROW_TILE,), [((ROW_TILE, d), _row, "tile")] * 4,
                   [((s, d), BF16, (ROW_TILE, d), _row)])
    return op(ga, gb, ya, yb)[0]


def loss_rows(name, h, g, target):
    s, d = h.shape

    def f(h, g, t):
        err = _rms(h) * g - t
        return (jnp.mean(err * err, axis=-1, keepdims=True),)

    op = blockwise(name, f, (s // ROW_TILE,),
                   [((ROW_TILE, d), _row, "tile"), ((1, d), _fixed, "acc"),
                    ((ROW_TILE, d), _row, "const")],
                   [((s, 1), F32, (ROW_TILE, 1), _row)])
    return op(h, g, target)[0]


def decay_beta(name, ab, a_log_pad, dt_bias_pad):
    s, n = ab.shape

    def f(ab, a_log, dt_bias):
        lane = lax.broadcasted_iota(jnp.int32, ab.shape, 1)
        beta = _sigmoid(ab)
        g = -jnp.exp(a_log) * _softplus(ab + dt_bias)
        return (jnp.where(lane < DN_HEADS, beta, jnp.where(lane < 2 * DN_HEADS, g, 0.0)),)

    op = blockwise(name, f, (s // ROW_TILE,),
                   [((ROW_TILE, n), _row, "tile"), ((1, n), _fixed, "acc"), ((1, n), _fixed, "acc")],
                   [((s, n), F32, (ROW_TILE, n), _row)])
    return op(ab, a_log_pad, dt_bias_pad)[0]


def _shift_rows(x, k):
    n = x.shape[0]

    @jax.custom_vjp
    def shift(x):
        row = lax.broadcasted_iota(jnp.int32, x.shape, 0)
        return jnp.where(row >= k, pltpu.roll(x, k, 0), 0.0)

    def shift_fwd(x):
        return shift(x), None

    def shift_bwd(_, g):
        row = lax.broadcasted_iota(jnp.int32, g.shape, 0)
        return (jnp.where(row < n - k, pltpu.roll(g, n - k, 0), 0.0),)

    shift.defvjp(shift_fwd, shift_bwd)
    return shift(x)


def conv_heads(name, x, w, mode):
    s, width = x.shape
    nh = width // LANES

    def f(x, w):
        y = w[DN_CONV - 1] * x
        for j in range(DN_CONV - 1):
            y = y + w[j] * _shift_rows(x, DN_CONV - 1 - j)
        y = _silu(y)
        if mode != "v":
            y = y * lax.rsqrt(jnp.sum(y * y, axis=-1, keepdims=True) + NORM_EPS)
        if mode == "q":
            y = y * (DN_DIM ** -0.5)
        return (y[None],)

    op = blockwise(name, f, (nh,),
                   [((s, LANES), lambda j: (0, j), "tile"),
                    ((DN_CONV, 1, LANES), lambda j: (0, 0, j), "tile")],
                   [((nh, s, LANES), F32, (1, s, LANES), lambda j: (j, 0, 0))])
    return op(x, w)[0]


def gated_head_norm(name, o, z, w):
    nh, s, dh = o.shape

    def f(o, z, w):
        return (_rms(o[0]) * w * _silu(z),)

    op = blockwise(name, f, (s // ROW_TILE, nh),
                   [((1, ROW_TILE, dh), lambda i, h: (h, i, 0), "tile"),
                    ((ROW_TILE, dh), lambda i, h: (i, h), "tile"),
                    ((1, dh), lambda i, h: (0, 0), "acc")],
                   [((s, nh * dh), BF16, (ROW_TILE, dh), lambda i, h: (i, h))])
    return op(o, z, w)[0]


def _alibi_slopes():
    slopes = 2.0 ** (-ALIBI_MAX_EXP * np.arange(1, DA_HEADS + 1, dtype=np.float64) / DA_HEADS)
    table = np.asarray(DA_DILATIONS, np.float64)[:, None] * slopes[None, :]
    return jnp.asarray(table.astype(np.float32))[:, :, None, None]


def dilated_attention(name, q, k, v):
    npat, nh, s, dh = q.shape
    nblk = s // DA_BLOCK

    def f(q, kp, kc, vp, vc, slope):
        p, j = pl.program_id(0), pl.program_id(1)
        first = (j & ((nblk >> (2 * p)) - 1)) == 0
        qi = lax.broadcasted_iota(jnp.int32, (1, DA_BLOCK, DA_BLOCK), 1)
        ki = lax.broadcasted_iota(jnp.int32, (1, DA_BLOCK, DA_BLOCK), 2)
        steps_prev = (qi + DA_BLOCK - ki).astype(F32)
        steps_cur = (qi - ki).astype(F32)
        slope = slope[0]
        ki_prev = ki - first.astype(jnp.int32) * (2 * DA_BLOCK)
        bias_prev = jnp.where(ki_prev >= qi, -slope * steps_prev, NEG)
        bias_cur = jnp.where(ki <= qi, -slope * steps_cur, NEG)
        sp = _bmm(q[0], kp[0], 2, 2) * (dh ** -0.5) + bias_prev
        sc = _bmm(q[0], kc[0], 2, 2) * (dh ** -0.5) + bias_cur
        mx = jnp.maximum(jnp.max(sp, axis=-1, keepdims=True), jnp.max(sc, axis=-1, keepdims=True))
        pp = jnp.exp(sp - mx)
        pc = jnp.exp(sc - mx)
        l = jnp.sum(pp, axis=-1, keepdims=True) + jnp.sum(pc, axis=-1, keepdims=True)
        o = _bmm(pp / l, vp[0], 2, 1) + _bmm(pc / l, vc[0], 2, 1)
        lse = mx + jnp.log(l)
        return o[None], jnp.broadcast_to(lse, o.shape)[None]

    blk = (1, nh, DA_BLOCK, dh)
    cur_map = lambda p, j: (p, 0, j, 0)
    prev_map = lambda p, j: (p, 0, jnp.maximum(j - 1, 0), 0)
    op = blockwise(
        name, f, (npat, nblk),
        [(blk, cur_map, "tile"), (blk, prev_map, "tile"), (blk, cur_map, "tile"),
         (blk, prev_map, "tile"), (blk, cur_map, "tile"),
         ((1, nh, 1, 1), lambda p, j: (p, 0, 0, 0), "const")],
        [((npat, nh, s, dh), F32, blk, cur_map), ((npat, nh, s, dh), F32, blk, cur_map)])
    slopes = _alibi_slopes()

    @jax.custom_vjp
    def attn(q, k, v):
        return op(q, k, k, v, v, slopes)

    def attn_fwd(q, k, v):
        return attn(q, k, v), (q, k, v)

    def attn_bwd(res, cts):
        q, k, v = res
        _, vjp = jax.vjp(lambda q, kp, kc, vp, vc: op(q, kp, kc, vp, vc, slopes), q, k, k, v, v)
        dq, dkp, dkc, dvp, dvc = vjp(cts)
        fill = lambda t: t.at[:, :, s - DA_BLOCK:].set(0.0)
        return dq, dkc + fill(dkp), dvc + fill(dvp)

    attn.defvjp(attn_fwd, attn_bwd)
    return attn(q, k, v)


def combine_patterns(name, o, lse):
    npat, nh, s, dh = o.shape

    def f(o, lse):
        lse = lse[:, 0]
        mx = jnp.max(lse, axis=0, keepdims=True)
        e = jnp.exp(lse - mx)
        w = e / jnp.sum(e, axis=0, keepdims=True)
        return (jnp.sum(w * o[:, 0], axis=0)[None],)

    blk = (npat, 1, ROW_TILE, dh)
    imap = lambda h, i: (0, h, i, 0)
    op = blockwise(name, f, (nh, s // ROW_TILE), [(blk, imap, "tile"), (blk, imap, "tile")],
                   [((nh, s, dh), F32, (1, ROW_TILE, dh), lambda h, i: (h, i, 0))])
    return op(o, lse)[0]


def _to_sub(t):
    s = t.shape[0]
    t = t.reshape(s, DA_HEADS, DA_DIM)
    outs = []
    for r in DA_DILATIONS:
        outs.append(t.reshape(s // r, r, DA_HEADS, DA_DIM).transpose(2, 1, 0, 3)
                    .reshape(DA_HEADS, s, DA_DIM))
    return jnp.stack(outs)


def _from_sub(t):
    _, nh, s, dh = t.shape
    outs = []
    for p, r in enumerate(DA_DILATIONS):
        outs.append(t[p].reshape(nh, r, s // r, dh).transpose(0, 2, 1, 3).reshape(nh, s, dh))
    return jnp.stack(outs)


def _raw_bmm(a, b, ca, cb):
    return lax.dot_general(a.astype(BF16), b.astype(BF16), (((ca,), (cb,)), ((0,), (0,))),
                           preferred_element_type=F32)


def _split(a):
    hi = a.astype(BF16)
    return hi, (a - hi.astype(F32)).astype(BF16)


def _passes_bmm(a, b, ca, cb, passes):
    if passes == 1:
        return _raw_bmm(a, b, ca, cb)
    (a_hi, a_lo), (b_hi, b_lo) = _split(a), _split(b)
    return _raw_bmm(a_hi, b_hi, ca, cb) + (_raw_bmm(a_hi, b_lo, ca, cb) + _raw_bmm(a_lo, b_hi, ca, cb))


def _bmm(a, b, ca, cb, passes=1):
    fa, fb = 3 - ca, 3 - cb

    @jax.custom_vjp
    def mm(a, b):
        return _passes_bmm(a, b, ca, cb, passes)

    def mm_fwd(a, b):
        return mm(a, b), (a, b)

    def mm_bwd(res, ct):
        a, b = res
        da = (_passes_bmm(ct, b, 2, fb, passes) if ca == 2
              else _passes_bmm(b, ct, fb, 2, passes))
        db = (_passes_bmm(a, ct, fa, 1, passes) if cb == 1
              else _passes_bmm(ct, a, 1, fa, passes))
        return da, db

    mm.defvjp(mm_fwd, mm_bwd)
    return mm(a, b)


def _delta_chunk(q, k, v, gcol, grow, bcol, state):
    c = q.shape[1]
    ii = lax.broadcasted_iota(jnp.int32, (1, c, c), 1)
    jj = lax.broadcasted_iota(jnp.int32, (1, c, c), 2)
    incl, strict = ii >= jj, ii > jj
    gc_col = jnp.sum(jnp.where(incl, grow, 0.0), axis=2, keepdims=True)
    gc_row = jnp.sum(jnp.where(ii <= jj, gcol, 0.0), axis=1, keepdims=True)
    decay = jnp.where(incl, jnp.exp(jnp.where(incl, gc_col - gc_row, 0.0)), 0.0)
    kb, vb = k * bcol, v * bcol
    m = jnp.where(strict, _bmm(kb, k, 2, 2) * decay, 0.0)
    eye = (ii == jj).astype(F32)
    p = -m
    inv = eye + p
    for _ in range(int(math.log2(c)) - 1):
        p = _bmm(p, p, 2, 1, 3)
        inv = inv + _bmm(inv, p, 2, 1, 3)
    e_col = jnp.exp(gc_col)
    u = _bmm(inv, vb, 2, 1)
    w = _bmm(inv, kb * e_col, 2, 1)
    qk = _bmm(q, k, 2, 2) * decay
    v_new = u - _bmm(w, state, 2, 1)
    o = _bmm(q * e_col, state, 2, 1) + _bmm(qk, v_new, 2, 1)
    g_last = jnp.sum(grow, axis=2, keepdims=True)
    new_state = state * jnp.exp(g_last) + _bmm(k * jnp.exp(g_last - gc_col), v_new, 1, 1)
    return o, new_state


def delta_rule(name, q, k, v, gcol, grow, bcol):
    nh, s, dh = q.shape
    c = DN_CHUNK
    n = s // c

    def specs(rev):
        t = (lambda i: n - 1 - i) if rev else (lambda i: i)
        seq = pl.BlockSpec((nh, c, dh), lambda i: (0, t(i), 0))
        col = pl.BlockSpec((nh, c, 1), lambda i: (0, t(i), 0))
        row = pl.BlockSpec((nh, 1, 1, c), lambda i: (0, t(i), 0, 0))
        st = pl.BlockSpec((nh, 1, dh, dh), lambda i: (0, t(i), 0, 0))
        return seq, col, row, st

    def fwd_call(q, k, v, gcol, grow, bcol):
        seq, col, row, st = specs(False)

        def body(q_ref, k_ref, v_ref, gc_ref, gr_ref, b_ref, o_ref, st_ref, state):
            @pl.when(pl.program_id(0) == 0)
            def _():
                state[...] = jnp.zeros_like(state)

            st_ref[:, 0] = state[...]
            o, new_state = _delta_chunk(q_ref[...], k_ref[...], v_ref[...], gc_ref[...],
                                        gr_ref[:, 0], b_ref[...], state[...])
            o_ref[...] = o
            state[...] = new_state

        return pl.pallas_call(
            body, name=name + "_fwd", grid=(n,),
            in_specs=[seq, seq, seq, col, row, col],
            out_specs=[seq, st],
            out_shape=[jax.ShapeDtypeStruct((nh, s, dh), F32),
                       jax.ShapeDtypeStruct((nh, n, dh, dh), F32)],
            scratch_shapes=[pltpu.VMEM((nh, dh, dh), F32)],
            compiler_params=_cparams(1),
        )(q, k, v, gcol, grow, bcol)

    def bwd_call(q, k, v, gcol, grow, bcol, states, do):
        seq, col, row, st = specs(True)

        def body(q_ref, k_ref, v_ref, gc_ref, gr_ref, b_ref, st_ref, do_ref,
                 dq_ref, dk_ref, dv_ref, dgc_ref, dgr_ref, db_ref, dstate):
            @pl.when(pl.program_id(0) == 0)
            def _():
                dstate[...] = jnp.zeros_like(dstate)

            _, vjp = jax.vjp(_delta_chunk, q_ref[...], k_ref[...], v_ref[...], gc_ref[...],
                             gr_ref[:, 0], b_ref[...], st_ref[:, 0])
            dq, dk, dv, dgc, dgr, db, dst = vjp((do_ref[...], dstate[...]))
            dq_ref[...] = dq
            dk_ref[...] = dk
            dv_ref[...] = dv
            dgc_ref[...] = dgc
            dgr_ref[:, 0] = dgr
            db_ref[...] = db
            dstate[...] = dst

        return pl.pallas_call(
            body, name=name + "_bwd", grid=(n,),
            in_specs=[seq, seq, seq, col, row, col, st, seq],
            out_specs=[seq, seq, seq, col, row, col],
            out_shape=[jax.ShapeDtypeStruct((nh, s, dh), F32)] * 3
            + [jax.ShapeDtypeStruct((nh, s, 1), F32), jax.ShapeDtypeStruct((nh, n, 1, c), F32),
               jax.ShapeDtypeStruct((nh, s, 1), F32)],
            scratch_shapes=[pltpu.VMEM((nh, dh, dh), F32)],
            compiler_params=_cparams(1),
        )(q, k, v, gcol, grow, bcol, states, do)

    @jax.custom_vjp
    def op(q, k, v, gcol, grow, bcol):
        return fwd_call(q, k, v, gcol, grow, bcol)[0]

    def op_fwd(q, k, v, gcol, grow, bcol):
        o, states = fwd_call(q, k, v, gcol, grow, bcol)
        return o, (q, k, v, gcol, grow, bcol, states)

    def op_bwd(res, do):
        return tuple(bwd_call(*res, do))

    op.defvjp(op_fwd, op_bwd)
    return op(q, k, v, gcol, grow, bcol)


def _my_place():
    return lax.axis_index("x"), lax.axis_index("y"), lax.axis_index("c")


def all_gather(name, shard):
    return all_gather_many(name, [shard])[0]


def all_gather_many(name, shards):
    n = len(shards)

    def body(*refs):
        x_refs, out_refs = refs[:n], refs[n:2 * n]
        send_sems, recv_sems, local_sems = refs[2 * n:]
        x, y, c = _my_place()
        me, sibling = (x, y, c), (x, y, 1 - c)
        chips = [(1 - x, y), (x, 1 - y), (1 - x, 1 - y)]

        def copy(i, k, block, to, own=False):
            px, py, pc = block
            slot = out_refs[i].at[4 * px + 2 * py + pc]
            return pltpu.make_async_remote_copy(
                src_ref=x_refs[i] if own else slot, dst_ref=slot,
                send_sem=send_sems.at[7 * i + k], recv_sem=recv_sems.at[7 * i + k],
                device_id=to, device_id_type=MESH)

        mine = [pltpu.make_async_copy(x_refs[i], out_refs[i].at[4 * x + 2 * y + c], local_sems.at[i])
                for i in range(n)]
        for cp in mine:
            cp.start()
        first = []
        for i in range(n):
            first.append(copy(i, 0, me, sibling, own=True))
            first += [copy(i, 1 + j, me, (*chip, c), own=True) for j, chip in enumerate(chips)]
        for cp in first:
            cp.start()
        passed = []
        for j, chip in enumerate(chips):
            for i in range(n):
                copy(i, 1 + j, (*chip, c), me).wait_recv()
                passed.append(copy(i, 4 + j, (*chip, c), sibling))
                passed[-1].start()
        for i in range(n):
            copy(i, 0, sibling, me).wait_recv()
        for j, chip in enumerate(chips):
            for i in range(n):
                copy(i, 4 + j, (*chip, 1 - c), me).wait_recv()
        for cp in first + passed:
            cp.wait_send()
        for cp in mine:
            cp.wait()

    any_spec = pl.BlockSpec(memory_space=pl.ANY)
    return pl.pallas_call(
        body, name=name,
        out_shape=[jax.ShapeDtypeStruct((N_DEV,) + t.shape, t.dtype) for t in shards],
        in_specs=[any_spec] * n, out_specs=[any_spec] * n,
        scratch_shapes=[pltpu.SemaphoreType.DMA((7 * n,)), pltpu.SemaphoreType.DMA((7 * n,)),
                        pltpu.SemaphoreType.DMA((n,))],
    )(*shards)


def all_to_all_many(name, parts):
    n = len(parts)

    def body(*refs):
        p_refs, out_refs = refs[:n], refs[n:2 * n]
        send_sems, recv_sems, local_sems = refs[2 * n:]
        x, y, c = _my_place()
        me = 4 * x + 2 * y + c
        flips = [(fx, fy, fc) for fx in (0, 1) for fy in (0, 1) for fc in (0, 1)][1:]
        peers = [((1 - x) if fx else x, (1 - y) if fy else y, (1 - c) if fc else c)
                 for fx, fy, fc in flips]

        def copy(i, k, landing):
            px, py, pc = peers[k]
            idx = 4 * px + 2 * py + pc
            return pltpu.make_async_remote_copy(
                src_ref=p_refs[i].at[idx], dst_ref=out_refs[i].at[idx if landing else me],
                send_sem=send_sems.at[7 * i + k], recv_sem=recv_sems.at[7 * i + k],
                device_id=peers[k], device_id_type=MESH)

        mine = [pltpu.make_async_copy(p_refs[i].at[me], out_refs[i].at[me], local_sems.at[i])
                for i in range(n)]
        for cp in mine:
            cp.start()
        copies = [copy(i, k, False) for k in range(7) for i in range(n)]
        for cp in copies:
            cp.start()
        for k in range(7):
            for i in range(n):
                copy(i, k, True).wait_recv()
        for cp in copies:
            cp.wait_send()
        for cp in mine:
            cp.wait()

    any_spec = pl.BlockSpec(memory_space=pl.ANY)
    return pl.pallas_call(
        body, name=name,
        out_shape=[jax.ShapeDtypeStruct(t.shape, t.dtype) for t in parts],
        in_specs=[any_spec] * n, out_specs=[any_spec] * n,
        scratch_shapes=[pltpu.SemaphoreType.DMA((7 * n,)), pltpu.SemaphoreType.DMA((7 * n,)),
                        pltpu.SemaphoreType.DMA((n,))],
    )(*parts)


ADAMW_BLOCK_BYTES = 3 * 512 * 1024


def adamw(name, grad, w, m, v):
    rows, cols = w.shape
    stacked = grad.ndim == 3
    row_bytes = 4 * LANES * (-(-cols // LANES))
    tile = _pick(rows, [t for t in (512, 352, 256, 128, 64, 32, 16, 8)
                        if t * row_bytes <= ADAMW_BLOCK_BYTES])

    def body(g_ref, w_ref, m_ref, v_ref, go_ref, d_ref, mo_ref, vo_ref):
        if stacked:
            g = g_ref[0].astype(F32)
            for s in range(1, N_DEV):
                g = g + g_ref[s].astype(F32)
        else:
            g = g_ref[...]
        m = ADAM_B1 * m_ref[...] + (1.0 - ADAM_B1) * g
        v = ADAM_B2 * v_ref[...] + (1.0 - ADAM_B2) * jnp.square(g)
        m_hat = m / (1.0 - ADAM_B1 ** ADAM_STEP)
        v_hat = v / (1.0 - ADAM_B2 ** ADAM_STEP)
        go_ref[...] = g
        d_ref[...] = -ADAM_LR * (m_hat / (jnp.sqrt(v_hat) + ADAM_EPS) + ADAM_WD * w_ref[...])
        mo_ref[...] = m
        vo_ref[...] = v

    flat = pl.BlockSpec((tile, cols), lambda i: (i, 0))
    g_spec = pl.BlockSpec((N_DEV, tile, cols), lambda i: (0, i, 0)) if stacked else flat
    return pl.pallas_call(
        body, name=name, grid=(rows // tile,),
        in_specs=[g_spec, flat, flat, flat], out_specs=[flat] * 4,
        out_shape=[jax.ShapeDtypeStruct((rows, cols), F32)] * 4,
        compiler_params=_cparams(1),
    )(grad, w, m, v)


def silu_rows(name, x):
    def body(x_ref, o_ref):
        o_ref[...] = _silu(x_ref[...])

    return pl.pallas_call(body, name=name, out_shape=jax.ShapeDtypeStruct(x.shape, F32))(x)


BIG = (("ffn1_wg", (D, D_FF // N_DEV), 1), ("ffn1_wu", (D, D_FF // N_DEV), 1),
       ("ffn1_wd", (D_FF // N_DEV, D), 0), ("w_in", (D, IN_COLS // N_DEV), 1),
       ("conv_w", (DN_CONV, 3 * D // N_DEV), 1), ("w_a", (D // N_DEV, D), 0),
       ("w_b", (DA_HEADS * DA_DIM, D // N_DEV), 1), ("w_o", (D // N_DEV, D), 0),
       ("ffn2_wg", (D, D_FF // N_DEV), 1), ("ffn2_wu", (D, D_FF // N_DEV), 1),
       ("ffn2_wd", (D_FF // N_DEV, D), 0))
SMALL = (("ada_b", (DEPTH, N_ADA * D)), ("ln_ffn1", (DEPTH, D)), ("ln_mix", (DEPTH, D)),
         ("ln_ffn2", (DEPTH, D)), ("a_log", (DEPTH, DN_HEADS)), ("dt_bias", (DEPTH, DN_HEADS)),
         ("dn_norm", (DEPTH, DN_DIM)), ("final_norm", (D,)))
SMALL_ROWS = 32


def _pack(arrays, rows):
    flat = jnp.concatenate([a.reshape(-1) for a in arrays])
    return jnp.pad(flat, (0, rows * D - flat.shape[0])).reshape(rows, D)


def _unpack(buf, shapes):
    flat = buf.reshape(-1)
    out, off = [], 0
    for shp in shapes:
        n = int(np.prod(shp))
        out.append(flat[off:off + n].reshape(shp))
        off += n
    return out


def _full_weights(gathered):
    full = {}
    for (name, (a, b), axis), t in zip(BIG, gathered):
        if axis == 1:
            full[name] = t.transpose(1, 2, 0, 3).reshape(DEPTH, a, N_DEV * b)
        else:
            full[name] = t.transpose(1, 0, 2, 3).reshape(DEPTH, N_DEV * a, b)
    return full


_Z0, _B0, _A0, _DQ0, _GA0 = 3072, 4096, 4104, 4112, 6416


def _reorder_in_proj(w):
    pad = jnp.zeros((w.shape[0], IN_COLS_PAD - IN_COLS), w.dtype)
    return jnp.concatenate([w[:, :_B0], w[:, _DQ0:], w[:, _B0:_DQ0], pad], axis=1)


def _split_cols(x, bounds):
    @jax.custom_vjp
    def split(x):
        return tuple(x[:, a:b] for a, b in zip(bounds[:-1], bounds[1:]))

    def split_fwd(x):
        return split(x), None

    def split_bwd(_, cts):
        return (jnp.concatenate(cts, axis=1),)

    split.defvjp(split_fwd, split_bwd)
    return split(x)


def _ffn(tag, h, ln, shift, scale, gate, w_gu, w_d):
    n = norm_mod(tag + "_norm", h, ln, shift, scale)
    gu = linear(tag + "_gu", n, w_gu, BF16)
    a = swiglu_act(tag + "_act", gu)
    f = linear(tag + "_down", a, w_d, F32)
    return residual(tag + "_res", h, f, gate, 0.5)


def _mixer(tag, u, w_in, conv_w, a_log, dt_bias, dn_norm, w_a, w_b, w_o):
    s = u.shape[0]
    proj = linear(tag + "_in", u, w_in, F32)
    (q_pre, k_pre, v_pre, z, da_q, da_k, da_v, gate_a, gate_b, ab, _) = _split_cols(
        proj, (0, 1024, 2048, 3072, 4096, 4864, 5632, 6400, 7424, 8448, 8576, IN_COLS_PAD))
    qkv_pre, da = (q_pre, k_pre, v_pre), (da_q, da_k, da_v)

    cw = conv_w.astype(F32).reshape(DN_CONV, 1, 3 * D)
    q, k, v = [conv_heads(f"{tag}_conv_{m}", qkv_pre[i], cw[:, :, i * D:(i + 1) * D], m)
               for i, m in enumerate("qkv")]
    pad = lambda t: jnp.pad(t, (DN_HEADS, LANES - 2 * DN_HEADS))[None]
    gb = decay_beta(tag + "_decay", ab, pad(a_log), pad(dt_bias))
    beta = gb[:, :DN_HEADS].T[:, :, None]
    g = gb[:, DN_HEADS:2 * DN_HEADS].T
    o = delta_rule(tag + "_delta", q, k, v, g[:, :, None],
                   g.reshape(DN_HEADS, s // DN_CHUNK, 1, DN_CHUNK), beta)
    o_a = gated_head_norm(tag + "_gnorm", o, z, dn_norm[None])
    y_a = linear(tag + "_wa", o_a, w_a, F32)

    o_s, lse_s = dilated_attention(tag + "_attn", *[_to_sub(t) for t in da])
    o_b = combine_patterns(tag + "_comb", _from_sub(o_s), _from_sub(lse_s))
    o_b = o_b.transpose(1, 0, 2).reshape(s, DA_HEADS * DA_DIM).astype(BF16)
    y_b = linear(tag + "_wb", o_b, w_b, F32)

    merged = merge_gates(tag + "_merge", gate_a, gate_b, y_a, y_b)
    return linear(tag + "_wo", merged, w_o, F32)


def _local_loss(x, gathered, small, mod, target):
    w = _full_weights(gathered)
    h = x
    for l in range(DEPTH):
        tag = f"l{l}"
        sh1, sc1, gt1, sh2, sc2, gt2, sh3, sc3, gt3 = [mod[l, i * D:(i + 1) * D][None]
                                                       for i in range(N_ADA)]
        w_gu1 = _interleave_gate_up(w["ffn1_wg"][l], w["ffn1_wu"][l])
        w_gu2 = _interleave_gate_up(w["ffn2_wg"][l], w["ffn2_wu"][l])
        h = _ffn(tag + "_ffn1", h, small["ln_ffn1"][l][None], sh1, sc1, gt1, w_gu1, w["ffn1_wd"][l])
        u = norm_mod(tag + "_mixnorm", h, small["ln_mix"][l][None], sh2, sc2)
        m = _mixer(tag + "_mix", u, _reorder_in_proj(w["w_in"][l]), w["conv_w"][l],
                   small["a_log"][l], small["dt_bias"][l], small["dn_norm"][l],
                   w["w_a"][l], w["w_b"][l], w["w_o"][l])
        h = residual(tag + "_mixres", h, m, gt2, 1.0)
        h = _ffn(tag + "_ffn2", h, small["ln_ffn2"][l][None], sh3, sc3, gt3, w_gu2, w["ffn2_wd"][l])
    rows = loss_rows("loss", h, small["final_norm"][None], target)
    return 0.5 * jnp.sum(rows)


def kernel(x, c, ada_w, ada_b, ln_ffn1, ln_mix, ln_ffn2, ffn1_wg, ffn1_wu, ffn1_wd, w_in, conv_w, a_log, dt_bias, dn_norm, w_a, w_b, w_o, ffn2_wg, ffn2_wu, ffn2_wd, final_norm, loss_target, m_ada_w, m_ada_b, m_ln_ffn1, m_ln_mix, m_ln_ffn2, m_ffn1_wg, m_ffn1_wu, m_ffn1_wd, m_w_in, m_conv_w, m_a_log, m_dt_bias, m_dn_norm, m_w_a, m_w_b, m_w_o, m_ffn2_wg, m_ffn2_wu, m_ffn2_wd, m_final_norm, v_ada_w, v_ada_b, v_ln_ffn1, v_ln_mix, v_ln_ffn2, v_ffn1_wg, v_ffn1_wu, v_ffn1_wd, v_w_in, v_conv_w, v_a_log, v_dt_bias, v_dn_norm, v_w_a, v_w_b, v_w_o, v_ffn2_wg, v_ffn2_wu, v_ffn2_wd, v_final_norm):
    args = dict(locals())
    big_names = [n for n, _, _ in BIG]
    small_names = [n for n, _ in SMALL]
    me = 4 * lax.axis_index("x") + 2 * lax.axis_index("y") + lax.axis_index("c")
    cols = N_ADA * D // N_DEV

    c_all = all_gather("gather_c", jnp.pad(silu_rows("silu_c", c), ((0, 7), (0, 0))))[:, 0]
    mod_cols = jnp.stack([matmul(f"ada{l}", c_all, ada_w[l], "nn", F32) for l in range(DEPTH)])
    mod_cols = mod_cols + lax.dynamic_slice_in_dim(ada_b, me * cols, cols, axis=1)[:, None, :]
    mod_all = all_gather("gather_mod", mod_cols.reshape(DEPTH * N_DEV, cols))
    mod_all = mod_all.reshape(N_DEV, DEPTH, N_DEV, cols)
    mod = lax.dynamic_index_in_dim(mod_all, me, axis=2, keepdims=False)
    mod = mod.transpose(1, 0, 2).reshape(DEPTH, N_ADA * D)

    gathered = all_gather_many("gather_w", [args[n].astype(BF16) for n in big_names])

    small = {n: args[n] for n in small_names if n != "ada_b"}
    loss, (dx, dgathered, dsmall, dmod) = jax.value_and_grad(_local_loss, argnums=(0, 1, 2, 3))(
        x[0], gathered, small, mod, loss_target[0])

    part = _pack([dmod] + [dsmall[n] for n in small_names[1:]], SMALL_ROWS)
    parts = all_gather("gather_small", part)
    sm_out = adamw("adamw_small", parts, _pack([args[n] for n in small_names], SMALL_ROWS),
                   _pack([args["m_" + n] for n in small_names], SMALL_ROWS),
                   _pack([args["v_" + n] for n in small_names], SMALL_ROWS))

    dmod_all = parts.reshape(N_DEV, -1)[:, :DEPTH * N_ADA * D].reshape(N_DEV, DEPTH, N_ADA * D)
    dmod_mine = lax.dynamic_slice_in_dim(dmod_all, me * cols, cols, axis=2)
    g_ada = jnp.stack([matmul(f"ada{l}_dw", c_all, dmod_mine[:, l], "tn", F32) for l in range(DEPTH)])
    flat2 = lambda t: t.reshape(-1, t.shape[-1])
    ada_out = adamw("adamw_ada_w", flat2(g_ada), flat2(ada_w), flat2(m_ada_w), flat2(v_ada_w))

    landed = all_to_all_many("scatter_grads", dgathered)
    big_out = {}
    for n, t in zip(big_names, landed):
        big_out[n] = adamw("adamw_" + n, t.reshape(N_DEV, -1, t.shape[-1]), flat2(args[n]),
                           flat2(args["m_" + n]), flat2(args["v_" + n]))

    small_shapes = [shp for _, shp in SMALL]
    names = ["ada_w", "ada_b", "ln_ffn1", "ln_mix", "ln_ffn2", "ffn1_wg", "ffn1_wu", "ffn1_wd", "w_in",
             "conv_w", "a_log", "dt_bias", "dn_norm", "w_a", "w_b", "w_o", "ffn2_wg", "ffn2_wu",
             "ffn2_wd", "final_norm"]
    outs = [lax.psum(loss, ("x", "y", "c")), dx[None]]
    for kind in range(4):
        table = {n: big_out[n][kind].reshape(args[n].shape) for n in big_names}
        table.update(zip(small_names, _unpack(sm_out[kind], small_shapes)))
        table["ada_w"] = ada_out[kind].reshape(ada_w.shape)
        outs += [table[n] for n in names]
    return tuple(outs)
```

```python
import functools
import math

import numpy as np
import jax
import jax.numpy as jnp
from jax import lax
from jax.experimental import pallas as pl
from jax.experimental.pallas import tpu as pltpu

F32 = jnp.float32
BF16 = jnp.bfloat16

D = 1024
SEQ = 4096
DEPTH = 2
N_DEV = 8
DN_HEADS = 8
DN_DIM = 128
DN_CHUNK = 64
DN_CONV = 4
DA_HEADS = 12
DA_DIM = 64
DA_BLOCK = 128
DA_DILATIONS = (1, 4, 16)
ALIBI_MAX_EXP = 8.0
D_FF = 2816
N_ADA = 9
NORM_EPS = 1e-6
IN_COLS = 8464
IN_COLS_PAD = 8704
ADAM_LR, ADAM_B1, ADAM_B2, ADAM_EPS, ADAM_WD, ADAM_STEP = 0.001, 0.9, 0.999, 1e-08, 0.01, 10
NEG = -1e30

VMEM_LIMIT = 56 * 1024 * 1024
LANES = 128

MESH = pl.DeviceIdType.MESH


def _cparams(n_grid):
    return pltpu.CompilerParams(dimension_semantics=("arbitrary",) * n_grid,
                                vmem_limit_bytes=VMEM_LIMIT)


def blockwise(name, f, grid, ins, outs):
    n_in, n_out = len(ins), len(outs)
    diff = [i for i, (_, _, kind) in enumerate(ins) if kind != "const"]

    def apply(*vals):
        res = f(*vals)
        return tuple(r.astype(dt) for r, (_, dt, _, _) in zip(res, outs))

    def fwd_call(*arrays):
        def body(*refs):
            res = apply(*[r[...] for r in refs[:n_in]])
            for r, v in zip(refs[n_in:], res):
                r[...] = v

        return pl.pallas_call(
            body, name=name + "_fwd", grid=grid,
            in_specs=[pl.BlockSpec(b, im) for (b, im, _) in ins],
            out_specs=[pl.BlockSpec(b, im) for (_, _, b, im) in outs],
            out_shape=[jax.ShapeDtypeStruct(s, dt) for (s, dt, _, _) in outs],
            compiler_params=_cparams(len(grid)),
        )(*arrays)

    def bwd_call(arrays, cts):
        def body(*refs):
            in_refs, ct_refs = refs[:n_in], refs[n_in:n_in + n_out]
            g_refs = refs[n_in + n_out:]
            vals = [r[...] for r in in_refs]

            def fd(*dvals):
                full = list(vals)
                for i, v in zip(diff, dvals):
                    full[i] = v
                return apply(*full)

            _, vjp = jax.vjp(fd, *[vals[i] for i in diff])
            grads = vjp(tuple(r[...] for r in ct_refs))
            first = functools.reduce(jnp.logical_and,
                                     [pl.program_id(a) == 0 for a in range(len(grid))])
            for g_ref, g, i in zip(g_refs, grads, diff):
                if ins[i][2] == "acc":
                    @pl.when(first)
                    def _(g_ref=g_ref):
                        g_ref[...] = jnp.zeros_like(g_ref)
                    g_ref[...] += g.astype(F32)
                else:
                    g_ref[...] = g.astype(g_ref.dtype)

        g_shapes = [jax.ShapeDtypeStruct(arrays[i].shape,
                                         F32 if ins[i][2] == "acc" else arrays[i].dtype)
                    for i in diff]
        return pl.pallas_call(
            body, name=name + "_bwd", grid=grid,
            in_specs=([pl.BlockSpec(b, im) for (b, im, _) in ins]
                      + [pl.BlockSpec(b, im) for (_, _, b, im) in outs]),
            out_specs=[pl.BlockSpec(ins[i][0], ins[i][1]) for i in diff],
            out_shape=g_shapes,
            compiler_params=_cparams(len(grid)),
        )(*arrays, *cts)

    @jax.custom_vjp
    def op(*arrays):
        return tuple(fwd_call(*arrays))

    def op_fwd(*arrays):
        return tuple(fwd_call(*arrays)), arrays

    def op_bwd(arrays, cts):
        grads = bwd_call(arrays, cts)
        full = [None] * n_in
        for i, g in zip(diff, grads):
            full[i] = g.astype(arrays[i].dtype)
        return tuple(full)

    op.defvjp(op_fwd, op_bwd)
    return op


def _pick(n, cands):
    for c in cands:
        if n % c == 0:
            return c
    return n


def matmul(name, a, b, form, out_dtype):
    if form == "nn":
        (m, k), (_, n) = a.shape, b.shape
    elif form == "nt":
        (m, k), (n, _) = a.shape, b.shape
    else:
        (k, m), (_, n) = a.shape, b.shape
    tm = _pick(m, (1408, 1024, 512, 256, 128, 8))
    tn = _pick(n, (1408, 1024, 512, 384, 256, 128))
    tk = _pick(k, (1024, 1408, 512, 384, 256, 128, 8))
    nk = k // tk
    a_spec = (pl.BlockSpec((tk, tm), lambda i, j, kk: (kk, i)) if form == "tn"
              else pl.BlockSpec((tm, tk), lambda i, j, kk: (i, kk)))
    b_spec = (pl.BlockSpec((tn, tk), lambda i, j, kk: (j, kk)) if form == "nt"
              else pl.BlockSpec((tk, tn), lambda i, j, kk: (kk, j)))
    dims = {"nn": (((1,), (0,)), ((), ())), "nt": (((1,), (1,)), ((), ())),
            "tn": (((0,), (0,)), ((), ()))}[form]

    def body(a_ref, b_ref, o_ref, acc_ref):
        kk = pl.program_id(2)
        part = lax.dot_general(a_ref[...].astype(BF16), b_ref[...].astype(BF16), dims,
                               preferred_element_type=F32)

        @pl.when(kk == 0)
        def _():
            acc_ref[...] = part

        @pl.when(kk > 0)
        def _():
            acc_ref[...] += part

        @pl.when(kk == nk - 1)
        def _():
            o_ref[...] = acc_ref[...].astype(o_ref.dtype)

    return pl.pallas_call(
        body, name=name, grid=(m // tm, n // tn, nk),
        in_specs=[a_spec, b_spec],
        out_specs=pl.BlockSpec((tm, tn), lambda i, j, kk: (i, j)),
        out_shape=jax.ShapeDtypeStruct((m, n), out_dtype),
        scratch_shapes=[pltpu.VMEM((tm, tn), F32)],
        compiler_params=_cparams(3),
    )(a, b)


def linear(name, x, w, out_dtype):
    @jax.custom_vjp
    def op(x, w):
        return matmul(name + "_y", x, w, "nn", out_dtype)

    def op_fwd(x, w):
        return op(x, w), (x, w)

    def op_bwd(res, dy):
        x, w = res
        dx = matmul(name + "_dx", dy, w, "nt", x.dtype)
        dw = matmul(name + "_dw", x, dy, "tn", w.dtype)
        return dx, dw

    op.defvjp(op_fwd, op_bwd)
    return op(x, w)


def _sigmoid(x):
    return 1.0 / (1.0 + jnp.exp(-x))


def _silu(x):
    return x * _sigmoid(x)


def _softplus(x):
    return jnp.maximum(x, 0.0) + jnp.log(1.0 + jnp.exp(-jnp.abs(x)))


def _rms(x):
    return x * lax.rsqrt(jnp.mean(x * x, axis=-1, keepdims=True) + NORM_EPS)


ROW_TILE = 512


def _row(i):
    return (i, 0)


def _fixed(*_):
    return (0, 0)


def norm_mod(name, h, ln, shift, scale):
    s, d = h.shape

    def f(h, ln, sh, sc):
        return ((_rms(h) * ln) * (1.0 + sc) + sh,)

    op = blockwise(name, f, (s // ROW_TILE,),
                   [((ROW_TILE, d), _row, "tile")] + [((1, d), _fixed, "acc")] * 3,
                   [((s, d), BF16, (ROW_TILE, d), _row)])
    return op(h, ln, shift, scale)[0]


FF_TILE = 1408


def _interleave_gate_up(wg, wu):
    parts = []
    for j in range(wg.shape[1] // FF_TILE):
        parts += [wg[:, j * FF_TILE:(j + 1) * FF_TILE], wu[:, j * FF_TILE:(j + 1) * FF_TILE]]
    return jnp.concatenate(parts, axis=1)


def swiglu_act(name, gu):
    s, f2 = gu.shape
    ff = f2 // 2

    def f(gu):
        g, u = gu[:, :FF_TILE].astype(F32), gu[:, FF_TILE:].astype(F32)
        return (_silu(g) * u,)

    op = blockwise(name, f, (s // ROW_TILE, ff // FF_TILE),
                   [((ROW_TILE, 2 * FF_TILE), lambda i, j: (i, j), "tile")],
                   [((s, ff), BF16, (ROW_TILE, FF_TILE), lambda i, j: (i, j))])
    return op(gu)[0]


def residual(name, h, y, gate, weight):
    s, d = h.shape

    def f(h, y, gate):
        return (h + (weight * gate) * y,)

    op = blockwise(name, f, (s // ROW_TILE,),
                   [((ROW_TILE, d), _row, "tile"), ((ROW_TILE, d), _row, "tile"),
                    ((1, d), _fixed, "acc")],
                   [((s, d), F32, (ROW_TILE, d), _row)])
    return op(h, y, gate)[0]


def merge_gates(name, ga, gb, ya, yb):
    s, d = ya.shape

    def f(ga, gb, ya, yb):
        return (_sigmoid(ga) * ya + _sigmoid(gb) * yb,)

    op = blockwise(name, f, (s // ROW_TILE,), [((ROW_TILE, d), _row, "tile")] * 4,
                   [((s, d), BF16, (ROW_TILE, d), _row)])
    return op(ga, gb, ya, yb)[0]


def loss_rows(name, h, g, target):
    s, d = h.shape

    def f(h, g, t):
        err = _rms(h) * g - t
        return (jnp.mean(err * err, axis=-1, keepdims=True),)

    op = blockwise(name, f, (s // ROW_TILE,),
                   [((ROW_TILE, d), _row, "tile"), ((1, d), _fixed, "acc"),
                    ((ROW_TILE, d), _row, "const")],
                   [((s, 1), F32, (ROW_TILE, 1), _row)])
    return op(h, g, target)[0]


def decay_beta(name, ab, a_log_pad, dt_bias_pad):
    s, n = ab.shape

    def f(ab, a_log, dt_bias):
        lane = lax.broadcasted_iota(jnp.int32, ab.shape, 1)
        beta = _sigmoid(ab)
        g = -jnp.exp(a_log) * _softplus(ab + dt_bias)
        return (jnp.where(lane < DN_HEADS, beta, jnp.where(lane < 2 * DN_HEADS, g, 0.0)),)

    op = blockwise(name, f, (s // ROW_TILE,),
                   [((ROW_TILE, n), _row, "tile"), ((1, n), _fixed, "acc"), ((1, n), _fixed, "acc")],
                   [((s, n), F32, (ROW_TILE, n), _row)])
    return op(ab, a_log_pad, dt_bias_pad)[0]


def _shift_rows(x, k):
    n = x.shape[0]

    @jax.custom_vjp
    def shift(x):
        row = lax.broadcasted_iota(jnp.int32, x.shape, 0)
        return jnp.where(row >= k, pltpu.roll(x, k, 0), 0.0)

    def shift_fwd(x):
        return shift(x), None

    def shift_bwd(_, g):
        row = lax.broadcasted_iota(jnp.int32, g.shape, 0)
        return (jnp.where(row < n - k, pltpu.roll(g, n - k, 0), 0.0),)

    shift.defvjp(shift_fwd, shift_bwd)
    return shift(x)


def conv_heads(name, x, w, mode):
    s, width = x.shape
    nh = width // LANES

    def f(x, w):
        y = w[DN_CONV - 1] * x
        for j in range(DN_CONV - 1):
            y = y + w[j] * _shift_rows(x, DN_CONV - 1 - j)
        y = _silu(y)
        if mode != "v":
            y = y * lax.rsqrt(jnp.sum(y * y, axis=-1, keepdims=True) + NORM_EPS)
        if mode == "q":
            y = y * (DN_DIM ** -0.5)
        return (y[None],)

    op = blockwise(name, f, (nh,),
                   [((s, LANES), lambda j: (0, j), "tile"),
                    ((DN_CONV, 1, LANES), lambda j: (0, 0, j), "tile")],
                   [((nh, s, LANES), F32, (1, s, LANES), lambda j: (j, 0, 0))])
    return op(x, w)[0]


def gated_head_norm(name, o, z, w):
    nh, s, dh = o.shape

    def f(o, z, w):
        return (_rms(o[0]) * w * _silu(z),)

    op = blockwise(name, f, (s // ROW_TILE, nh),
                   [((1, ROW_TILE, dh), lambda i, h: (h, i, 0), "tile"),
                    ((ROW_TILE, dh), lambda i, h: (i, h), "tile"),
                    ((1, dh), lambda i, h: (0, 0), "acc")],
                   [((s, nh * dh), BF16, (ROW_TILE, dh), lambda i, h: (i, h))])
    return op(o, z, w)[0]


def _mm(a, b, ca, cb):
    return _bmm(a[None], b[None], ca + 1, cb + 1)[0]


def dilated_attention(name, q, k, v, dilation):
    s, width = q.shape
    nblk = s // DA_BLOCK
    per_sub = nblk // dilation
    slopes = [dilation * 2.0 ** (-ALIBI_MAX_EXP * (h + 1) / DA_HEADS) for h in range(DA_HEADS)]

    def f(q, kp, kc, vp, vc):
        first = (pl.program_id(0) % per_sub) == 0
        qi = lax.broadcasted_iota(jnp.int32, (DA_BLOCK, DA_BLOCK), 0)
        ki = lax.broadcasted_iota(jnp.int32, (DA_BLOCK, DA_BLOCK), 1)
        steps_prev = (qi + DA_BLOCK - ki).astype(F32)
        steps_cur = (qi - ki).astype(F32)
        valid_prev = (ki - first.astype(jnp.int32) * (2 * DA_BLOCK)) >= qi
        valid_cur = ki <= qi
        lane_head = lax.broadcasted_iota(jnp.int32, (DA_BLOCK, LANES), 1) // DA_DIM
        o_parts, lse_parts = [], []
        for pair in range(width // LANES):
            cols = slice(pair * LANES, (pair + 1) * LANES)
            o_pair = jnp.zeros((DA_BLOCK, LANES), F32)
            lse_pair = jnp.zeros((DA_BLOCK, LANES), F32)
            for half in range(LANES // DA_DIM):
                slope = slopes[pair * (LANES // DA_DIM) + half]
                mine = lane_head == half
                qh = jnp.where(mine, q[:, cols], 0.0)
                sp = _mm(qh, kp[:, cols], 1, 1) * (DA_DIM ** -0.5) + jnp.where(
                    valid_prev, -slope * steps_prev, NEG)
                sc = _mm(qh, kc[:, cols], 1, 1) * (DA_DIM ** -0.5) + jnp.where(
                    valid_cur, -slope * steps_cur, NEG)
                mx = jnp.maximum(jnp.max(sp, axis=-1, keepdims=True),
                                 jnp.max(sc, axis=-1, keepdims=True))
                pp = jnp.exp(sp - mx)
                pc = jnp.exp(sc - mx)
                l = jnp.sum(pp, axis=-1, keepdims=True) + jnp.sum(pc, axis=-1, keepdims=True)
                o = _mm(pp / l, vp[:, cols], 1, 0) + _mm(pc / l, vc[:, cols], 1, 0)
                o_pair = o_pair + jnp.where(mine, o, 0.0)
                lse_pair = lse_pair + jnp.where(mine, mx + jnp.log(l), 0.0)
            o_parts.append(o_pair)
            lse_parts.append(lse_pair)
        return jnp.concatenate(o_parts, axis=1), jnp.concatenate(lse_parts, axis=1)

    blk = (DA_BLOCK, width)
    cur_map = lambda j: (j, 0)
    prev_map = lambda j: (jnp.maximum(j - 1, 0), 0)
    op = blockwise(
        name, f, (nblk,),
        [(blk, cur_map, "tile"), (blk, prev_map, "tile"), (blk, cur_map, "tile"),
         (blk, prev_map, "tile"), (blk, cur_map, "tile")],
        [((s, width), F32, blk, cur_map), ((s, width), F32, blk, cur_map)])

    @jax.custom_vjp
    def attn(q, k, v):
        return op(q, k, k, v, v)

    def attn_fwd(q, k, v):
        return attn(q, k, v), (q, k, v)

    def attn_bwd(res, cts):
        q, k, v = res
        _, vjp = jax.vjp(op, q, k, k, v, v)
        dq, dkp, dkc, dvp, dvc = vjp(cts)
        fill = lambda t: t.at[s - DA_BLOCK:].set(0.0)
        return dq, dkc + fill(dkp), dvc + fill(dvp)

    attn.defvjp(attn_fwd, attn_bwd)
    return attn(q, k, v)


def combine_patterns(name, outs, lses):
    s, width = outs[0].shape
    n = len(outs)

    def f(*vals):
        o, lse = vals[:n], vals[n:]
        mx = functools.reduce(jnp.maximum, lse)
        e = [jnp.exp(t - mx) for t in lse]
        return (sum(ei * oi for ei, oi in zip(e, o)) / sum(e),)

    tile = ROW_TILE // 2
    op = blockwise(name, f, (s // tile,), [((tile, width), _row, "tile")] * (2 * n),
                   [((s, width), BF16, (tile, width), _row)])
    return op(*outs, *lses)[0]


def _sub_order(t, r):
    if r == 1:
        return t
    s, c = t.shape
    return t.reshape(s // r, r, c).transpose(1, 0, 2).reshape(s, c)


def _seq_order(t, r):
    if r == 1:
        return t
    s, c = t.shape
    return t.reshape(r, s // r, c).transpose(1, 0, 2).reshape(s, c)


def _raw_bmm(a, b, ca, cb):
    return lax.dot_general(a.astype(BF16), b.astype(BF16), (((ca,), (cb,)), ((0,), (0,))),
                           preferred_element_type=F32)


def _split(a):
    hi = a.astype(BF16)
    return hi, (a - hi.astype(F32)).astype(BF16)


def _passes_bmm(a, b, ca, cb, passes):
    if passes == 1:
        return _raw_bmm(a, b, ca, cb)
    (a_hi, a_lo), (b_hi, b_lo) = _split(a), _split(b)
    return _raw_bmm(a_hi, b_hi, ca, cb) + (_raw_bmm(a_hi, b_lo, ca, cb) + _raw_bmm(a_lo, b_hi, ca, cb))


def _bmm(a, b, ca, cb, passes=1):
    fa, fb = 3 - ca, 3 - cb

    @jax.custom_vjp
    def mm(a, b):
        return _passes_bmm(a, b, ca, cb, passes)

    def mm_fwd(a, b):
        return mm(a, b), (a, b)

    def mm_bwd(res, ct):
        a, b = res
        da = (_passes_bmm(ct, b, 2, fb, passes) if ca == 2
              else _passes_bmm(b, ct, fb, 2, passes))
        db = (_passes_bmm(a, ct, fa, 1, passes) if cb == 1
              else _passes_bmm(ct, a, 1, fa, passes))
        return da, db

    mm.defvjp(mm_fwd, mm_bwd)
    return mm(a, b)


def _delta_chunk(q, k, v, gcol, grow, bcol, state):
    c = q.shape[1]
    ii = lax.broadcasted_iota(jnp.int32, (1, c, c), 1)
    jj = lax.broadcasted_iota(jnp.int32, (1, c, c), 2)
    incl, strict = ii >= jj, ii > jj
    gc_col = jnp.sum(jnp.where(incl, grow, 0.0), axis=2, keepdims=True)
    gc_row = jnp.sum(jnp.where(ii <= jj, gcol, 0.0), axis=1, keepdims=True)
    decay = jnp.where(incl, jnp.exp(jnp.where(incl, gc_col - gc_row, 0.0)), 0.0)
    kb, vb = k * bcol, v * bcol
    m = jnp.where(strict, _bmm(kb, k, 2, 2) * decay, 0.0)
    eye = (ii == jj).astype(F32)
    p = -m
    inv = eye + p
    for _ in range(int(math.log2(c)) - 1):
        p = _bmm(p, p, 2, 1, 3)
        inv = inv + _bmm(inv, p, 2, 1, 3)
    e_col = jnp.exp(gc_col)
    u = _bmm(inv, vb, 2, 1)
    w = _bmm(inv, kb * e_col, 2, 1)
    qk = _bmm(q, k, 2, 2) * decay
    v_new = u - _bmm(w, state, 2, 1)
    o = _bmm(q * e_col, state, 2, 1) + _bmm(qk, v_new, 2, 1)
    g_last = jnp.sum(grow, axis=2, keepdims=True)
    new_state = state * jnp.exp(g_last) + _bmm(k * jnp.exp(g_last - gc_col), v_new, 1, 1)
    return o, new_state


def _delta_chunk_packed(q, k, v, gb, state):
    nh, c = q.shape[0], q.shape[1]
    lane = lax.broadcasted_iota(jnp.int32, gb.shape, 1)
    eye = (lax.broadcasted_iota(jnp.int32, (c, c), 0) == lax.broadcasted_iota(jnp.int32, (c, c), 1))
    column = lambda l: jnp.sum(jnp.where(lane == l, gb, 0.0), axis=1, keepdims=True)
    bcol = jnp.stack([column(h) for h in range(nh)])
    gcols = [column(nh + h) for h in range(nh)]
    gcol = jnp.stack(gcols)
    grow = jnp.stack([jnp.sum(jnp.where(eye, g, 0.0), axis=0, keepdims=True) for g in gcols])
    return _delta_chunk(q, k, v, gcol, grow, bcol, state)


def delta_rule(name, q, k, v, gb):
    nh, s, dh = q.shape
    c = DN_CHUNK
    n = s // c

    def specs(rev):
        t = (lambda i: n - 1 - i) if rev else (lambda i: i)
        seq = pl.BlockSpec((nh, c, dh), lambda i: (0, t(i), 0))
        gate = pl.BlockSpec((c, LANES), lambda i: (t(i), 0))
        st = pl.BlockSpec((nh, 1, dh, dh), lambda i: (0, t(i), 0, 0))
        return seq, gate, st

    def fwd_call(q, k, v, gb):
        seq, gate, st = specs(False)

        def body(q_ref, k_ref, v_ref, gb_ref, o_ref, st_ref, state):
            @pl.when(pl.program_id(0) == 0)
            def _():
                state[...] = jnp.zeros_like(state)

            st_ref[:, 0] = state[...]
            o, new_state = _delta_chunk_packed(q_ref[...], k_ref[...], v_ref[...], gb_ref[...],
                                               state[...])
            o_ref[...] = o
            state[...] = new_state

        return pl.pallas_call(
            body, name=name + "_fwd", grid=(n,),
            in_specs=[seq, seq, seq, gate],
            out_specs=[seq, st],
            out_shape=[jax.ShapeDtypeStruct((nh, s, dh), F32),
                       jax.ShapeDtypeStruct((nh, n, dh, dh), F32)],
            scratch_shapes=[pltpu.VMEM((nh, dh, dh), F32)],
            compiler_params=_cparams(1),
        )(q, k, v, gb)

    def bwd_call(q, k, v, gb, states, do):
        seq, gate, st = specs(True)

        def body(q_ref, k_ref, v_ref, gb_ref, st_ref, do_ref,
                 dq_ref, dk_ref, dv_ref, dgb_ref, dstate):
            @pl.when(pl.program_id(0) == 0)
            def _():
                dstate[...] = jnp.zeros_like(dstate)

            _, vjp = jax.vjp(_delta_chunk_packed, q_ref[...], k_ref[...], v_ref[...], gb_ref[...],
                             st_ref[:, 0])
            dq, dk, dv, dgb, dst = vjp((do_ref[...], dstate[...]))
            dq_ref[...] = dq
            dk_ref[...] = dk
            dv_ref[...] = dv
            dgb_ref[...] = dgb
            dstate[...] = dst

        return pl.pallas_call(
            body, name=name + "_bwd", grid=(n,),
            in_specs=[seq, seq, seq, gate, st, seq],
            out_specs=[seq, seq, seq, gate],
            out_shape=[jax.ShapeDtypeStruct((nh, s, dh), F32)] * 3
            + [jax.ShapeDtypeStruct((s, LANES), F32)],
            scratch_shapes=[pltpu.VMEM((nh, dh, dh), F32)],
            compiler_params=_cparams(1),
        )(q, k, v, gb, states, do)

    @jax.custom_vjp
    def op(q, k, v, gb):
        return fwd_call(q, k, v, gb)[0]

    def op_fwd(q, k, v, gb):
        o, states = fwd_call(q, k, v, gb)
        return o, (q, k, v, gb, states)

    def op_bwd(res, do):
        return tuple(bwd_call(*res, do))

    op.defvjp(op_fwd, op_bwd)
    return op(q, k, v, gb)


def _my_place():
    return lax.axis_index("x"), lax.axis_index("y"), lax.axis_index("c")


def all_gather(name, shard):
    return all_gather_many(name, [shard])[0]


def all_gather_many(name, shards):
    n = len(shards)

    def body(*refs):
        x_refs, out_refs = refs[:n], refs[n:2 * n]
        send_sems, recv_sems, local_sems = refs[2 * n:]
        x, y, c = _my_place()
        me, sibling = (x, y, c), (x, y, 1 - c)
        chips = [(1 - x, y), (x, 1 - y), (1 - x, 1 - y)]

        def copy(i, k, block, to, own=False):
            px, py, pc = block
            slot = out_refs[i].at[4 * px + 2 * py + pc]
            return pltpu.make_async_remote_copy(
                src_ref=x_refs[i] if own else slot, dst_ref=slot,
                send_sem=send_sems.at[7 * i + k], recv_sem=recv_sems.at[7 * i + k],
                device_id=to, device_id_type=MESH)

        mine = [pltpu.make_async_copy(x_refs[i], out_refs[i].at[4 * x + 2 * y + c], local_sems.at[i])
                for i in range(n)]
        for cp in mine:
            cp.start()
        first = []
        for i in range(n):
            first.append(copy(i, 0, me, sibling, own=True))
            first += [copy(i, 1 + j, me, (*chip, c), own=True) for j, chip in enumerate(chips)]
        for cp in first:
            cp.start()
        passed = []
        for j, chip in enumerate(chips):
            for i in range(n):
                copy(i, 1 + j, (*chip, c), me).wait_recv()
                passed.append(copy(i, 4 + j, (*chip, c), sibling))
                passed[-1].start()
        for i in range(n):
            copy(i, 0, sibling, me).wait_recv()
        for j, chip in enumerate(chips):
            for i in range(n):
                copy(i, 4 + j, (*chip, 1 - c), me).wait_recv()
        for cp in first + passed:
            cp.wait_send()
        for cp in mine:
            cp.wait()

    any_spec = pl.BlockSpec(memory_space=pl.ANY)
    return pl.pallas_call(
        body, name=name,
        out_shape=[jax.ShapeDtypeStruct((N_DEV,) + t.shape, t.dtype) for t in shards],
        in_specs=[any_spec] * n, out_specs=[any_spec] * n,
        scratch_shapes=[pltpu.SemaphoreType.DMA((7 * n,)), pltpu.SemaphoreType.DMA((7 * n,)),
                        pltpu.SemaphoreType.DMA((n,))],
    )(*shards)


def sibling_exchange(name, parts):
    n = len(parts)

    def body(*refs):
        p_refs, out_refs = refs[:n], refs[n:2 * n]
        send_sems, recv_sems = refs[2 * n:]
        x, y, c = _my_place()
        copies = []
        for i in range(n):
            for chip in range(4):
                copies.append(pltpu.make_async_remote_copy(
                    src_ref=p_refs[i].at[2 * chip + (1 - c)], dst_ref=out_refs[i].at[chip],
                    send_sem=send_sems.at[4 * i + chip], recv_sem=recv_sems.at[4 * i + chip],
                    device_id=(x, y, 1 - c), device_id_type=MESH))
        for cp in copies:
            cp.start()
        for cp in copies:
            cp.wait_recv()
        for cp in copies:
            cp.wait_send()

    any_spec = pl.BlockSpec(memory_space=pl.ANY)
    return pl.pallas_call(
        body, name=name,
        out_shape=[jax.ShapeDtypeStruct((4,) + t.shape[1:], t.dtype) for t in parts],
        in_specs=[any_spec] * n, out_specs=[any_spec] * n,
        scratch_shapes=[pltpu.SemaphoreType.DMA((4 * n,)), pltpu.SemaphoreType.DMA((4 * n,))],
    )(*parts)


def chip_exchange(name, parts):
    n = len(parts)

    def body(*refs):
        p_refs, out_refs = refs[:n], refs[n:2 * n]
        send_sems, recv_sems, local_sems = refs[2 * n:]
        x, y, c = _my_place()
        me = 2 * x + y
        peers = [(1 - x, y), (x, 1 - y), (1 - x, 1 - y)]

        def copy(i, j, landing):
            px, py = peers[j]
            return pltpu.make_async_remote_copy(
                src_ref=p_refs[i].at[2 * px + py],
                dst_ref=out_refs[i].at[(2 * px + py) if landing else me],
                send_sem=send_sems.at[3 * i + j], recv_sem=recv_sems.at[3 * i + j],
                device_id=(px, py, c), device_id_type=MESH)

        mine = [pltpu.make_async_copy(p_refs[i].at[me], out_refs[i].at[me], local_sems.at[i])
                for i in range(n)]
        for cp in mine:
            cp.start()
        copies = [copy(i, j, False) for j in range(3) for i in range(n)]
        for cp in copies:
            cp.start()
        for j in range(3):
            for i in range(n):
                copy(i, j, True).wait_recv()
        for cp in copies:
            cp.wait_send()
        for cp in mine:
            cp.wait()

    any_spec = pl.BlockSpec(memory_space=pl.ANY)
    return pl.pallas_call(
        body, name=name,
        out_shape=[jax.ShapeDtypeStruct(t.shape, t.dtype) for t in parts],
        in_specs=[any_spec] * n, out_specs=[any_spec] * n,
        scratch_shapes=[pltpu.SemaphoreType.DMA((3 * n,)), pltpu.SemaphoreType.DMA((3 * n,)),
                        pltpu.SemaphoreType.DMA((n,))],
    )(*parts)


def pair_sum(name, a, b):
    _, rows, cols = a.shape
    row_bytes = 4 * LANES * (-(-cols // LANES))
    tile = _pick(rows, [t for t in (512, 352, 256, 128, 64, 32, 16, 8)
                        if t * row_bytes <= ADAMW_BLOCK_BYTES])

    def body(a_ref, b_ref, o_ref):
        o_ref[...] = (a_ref[...].astype(F32) + b_ref[...].astype(F32)).astype(o_ref.dtype)

    spec = pl.BlockSpec((4, tile, cols), lambda i: (0, i, 0))
    return pl.pallas_call(
        body, name=name, grid=(rows // tile,), in_specs=[spec, spec], out_specs=spec,
        out_shape=jax.ShapeDtypeStruct(a.shape, a.dtype), compiler_params=_cparams(1),
    )(a, b)


ADAMW_BLOCK_BYTES = 3 * 512 * 1024


def adamw(name, grad, w, m, v):
    rows, cols = w.shape
    stacked = grad.ndim == 3
    row_bytes = 4 * LANES * (-(-cols // LANES))
    tile = _pick(rows, [t for t in (512, 352, 256, 128, 64, 32, 16, 8)
                        if t * row_bytes <= ADAMW_BLOCK_BYTES])

    def body(g_ref, w_ref, m_ref, v_ref, go_ref, d_ref, mo_ref, vo_ref):
        if stacked:
            g = g_ref[0].astype(F32)
            for s in range(1, grad.shape[0]):
                g = g + g_ref[s].astype(F32)
        else:
            g = g_ref[...]
        m = ADAM_B1 * m_ref[...] + (1.0 - ADAM_B1) * g
        v = ADAM_B2 * v_ref[...] + (1.0 - ADAM_B2) * jnp.square(g)
        m_hat = m / (1.0 - ADAM_B1 ** ADAM_STEP)
        v_hat = v / (1.0 - ADAM_B2 ** ADAM_STEP)
        go_ref[...] = g
        d_ref[...] = -ADAM_LR * (m_hat / (jnp.sqrt(v_hat) + ADAM_EPS) + ADAM_WD * w_ref[...])
        mo_ref[...] = m
        vo_ref[...] = v

    flat = pl.BlockSpec((tile, cols), lambda i: (i, 0))
    g_spec = pl.BlockSpec((grad.shape[0], tile, cols), lambda i: (0, i, 0)) if stacked else flat
    return pl.pallas_call(
        body, name=name, grid=(rows // tile,),
        in_specs=[g_spec, flat, flat, flat], out_specs=[flat] * 4,
        out_shape=[jax.ShapeDtypeStruct((rows, cols), F32)] * 4,
        compiler_params=_cparams(1),
    )(grad, w, m, v)


def silu_rows(name, x):
    def body(x_ref, o_ref):
        o_ref[...] = _silu(x_ref[...])

    return pl.pallas_call(body, name=name, out_shape=jax.ShapeDtypeStruct(x.shape, F32))(x)


BIG = (("ffn1_wg", (D, D_FF // N_DEV), 1), ("ffn1_wu", (D, D_FF // N_DEV), 1),
       ("ffn1_wd", (D_FF // N_DEV, D), 0), ("w_in", (D, IN_COLS // N_DEV), 1),
       ("conv_w", (DN_CONV, 3 * D // N_DEV), 1), ("w_a", (D // N_DEV, D), 0),
       ("w_b", (DA_HEADS * DA_DIM, D // N_DEV), 1), ("w_o", (D // N_DEV, D), 0),
       ("ffn2_wg", (D, D_FF // N_DEV), 1), ("ffn2_wu", (D, D_FF // N_DEV), 1),
       ("ffn2_wd", (D_FF // N_DEV, D), 0))
SMALL = (("ada_b", (DEPTH, N_ADA * D)), ("ln_ffn1", (DEPTH, D)), ("ln_mix", (DEPTH, D)),
         ("ln_ffn2", (DEPTH, D)), ("a_log", (DEPTH, DN_HEADS)), ("dt_bias", (DEPTH, DN_HEADS)),
         ("dn_norm", (DEPTH, DN_DIM)), ("final_norm", (D,)))
SMALL_ROWS = 32


def _pack(arrays, rows):
    flat = jnp.concatenate([a.reshape(-1) for a in arrays])
    return jnp.pad(flat, (0, rows * D - flat.shape[0])).reshape(rows, D)


def _unpack(buf, shapes):
    flat = buf.reshape(-1)
    out, off = [], 0
    for shp in shapes:
        n = int(np.prod(shp))
        out.append(flat[off:off + n].reshape(shp))
        off += n
    return out


def _full_weights(gathered):
    full = {}
    for (name, (a, b), axis), t in zip(BIG, gathered):
        if axis == 1:
            full[name] = t.transpose(1, 2, 0, 3).reshape(DEPTH, a, N_DEV * b)
        else:
            full[name] = t.transpose(1, 0, 2, 3).reshape(DEPTH, N_DEV * a, b)
    return full


_Z0, _B0, _A0, _DQ0, _GA0 = 3072, 4096, 4104, 4112, 6416


def _reorder_in_proj(w):
    pad = jnp.zeros((w.shape[0], IN_COLS_PAD - IN_COLS), w.dtype)
    return jnp.concatenate([w[:, :_B0], w[:, _DQ0:], w[:, _B0:_DQ0], pad], axis=1)


def _split_cols(x, bounds):
    @jax.custom_vjp
    def split(x):
        return tuple(x[:, a:b] for a, b in zip(bounds[:-1], bounds[1:]))

    def split_fwd(x):
        return split(x), None

    def split_bwd(_, cts):
        return (jnp.concatenate(cts, axis=1),)

    split.defvjp(split_fwd, split_bwd)
    return split(x)


def _ffn(tag, h, ln, shift, scale, gate, w_gu, w_d):
    n = norm_mod(tag + "_norm", h, ln, shift, scale)
    gu = linear(tag + "_gu", n, w_gu, BF16)
    a = swiglu_act(tag + "_act", gu)
    f = linear(tag + "_down", a, w_d, F32)
    return residual(tag + "_res", h, f, gate, 0.5)


def _mixer(tag, u, w_in, conv_w, a_log, dt_bias, dn_norm, w_a, w_b, w_o):
    s = u.shape[0]
    proj = linear(tag + "_in", u, w_in, F32)
    (q_pre, k_pre, v_pre, z, da_q, da_k, da_v, gate_a, gate_b, ab, _) = _split_cols(
        proj, (0, 1024, 2048, 3072, 4096, 4864, 5632, 6400, 7424, 8448, 8576, IN_COLS_PAD))
    qkv_pre, da = (q_pre, k_pre, v_pre), (da_q, da_k, da_v)

    cw = conv_w.astype(F32).reshape(DN_CONV, 1, 3 * D)
    q, k, v = [conv_heads(f"{tag}_conv_{m}", qkv_pre[i], cw[:, :, i * D:(i + 1) * D], m)
               for i, m in enumerate("qkv")]
    pad = lambda t: jnp.pad(t, (DN_HEADS, LANES - 2 * DN_HEADS))[None]
    gb = decay_beta(tag + "_decay", ab, pad(a_log), pad(dt_bias))
    o = delta_rule(tag + "_delta", q, k, v, gb)
    o_a = gated_head_norm(tag + "_gnorm", o, z, dn_norm[None])
    y_a = linear(tag + "_wa", o_a, w_a, F32)

    outs, lses = [], []
    for r in DA_DILATIONS:
        o_r, lse_r = dilated_attention(f"{tag}_attn{r}", *[_sub_order(t, r) for t in da], r)
        outs.append(_seq_order(o_r, r))
        lses.append(_seq_order(lse_r, r))
    o_b = combine_patterns(tag + "_comb", outs, lses)
    y_b = linear(tag + "_wb", o_b, w_b, F32)

    merged = merge_gates(tag + "_merge", gate_a, gate_b, y_a, y_b)
    return linear(tag + "_wo", merged, w_o, F32)


def _local_loss(x, gathered, small, mod, target):
    w = _full_weights(gathered)
    h = x
    for l in range(DEPTH):
        tag = f"l{l}"
        sh1, sc1, gt1, sh2, sc2, gt2, sh3, sc3, gt3 = [mod[l, i * D:(i + 1) * D][None]
                                                       for i in range(N_ADA)]
        w_gu1 = _interleave_gate_up(w["ffn1_wg"][l], w["ffn1_wu"][l])
        w_gu2 = _interleave_gate_up(w["ffn2_wg"][l], w["ffn2_wu"][l])
        h = _ffn(tag + "_ffn1", h, small["ln_ffn1"][l][None], sh1, sc1, gt1, w_gu1, w["ffn1_wd"][l])
        u = norm_mod(tag + "_mixnorm", h, small["ln_mix"][l][None], sh2, sc2)
        m = _mixer(tag + "_mix", u, _reorder_in_proj(w["w_in"][l]), w["conv_w"][l],
                   small["a_log"][l], small["dt_bias"][l], small["dn_norm"][l],
                   w["w_a"][l], w["w_b"][l], w["w_o"][l])
        h = residual(tag + "_mixres", h, m, gt2, 1.0)
        h = _ffn(tag + "_ffn2", h, small["ln_ffn2"][l][None], sh3, sc3, gt3, w_gu2, w["ffn2_wd"][l])
    rows = loss_rows("loss", h, small["final_norm"][None], target)
    return 0.5 * jnp.sum(rows)


def kernel(x, c, ada_w, ada_b, ln_ffn1, ln_mix, ln_ffn2, ffn1_wg, ffn1_wu, ffn1_wd, w_in, conv_w, a_log, dt_bias, dn_norm, w_a, w_b, w_o, ffn2_wg, ffn2_wu, ffn2_wd, final_norm, loss_target, m_ada_w, m_ada_b, m_ln_ffn1, m_ln_mix, m_ln_ffn2, m_ffn1_wg, m_ffn1_wu, m_ffn1_wd, m_w_in, m_conv_w, m_a_log, m_dt_bias, m_dn_norm, m_w_a, m_w_b, m_w_o, m_ffn2_wg, m_ffn2_wu, m_ffn2_wd, m_final_norm, v_ada_w, v_ada_b, v_ln_ffn1, v_ln_mix, v_ln_ffn2, v_ffn1_wg, v_ffn1_wu, v_ffn1_wd, v_w_in, v_conv_w, v_a_log, v_dt_bias, v_dn_norm, v_w_a, v_w_b, v_w_o, v_ffn2_wg, v_ffn2_wu, v_ffn2_wd, v_final_norm):
    args = dict(locals())
    big_names = [n for n, _, _ in BIG]
    small_names = [n for n, _ in SMALL]
    me = 4 * lax.axis_index("x") + 2 * lax.axis_index("y") + lax.axis_index("c")
    cols = N_ADA * D // N_DEV

    c_all = all_gather("gather_c", jnp.pad(silu_rows("silu_c", c), ((0, 7), (0, 0))))[:, 0]
    mod_cols = jnp.stack([matmul(f"ada{l}", c_all, ada_w[l], "nn", F32) for l in range(DEPTH)])
    mod_cols = mod_cols + lax.dynamic_slice_in_dim(ada_b, me * cols, cols, axis=1)[:, None, :]
    mod_all = all_gather("gather_mod", mod_cols.reshape(DEPTH * N_DEV, cols))
    mod_all = mod_all.reshape(N_DEV, DEPTH, N_DEV, cols)
    mod = lax.dynamic_index_in_dim(mod_all, me, axis=2, keepdims=False)
    mod = mod.transpose(1, 0, 2).reshape(DEPTH, N_ADA * D)

    gathered = all_gather_many("gather_w", [args[n].astype(BF16) for n in big_names])

    small = {n: args[n] for n in small_names if n != "ada_b"}
    loss, (dx, dgathered, dsmall, dmod) = jax.value_and_grad(_local_loss, argnums=(0, 1, 2, 3))(
        x[0], gathered, small, mod, loss_target[0])

    part = _pack([dmod] + [dsmall[n] for n in small_names[1:]], SMALL_ROWS)
    parts = all_gather("gather_small", part)
    sm_out = adamw("adamw_small", parts, _pack([args[n] for n in small_names], SMALL_ROWS),
                   _pack([args["m_" + n] for n in small_names], SMALL_ROWS),
                   _pack([args["v_" + n] for n in small_names], SMALL_ROWS))

    dmod_all = parts.reshape(N_DEV, -1)[:, :DEPTH * N_ADA * D].reshape(N_DEV, DEPTH, N_ADA * D)
    dmod_mine = lax.dynamic_slice_in_dim(dmod_all, me * cols, cols, axis=2)
    g_ada = jnp.stack([matmul(f"ada{l}_dw", c_all, dmod_mine[:, l], "tn", F32) for l in range(DEPTH)])
    flat2 = lambda t: t.reshape(-1, t.shape[-1])
    ada_out = adamw("adamw_ada_w", flat2(g_ada), flat2(ada_w), flat2(m_ada_w), flat2(v_ada_w))

    my_core = lax.axis_index("c")
    rows3 = lambda t: t.reshape(t.shape[0], -1, t.shape[-1])
    from_sibling = sibling_exchange("pair_grads", dgathered)
    pair = []
    for n, t, got in zip(big_names, dgathered, from_sibling):
        own = lax.dynamic_index_in_dim(t.reshape((4, 2) + t.shape[1:]), my_core, axis=1, keepdims=False)
        pair.append(pair_sum("pair_sum_" + n, rows3(own), rows3(got)))
    landed = chip_exchange("scatter_grads", pair)
    big_out = {}
    for n, t in zip(big_names, landed):
        big_out[n] = adamw("adamw_" + n, t, flat2(args[n]), flat2(args["m_" + n]), flat2(args["v_" + n]))

    small_shapes = [shp for _, shp in SMALL]
    names = ["ada_w", "ada_b", "ln_ffn1", "ln_mix", "ln_ffn2", "ffn1_wg", "ffn1_wu", "ffn1_wd", "w_in",
             "conv_w", "a_log", "dt_bias", "dn_norm", "w_a", "w_b", "w_o", "ffn2_wg", "ffn2_wu",
             "ffn2_wd", "final_norm"]
    outs = [lax.psum(loss, ("x", "y", "c")), dx[None]]
    for kind in range(4):
        table = {n: big_out[n][kind].reshape(args[n].shape) for n in big_names}
        table.update(zip(small_names, _unpack(sm_out[kind], small_shapes)))
        table["ada_w"] = ada_out[kind].reshape(ada_w.shape)
        outs += [table[n] for n in names]
    return tuple(outs)
```

```python
import functools
import math

import numpy as np
import jax
import jax.numpy as jnp
from jax import lax
from jax.experimental import pallas as pl
from jax.experimental.pallas import tpu as pltpu

F32 = jnp.float32
BF16 = jnp.bfloat16

D = 1024
SEQ = 4096
DEPTH = 2
N_DEV = 8
DN_HEADS = 8
DN_DIM = 128
DN_CHUNK = 64
DN_CONV = 4
DA_HEADS = 12
DA_DIM = 64
DA_BLOCK = 128
DA_DILATIONS = (1, 4, 16)
ALIBI_MAX_EXP = 8.0
D_FF = 2816
N_ADA = 9
NORM_EPS = 1e-6
IN_COLS = 8464
IN_COLS_PAD = 8704
ADAM_LR, ADAM_B1, ADAM_B2, ADAM_EPS, ADAM_WD, ADAM_STEP = 0.001, 0.9, 0.999, 1e-08, 0.01, 10
NEG = -1e30

VMEM_LIMIT = 56 * 1024 * 1024
LANES = 128

MESH = pl.DeviceIdType.MESH


def _cparams(n_grid):
    return pltpu.CompilerParams(dimension_semantics=("arbitrary",) * n_grid,
                                vmem_limit_bytes=VMEM_LIMIT)


def blockwise(name, f, grid, ins, outs):
    n_in, n_out = len(ins), len(outs)
    diff = [i for i, (_, _, kind) in enumerate(ins) if kind != "const"]

    def apply(*vals):
        res = f(*vals)
        return tuple(r.astype(dt) for r, (_, dt, _, _) in zip(res, outs))

    def fwd_call(*arrays):
        def body(*refs):
            res = apply(*[r[...] for r in refs[:n_in]])
            for r, v in zip(refs[n_in:], res):
                r[...] = v

        return pl.pallas_call(
            body, name=name + "_fwd", grid=grid,
            in_specs=[pl.BlockSpec(b, im) for (b, im, _) in ins],
            out_specs=[pl.BlockSpec(b, im) for (_, _, b, im) in outs],
            out_shape=[jax.ShapeDtypeStruct(s, dt) for (s, dt, _, _) in outs],
            compiler_params=_cparams(len(grid)),
        )(*arrays)

    def bwd_call(arrays, cts):
        def body(*refs):
            in_refs, ct_refs = refs[:n_in], refs[n_in:n_in + n_out]
            g_refs = refs[n_in + n_out:]
            vals = [r[...] for r in in_refs]

            def fd(*dvals):
                full = list(vals)
                for i, v in zip(diff, dvals):
                    full[i] = v
                return apply(*full)

            _, vjp = jax.vjp(fd, *[vals[i] for i in diff])
            grads = vjp(tuple(r[...] for r in ct_refs))
            first = functools.reduce(jnp.logical_and,
                                     [pl.program_id(a) == 0 for a in range(len(grid))])
            for g_ref, g, i in zip(g_refs, grads, diff):
                if ins[i][2] == "acc":
                    @pl.when(first)
                    def _(g_ref=g_ref):
                        g_ref[...] = jnp.zeros_like(g_ref)
                    g_ref[...] += g.astype(F32)
                else:
                    g_ref[...] = g.astype(g_ref.dtype)

        g_shapes = [jax.ShapeDtypeStruct(arrays[i].shape,
                                         F32 if ins[i][2] == "acc" else arrays[i].dtype)
                    for i in diff]
        return pl.pallas_call(
            body, name=name + "_bwd", grid=grid,
            in_specs=([pl.BlockSpec(b, im) for (b, im, _) in ins]
                      + [pl.BlockSpec(b, im) for (_, _, b, im) in outs]),
            out_specs=[pl.BlockSpec(ins[i][0], ins[i][1]) for i in diff],
            out_shape=g_shapes,
            compiler_params=_cparams(len(grid)),
        )(*arrays, *cts)

    @jax.custom_vjp
    def op(*arrays):
        return tuple(fwd_call(*arrays))

    def op_fwd(*arrays):
        return tuple(fwd_call(*arrays)), arrays

    def op_bwd(arrays, cts):
        grads = bwd_call(arrays, cts)
        full = [None] * n_in
        for i, g in zip(diff, grads):
            full[i] = g.astype(arrays[i].dtype)
        return tuple(full)

    op.defvjp(op_fwd, op_bwd)
    return op


def _pick(n, cands):
    for c in cands:
        if n % c == 0:
            return c
    return n


def matmul(name, a, b, form, out_dtype):
    if form == "nn":
        (m, k), (_, n) = a.shape, b.shape
    elif form == "nt":
        (m, k), (n, _) = a.shape, b.shape
    else:
        (k, m), (_, n) = a.shape, b.shape
    tm = _pick(m, (1408, 1024, 512, 256, 128, 8))
    tn = _pick(n, (1408, 1024, 512, 384, 256, 128))
    tk = _pick(k, (1024, 1408, 512, 384, 256, 128, 8))
    nk = k // tk
    a_spec = (pl.BlockSpec((tk, tm), lambda i, j, kk: (kk, i)) if form == "tn"
              else pl.BlockSpec((tm, tk), lambda i, j, kk: (i, kk)))
    b_spec = (pl.BlockSpec((tn, tk), lambda i, j, kk: (j, kk)) if form == "nt"
              else pl.BlockSpec((tk, tn), lambda i, j, kk: (kk, j)))
    dims = {"nn": (((1,), (0,)), ((), ())), "nt": (((1,), (1,)), ((), ())),
            "tn": (((0,), (0,)), ((), ()))}[form]

    def body(a_ref, b_ref, o_ref, acc_ref):
        kk = pl.program_id(2)
        part = lax.dot_general(a_ref[...].astype(BF16), b_ref[...].astype(BF16), dims,
                               preferred_element_type=F32)

        @pl.when(kk == 0)
        def _():
            acc_ref[...] = part

        @pl.when(kk > 0)
        def _():
            acc_ref[...] += part

        @pl.when(kk == nk - 1)
        def _():
            o_ref[...] = acc_ref[...].astype(o_ref.dtype)

    return pl.pallas_call(
        body, name=name, grid=(m // tm, n // tn, nk),
        in_specs=[a_spec, b_spec],
        out_specs=pl.BlockSpec((tm, tn), lambda i, j, kk: (i, j)),
        out_shape=jax.ShapeDtypeStruct((m, n), out_dtype),
        scratch_shapes=[pltpu.VMEM((tm, tn), F32)],
        compiler_params=_cparams(3),
    )(a, b)


def linear(name, x, w, out_dtype):
    @jax.custom_vjp
    def op(x, w):
        return matmul(name + "_y", x, w, "nn", out_dtype)

    def op_fwd(x, w):
        return op(x, w), (x, w)

    def op_bwd(res, dy):
        x, w = res
        dx = matmul(name + "_dx", dy, w, "nt", x.dtype)
        dw = matmul(name + "_dw", x, dy, "tn", w.dtype)
        return dx, dw

    op.defvjp(op_fwd, op_bwd)
    return op(x, w)


@jax.custom_vjp
def _sigmoid(x):
    return 1.0 / (1.0 + jnp.exp(-x))


def _sigmoid_fwd(x):
    s = _sigmoid(x)
    return s, s


def _sigmoid_bwd(s, ct):
    return (ct * (s * (1.0 - s)),)


_sigmoid.defvjp(_sigmoid_fwd, _sigmoid_bwd)


@jax.custom_vjp
def _silu(x):
    return x * _sigmoid(x)


def _silu_fwd(x):
    s = _sigmoid(x)
    return x * s, (x, s)


def _silu_bwd(res, ct):
    x, s = res
    return (ct * (s * (1.0 + x * (1.0 - s))),)


_silu.defvjp(_silu_fwd, _silu_bwd)


def _softplus(x):
    return jnp.maximum(x, 0.0) + jnp.log(1.0 + jnp.exp(-jnp.abs(x)))


def _rms(x):
    return x * lax.rsqrt(jnp.mean(x * x, axis=-1, keepdims=True) + NORM_EPS)


ROW_TILE = 512


def _row(i):
    return (i, 0)


def _fixed(*_):
    return (0, 0)


def norm_mod(name, h, ln, shift, scale):
    s, d = h.shape

    def f(h, ln, sh, sc):
        return ((_rms(h) * ln) * (1.0 + sc) + sh,)

    op = blockwise(name, f, (s // ROW_TILE,),
                   [((ROW_TILE, d), _row, "tile")] + [((1, d), _fixed, "acc")] * 3,
                   [((s, d), BF16, (ROW_TILE, d), _row)])
    return op(h, ln, shift, scale)[0]


FF_TILE = 1408


def _interleave_gate_up(wg, wu):
    parts = []
    for j in range(wg.shape[1] // FF_TILE):
        parts += [wg[:, j * FF_TILE:(j + 1) * FF_TILE], wu[:, j * FF_TILE:(j + 1) * FF_TILE]]
    return jnp.concatenate(parts, axis=1)


def swiglu_act(name, gu):
    s, f2 = gu.shape
    ff = f2 // 2

    def f(gu):
        g, u = gu[:, :FF_TILE].astype(F32), gu[:, FF_TILE:].astype(F32)
        return (_silu(g) * u,)

    op = blockwise(name, f, (s // ROW_TILE, ff // FF_TILE),
                   [((ROW_TILE, 2 * FF_TILE), lambda i, j: (i, j), "tile")],
                   [((s, ff), BF16, (ROW_TILE, FF_TILE), lambda i, j: (i, j))])
    return op(gu)[0]


def residual(name, h, y, gate, weight):
    s, d = h.shape

    def f(h, y, gate):
        return (h + (weight * gate) * y,)

    op = blockwise(name, f, (s // ROW_TILE,),
                   [((ROW_TILE, d), _row, "tile"), ((ROW_TILE, d), _row, "tile"),
                    ((1, d), _fixed, "acc")],
                   [((s, d), F32, (ROW_TILE, d), _row)])
    return op(h, y, gate)[0]


def merge_gates(name, ga, gb, ya, yb):
    s, d = ya.shape

    def f(ga, gb, ya, yb):
        return (_sigmoid(ga) * ya + _sigmoid(gb) * yb,)

    op = blockwise(name, f, (s // ROW_TILE,), [((ROW_TILE, d), _row, "tile")] * 4,
                   [((s, d), BF16, (ROW_TILE, d), _row)])
    return op(ga, gb, ya, yb)[0]


def loss_rows(name, h, g, target):
    s, d = h.shape

    def f(h, g, t):
        err = _rms(h) * g - t
        return (jnp.mean(err * err, axis=-1, keepdims=True),)

    op = blockwise(name, f, (s // ROW_TILE,),
                   [((ROW_TILE, d), _row, "tile"), ((1, d), _fixed, "acc"),
                    ((ROW_TILE, d), _row, "const")],
                   [((s, 1), F32, (ROW_TILE, 1), _row)])
    return op(h, g, target)[0]


def decay_beta(name, ab, a_log_pad, dt_bias_pad):
    s, n = ab.shape

    def f(ab, a_log, dt_bias):
        lane = lax.broadcasted_iota(jnp.int32, ab.shape, 1)
        beta = _sigmoid(ab)
        g = -jnp.exp(a_log) * _softplus(ab + dt_bias)
        return (jnp.where(lane < DN_HEADS, beta, jnp.where(lane < 2 * DN_HEADS, g, 0.0)),)

    op = blockwise(name, f, (s // ROW_TILE,),
                   [((ROW_TILE, n), _row, "tile"), ((1, n), _fixed, "acc"), ((1, n), _fixed, "acc")],
                   [((s, n), F32, (ROW_TILE, n), _row)])
    return op(ab, a_log_pad, dt_bias_pad)[0]


def _shift_rows(x, k):
    n = x.shape[0]

    @jax.custom_vjp
    def shift(x):
        row = lax.broadcasted_iota(jnp.int32, x.shape, 0)
        return jnp.where(row >= k, pltpu.roll(x, k, 0), 0.0)

    def shift_fwd(x):
        return shift(x), None

    def shift_bwd(_, g):
        row = lax.broadcasted_iota(jnp.int32, g.shape, 0)
        return (jnp.where(row < n - k, pltpu.roll(g, n - k, 0), 0.0),)

    shift.defvjp(shift_fwd, shift_bwd)
    return shift(x)


def conv_heads(name, x, w, mode):
    s, width = x.shape
    nh = width // LANES

    def f(x, w):
        y = w[DN_CONV - 1] * x
        for j in range(DN_CONV - 1):
            y = y + w[j] * _shift_rows(x, DN_CONV - 1 - j)
        y = _silu(y)
        if mode != "v":
            y = y * lax.rsqrt(jnp.sum(y * y, axis=-1, keepdims=True) + NORM_EPS)
        if mode == "q":
            y = y * (DN_DIM ** -0.5)
        return (y[None],)

    op = blockwise(name, f, (nh,),
                   [((s, LANES), lambda j: (0, j), "tile"),
                    ((DN_CONV, 1, LANES), lambda j: (0, 0, j), "tile")],
                   [((nh, s, LANES), F32, (1, s, LANES), lambda j: (j, 0, 0))])
    return op(x, w)[0]


def gated_head_norm(name, o, z, w):
    nh, s, dh = o.shape

    def f(o, z, w):
        return (_rms(o[0]) * w * _silu(z),)

    op = blockwise(name, f, (s // ROW_TILE, nh),
                   [((1, ROW_TILE, dh), lambda i, h: (h, i, 0), "tile"),
                    ((ROW_TILE, dh), lambda i, h: (i, h), "tile"),
                    ((1, dh), lambda i, h: (0, 0), "acc")],
                   [((s, nh * dh), BF16, (ROW_TILE, dh), lambda i, h: (i, h))])
    return op(o, z, w)[0]


def _mm(a, b, ca, cb):
    return _bmm(a[None], b[None], ca + 1, cb + 1)[0]


def dilated_attention(name, q, k, v, dilation):
    s, width = q.shape
    nblk = s // DA_BLOCK
    per_sub = nblk // dilation
    slopes = [dilation * 2.0 ** (-ALIBI_MAX_EXP * (h + 1) / DA_HEADS) for h in range(DA_HEADS)]

    def f(q, kp, kc, vp, vc):
        first = (pl.program_id(0) % per_sub) == 0
        qi = lax.broadcasted_iota(jnp.int32, (DA_BLOCK, 2 * DA_BLOCK), 0)
        ki = lax.broadcasted_iota(jnp.int32, (DA_BLOCK, 2 * DA_BLOCK), 1)
        steps = (qi + DA_BLOCK - ki).astype(F32)
        lowest = qi + first.astype(jnp.int32) * (DA_BLOCK - qi)
        valid = jnp.logical_and(ki >= lowest, ki <= qi + DA_BLOCK)
        top = lax.broadcasted_iota(jnp.int32, (DA_BLOCK, LANES), 1) < DA_DIM
        o_parts, lse_parts = [], []
        for pair in range(width // LANES):
            cols = slice(pair * LANES, (pair + 1) * LANES)
            q2 = jnp.concatenate([jnp.where(top, q[:, cols], 0.0), jnp.where(top, 0.0, q[:, cols])], axis=0)
            k2 = jnp.concatenate([kp[:, cols], kc[:, cols]], axis=0)
            v2 = jnp.concatenate([vp[:, cols], vc[:, cols]], axis=0)
            bias = jnp.concatenate([jnp.where(valid, -slopes[2 * pair + half] * steps, NEG)
                                    for half in range(2)], axis=0)
            sc = _mm(q2, k2, 1, 1) * (DA_DIM ** -0.5) + bias
            mx = jnp.max(sc, axis=-1, keepdims=True)
            p = jnp.exp(sc - mx)
            l = jnp.sum(p, axis=-1, keepdims=True)
            o2 = _mm(p / l, v2, 1, 0)
            lse2 = mx + jnp.log(l)
            o_parts.append(jnp.where(top, o2[:DA_BLOCK], o2[DA_BLOCK:]))
            lse_parts.append(jnp.where(top, lse2[:DA_BLOCK], lse2[DA_BLOCK:]))
        return jnp.concatenate(o_parts, axis=1), jnp.concatenate(lse_parts, axis=1)

    blk = (DA_BLOCK, width)
    cur_map = lambda j: (j, 0)
    prev_map = lambda j: (jnp.maximum(j - 1, 0), 0)
    op = blockwise(
        name, f, (nblk,),
        [(blk, cur_map, "tile"), (blk, prev_map, "tile"), (blk, cur_map, "tile"),
         (blk, prev_map, "tile"), (blk, cur_map, "tile")],
        [((s, width), F32, blk, cur_map), ((s, width), F32, blk, cur_map)])

    @jax.custom_vjp
    def attn(q, k, v):
        return op(q, k, k, v, v)

    def attn_fwd(q, k, v):
        return attn(q, k, v), (q, k, v)

    def attn_bwd(res, cts):
        q, k, v = res
        _, vjp = jax.vjp(op, q, k, k, v, v)
        dq, dkp, dkc, dvp, dvc = vjp(cts)
        fill = lambda t: t.at[s - DA_BLOCK:].set(0.0)
        return dq, dkc + fill(dkp), dvc + fill(dvp)

    attn.defvjp(attn_fwd, attn_bwd)
    return attn(q, k, v)


def combine_patterns(name, outs, lses):
    s, width = outs[0].shape
    n = len(outs)

    def f(*vals):
        o, lse = vals[:n], vals[n:]
        mx = functools.reduce(jnp.maximum, lse)
        e = [jnp.exp(t - mx) for t in lse]
        return (sum(ei * oi for ei, oi in zip(e, o)) / sum(e),)

    tile = ROW_TILE // 2
    op = blockwise(name, f, (s // tile,), [((tile, width), _row, "tile")] * (2 * n),
                   [((s, width), BF16, (tile, width), _row)])
    return op(*outs, *lses)[0]


def _sub_order(t, r):
    if r == 1:
        return t
    s, c = t.shape
    return t.reshape(s // r, r, c).transpose(1, 0, 2).reshape(s, c)


def _seq_order(t, r):
    if r == 1:
        return t
    s, c = t.shape
    return t.reshape(r, s // r, c).transpose(1, 0, 2).reshape(s, c)


def _raw_bmm(a, b, ca, cb):
    return lax.dot_general(a.astype(BF16), b.astype(BF16), (((ca,), (cb,)), ((0,), (0,))),
                           preferred_element_type=F32)


def _split(a):
    hi = a.astype(BF16)
    return hi, (a - hi.astype(F32)).astype(BF16)


def _passes_bmm(a, b, ca, cb, passes):
    if passes == 1:
        return _raw_bmm(a, b, ca, cb)
    (a_hi, a_lo), (b_hi, b_lo) = _split(a), _split(b)
    return _raw_bmm(a_hi, b_hi, ca, cb) + (_raw_bmm(a_hi, b_lo, ca, cb) + _raw_bmm(a_lo, b_hi, ca, cb))


def _bmm(a, b, ca, cb, passes=1):
    fa, fb = 3 - ca, 3 - cb

    @jax.custom_vjp
    def mm(a, b):
        return _passes_bmm(a, b, ca, cb, passes)

    def mm_fwd(a, b):
        return mm(a, b), (a, b)

    def mm_bwd(res, ct):
        a, b = res
        da = (_passes_bmm(ct, b, 2, fb, passes) if ca == 2
              else _passes_bmm(b, ct, fb, 2, passes))
        db = (_passes_bmm(a, ct, fa, 1, passes) if cb == 1
              else _passes_bmm(ct, a, 1, fa, passes))
        return da, db

    mm.defvjp(mm_fwd, mm_bwd)
    return mm(a, b)


def _delta_chunk(q, k, v, gcol, grow, bcol, state):
    c = q.shape[1]
    ii = lax.broadcasted_iota(jnp.int32, (1, c, c), 1)
    jj = lax.broadcasted_iota(jnp.int32, (1, c, c), 2)
    incl, strict = ii >= jj, ii > jj
    gc_col = jnp.sum(jnp.where(incl, grow, 0.0), axis=2, keepdims=True)
    gc_row = jnp.sum(jnp.where(ii <= jj, gcol, 0.0), axis=1, keepdims=True)
    decay = jnp.where(incl, jnp.exp(jnp.where(incl, gc_col - gc_row, 0.0)), 0.0)
    kb, vb = k * bcol, v * bcol
    m = jnp.where(strict, _bmm(kb, k, 2, 2) * decay, 0.0)
    eye = (ii == jj).astype(F32)
    p = -m
    inv = eye + p
    for _ in range(int(math.log2(c)) - 1):
        p = _bmm(p, p, 2, 1, 3)
        inv = inv + _bmm(inv, p, 2, 1, 3)
    e_col = jnp.exp(gc_col)
    u = _bmm(inv, vb, 2, 1)
    w = _bmm(inv, kb * e_col, 2, 1)
    qk = _bmm(q, k, 2, 2) * decay
    v_new = u - _bmm(w, state, 2, 1)
    o = _bmm(q * e_col, state, 2, 1) + _bmm(qk, v_new, 2, 1)
    g_last = jnp.sum(grow, axis=2, keepdims=True)
    new_state = state * jnp.exp(g_last) + _bmm(k * jnp.exp(g_last - gc_col), v_new, 1, 1)
    return o, new_state


def _delta_chunk_packed(q, k, v, gb, state):
    nh, c = q.shape[0], q.shape[1]
    lane = lax.broadcasted_iota(jnp.int32, gb.shape, 1)
    eye = (lax.broadcasted_iota(jnp.int32, (c, c), 0) == lax.broadcasted_iota(jnp.int32, (c, c), 1))
    column = lambda l: jnp.sum(jnp.where(lane == l, gb, 0.0), axis=1, keepdims=True)
    heads = lambda parts: jnp.concatenate([t[None] for t in parts], axis=0)
    bcol = heads([column(h) for h in range(nh)])
    gcols = [column(nh + h) for h in range(nh)]
    gcol = heads(gcols)
    grow = heads([jnp.sum(jnp.where(eye, g, 0.0), axis=0, keepdims=True) for g in gcols])
    return _delta_chunk(q, k, v, gcol, grow, bcol, state)


def delta_rule(name, q, k, v, gb):
    nh, s, dh = q.shape
    c = DN_CHUNK
    n = s // c

    def specs(rev):
        t = (lambda i: n - 1 - i) if rev else (lambda i: i)
        seq = pl.BlockSpec((nh, c, dh), lambda i: (0, t(i), 0))
        gate = pl.BlockSpec((c, LANES), lambda i: (t(i), 0))
        st = pl.BlockSpec((nh, 1, dh, dh), lambda i: (0, t(i), 0, 0))
        return seq, gate, st

    def fwd_call(q, k, v, gb):
        seq, gate, st = specs(False)

        def body(q_ref, k_ref, v_ref, gb_ref, o_ref, st_ref, state):
            @pl.when(pl.program_id(0) == 0)
            def _():
                state[...] = jnp.zeros_like(state)

            st_ref[:, 0] = state[...]
            o, new_state = _delta_chunk_packed(q_ref[...], k_ref[...], v_ref[...], gb_ref[...],
                                               state[...])
            o_ref[...] = o
            state[...] = new_state

        return pl.pallas_call(
            body, name=name + "_fwd", grid=(n,),
            in_specs=[seq, seq, seq, gate],
            out_specs=[seq, st],
            out_shape=[jax.ShapeDtypeStruct((nh, s, dh), F32),
                       jax.ShapeDtypeStruct((nh, n, dh, dh), F32)],
            scratch_shapes=[pltpu.VMEM((nh, dh, dh), F32)],
            compiler_params=_cparams(1),
        )(q, k, v, gb)

    def bwd_call(q, k, v, gb, states, do):
        seq, gate, st = specs(True)

        def body(q_ref, k_ref, v_ref, gb_ref, st_ref, do_ref,
                 dq_ref, dk_ref, dv_ref, dgb_ref, dstate):
            @pl.when(pl.program_id(0) == 0)
            def _():
                dstate[...] = jnp.zeros_like(dstate)

            _, vjp = jax.vjp(_delta_chunk_packed, q_ref[...], k_ref[...], v_ref[...], gb_ref[...],
                             st_ref[:, 0])
            dq, dk, dv, dgb, dst = vjp((do_ref[...], dstate[...]))
            dq_ref[...] = dq
            dk_ref[...] = dk
            dv_ref[...] = dv
            dgb_ref[...] = dgb
            dstate[...] = dst

        return pl.pallas_call(
            body, name=name + "_bwd", grid=(n,),
            in_specs=[seq, seq, seq, gate, st, seq],
            out_specs=[seq, seq, seq, gate],
            out_shape=[jax.ShapeDtypeStruct((nh, s, dh), F32)] * 3
            + [jax.ShapeDtypeStruct((s, LANES), F32)],
            scratch_shapes=[pltpu.VMEM((nh, dh, dh), F32)],
            compiler_params=_cparams(1),
        )(q, k, v, gb, states, do)

    @jax.custom_vjp
    def op(q, k, v, gb):
        return fwd_call(q, k, v, gb)[0]

    def op_fwd(q, k, v, gb):
        o, states = fwd_call(q, k, v, gb)
        return o, (q, k, v, gb, states)

    def op_bwd(res, do):
        return tuple(bwd_call(*res, do))

    op.defvjp(op_fwd, op_bwd)
    return op(q, k, v, gb)


def _my_place():
    return lax.axis_index("x"), lax.axis_index("y"), lax.axis_index("c")


def all_gather(name, shard):
    return all_gather_many(name, [shard])[0]


def all_gather_many(name, shards):
    n = len(shards)

    def body(*refs):
        x_refs, out_refs = refs[:n], refs[n:2 * n]
        send_sems, recv_sems, local_sems = refs[2 * n:]
        x, y, c = _my_place()
        me, sibling = (x, y, c), (x, y, 1 - c)
        chips = [(1 - x, y), (x, 1 - y), (1 - x, 1 - y)]

        def copy(i, k, block, to, own=False):
            px, py, pc = block
            slot = out_refs[i].at[4 * px + 2 * py + pc]
            return pltpu.make_async_remote_copy(
                src_ref=x_refs[i] if own else slot, dst_ref=slot,
                send_sem=send_sems.at[7 * i + k], recv_sem=recv_sems.at[7 * i + k],
                device_id=to, device_id_type=MESH)

        mine = [pltpu.make_async_copy(x_refs[i], out_refs[i].at[4 * x + 2 * y + c], local_sems.at[i])
                for i in range(n)]
        for cp in mine:
            cp.start()
        first = []
        for i in range(n):
            first.append(copy(i, 0, me, sibling, own=True))
            first += [copy(i, 1 + j, me, (*chip, c), own=True) for j, chip in enumerate(chips)]
        for cp in first:
            cp.start()
        passed = []
        for j, chip in enumerate(chips):
            for i in range(n):
                copy(i, 1 + j, (*chip, c), me).wait_recv()
                passed.append(copy(i, 4 + j, (*chip, c), sibling))
                passed[-1].start()
        for i in range(n):
            copy(i, 0, sibling, me).wait_recv()
        for j, chip in enumerate(chips):
            for i in range(n):
                copy(i, 4 + j, (*chip, 1 - c), me).wait_recv()
        for cp in first + passed:
            cp.wait_send()
        for cp in mine:
            cp.wait()

    any_spec = pl.BlockSpec(memory_space=pl.ANY)
    return pl.pallas_call(
        body, name=name,
        out_shape=[jax.ShapeDtypeStruct((N_DEV,) + t.shape, t.dtype) for t in shards],
        in_specs=[any_spec] * n, out_specs=[any_spec] * n,
        scratch_shapes=[pltpu.SemaphoreType.DMA((7 * n,)), pltpu.SemaphoreType.DMA((7 * n,)),
                        pltpu.SemaphoreType.DMA((n,))],
    )(*shards)


def sibling_exchange(name, parts):
    n = len(parts)

    def body(*refs):
        p_refs, out_refs = refs[:n], refs[n:2 * n]
        send_sems, recv_sems = refs[2 * n:]
        x, y, c = _my_place()
        copies = []
        for i in range(n):
            for chip in range(4):
                copies.append(pltpu.make_async_remote_copy(
                    src_ref=p_refs[i].at[2 * chip + (1 - c)], dst_ref=out_refs[i].at[chip],
                    send_sem=send_sems.at[4 * i + chip], recv_sem=recv_sems.at[4 * i + chip],
                    device_id=(x, y, 1 - c), device_id_type=MESH))
        for cp in copies:
            cp.start()
        for cp in copies:
            cp.wait_recv()
        for cp in copies:
            cp.wait_send()

    any_spec = pl.BlockSpec(memory_space=pl.ANY)
    return pl.pallas_call(
        body, name=name,
        out_shape=[jax.ShapeDtypeStruct((4,) + t.shape[1:], t.dtype) for t in parts],
        in_specs=[any_spec] * n, out_specs=[any_spec] * n,
        scratch_shapes=[pltpu.SemaphoreType.DMA((4 * n,)), pltpu.SemaphoreType.DMA((4 * n,))],
    )(*parts)


def chip_exchange(name, parts):
    n = len(parts)

    def body(*refs):
        p_refs, out_refs = refs[:n], refs[n:2 * n]
        send_sems, recv_sems, local_sems = refs[2 * n:]
        x, y, c = _my_place()
        me = 2 * x + y
        peers = [(1 - x, y), (x, 1 - y), (1 - x, 1 - y)]

        def copy(i, j, landing):
            px, py = peers[j]
            return pltpu.make_async_remote_copy(
                src_ref=p_refs[i].at[2 * px + py],
                dst_ref=out_refs[i].at[(2 * px + py) if landing else me],
                send_sem=send_sems.at[3 * i + j], recv_sem=recv_sems.at[3 * i + j],
                device_id=(px, py, c), device_id_type=MESH)

        mine = [pltpu.make_async_copy(p_refs[i].at[me], out_refs[i].at[me], local_sems.at[i])
                for i in range(n)]
        for cp in mine:
            cp.start()
        copies = [copy(i, j, False) for j in range(3) for i in range(n)]
        for cp in copies:
            cp.start()
        for j in range(3):
            for i in range(n):
                copy(i, j, True).wait_recv()
        for cp in copies:
            cp.wait_send()
        for cp in mine:
            cp.wait()

    any_spec = pl.BlockSpec(memory_space=pl.ANY)
    return pl.pallas_call(
        body, name=name,
        out_shape=[jax.ShapeDtypeStruct(t.shape, t.dtype) for t in parts],
        in_specs=[any_spec] * n, out_specs=[any_spec] * n,
        scratch_shapes=[pltpu.SemaphoreType.DMA((3 * n,)), pltpu.SemaphoreType.DMA((3 * n,)),
                        pltpu.SemaphoreType.DMA((n,))],
    )(*parts)


def pair_sum(name, a, b):
    _, rows, cols = a.shape
    row_bytes = 4 * LANES * (-(-cols // LANES))
    tile = _pick(rows, [t for t in (512, 352, 256, 128, 64, 32, 16, 8)
                        if t * row_bytes <= ADAMW_BLOCK_BYTES])

    def body(a_ref, b_ref, o_ref):
        o_ref[...] = (a_ref[...].astype(F32) + b_ref[...].astype(F32)).astype(o_ref.dtype)

    spec = pl.BlockSpec((4, tile, cols), lambda i: (0, i, 0))
    return pl.pallas_call(
        body, name=name, grid=(rows // tile,), in_specs=[spec, spec], out_specs=spec,
        out_shape=jax.ShapeDtypeStruct(a.shape, a.dtype), compiler_params=_cparams(1),
    )(a, b)


ADAMW_BLOCK_BYTES = 3 * 512 * 1024


def adamw(name, grad, w, m, v):
    rows, cols = w.shape
    stacked = grad.ndim == 3
    row_bytes = 4 * LANES * (-(-cols // LANES))
    tile = _pick(rows, [t for t in (512, 352, 256, 128, 64, 32, 16, 8)
                        if t * row_bytes <= ADAMW_BLOCK_BYTES])

    def body(g_ref, w_ref, m_ref, v_ref, go_ref, d_ref, mo_ref, vo_ref):
        if stacked:
            g = g_ref[0].astype(F32)
            for s in range(1, grad.shape[0]):
                g = g + g_ref[s].astype(F32)
        else:
            g = g_ref[...]
        m = ADAM_B1 * m_ref[...] + (1.0 - ADAM_B1) * g
        v = ADAM_B2 * v_ref[...] + (1.0 - ADAM_B2) * jnp.square(g)
        m_hat = m / (1.0 - ADAM_B1 ** ADAM_STEP)
        v_hat = v / (1.0 - ADAM_B2 ** ADAM_STEP)
        go_ref[...] = g
        d_ref[...] = -ADAM_LR * (m_hat / (jnp.sqrt(v_hat) + ADAM_EPS) + ADAM_WD * w_ref[...])
        mo_ref[...] = m
        vo_ref[...] = v

    flat = pl.BlockSpec((tile, cols), lambda i: (i, 0))
    g_spec = pl.BlockSpec((grad.shape[0], tile, cols), lambda i: (0, i, 0)) if stacked else flat
    return pl.pallas_call(
        body, name=name, grid=(rows // tile,),
        in_specs=[g_spec, flat, flat, flat], out_specs=[flat] * 4,
        out_shape=[jax.ShapeDtypeStruct((rows, cols), F32)] * 4,
        compiler_params=_cparams(1),
    )(grad, w, m, v)


def silu_rows(name, x):
    def body(x_ref, o_ref):
        o_ref[...] = _silu(x_ref[...])

    return pl.pallas_call(body, name=name, out_shape=jax.ShapeDtypeStruct(x.shape, F32))(x)


BIG = (("ffn1_wg", (D, D_FF // N_DEV), 1), ("ffn1_wu", (D, D_FF // N_DEV), 1),
       ("ffn1_wd", (D_FF // N_DEV, D), 0), ("w_in", (D, IN_COLS // N_DEV), 1),
       ("conv_w", (DN_CONV, 3 * D // N_DEV), 1), ("w_a", (D // N_DEV, D), 0),
       ("w_b", (DA_HEADS * DA_DIM, D // N_DEV), 1), ("w_o", (D // N_DEV, D), 0),
       ("ffn2_wg", (D, D_FF // N_DEV), 1), ("ffn2_wu", (D, D_FF // N_DEV), 1),
       ("ffn2_wd", (D_FF // N_DEV, D), 0))
SMALL = (("ada_b", (DEPTH, N_ADA * D)), ("ln_ffn1", (DEPTH, D)), ("ln_mix", (DEPTH, D)),
         ("ln_ffn2", (DEPTH, D)), ("a_log", (DEPTH, DN_HEADS)), ("dt_bias", (DEPTH, DN_HEADS)),
         ("dn_norm", (DEPTH, DN_DIM)), ("final_norm", (D,)))
SMALL_ROWS = 32


def _pack(arrays, rows):
    flat = jnp.concatenate([a.reshape(-1) for a in arrays])
    return jnp.pad(flat, (0, rows * D - flat.shape[0])).reshape(rows, D)


def _unpack(buf, shapes):
    flat = buf.reshape(-1)
    out, off = [], 0
    for shp in shapes:
        n = int(np.prod(shp))
        out.append(flat[off:off + n].reshape(shp))
        off += n
    return out


def _full_weights(gathered):
    full = {}
    for (name, (a, b), axis), t in zip(BIG, gathered):
        if axis == 1:
            full[name] = t.transpose(1, 2, 0, 3).reshape(DEPTH, a, N_DEV * b)
        else:
            full[name] = t.transpose(1, 0, 2, 3).reshape(DEPTH, N_DEV * a, b)
    return full


_Z0, _B0, _A0, _DQ0, _GA0 = 3072, 4096, 4104, 4112, 6416


def _reorder_in_proj(w):
    pad = jnp.zeros((w.shape[0], IN_COLS_PAD - IN_COLS), w.dtype)
    return jnp.concatenate([w[:, :_B0], w[:, _DQ0:], w[:, _B0:_DQ0], pad], axis=1)


def _split_cols(x, bounds):
    @jax.custom_vjp
    def split(x):
        return tuple(x[:, a:b] for a, b in zip(bounds[:-1], bounds[1:]))

    def split_fwd(x):
        return split(x), None

    def split_bwd(_, cts):
        return (jnp.concatenate(cts, axis=1),)

    split.defvjp(split_fwd, split_bwd)
    return split(x)


def _ffn(tag, h, ln, shift, scale, gate, w_gu, w_d):
    n = norm_mod(tag + "_norm", h, ln, shift, scale)
    gu = linear(tag + "_gu", n, w_gu, BF16)
    a = swiglu_act(tag + "_act", gu)
    f = linear(tag + "_down", a, w_d, F32)
    return residual(tag + "_res", h, f, gate, 0.5)


def _mixer(tag, u, w_in, conv_w, a_log, dt_bias, dn_norm, w_a, w_b, w_o):
    s = u.shape[0]
    proj = linear(tag + "_in", u, w_in, F32)
    (q_pre, k_pre, v_pre, z, da_q, da_k, da_v, gate_a, gate_b, ab, _) = _split_cols(
        proj, (0, 1024, 2048, 3072, 4096, 4864, 5632, 6400, 7424, 8448, 8576, IN_COLS_PAD))
    qkv_pre, da = (q_pre, k_pre, v_pre), (da_q, da_k, da_v)

    cw = conv_w.astype(F32).reshape(DN_CONV, 1, 3 * D)
    q, k, v = [conv_heads(f"{tag}_conv_{m}", qkv_pre[i], cw[:, :, i * D:(i + 1) * D], m)
               for i, m in enumerate("qkv")]
    pad = lambda t: jnp.pad(t, (DN_HEADS, LANES - 2 * DN_HEADS))[None]
    gb = decay_beta(tag + "_decay", ab, pad(a_log), pad(dt_bias))
    o = delta_rule(tag + "_delta", q, k, v, gb)
    o_a = gated_head_norm(tag + "_gnorm", o, z, dn_norm[None])
    y_a = linear(tag + "_wa", o_a, w_a, F32)

    outs, lses = [], []
    for r in DA_DILATIONS:
        o_r, lse_r = dilated_attention(f"{tag}_attn{r}", *[_sub_order(t, r) for t in da], r)
        outs.append(_seq_order(o_r, r))
        lses.append(_seq_order(lse_r, r))
    o_b = combine_patterns(tag + "_comb", outs, lses)
    y_b = linear(tag + "_wb", o_b, w_b, F32)

    merged = merge_gates(tag + "_merge", gate_a, gate_b, y_a, y_b)
    return linear(tag + "_wo", merged, w_o, F32)


def _local_loss(x, gathered, small, mod, target):
    w = _full_weights(gathered)
    h = x
    for l in range(DEPTH):
        tag = f"l{l}"
        sh1, sc1, gt1, sh2, sc2, gt2, sh3, sc3, gt3 = [mod[l, i * D:(i + 1) * D][None]
                                                       for i in range(N_ADA)]
        w_gu1 = _interleave_gate_up(w["ffn1_wg"][l], w["ffn1_wu"][l])
        w_gu2 = _interleave_gate_up(w["ffn2_wg"][l], w["ffn2_wu"][l])
        h = _ffn(tag + "_ffn1", h, small["ln_ffn1"][l][None], sh1, sc1, gt1, w_gu1, w["ffn1_wd"][l])
        u = norm_mod(tag + "_mixnorm", h, small["ln_mix"][l][None], sh2, sc2)
        m = _mixer(tag + "_mix", u, _reorder_in_proj(w["w_in"][l]), w["conv_w"][l],
                   small["a_log"][l], small["dt_bias"][l], small["dn_norm"][l],
                   w["w_a"][l], w["w_b"][l], w["w_o"][l])
        h = residual(tag + "_mixres", h, m, gt2, 1.0)
        h = _ffn(tag + "_ffn2", h, small["ln_ffn2"][l][None], sh3, sc3, gt3, w_gu2, w["ffn2_wd"][l])
    rows = loss_rows("loss", h, small["final_norm"][None], target)
    return 0.5 * jnp.sum(rows)


def kernel(x, c, ada_w, ada_b, ln_ffn1, ln_mix, ln_ffn2, ffn1_wg, ffn1_wu, ffn1_wd, w_in, conv_w, a_log, dt_bias, dn_norm, w_a, w_b, w_o, ffn2_wg, ffn2_wu, ffn2_wd, final_norm, loss_target, m_ada_w, m_ada_b, m_ln_ffn1, m_ln_mix, m_ln_ffn2, m_ffn1_wg, m_ffn1_wu, m_ffn1_wd, m_w_in, m_conv_w, m_a_log, m_dt_bias, m_dn_norm, m_w_a, m_w_b, m_w_o, m_ffn2_wg, m_ffn2_wu, m_ffn2_wd, m_final_norm, v_ada_w, v_ada_b, v_ln_ffn1, v_ln_mix, v_ln_ffn2, v_ffn1_wg, v_ffn1_wu, v_ffn1_wd, v_w_in, v_conv_w, v_a_log, v_dt_bias, v_dn_norm, v_w_a, v_w_b, v_w_o, v_ffn2_wg, v_ffn2_wu, v_ffn2_wd, v_final_norm):
    args = dict(locals())
    big_names = [n for n, _, _ in BIG]
    small_names = [n for n, _ in SMALL]
    me = 4 * lax.axis_index("x") + 2 * lax.axis_index("y") + lax.axis_index("c")
    cols = N_ADA * D // N_DEV

    c_all = all_gather("gather_c", jnp.pad(silu_rows("silu_c", c), ((0, 7), (0, 0))))[:, 0]
    mod_cols = jnp.stack([matmul(f"ada{l}", c_all, ada_w[l], "nn", F32) for l in range(DEPTH)])
    mod_cols = mod_cols + lax.dynamic_slice_in_dim(ada_b, me * cols, cols, axis=1)[:, None, :]
    mod_all = all_gather("gather_mod", mod_cols.reshape(DEPTH * N_DEV, cols))
    mod_all = mod_all.reshape(N_DEV, DEPTH, N_DEV, cols)
    mod = lax.dynamic_index_in_dim(mod_all, me, axis=2, keepdims=False)
    mod = mod.transpose(1, 0, 2).reshape(DEPTH, N_ADA * D)

    gathered = all_gather_many("gather_w", [args[n].astype(BF16) for n in big_names])

    small = {n: args[n] for n in small_names if n != "ada_b"}
    loss, (dx, dgathered, dsmall, dmod) = jax.value_and_grad(_local_loss, argnums=(0, 1, 2, 3))(
        x[0], gathered, small, mod, loss_target[0])

    part = _pack([dmod] + [dsmall[n] for n in small_names[1:]], SMALL_ROWS)
    parts = all_gather("gather_small", part)
    sm_out = adamw("adamw_small", parts, _pack([args[n] for n in small_names], SMALL_ROWS),
                   _pack([args["m_" + n] for n in small_names], SMALL_ROWS),
                   _pack([args["v_" + n] for n in small_names], SMALL_ROWS))

    dmod_all = parts.reshape(N_DEV, -1)[:, :DEPTH * N_ADA * D].reshape(N_DEV, DEPTH, N_ADA * D)
    dmod_mine = lax.dynamic_slice_in_dim(dmod_all, me * cols, cols, axis=2)
    g_ada = jnp.stack([matmul(f"ada{l}_dw", c_all, dmod_mine[:, l], "tn", F32) for l in range(DEPTH)])
    flat2 = lambda t: t.reshape(-1, t.shape[-1])
    ada_out = adamw("adamw_ada_w", flat2(g_ada), flat2(ada_w), flat2(m_ada_w), flat2(v_ada_w))

    my_core = lax.axis_index("c")
    rows3 = lambda t: t.reshape(t.shape[0], -1, t.shape[-1])
    from_sibling = sibling_exchange("pair_grads", dgathered)
    pair = []
    for n, t, got in zip(big_names, dgathered, from_sibling):
        own = lax.dynamic_index_in_dim(t.reshape((4, 2) + t.shape[1:]), my_core, axis=1, keepdims=False)
        pair.append(pair_sum("pair_sum_" + n, rows3(own), rows3(got)))
    landed = chip_exchange("scatter_grads", pair)
    big_out = {}
    for n, t in zip(big_names, landed):
        big_out[n] = adamw("adamw_" + n, t, flat2(args[n]), flat2(args["m_" + n]), flat2(args["v_" + n]))

    small_shapes = [shp for _, shp in SMALL]
    names = ["ada_w", "ada_b", "ln_ffn1", "ln_mix", "ln_ffn2", "ffn1_wg", "ffn1_wu", "ffn1_wd", "w_in",
             "conv_w", "a_log", "dt_bias", "dn_norm", "w_a", "w_b", "w_o", "ffn2_wg", "ffn2_wu",
             "ffn2_wd", "final_norm"]
    outs = [lax.psum(loss, ("x", "y", "c")), dx[None]]
    for kind in range(4):
        table = {n: big_out[n][kind].reshape(args[n].shape) for n in big_names}
        table.update(zip(small_names, _unpack(sm_out[kind], small_shapes)))
        table["ada_w"] = ada_out[kind].reshape(ada_w.shape)
        outs += [table[n] for n in names]
    return tuple(outs)
```

```python
import functools
import math

import numpy as np
import jax
import jax.numpy as jnp
from jax import lax
from jax.experimental import pallas as pl
from jax.experimental.pallas import tpu as pltpu

F32 = jnp.float32
BF16 = jnp.bfloat16

D = 1024
SEQ = 4096
DEPTH = 2
N_DEV = 8
DN_HEADS = 8
DN_DIM = 128
DN_CHUNK = 64
DN_CONV = 4
DA_HEADS = 12
DA_DIM = 64
DA_BLOCK = 128
DA_DILATIONS = (1, 4, 16)
ALIBI_MAX_EXP = 8.0
D_FF = 2816
N_ADA = 9
NORM_EPS = 1e-6
IN_COLS = 8464
IN_COLS_PAD = 8704
ADAM_LR, ADAM_B1, ADAM_B2, ADAM_EPS, ADAM_WD, ADAM_STEP = 0.001, 0.9, 0.999, 1e-08, 0.01, 10
NEG = -1e30

VMEM_LIMIT = 56 * 1024 * 1024
LANES = 128

MESH = pl.DeviceIdType.MESH


def _cparams(n_grid):
    return pltpu.CompilerParams(dimension_semantics=("arbitrary",) * n_grid,
                                vmem_limit_bytes=VMEM_LIMIT)


def blockwise(name, f, grid, ins, outs):
    n_in, n_out = len(ins), len(outs)
    diff = [i for i, (_, _, kind) in enumerate(ins) if kind != "const"]

    def apply(*vals):
        res = f(*vals)
        return tuple(r.astype(dt) for r, (_, dt, _, _) in zip(res, outs))

    def fwd_call(*arrays):
        def body(*refs):
            res = apply(*[r[...] for r in refs[:n_in]])
            for r, v in zip(refs[n_in:], res):
                r[...] = v

        return pl.pallas_call(
            body, name=name + "_fwd", grid=grid,
            in_specs=[pl.BlockSpec(b, im) for (b, im, _) in ins],
            out_specs=[pl.BlockSpec(b, im) for (_, _, b, im) in outs],
            out_shape=[jax.ShapeDtypeStruct(s, dt) for (s, dt, _, _) in outs],
            compiler_params=_cparams(len(grid)),
        )(*arrays)

    def bwd_call(arrays, cts):
        def body(*refs):
            in_refs, ct_refs = refs[:n_in], refs[n_in:n_in + n_out]
            g_refs = refs[n_in + n_out:]
            vals = [r[...] for r in in_refs]

            def fd(*dvals):
                full = list(vals)
                for i, v in zip(diff, dvals):
                    full[i] = v
                return apply(*full)

            _, vjp = jax.vjp(fd, *[vals[i] for i in diff])
            grads = vjp(tuple(r[...] for r in ct_refs))
            first = functools.reduce(jnp.logical_and,
                                     [pl.program_id(a) == 0 for a in range(len(grid))])
            for g_ref, g, i in zip(g_refs, grads, diff):
                if ins[i][2] == "acc":
                    @pl.when(first)
                    def _(g_ref=g_ref):
                        g_ref[...] = jnp.zeros_like(g_ref)
                    g_ref[...] += g.astype(F32)
                else:
                    g_ref[...] = g.astype(g_ref.dtype)

        g_shapes = [jax.ShapeDtypeStruct(arrays[i].shape,
                                         F32 if ins[i][2] == "acc" else arrays[i].dtype)
                    for i in diff]
        return pl.pallas_call(
            body, name=name + "_bwd", grid=grid,
            in_specs=([pl.BlockSpec(b, im) for (b, im, _) in ins]
                      + [pl.BlockSpec(b, im) for (_, _, b, im) in outs]),
            out_specs=[pl.BlockSpec(ins[i][0], ins[i][1]) for i in diff],
            out_shape=g_shapes,
            compiler_params=_cparams(len(grid)),
        )(*arrays, *cts)

    @jax.custom_vjp
    def op(*arrays):
        return tuple(fwd_call(*arrays))

    def op_fwd(*arrays):
        return tuple(fwd_call(*arrays)), arrays

    def op_bwd(arrays, cts):
        grads = bwd_call(arrays, cts)
        full = [None] * n_in
        for i, g in zip(diff, grads):
            full[i] = g.astype(arrays[i].dtype)
        return tuple(full)

    op.defvjp(op_fwd, op_bwd)
    return op


def _pick(n, cands):
    for c in cands:
        if n % c == 0:
            return c
    return n


def matmul(name, a, b, form, out_dtype):
    if form == "nn":
        (m, k), (_, n) = a.shape, b.shape
    elif form == "nt":
        (m, k), (n, _) = a.shape, b.shape
    else:
        (k, m), (_, n) = a.shape, b.shape
    tm = _pick(m, (1408, 1024, 512, 256, 128, 8))
    tn = _pick(n, (1408, 1024, 512, 384, 256, 128))
    tk = _pick(k, (1024, 1408, 512, 384, 256, 128, 8))
    nk = k // tk
    a_spec = (pl.BlockSpec((tk, tm), lambda i, j, kk: (kk, i)) if form == "tn"
              else pl.BlockSpec((tm, tk), lambda i, j, kk: (i, kk)))
    b_spec = (pl.BlockSpec((tn, tk), lambda i, j, kk: (j, kk)) if form == "nt"
              else pl.BlockSpec((tk, tn), lambda i, j, kk: (kk, j)))
    dims = {"nn": (((1,), (0,)), ((), ())), "nt": (((1,), (1,)), ((), ())),
            "tn": (((0,), (0,)), ((), ()))}[form]

    def body(a_ref, b_ref, o_ref, acc_ref):
        kk = pl.program_id(2)
        part = lax.dot_general(a_ref[...].astype(BF16), b_ref[...].astype(BF16), dims,
                               preferred_element_type=F32)

        @pl.when(kk == 0)
        def _():
            acc_ref[...] = part

        @pl.when(kk > 0)
        def _():
            acc_ref[...] += part

        @pl.when(kk == nk - 1)
        def _():
            o_ref[...] = acc_ref[...].astype(o_ref.dtype)

    return pl.pallas_call(
        body, name=name, grid=(m // tm, n // tn, nk),
        in_specs=[a_spec, b_spec],
        out_specs=pl.BlockSpec((tm, tn), lambda i, j, kk: (i, j)),
        out_shape=jax.ShapeDtypeStruct((m, n), out_dtype),
        scratch_shapes=[pltpu.VMEM((tm, tn), F32)],
        compiler_params=_cparams(3),
    )(a, b)


def linear(name, x, w, out_dtype):
    @jax.custom_vjp
    def op(x, w):
        return matmul(name + "_y", x, w, "nn", out_dtype)

    def op_fwd(x, w):
        return op(x, w), (x, w)

    def op_bwd(res, dy):
        x, w = res
        dx = matmul(name + "_dx", dy, w, "nt", x.dtype)
        dw = matmul(name + "_dw", x, dy, "tn", w.dtype)
        return dx, dw

    op.defvjp(op_fwd, op_bwd)
    return op(x, w)


SPLIT_TILE = 256


def linear_split(name, x, w, seg_tiles, out_dtype):
    m, k = x.shape
    n = w.shape[1]
    nt = n // SPLIT_TILE
    starts = [sum(seg_tiles[:s]) for s in range(len(seg_tiles))]
    assert sum(seg_tiles) == nt
    tm = _pick(m, (1024, 512, 256, 128, 8))

    def inside(j, s):
        return jnp.logical_and(j >= starts[s], j < starts[s] + seg_tiles[s])

    def local(j, s):
        return jnp.clip(j - starts[s], 0, seg_tiles[s] - 1)

    def fwd_call(x, w):
        def body(x_ref, w_ref, *o_refs):
            j = pl.program_id(1)
            y = jnp.dot(x_ref[...].astype(BF16), w_ref[...].astype(BF16), preferred_element_type=F32)
            for s, o_ref in enumerate(o_refs):
                @pl.when(inside(j, s))
                def _(o_ref=o_ref):
                    o_ref[...] = y.astype(o_ref.dtype)

        return pl.pallas_call(
            body, name=name + "_y", grid=(m // tm, nt),
            in_specs=[pl.BlockSpec((tm, k), lambda i, j: (i, 0)),
                      pl.BlockSpec((k, SPLIT_TILE), lambda i, j: (0, j))],
            out_specs=[pl.BlockSpec((tm, SPLIT_TILE), lambda i, j, s=s: (i, local(j, s)))
                       for s in range(len(seg_tiles))],
            out_shape=[jax.ShapeDtypeStruct((m, t * SPLIT_TILE), out_dtype) for t in seg_tiles],
            compiler_params=_cparams(2),
        )(x, w)

    def dx_call(dys, w):
        def body(*refs):
            dy_refs, w_ref, o_ref, acc_ref = refs[:-3], refs[-3], refs[-2], refs[-1]
            j = pl.program_id(1)
            for s, dy_ref in enumerate(dy_refs):
                @pl.when(inside(j, s))
                def _(dy_ref=dy_ref):
                    part = lax.dot_general(dy_ref[...].astype(BF16), w_ref[...].astype(BF16),
                                           (((1,), (1,)), ((), ())), preferred_element_type=F32)

                    @pl.when(j == 0)
                    def _():
                        acc_ref[...] = part

                    @pl.when(j > 0)
                    def _():
                        acc_ref[...] += part

            @pl.when(j == nt - 1)
            def _():
                o_ref[...] = acc_ref[...].astype(o_ref.dtype)

        return pl.pallas_call(
            body, name=name + "_dx", grid=(m // tm, nt),
            in_specs=[pl.BlockSpec((tm, SPLIT_TILE), lambda i, j, s=s: (i, local(j, s)))
                      for s in range(len(seg_tiles))]
            + [pl.BlockSpec((k, SPLIT_TILE), lambda i, j: (0, j))],
            out_specs=pl.BlockSpec((tm, k), lambda i, j: (i, 0)),
            out_shape=jax.ShapeDtypeStruct((m, k), x.dtype),
            scratch_shapes=[pltpu.VMEM((tm, k), F32)],
            compiler_params=_cparams(2),
        )(*dys, w)

    def dw_call(x, dys):
        nm = m // tm

        def body(*refs):
            x_ref, dy_refs, o_ref, acc_ref = refs[0], refs[1:-2], refs[-2], refs[-1]
            j, kk = pl.program_id(0), pl.program_id(1)
            for s, dy_ref in enumerate(dy_refs):
                @pl.when(inside(j, s))
                def _(dy_ref=dy_ref):
                    rows = x_ref[pl.ds(pl.multiple_of(kk * tm, tm), tm), :]
                    part = lax.dot_general(rows.astype(BF16), dy_ref[...].astype(BF16),
                                           (((0,), (0,)), ((), ())), preferred_element_type=F32)

                    @pl.when(kk == 0)
                    def _():
                        acc_ref[...] = part

                    @pl.when(kk > 0)
                    def _():
                        acc_ref[...] += part

            @pl.when(kk == nm - 1)
            def _():
                o_ref[...] = acc_ref[...].astype(o_ref.dtype)

        return pl.pallas_call(
            body, name=name + "_dw", grid=(nt, nm),
            in_specs=[pl.BlockSpec((m, k), lambda j, kk: (0, 0))]
            + [pl.BlockSpec((tm, SPLIT_TILE),
                            lambda j, kk, s=s: (jnp.where(inside(j, s), kk, 0), local(j, s)))
               for s in range(len(seg_tiles))],
            out_specs=pl.BlockSpec((k, SPLIT_TILE), lambda j, kk: (0, j)),
            out_shape=jax.ShapeDtypeStruct((k, n), w.dtype),
            scratch_shapes=[pltpu.VMEM((k, SPLIT_TILE), F32)],
            compiler_params=_cparams(2),
        )(x, *dys)

    @jax.custom_vjp
    def op(x, w):
        return tuple(fwd_call(x, w))

    def op_fwd(x, w):
        return op(x, w), (x, w)

    def op_bwd(res, dys):
        x, w = res
        return dx_call(dys, w), dw_call(x, dys)

    op.defvjp(op_fwd, op_bwd)
    return op(x, w)


@jax.custom_vjp
def _sigmoid(x):
    return 1.0 / (1.0 + jnp.exp(-x))


def _sigmoid_fwd(x):
    s = _sigmoid(x)
    return s, s


def _sigmoid_bwd(s, ct):
    return (ct * (s * (1.0 - s)),)


_sigmoid.defvjp(_sigmoid_fwd, _sigmoid_bwd)


@jax.custom_vjp
def _silu(x):
    return x * _sigmoid(x)


def _silu_fwd(x):
    s = _sigmoid(x)
    return x * s, (x, s)


def _silu_bwd(res, ct):
    x, s = res
    return (ct * (s * (1.0 + x * (1.0 - s))),)


_silu.defvjp(_silu_fwd, _silu_bwd)


def _softplus(x):
    return jnp.maximum(x, 0.0) + jnp.log(1.0 + jnp.exp(-jnp.abs(x)))


def _rms(x):
    return x * lax.rsqrt(jnp.mean(x * x, axis=-1, keepdims=True) + NORM_EPS)


ROW_TILE = 512


def _row(i):
    return (i, 0)


def _fixed(*_):
    return (0, 0)


def norm_mod(name, h, ln, shift, scale):
    s, d = h.shape

    def f(h, ln, sh, sc):
        return ((_rms(h) * ln) * (1.0 + sc) + sh,)

    op = blockwise(name, f, (s // ROW_TILE,),
                   [((ROW_TILE, d), _row, "tile")] + [((1, d), _fixed, "acc")] * 3,
                   [((s, d), BF16, (ROW_TILE, d), _row)])
    return op(h, ln, shift, scale)[0]


FF_TILE = 1408


def _interleave_gate_up(wg, wu):
    parts = []
    for j in range(wg.shape[1] // FF_TILE):
        parts += [wg[:, j * FF_TILE:(j + 1) * FF_TILE], wu[:, j * FF_TILE:(j + 1) * FF_TILE]]
    return jnp.concatenate(parts, axis=1)


def swiglu_act(name, gu):
    s, f2 = gu.shape
    ff = f2 // 2

    def f(gu):
        g, u = gu[:, :FF_TILE].astype(F32), gu[:, FF_TILE:].astype(F32)
        return (_silu(g) * u,)

    op = blockwise(name, f, (s // ROW_TILE, ff // FF_TILE),
                   [((ROW_TILE, 2 * FF_TILE), lambda i, j: (i, j), "tile")],
                   [((s, ff), BF16, (ROW_TILE, FF_TILE), lambda i, j: (i, j))])
    return op(gu)[0]


def residual(name, h, y, gate, weight):
    s, d = h.shape

    def f(h, y, gate):
        return (h + (weight * gate) * y,)

    op = blockwise(name, f, (s // ROW_TILE,),
                   [((ROW_TILE, d), _row, "tile"), ((ROW_TILE, d), _row, "tile"),
                    ((1, d), _fixed, "acc")],
                   [((s, d), F32, (ROW_TILE, d), _row)])
    return op(h, y, gate)[0]


def merge_gates(name, ga, gb, ya, yb):
    s, d = ya.shape

    def f(ga, gb, ya, yb):
        return (_sigmoid(ga) * ya + _sigmoid(gb) * yb,)

    op = blockwise(name, f, (s // ROW_TILE,), [((ROW_TILE, d), _row, "tile")] * 4,
                   [((s, d), BF16, (ROW_TILE, d), _row)])
    return op(ga, gb, ya, yb)[0]


def loss_rows(name, h, g, target):
    s, d = h.shape

    def f(h, g, t):
        err = _rms(h) * g - t
        return (jnp.mean(err * err, axis=-1, keepdims=True),)

    op = blockwise(name, f, (s // ROW_TILE,),
                   [((ROW_TILE, d), _row, "tile"), ((1, d), _fixed, "acc"),
                    ((ROW_TILE, d), _row, "const")],
                   [((s, 1), F32, (ROW_TILE, 1), _row)])
    return op(h, g, target)[0]


def decay_beta(name, ab, a_log_pad, dt_bias_pad):
    s, n = ab.shape

    def f(ab, a_log, dt_bias):
        lane = lax.broadcasted_iota(jnp.int32, ab.shape, 1)
        beta = _sigmoid(ab)
        g = -jnp.exp(a_log) * _softplus(ab + dt_bias)
        return (jnp.where(lane < DN_HEADS, beta, jnp.where(lane < 2 * DN_HEADS, g, 0.0)),)

    op = blockwise(name, f, (s // ROW_TILE,),
                   [((ROW_TILE, n), _row, "tile"), ((1, n), _fixed, "acc"), ((1, n), _fixed, "acc")],
                   [((s, n), F32, (ROW_TILE, n), _row)])
    return op(ab, a_log_pad, dt_bias_pad)[0]


def _shift_rows(x, k):
    n = x.shape[0]

    @jax.custom_vjp
    def shift(x):
        row = lax.broadcasted_iota(jnp.int32, x.shape, 0)
        return jnp.where(row >= k, pltpu.roll(x, k, 0), 0.0)

    def shift_fwd(x):
        return shift(x), None

    def shift_bwd(_, g):
        row = lax.broadcasted_iota(jnp.int32, g.shape, 0)
        return (jnp.where(row < n - k, pltpu.roll(g, n - k, 0), 0.0),)

    shift.defvjp(shift_fwd, shift_bwd)
    return shift(x)


def conv_heads(name, x, w, mode):
    s, width = x.shape
    nh = width // LANES

    def f(x, w):
        y = w[DN_CONV - 1] * x
        for j in range(DN_CONV - 1):
            y = y + w[j] * _shift_rows(x, DN_CONV - 1 - j)
        y = _silu(y)
        if mode != "v":
            y = y * lax.rsqrt(jnp.sum(y * y, axis=-1, keepdims=True) + NORM_EPS)
        if mode == "q":
            y = y * (DN_DIM ** -0.5)
        return (y[None],)

    op = blockwise(name, f, (nh,),
                   [((s, LANES), lambda j: (0, j), "tile"),
                    ((DN_CONV, 1, LANES), lambda j: (0, 0, j), "tile")],
                   [((nh, s, LANES), F32, (1, s, LANES), lambda j: (j, 0, 0))])
    return op(x, w)[0]


def gated_head_norm(name, o, z, w):
    nh, s, dh = o.shape

    def f(o, z, w):
        return (_rms(o[0]) * w * _silu(z),)

    op = blockwise(name, f, (s // ROW_TILE, nh),
                   [((1, ROW_TILE, dh), lambda i, h: (h, i, 0), "tile"),
                    ((ROW_TILE, dh), lambda i, h: (i, h), "tile"),
                    ((1, dh), lambda i, h: (0, 0), "acc")],
                   [((s, nh * dh), BF16, (ROW_TILE, dh), lambda i, h: (i, h))])
    return op(o, z, w)[0]


def _mm(a, b, ca, cb):
    return _bmm(a[None], b[None], ca + 1, cb + 1)[0]


def dilated_attention(name, q, k, v, dilation):
    s, width = q.shape
    nblk = s // DA_BLOCK
    per_sub = nblk // dilation
    slopes = [dilation * 2.0 ** (-ALIBI_MAX_EXP * (h + 1) / DA_HEADS) for h in range(DA_HEADS)]

    def f(q, kp, kc, vp, vc):
        first = (pl.program_id(0) % per_sub) == 0
        qi = lax.broadcasted_iota(jnp.int32, (DA_BLOCK, 2 * DA_BLOCK), 0)
        ki = lax.broadcasted_iota(jnp.int32, (DA_BLOCK, 2 * DA_BLOCK), 1)
        steps = (qi + DA_BLOCK - ki).astype(F32)
        lowest = qi + first.astype(jnp.int32) * (DA_BLOCK - qi)
        valid = jnp.logical_and(ki >= lowest, ki <= qi + DA_BLOCK)
        top = lax.broadcasted_iota(jnp.int32, (DA_BLOCK, LANES), 1) < DA_DIM
        o_parts, lse_parts = [], []
        for pair in range(width // LANES):
            cols = slice(pair * LANES, (pair + 1) * LANES)
            q2 = jnp.concatenate([jnp.where(top, q[:, cols], 0.0), jnp.where(top, 0.0, q[:, cols])], axis=0)
            k2 = jnp.concatenate([kp[:, cols], kc[:, cols]], axis=0)
            v2 = jnp.concatenate([vp[:, cols], vc[:, cols]], axis=0)
            bias = jnp.concatenate([jnp.where(valid, -slopes[2 * pair + half] * steps, NEG)
                                    for half in range(2)], axis=0)
            sc = _mm(q2, k2, 1, 1) * (DA_DIM ** -0.5) + bias
            mx = jnp.max(sc, axis=-1, keepdims=True)
            p = jnp.exp(sc - mx)
            l = jnp.sum(p, axis=-1, keepdims=True)
            o2 = _mm(p / l, v2, 1, 0)
            lse2 = mx + jnp.log(l)
            o_parts.append(jnp.where(top, o2[:DA_BLOCK], o2[DA_BLOCK:]))
            lse_parts.append(jnp.where(top, lse2[:DA_BLOCK], lse2[DA_BLOCK:]))
        return jnp.concatenate(o_parts, axis=1), jnp.concatenate(lse_parts, axis=1)

    blk = (DA_BLOCK, width)
    cur_map = lambda j: (j, 0)
    prev_map = lambda j: (jnp.maximum(j - 1, 0), 0)
    op = blockwise(
        name, f, (nblk,),
        [(blk, cur_map, "tile"), (blk, prev_map, "tile"), (blk, cur_map, "tile"),
         (blk, prev_map, "tile"), (blk, cur_map, "tile")],
        [((s, width), F32, blk, cur_map), ((s, width), F32, blk, cur_map)])

    @jax.custom_vjp
    def attn(q, k, v):
        return op(q, k, k, v, v)

    def attn_fwd(q, k, v):
        return attn(q, k, v), (q, k, v)

    def attn_bwd(res, cts):
        q, k, v = res
        _, vjp = jax.vjp(op, q, k, k, v, v)
        dq, dkp, dkc, dvp, dvc = vjp(cts)
        fill = lambda t: t.at[s - DA_BLOCK:].set(0.0)
        return dq, dkc + fill(dkp), dvc + fill(dvp)

    attn.defvjp(attn_fwd, attn_bwd)
    return attn(q, k, v)


def combine_patterns(name, outs, lses):
    s, width = outs[0].shape
    n = len(outs)

    def f(*vals):
        o, lse = vals[:n], vals[n:]
        mx = functools.reduce(jnp.maximum, lse)
        e = [jnp.exp(t - mx) for t in lse]
        return (sum(ei * oi for ei, oi in zip(e, o)) / sum(e),)

    tile = ROW_TILE // 2
    op = blockwise(name, f, (s // tile,), [((tile, width), _row, "tile")] * (2 * n),
                   [((s, width), BF16, (tile, width), _row)])
    return op(*outs, *lses)[0]


def _sub_order(t, r):
    if r == 1:
        return t
    s, c = t.shape
    return t.reshape(s // r, r, c).transpose(1, 0, 2).reshape(s, c)


def _seq_order(t, r):
    if r == 1:
        return t
    s, c = t.shape
    return t.reshape(r, s // r, c).transpose(1, 0, 2).reshape(s, c)


def _raw_bmm(a, b, ca, cb):
    return lax.dot_general(a.astype(BF16), b.astype(BF16), (((ca,), (cb,)), ((0,), (0,))),
                           preferred_element_type=F32)


def _split(a):
    hi = a.astype(BF16)
    return hi, (a - hi.astype(F32)).astype(BF16)


def _passes_bmm(a, b, ca, cb, passes):
    if passes == 1:
        return _raw_bmm(a, b, ca, cb)
    (a_hi, a_lo), (b_hi, b_lo) = _split(a), _split(b)
    return _raw_bmm(a_hi, b_hi, ca, cb) + (_raw_bmm(a_hi, b_lo, ca, cb) + _raw_bmm(a_lo, b_hi, ca, cb))


def _bmm(a, b, ca, cb, passes=1):
    fa, fb = 3 - ca, 3 - cb

    @jax.custom_vjp
    def mm(a, b):
        return _passes_bmm(a, b, ca, cb, passes)

    def mm_fwd(a, b):
        return mm(a, b), (a, b)

    def mm_bwd(res, ct):
        a, b = res
        da = (_passes_bmm(ct, b, 2, fb, passes) if ca == 2
              else _passes_bmm(b, ct, fb, 2, passes))
        db = (_passes_bmm(a, ct, fa, 1, passes) if cb == 1
              else _passes_bmm(ct, a, 1, fa, passes))
        return da, db

    mm.defvjp(mm_fwd, mm_bwd)
    return mm(a, b)


def _delta_chunk(q, k, v, gcol, grow, bcol, state):
    c = q.shape[1]
    ii = lax.broadcasted_iota(jnp.int32, (1, c, c), 1)
    jj = lax.broadcasted_iota(jnp.int32, (1, c, c), 2)
    incl, strict = ii >= jj, ii > jj
    gc_col = jnp.sum(jnp.where(incl, grow, 0.0), axis=2, keepdims=True)
    gc_row = jnp.sum(jnp.where(ii <= jj, gcol, 0.0), axis=1, keepdims=True)
    decay = jnp.where(incl, jnp.exp(jnp.where(incl, gc_col - gc_row, 0.0)), 0.0)
    kb, vb = k * bcol, v * bcol
    m = jnp.where(strict, _bmm(kb, k, 2, 2) * decay, 0.0)
    eye = (ii == jj).astype(F32)
    p = -m
    inv = eye + p
    for _ in range(int(math.log2(c)) - 1):
        p = _bmm(p, p, 2, 1, 3)
        inv = inv + _bmm(inv, p, 2, 1, 3)
    e_col = jnp.exp(gc_col)
    u = _bmm(inv, vb, 2, 1)
    w = _bmm(inv, kb * e_col, 2, 1)
    qk = _bmm(q, k, 2, 2) * decay
    v_new = u - _bmm(w, state, 2, 1)
    o = _bmm(q * e_col, state, 2, 1) + _bmm(qk, v_new, 2, 1)
    g_last = jnp.sum(grow, axis=2, keepdims=True)
    new_state = state * jnp.exp(g_last) + _bmm(k * jnp.exp(g_last - gc_col), v_new, 1, 1)
    return o, new_state


def _delta_chunk_packed(q, k, v, gb, state):
    nh, c = q.shape[0], q.shape[1]
    lane = lax.broadcasted_iota(jnp.int32, gb.shape, 1)
    eye = (lax.broadcasted_iota(jnp.int32, (c, c), 0) == lax.broadcasted_iota(jnp.int32, (c, c), 1))
    column = lambda l: jnp.sum(jnp.where(lane == l, gb, 0.0), axis=1, keepdims=True)
    heads = lambda parts: jnp.concatenate([t[None] for t in parts], axis=0)
    bcol = heads([column(h) for h in range(nh)])
    gcols = [column(nh + h) for h in range(nh)]
    gcol = heads(gcols)
    grow = heads([jnp.sum(jnp.where(eye, g, 0.0), axis=0, keepdims=True) for g in gcols])
    return _delta_chunk(q, k, v, gcol, grow, bcol, state)


def delta_rule(name, q, k, v, gb):
    nh, s, dh = q.shape
    c = DN_CHUNK
    n = s // c

    def specs(rev):
        t = (lambda i: n - 1 - i) if rev else (lambda i: i)
        seq = pl.BlockSpec((nh, c, dh), lambda i: (0, t(i), 0))
        gate = pl.BlockSpec((c, LANES), lambda i: (t(i), 0))
        st = pl.BlockSpec((nh, 1, dh, dh), lambda i: (0, t(i), 0, 0))
        return seq, gate, st

    def fwd_call(q, k, v, gb):
        seq, gate, st = specs(False)

        def body(q_ref, k_ref, v_ref, gb_ref, o_ref, st_ref, state):
            @pl.when(pl.program_id(0) == 0)
            def _():
                state[...] = jnp.zeros_like(state)

            st_ref[:, 0] = state[...]
            o, new_state = _delta_chunk_packed(q_ref[...], k_ref[...], v_ref[...], gb_ref[...],
                                               state[...])
            o_ref[...] = o
            state[...] = new_state

        return pl.pallas_call(
            body, name=name + "_fwd", grid=(n,),
            in_specs=[seq, seq, seq, gate],
            out_specs=[seq, st],
            out_shape=[jax.ShapeDtypeStruct((nh, s, dh), F32),
                       jax.ShapeDtypeStruct((nh, n, dh, dh), F32)],
            scratch_shapes=[pltpu.VMEM((nh, dh, dh), F32)],
            compiler_params=_cparams(1),
        )(q, k, v, gb)

    def bwd_call(q, k, v, gb, states, do):
        seq, gate, st = specs(True)

        def body(q_ref, k_ref, v_ref, gb_ref, st_ref, do_ref,
                 dq_ref, dk_ref, dv_ref, dgb_ref, dstate):
            @pl.when(pl.program_id(0) == 0)
            def _():
                dstate[...] = jnp.zeros_like(dstate)

            _, vjp = jax.vjp(_delta_chunk_packed, q_ref[...], k_ref[...], v_ref[...], gb_ref[...],
                             st_ref[:, 0])
            dq, dk, dv, dgb, dst = vjp((do_ref[...], dstate[...]))
            dq_ref[...] = dq
            dk_ref[...] = dk
            dv_ref[...] = dv
            dgb_ref[...] = dgb
            dstate[...] = dst

        return pl.pallas_call(
            body, name=name + "_bwd", grid=(n,),
            in_specs=[seq, seq, seq, gate, st, seq],
            out_specs=[seq, seq, seq, gate],
            out_shape=[jax.ShapeDtypeStruct((nh, s, dh), F32)] * 3
            + [jax.ShapeDtypeStruct((s, LANES), F32)],
            scratch_shapes=[pltpu.VMEM((nh, dh, dh), F32)],
            compiler_params=_cparams(1),
        )(q, k, v, gb, states, do)

    @jax.custom_vjp
    def op(q, k, v, gb):
        return fwd_call(q, k, v, gb)[0]

    def op_fwd(q, k, v, gb):
        o, states = fwd_call(q, k, v, gb)
        return o, (q, k, v, gb, states)

    def op_bwd(res, do):
        return tuple(bwd_call(*res, do))

    op.defvjp(op_fwd, op_bwd)
    return op(q, k, v, gb)


def _my_place():
    return lax.axis_index("x"), lax.axis_index("y"), lax.axis_index("c")


def all_gather(name, shard):
    return all_gather_many(name, [shard])[0]


def all_gather_many(name, shards):
    n = len(shards)

    def body(*refs):
        x_refs, out_refs = refs[:n], refs[n:2 * n]
        send_sems, recv_sems, local_sems = refs[2 * n:]
        x, y, c = _my_place()
        me, sibling = (x, y, c), (x, y, 1 - c)
        chips = [(1 - x, y), (x, 1 - y), (1 - x, 1 - y)]

        def copy(i, k, block, to, own=False):
            px, py, pc = block
            slot = out_refs[i].at[4 * px + 2 * py + pc]
            return pltpu.make_async_remote_copy(
                src_ref=x_refs[i] if own else slot, dst_ref=slot,
                send_sem=send_sems.at[7 * i + k], recv_sem=recv_sems.at[7 * i + k],
                device_id=to, device_id_type=MESH)

        mine = [pltpu.make_async_copy(x_refs[i], out_refs[i].at[4 * x + 2 * y + c], local_sems.at[i])
                for i in range(n)]
        for cp in mine:
            cp.start()
        first = []
        for i in range(n):
            first.append(copy(i, 0, me, sibling, own=True))
            first += [copy(i, 1 + j, me, (*chip, c), own=True) for j, chip in enumerate(chips)]
        for cp in first:
            cp.start()
        passed = []
        for j, chip in enumerate(chips):
            for i in range(n):
                copy(i, 1 + j, (*chip, c), me).wait_recv()
                passed.append(copy(i, 4 + j, (*chip, c), sibling))
                passed[-1].start()
        for i in range(n):
            copy(i, 0, sibling, me).wait_recv()
        for j, chip in enumerate(chips):
            for i in range(n):
                copy(i, 4 + j, (*chip, 1 - c), me).wait_recv()
        for cp in first + passed:
            cp.wait_send()
        for cp in mine:
            cp.wait()

    any_spec = pl.BlockSpec(memory_space=pl.ANY)
    return pl.pallas_call(
        body, name=name,
        out_shape=[jax.ShapeDtypeStruct((N_DEV,) + t.shape, t.dtype) for t in shards],
        in_specs=[any_spec] * n, out_specs=[any_spec] * n,
        scratch_shapes=[pltpu.SemaphoreType.DMA((7 * n,)), pltpu.SemaphoreType.DMA((7 * n,)),
                        pltpu.SemaphoreType.DMA((n,))],
    )(*shards)


def sibling_exchange(name, parts):
    n = len(parts)

    def body(*refs):
        p_refs, out_refs = refs[:n], refs[n:2 * n]
        send_sems, recv_sems = refs[2 * n:]
        x, y, c = _my_place()
        copies = []
        for i in range(n):
            for chip in range(4):
                copies.append(pltpu.make_async_remote_copy(
                    src_ref=p_refs[i].at[2 * chip + (1 - c)], dst_ref=out_refs[i].at[chip],
                    send_sem=send_sems.at[4 * i + chip], recv_sem=recv_sems.at[4 * i + chip],
                    device_id=(x, y, 1 - c), device_id_type=MESH))
        for cp in copies:
            cp.start()
        for cp in copies:
            cp.wait_recv()
        for cp in copies:
            cp.wait_send()

    any_spec = pl.BlockSpec(memory_space=pl.ANY)
    return pl.pallas_call(
        body, name=name,
        out_shape=[jax.ShapeDtypeStruct((4,) + t.shape[1:], t.dtype) for t in parts],
        in_specs=[any_spec] * n, out_specs=[any_spec] * n,
        scratch_shapes=[pltpu.SemaphoreType.DMA((4 * n,)), pltpu.SemaphoreType.DMA((4 * n,))],
    )(*parts)


def chip_exchange(name, parts):
    n = len(parts)

    def body(*refs):
        p_refs, out_refs = refs[:n], refs[n:2 * n]
        send_sems, recv_sems, local_sems = refs[2 * n:]
        x, y, c = _my_place()
        me = 2 * x + y
        peers = [(1 - x, y), (x, 1 - y), (1 - x, 1 - y)]

        def copy(i, j, landing):
            px, py = peers[j]
            return pltpu.make_async_remote_copy(
                src_ref=p_refs[i].at[2 * px + py],
                dst_ref=out_refs[i].at[(2 * px + py) if landing else me],
                send_sem=send_sems.at[3 * i + j], recv_sem=recv_sems.at[3 * i + j],
                device_id=(px, py, c), device_id_type=MESH)

        mine = [pltpu.make_async_copy(p_refs[i].at[me], out_refs[i].at[me], local_sems.at[i])
                for i in range(n)]
        for cp in mine:
            cp.start()
        copies = [copy(i, j, False) for j in range(3) for i in range(n)]
        for cp in copies:
            cp.start()
        for j in range(3):
            for i in range(n):
                copy(i, j, True).wait_recv()
        for cp in copies:
            cp.wait_send()
        for cp in mine:
            cp.wait()

    any_spec = pl.BlockSpec(memory_space=pl.ANY)
    return pl.pallas_call(
        body, name=name,
        out_shape=[jax.ShapeDtypeStruct(t.shape, t.dtype) for t in parts],
        in_specs=[any_spec] * n, out_specs=[any_spec] * n,
        scratch_shapes=[pltpu.SemaphoreType.DMA((3 * n,)), pltpu.SemaphoreType.DMA((3 * n,)),
                        pltpu.SemaphoreType.DMA((n,))],
    )(*parts)


def pair_sum(name, mine, got, core):
    _, rows, cols = got.shape
    row_bytes = 4 * LANES * (-(-cols // LANES))
    tile = _pick(rows, [t for t in (512, 352, 256, 128, 64, 32, 16, 8)
                        if t * row_bytes <= ADAMW_BLOCK_BYTES])

    def body(core_ref, a_ref, b_ref, o_ref):
        o_ref[...] = (a_ref[...].astype(F32) + b_ref[...].astype(F32)).astype(o_ref.dtype)

    spec = pl.BlockSpec((4, tile, cols), lambda i, core_ref: (0, i, 0))
    return pl.pallas_call(
        body, name=name,
        grid_spec=pltpu.PrefetchScalarGridSpec(
            num_scalar_prefetch=1, grid=(rows // tile,),
            in_specs=[pl.BlockSpec((4, None, tile, cols), lambda i, core_ref: (0, core_ref[0], i, 0)),
                      spec],
            out_specs=spec),
        out_shape=jax.ShapeDtypeStruct(got.shape, got.dtype), compiler_params=_cparams(1),
    )(core, mine, got)


ADAMW_BLOCK_BYTES = 3 * 512 * 1024


def adamw(name, grad, w, m, v):
    rows, cols = w.shape
    stacked = grad.ndim == 3
    row_bytes = 4 * LANES * (-(-cols // LANES))
    tile = _pick(rows, [t for t in (512, 352, 256, 128, 64, 32, 16, 8)
                        if t * row_bytes <= ADAMW_BLOCK_BYTES])

    def body(g_ref, w_ref, m_ref, v_ref, go_ref, d_ref, mo_ref, vo_ref):
        if stacked:
            g = g_ref[0].astype(F32)
            for s in range(1, grad.shape[0]):
                g = g + g_ref[s].astype(F32)
        else:
            g = g_ref[...]
        m = ADAM_B1 * m_ref[...] + (1.0 - ADAM_B1) * g
        v = ADAM_B2 * v_ref[...] + (1.0 - ADAM_B2) * jnp.square(g)
        m_hat = m / (1.0 - ADAM_B1 ** ADAM_STEP)
        v_hat = v / (1.0 - ADAM_B2 ** ADAM_STEP)
        go_ref[...] = g
        d_ref[...] = -ADAM_LR * (m_hat / (jnp.sqrt(v_hat) + ADAM_EPS) + ADAM_WD * w_ref[...])
        mo_ref[...] = m
        vo_ref[...] = v

    flat = pl.BlockSpec((tile, cols), lambda i: (i, 0))
    g_spec = pl.BlockSpec((grad.shape[0], tile, cols), lambda i: (0, i, 0)) if stacked else flat
    return pl.pallas_call(
        body, name=name, grid=(rows // tile,),
        in_specs=[g_spec, flat, flat, flat], out_specs=[flat] * 4,
        out_shape=[jax.ShapeDtypeStruct((rows, cols), F32)] * 4,
        compiler_params=_cparams(1),
    )(grad, w, m, v)


def silu_rows(name, x):
    def body(x_ref, o_ref):
        o_ref[...] = _silu(x_ref[...])

    return pl.pallas_call(body, name=name, out_shape=jax.ShapeDtypeStruct(x.shape, F32))(x)


BIG = (("ffn1_wg", (D, D_FF // N_DEV), 1), ("ffn1_wu", (D, D_FF // N_DEV), 1),
       ("ffn1_wd", (D_FF // N_DEV, D), 0), ("w_in", (D, IN_COLS // N_DEV), 1),
       ("conv_w", (DN_CONV, 3 * D // N_DEV), 1), ("w_a", (D // N_DEV, D), 0),
       ("w_b", (DA_HEADS * DA_DIM, D // N_DEV), 1), ("w_o", (D // N_DEV, D), 0),
       ("ffn2_wg", (D, D_FF // N_DEV), 1), ("ffn2_wu", (D, D_FF // N_DEV), 1),
       ("ffn2_wd", (D_FF // N_DEV, D), 0))
SMALL = (("ada_b", (DEPTH, N_ADA * D)), ("ln_ffn1", (DEPTH, D)), ("ln_mix", (DEPTH, D)),
         ("ln_ffn2", (DEPTH, D)), ("a_log", (DEPTH, DN_HEADS)), ("dt_bias", (DEPTH, DN_HEADS)),
         ("dn_norm", (DEPTH, DN_DIM)), ("final_norm", (D,)))
SMALL_ROWS = 32


def _pack(arrays, rows):
    flat = jnp.concatenate([a.reshape(-1) for a in arrays])
    return jnp.pad(flat, (0, rows * D - flat.shape[0])).reshape(rows, D)


def _unpack(buf, shapes):
    flat = buf.reshape(-1)
    out, off = [], 0
    for shp in shapes:
        n = int(np.prod(shp))
        out.append(flat[off:off + n].reshape(shp))
        off += n
    return out


def _full_weights(gathered):
    full = {}
    for (name, (a, b), axis), t in zip(BIG, gathered):
        if axis == 1:
            full[name] = t.transpose(1, 2, 0, 3).reshape(DEPTH, a, N_DEV * b)
        else:
            full[name] = t.transpose(1, 0, 2, 3).reshape(DEPTH, N_DEV * a, b)
    return full


_Z0, _B0, _A0, _DQ0, _GA0 = 3072, 4096, 4104, 4112, 6416


def _reorder_in_proj(w):
    pad = jnp.zeros((w.shape[0], IN_COLS_PAD - IN_COLS), w.dtype)
    return jnp.concatenate([w[:, :_B0], w[:, _DQ0:], w[:, _B0:_DQ0], pad], axis=1)


def _ffn(tag, h, ln, shift, scale, gate, w_gu, w_d):
    n = norm_mod(tag + "_norm", h, ln, shift, scale)
    gu = linear(tag + "_gu", n, w_gu, BF16)
    a = swiglu_act(tag + "_act", gu)
    f = linear(tag + "_down", a, w_d, F32)
    return residual(tag + "_res", h, f, gate, 0.5)


def _mixer(tag, u, w_in, conv_w, a_log, dt_bias, dn_norm, w_a, w_b, w_o):
    s = u.shape[0]
    (q_pre, k_pre, v_pre, z, da_q, da_k, da_v, gate_a, gate_b, ab) = linear_split(
        tag + "_in", u, w_in, (4, 4, 4, 4, 3, 3, 3, 4, 4, 1), F32)
    qkv_pre, da = (q_pre, k_pre, v_pre), (da_q, da_k, da_v)

    cw = conv_w.astype(F32).reshape(DN_CONV, 1, 3 * D)
    q, k, v = [conv_heads(f"{tag}_conv_{m}", qkv_pre[i], cw[:, :, i * D:(i + 1) * D], m)
               for i, m in enumerate("qkv")]
    pad = lambda t: jnp.pad(t, (DN_HEADS, ab.shape[1] - 2 * DN_HEADS))[None]
    gb = decay_beta(tag + "_decay", ab, pad(a_log), pad(dt_bias))
    o = delta_rule(tag + "_delta", q, k, v, gb[:, :LANES])
    o_a = gated_head_norm(tag + "_gnorm", o, z, dn_norm[None])
    y_a = linear(tag + "_wa", o_a, w_a, F32)

    outs, lses = [], []
    for r in DA_DILATIONS:
        o_r, lse_r = dilated_attention(f"{tag}_attn{r}", *[_sub_order(t, r) for t in da], r)
        outs.append(_seq_order(o_r, r))
        lses.append(_seq_order(lse_r, r))
    o_b = combine_patterns(tag + "_comb", outs, lses)
    y_b = linear(tag + "_wb", o_b, w_b, F32)

    merged = merge_gates(tag + "_merge", gate_a, gate_b, y_a, y_b)
    return linear(tag + "_wo", merged, w_o, F32)


def _local_loss(x, gathered, small, mod, target):
    w = _full_weights(gathered)
    h = x
    for l in range(DEPTH):
        tag = f"l{l}"
        sh1, sc1, gt1, sh2, sc2, gt2, sh3, sc3, gt3 = [mod[l, i * D:(i + 1) * D][None]
                                                       for i in range(N_ADA)]
        w_gu1 = _interleave_gate_up(w["ffn1_wg"][l], w["ffn1_wu"][l])
        w_gu2 = _interleave_gate_up(w["ffn2_wg"][l], w["ffn2_wu"][l])
        h = _ffn(tag + "_ffn1", h, small["ln_ffn1"][l][None], sh1, sc1, gt1, w_gu1, w["ffn1_wd"][l])
        u = norm_mod(tag + "_mixnorm", h, small["ln_mix"][l][None], sh2, sc2)
        m = _mixer(tag + "_mix", u, _reorder_in_proj(w["w_in"][l]), w["conv_w"][l],
                   small["a_log"][l], small["dt_bias"][l], small["dn_norm"][l],
                   w["w_a"][l], w["w_b"][l], w["w_o"][l])
        h = residual(tag + "_mixres", h, m, gt2, 1.0)
        h = _ffn(tag + "_ffn2", h, small["ln_ffn2"][l][None], sh3, sc3, gt3, w_gu2, w["ffn2_wd"][l])
    rows = loss_rows("loss", h, small["final_norm"][None], target)
    return 0.5 * jnp.sum(rows)


def kernel(x, c, ada_w, ada_b, ln_ffn1, ln_mix, ln_ffn2, ffn1_wg, ffn1_wu, ffn1_wd, w_in, conv_w, a_log, dt_bias, dn_norm, w_a, w_b, w_o, ffn2_wg, ffn2_wu, ffn2_wd, final_norm, loss_target, m_ada_w, m_ada_b, m_ln_ffn1, m_ln_mix, m_ln_ffn2, m_ffn1_wg, m_ffn1_wu, m_ffn1_wd, m_w_in, m_conv_w, m_a_log, m_dt_bias, m_dn_norm, m_w_a, m_w_b, m_w_o, m_ffn2_wg, m_ffn2_wu, m_ffn2_wd, m_final_norm, v_ada_w, v_ada_b, v_ln_ffn1, v_ln_mix, v_ln_ffn2, v_ffn1_wg, v_ffn1_wu, v_ffn1_wd, v_w_in, v_conv_w, v_a_log, v_dt_bias, v_dn_norm, v_w_a, v_w_b, v_w_o, v_ffn2_wg, v_ffn2_wu, v_ffn2_wd, v_final_norm):
    args = dict(locals())
    big_names = [n for n, _, _ in BIG]
    small_names = [n for n, _ in SMALL]
    me = 4 * lax.axis_index("x") + 2 * lax.axis_index("y") + lax.axis_index("c")
    cols = N_ADA * D // N_DEV

    c_all = all_gather("gather_c", jnp.pad(silu_rows("silu_c", c), ((0, 7), (0, 0))))[:, 0]
    mod_cols = jnp.stack([matmul(f"ada{l}", c_all, ada_w[l], "nn", F32) for l in range(DEPTH)])
    mod_cols = mod_cols + lax.dynamic_slice_in_dim(ada_b, me * cols, cols, axis=1)[:, None, :]
    mod_all = all_gather("gather_mod", mod_cols.reshape(DEPTH * N_DEV, cols))
    mod_all = mod_all.reshape(N_DEV, DEPTH, N_DEV, cols)
    mod = lax.dynamic_index_in_dim(mod_all, me, axis=2, keepdims=False)
    mod = mod.transpose(1, 0, 2).reshape(DEPTH, N_ADA * D)

    gathered = all_gather_many("gather_w", [args[n].astype(BF16) for n in big_names])

    small = {n: args[n] for n in small_names if n != "ada_b"}
    loss, (dx, dgathered, dsmall, dmod) = jax.value_and_grad(_local_loss, argnums=(0, 1, 2, 3))(
        x[0], gathered, small, mod, loss_target[0])

    part = _pack([dmod] + [dsmall[n] for n in small_names[1:]], SMALL_ROWS)
    parts = all_gather("gather_small", part)
    sm_out = adamw("adamw_small", parts, _pack([args[n] for n in small_names], SMALL_ROWS),
                   _pack([args["m_" + n] for n in small_names], SMALL_ROWS),
                   _pack([args["v_" + n] for n in small_names], SMALL_ROWS))

    dmod_all = parts.reshape(N_DEV, -1)[:, :DEPTH * N_ADA * D].reshape(N_DEV, DEPTH, N_ADA * D)
    dmod_mine = lax.dynamic_slice_in_dim(dmod_all, me * cols, cols, axis=2)
    g_ada = jnp.stack([matmul(f"ada{l}_dw", c_all, dmod_mine[:, l], "tn", F32) for l in range(DEPTH)])
    flat2 = lambda t: t.reshape(-1, t.shape[-1])
    ada_out = adamw("adamw_ada_w", flat2(g_ada), flat2(ada_w), flat2(m_ada_w), flat2(v_ada_w))

    my_core = lax.axis_index("c").astype(jnp.int32).reshape(1)
    from_sibling = sibling_exchange("pair_grads", dgathered)
    pair = []
    for n, t, got in zip(big_names, dgathered, from_sibling):
        cols_n = t.shape[-1]
        pair.append(pair_sum("pair_sum_" + n, t.reshape(4, 2, -1, cols_n),
                             got.reshape(4, -1, cols_n), my_core))
    landed = chip_exchange("scatter_grads", pair)
    big_out = {}
    for n, t in zip(big_names, landed):
        big_out[n] = adamw("adamw_" + n, t, flat2(args[n]), flat2(args["m_" + n]), flat2(args["v_" + n]))

    small_shapes = [shp for _, shp in SMALL]
    names = ["ada_w", "ada_b", "ln_ffn1", "ln_mix", "ln_ffn2", "ffn1_wg", "ffn1_wu", "ffn1_wd", "w_in",
             "conv_w", "a_log", "dt_bias", "dn_norm", "w_a", "w_b", "w_o", "ffn2_wg", "ffn2_wu",
             "ffn2_wd", "final_norm"]
    outs = [lax.psum(loss, ("x", "y", "c")), dx[None]]
    for kind in range(4):
        table = {n: big_out[n][kind].reshape(args[n].shape) for n in big_names}
        table.update(zip(small_names, _unpack(sm_out[kind], small_shapes)))
        table["ada_w"] = ada_out[kind].reshape(ada_w.shape)
        outs += [table[n] for n in names]
    return tuple(outs)
```

```python
import functools
import math

import numpy as np
import jax
import jax.numpy as jnp
from jax import lax
from jax.experimental import pallas as pl
from jax.experimental.pallas import tpu as pltpu

F32 = jnp.float32
BF16 = jnp.bfloat16

D = 1024
SEQ = 4096
DEPTH = 2
N_DEV = 8
DN_HEADS = 8
DN_DIM = 128
DN_CHUNK = 64
DN_CONV = 4
DA_HEADS = 12
DA_DIM = 64
DA_BLOCK = 128
DA_DILATIONS = (1, 4, 16)
ALIBI_MAX_EXP = 8.0
D_FF = 2816
N_ADA = 9
NORM_EPS = 1e-6
IN_COLS = 8464
IN_COLS_PAD = 8704
ADAM_LR, ADAM_B1, ADAM_B2, ADAM_EPS, ADAM_WD, ADAM_STEP = 0.001, 0.9, 0.999, 1e-08, 0.01, 10
NEG = -1e30

VMEM_LIMIT = 56 * 1024 * 1024
LANES = 128

MESH = pl.DeviceIdType.MESH


def _cparams(n_grid):
    return pltpu.CompilerParams(dimension_semantics=("arbitrary",) * n_grid,
                                vmem_limit_bytes=VMEM_LIMIT)


def blockwise(name, f, grid, ins, outs):
    n_in, n_out = len(ins), len(outs)
    diff = [i for i, (_, _, kind) in enumerate(ins) if kind != "const"]

    def apply(*vals):
        res = f(*vals)
        return tuple(r.astype(dt) for r, (_, dt, _, _) in zip(res, outs))

    def fwd_call(*arrays):
        def body(*refs):
            res = apply(*[r[...] for r in refs[:n_in]])
            for r, v in zip(refs[n_in:], res):
                r[...] = v

        return pl.pallas_call(
            body, name=name + "_fwd", grid=grid,
            in_specs=[pl.BlockSpec(b, im) for (b, im, _) in ins],
            out_specs=[pl.BlockSpec(b, im) for (_, _, b, im) in outs],
            out_shape=[jax.ShapeDtypeStruct(s, dt) for (s, dt, _, _) in outs],
            compiler_params=_cparams(len(grid)),
        )(*arrays)

    def bwd_call(arrays, cts):
        def body(*refs):
            in_refs, ct_refs = refs[:n_in], refs[n_in:n_in + n_out]
            g_refs = refs[n_in + n_out:]
            vals = [r[...] for r in in_refs]

            def fd(*dvals):
                full = list(vals)
                for i, v in zip(diff, dvals):
                    full[i] = v
                return apply(*full)

            _, vjp = jax.vjp(fd, *[vals[i] for i in diff])
            grads = vjp(tuple(r[...] for r in ct_refs))
            first = functools.reduce(jnp.logical_and,
                                     [pl.program_id(a) == 0 for a in range(len(grid))])
            for g_ref, g, i in zip(g_refs, grads, diff):
                if ins[i][2] == "acc":
                    @pl.when(first)
                    def _(g_ref=g_ref):
                        g_ref[...] = jnp.zeros_like(g_ref)
                    g_ref[...] += g.astype(F32)
                else:
                    g_ref[...] = g.astype(g_ref.dtype)

        g_shapes = [jax.ShapeDtypeStruct(arrays[i].shape,
                                         F32 if ins[i][2] == "acc" else arrays[i].dtype)
                    for i in diff]
        return pl.pallas_call(
            body, name=name + "_bwd", grid=grid,
            in_specs=([pl.BlockSpec(b, im) for (b, im, _) in ins]
                      + [pl.BlockSpec(b, im) for (_, _, b, im) in outs]),
            out_specs=[pl.BlockSpec(ins[i][0], ins[i][1]) for i in diff],
            out_shape=g_shapes,
            compiler_params=_cparams(len(grid)),
        )(*arrays, *cts)

    @jax.custom_vjp
    def op(*arrays):
        return tuple(fwd_call(*arrays))

    def op_fwd(*arrays):
        return tuple(fwd_call(*arrays)), arrays

    def op_bwd(arrays, cts):
        grads = bwd_call(arrays, cts)
        full = [None] * n_in
        for i, g in zip(diff, grads):
            full[i] = g.astype(arrays[i].dtype)
        return tuple(full)

    op.defvjp(op_fwd, op_bwd)
    return op


def _pick(n, cands):
    for c in cands:
        if n % c == 0:
            return c
    return n


def matmul(name, a, b, form, out_dtype):
    if form == "nn":
        (m, k), (_, n) = a.shape, b.shape
    elif form == "nt":
        (m, k), (n, _) = a.shape, b.shape
    else:
        (k, m), (_, n) = a.shape, b.shape
    tm = _pick(m, (1408, 1024, 512, 256, 128, 8))
    tn = _pick(n, (1408, 1024, 512, 384, 256, 128))
    tk = _pick(k, (1024, 1408, 512, 384, 256, 128, 8))
    nk = k // tk
    a_spec = (pl.BlockSpec((tk, tm), lambda i, j, kk: (kk, i)) if form == "tn"
              else pl.BlockSpec((tm, tk), lambda i, j, kk: (i, kk)))
    b_spec = (pl.BlockSpec((tn, tk), lambda i, j, kk: (j, kk)) if form == "nt"
              else pl.BlockSpec((tk, tn), lambda i, j, kk: (kk, j)))
    dims = {"nn": (((1,), (0,)), ((), ())), "nt": (((1,), (1,)), ((), ())),
            "tn": (((0,), (0,)), ((), ()))}[form]

    def body(a_ref, b_ref, o_ref, acc_ref):
        kk = pl.program_id(2)
        part = lax.dot_general(a_ref[...].astype(BF16), b_ref[...].astype(BF16), dims,
                               preferred_element_type=F32)

        @pl.when(kk == 0)
        def _():
            acc_ref[...] = part

        @pl.when(kk > 0)
        def _():
            acc_ref[...] += part

        @pl.when(kk == nk - 1)
        def _():
            o_ref[...] = acc_ref[...].astype(o_ref.dtype)

    return pl.pallas_call(
        body, name=name, grid=(m // tm, n // tn, nk),
        in_specs=[a_spec, b_spec],
        out_specs=pl.BlockSpec((tm, tn), lambda i, j, kk: (i, j)),
        out_shape=jax.ShapeDtypeStruct((m, n), out_dtype),
        scratch_shapes=[pltpu.VMEM((tm, tn), F32)],
        compiler_params=_cparams(3),
    )(a, b)


def linear(name, x, w, out_dtype):
    @jax.custom_vjp
    def op(x, w):
        return matmul(name + "_y", x, w, "nn", out_dtype)

    def op_fwd(x, w):
        return op(x, w), (x, w)

    def op_bwd(res, dy):
        x, w = res
        dx = matmul(name + "_dx", dy, w, "nt", x.dtype)
        dw = matmul(name + "_dw", x, dy, "tn", w.dtype)
        return dx, dw

    op.defvjp(op_fwd, op_bwd)
    return op(x, w)


SPLIT_TILE = 256


def linear_split(name, x, w, seg_tiles, out_dtypes):
    m, k = x.shape
    n = w.shape[1]
    nt = n // SPLIT_TILE
    starts = [sum(seg_tiles[:s]) for s in range(len(seg_tiles))]
    assert sum(seg_tiles) == nt
    tm = _pick(m, (2048, 1024, 512, 256, 128, 8))

    def inside(j, s):
        return jnp.logical_and(j >= starts[s], j < starts[s] + seg_tiles[s])

    def local(j, s):
        return jnp.clip(j - starts[s], 0, seg_tiles[s] - 1)

    def fwd_call(x, w):
        def body(x_ref, w_ref, *o_refs):
            j = pl.program_id(1)
            y = jnp.dot(x_ref[...].astype(BF16), w_ref[...].astype(BF16), preferred_element_type=F32)
            for s, o_ref in enumerate(o_refs):
                @pl.when(inside(j, s))
                def _(o_ref=o_ref):
                    o_ref[...] = y.astype(o_ref.dtype)

        return pl.pallas_call(
            body, name=name + "_y", grid=(m // tm, nt),
            in_specs=[pl.BlockSpec((tm, k), lambda i, j: (i, 0)),
                      pl.BlockSpec((k, SPLIT_TILE), lambda i, j: (0, j))],
            out_specs=[pl.BlockSpec((tm, SPLIT_TILE), lambda i, j, s=s: (i, local(j, s)))
                       for s in range(len(seg_tiles))],
            out_shape=[jax.ShapeDtypeStruct((m, t * SPLIT_TILE), dt)
                       for t, dt in zip(seg_tiles, out_dtypes)],
            compiler_params=_cparams(2),
        )(x, w)

    def dx_call(dys, w):
        def body(*refs):
            dy_refs, w_ref, o_ref, acc_ref = refs[:-3], refs[-3], refs[-2], refs[-1]
            j = pl.program_id(1)
            for s, dy_ref in enumerate(dy_refs):
                @pl.when(inside(j, s))
                def _(dy_ref=dy_ref):
                    part = lax.dot_general(dy_ref[...].astype(BF16), w_ref[...].astype(BF16),
                                           (((1,), (1,)), ((), ())), preferred_element_type=F32)

                    @pl.when(j == 0)
                    def _():
                        acc_ref[...] = part

                    @pl.when(j > 0)
                    def _():
                        acc_ref[...] += part

            @pl.when(j == nt - 1)
            def _():
                o_ref[...] = acc_ref[...].astype(o_ref.dtype)

        return pl.pallas_call(
            body, name=name + "_dx", grid=(m // tm, nt),
            in_specs=[pl.BlockSpec((tm, SPLIT_TILE), lambda i, j, s=s: (i, local(j, s)))
                      for s in range(len(seg_tiles))]
            + [pl.BlockSpec((k, SPLIT_TILE), lambda i, j: (0, j))],
            out_specs=pl.BlockSpec((tm, k), lambda i, j: (i, 0)),
            out_shape=jax.ShapeDtypeStruct((m, k), x.dtype),
            scratch_shapes=[pltpu.VMEM((tm, k), F32)],
            compiler_params=_cparams(2),
        )(*dys, w)

    def dw_call(x, dys):
        nm = m // tm

        def body(*refs):
            x_ref, dy_refs, o_ref, acc_ref = refs[0], refs[1:-2], refs[-2], refs[-1]
            j, kk = pl.program_id(0), pl.program_id(1)
            for s, dy_ref in enumerate(dy_refs):
                @pl.when(inside(j, s))
                def _(dy_ref=dy_ref):
                    rows = x_ref[pl.ds(pl.multiple_of(kk * tm, tm), tm), :]
                    part = lax.dot_general(rows.astype(BF16), dy_ref[...].astype(BF16),
                                           (((0,), (0,)), ((), ())), preferred_element_type=F32)

                    @pl.when(kk == 0)
                    def _():
                        acc_ref[...] = part

                    @pl.when(kk > 0)
                    def _():
                        acc_ref[...] += part

            @pl.when(kk == nm - 1)
            def _():
                o_ref[...] = acc_ref[...].astype(o_ref.dtype)

        return pl.pallas_call(
            body, name=name + "_dw", grid=(nt, nm),
            in_specs=[pl.BlockSpec((m, k), lambda j, kk: (0, 0))]
            + [pl.BlockSpec((tm, SPLIT_TILE),
                            lambda j, kk, s=s: (jnp.where(inside(j, s), kk, 0), local(j, s)))
               for s in range(len(seg_tiles))],
            out_specs=pl.BlockSpec((k, SPLIT_TILE), lambda j, kk: (0, j)),
            out_shape=jax.ShapeDtypeStruct((k, n), w.dtype),
            scratch_shapes=[pltpu.VMEM((k, SPLIT_TILE), F32)],
            compiler_params=_cparams(2),
        )(x, *dys)

    @jax.custom_vjp
    def op(x, w):
        return tuple(fwd_call(x, w))

    def op_fwd(x, w):
        return op(x, w), (x, w)

    def op_bwd(res, dys):
        x, w = res
        return dx_call(dys, w), dw_call(x, dys)

    op.defvjp(op_fwd, op_bwd)
    return op(x, w)


@jax.custom_vjp
def _sigmoid(x):
    return 1.0 / (1.0 + jnp.exp(-x))


def _sigmoid_fwd(x):
    s = _sigmoid(x)
    return s, s


def _sigmoid_bwd(s, ct):
    return (ct * (s * (1.0 - s)),)


_sigmoid.defvjp(_sigmoid_fwd, _sigmoid_bwd)


@jax.custom_vjp
def _silu(x):
    return x * _sigmoid(x)


def _silu_fwd(x):
    s = _sigmoid(x)
    return x * s, (x, s)


def _silu_bwd(res, ct):
    x, s = res
    return (ct * (s * (1.0 + x * (1.0 - s))),)


_silu.defvjp(_silu_fwd, _silu_bwd)


def _softplus(x):
    return jnp.maximum(x, 0.0) + jnp.log(1.0 + jnp.exp(-jnp.abs(x)))


def _rms(x):
    return x * lax.rsqrt(jnp.mean(x * x, axis=-1, keepdims=True) + NORM_EPS)


ROW_TILE = 512


def _row(i):
    return (i, 0)


def _fixed(*_):
    return (0, 0)


def norm_mod(name, h, ln, shift, scale):
    s, d = h.shape

    def f(h, ln, sh, sc):
        return ((_rms(h) * ln) * (1.0 + sc) + sh,)

    op = blockwise(name, f, (s // ROW_TILE,),
                   [((ROW_TILE, d), _row, "tile")] + [((1, d), _fixed, "acc")] * 3,
                   [((s, d), BF16, (ROW_TILE, d), _row)])
    return op(h, ln, shift, scale)[0]


FF_TILE = 1408


def _interleave_gate_up(wg, wu):
    parts = []
    for j in range(wg.shape[1] // FF_TILE):
        parts += [wg[:, j * FF_TILE:(j + 1) * FF_TILE], wu[:, j * FF_TILE:(j + 1) * FF_TILE]]
    return jnp.concatenate(parts, axis=1)


def swiglu_act(name, gu):
    s, f2 = gu.shape
    ff = f2 // 2

    def f(gu):
        g, u = gu[:, :FF_TILE].astype(F32), gu[:, FF_TILE:].astype(F32)
        return (_silu(g) * u,)

    op = blockwise(name, f, (s // ROW_TILE, ff // FF_TILE),
                   [((ROW_TILE, 2 * FF_TILE), lambda i, j: (i, j), "tile")],
                   [((s, ff), BF16, (ROW_TILE, FF_TILE), lambda i, j: (i, j))])
    return op(gu)[0]


def residual(name, h, y, gate, weight):
    s, d = h.shape

    def f(h, y, gate):
        return (h + (weight * gate) * y,)

    op = blockwise(name, f, (s // ROW_TILE,),
                   [((ROW_TILE, d), _row, "tile"), ((ROW_TILE, d), _row, "tile"),
                    ((1, d), _fixed, "acc")],
                   [((s, d), F32, (ROW_TILE, d), _row)])
    return op(h, y, gate)[0]


def merge_gates(name, ga, gb, ya, yb):
    s, d = ya.shape

    def f(ga, gb, ya, yb):
        return (_sigmoid(ga.astype(F32)) * ya + _sigmoid(gb.astype(F32)) * yb,)

    op = blockwise(name, f, (s // ROW_TILE,), [((ROW_TILE, d), _row, "tile")] * 4,
                   [((s, d), BF16, (ROW_TILE, d), _row)])
    return op(ga, gb, ya, yb)[0]


def loss_rows(name, h, g, target):
    s, d = h.shape

    def f(h, g, t):
        err = _rms(h) * g - t
        return (jnp.mean(err * err, axis=-1, keepdims=True),)

    op = blockwise(name, f, (s // ROW_TILE,),
                   [((ROW_TILE, d), _row, "tile"), ((1, d), _fixed, "acc"),
                    ((ROW_TILE, d), _row, "const")],
                   [((s, 1), F32, (ROW_TILE, 1), _row)])
    return op(h, g, target)[0]


def decay_beta(name, ab, a_log_pad, dt_bias_pad):
    s, n = ab.shape

    def f(ab, a_log, dt_bias):
        lane = lax.broadcasted_iota(jnp.int32, ab.shape, 1)
        beta = _sigmoid(ab)
        g = -jnp.exp(a_log) * _softplus(ab + dt_bias)
        return (jnp.where(lane < DN_HEADS, beta, jnp.where(lane < 2 * DN_HEADS, g, 0.0)),)

    op = blockwise(name, f, (s // ROW_TILE,),
                   [((ROW_TILE, n), _row, "tile"), ((1, n), _fixed, "acc"), ((1, n), _fixed, "acc")],
                   [((s, n), F32, (ROW_TILE, n), _row)])
    return op(ab, a_log_pad, dt_bias_pad)[0]


def _shift_rows(x, k):
    n = x.shape[0]

    @jax.custom_vjp
    def shift(x):
        row = lax.broadcasted_iota(jnp.int32, x.shape, 0)
        return jnp.where(row >= k, pltpu.roll(x, k, 0), 0.0)

    def shift_fwd(x):
        return shift(x), None

    def shift_bwd(_, g):
        row = lax.broadcasted_iota(jnp.int32, g.shape, 0)
        return (jnp.where(row < n - k, pltpu.roll(g, n - k, 0), 0.0),)

    shift.defvjp(shift_fwd, shift_bwd)
    return shift(x)


def conv_heads(name, x, w, mode):
    s, width = x.shape
    nh = width // LANES

    def f(x, w):
        x = x.astype(F32)
        y = w[DN_CONV - 1] * x
        for j in range(DN_CONV - 1):
            y = y + w[j] * _shift_rows(x, DN_CONV - 1 - j)
        y = _silu(y)
        if mode != "v":
            y = y * lax.rsqrt(jnp.sum(y * y, axis=-1, keepdims=True) + NORM_EPS)
        if mode == "q":
            y = y * (DN_DIM ** -0.5)
        return (y[None],)

    op = blockwise(name, f, (nh,),
                   [((s, LANES), lambda j: (0, j), "tile"),
                    ((DN_CONV, 1, LANES), lambda j: (0, 0, j), "tile")],
                   [((nh, s, LANES), F32, (1, s, LANES), lambda j: (j, 0, 0))])
    return op(x, w)[0]


def gated_head_norm(name, o, z, w):
    nh, s, dh = o.shape

    def f(o, z, w):
        return (_rms(o[0]) * w * _silu(z.astype(F32)),)

    op = blockwise(name, f, (s // ROW_TILE, nh),
                   [((1, ROW_TILE, dh), lambda i, h: (h, i, 0), "tile"),
                    ((ROW_TILE, dh), lambda i, h: (i, h), "tile"),
                    ((1, dh), lambda i, h: (0, 0), "acc")],
                   [((s, nh * dh), BF16, (ROW_TILE, dh), lambda i, h: (i, h))])
    return op(o, z, w)[0]


def _mm(a, b, ca, cb):
    return _bmm(a[None], b[None], ca + 1, cb + 1)[0]


def dilated_attention(name, q, k, v, dilation):
    s, width = q.shape
    nblk = s // DA_BLOCK
    per_sub = nblk // dilation
    slopes = [dilation * 2.0 ** (-ALIBI_MAX_EXP * (h + 1) / DA_HEADS) for h in range(DA_HEADS)]

    def f(q, kp, kc, vp, vc):
        first = (pl.program_id(0) % per_sub) == 0
        qi = lax.broadcasted_iota(jnp.int32, (DA_BLOCK, 2 * DA_BLOCK), 0)
        ki = lax.broadcasted_iota(jnp.int32, (DA_BLOCK, 2 * DA_BLOCK), 1)
        steps = (qi + DA_BLOCK - ki).astype(F32)
        lowest = qi + first.astype(jnp.int32) * (DA_BLOCK - qi)
        valid = jnp.logical_and(ki >= lowest, ki <= qi + DA_BLOCK)
        top = lax.broadcasted_iota(jnp.int32, (DA_BLOCK, LANES), 1) < DA_DIM
        o_parts, lse_parts = [], []
        for pair in range(width // LANES):
            cols = slice(pair * LANES, (pair + 1) * LANES)
            q2 = jnp.concatenate([jnp.where(top, q[:, cols], 0.0), jnp.where(top, 0.0, q[:, cols])], axis=0)
            k2 = jnp.concatenate([kp[:, cols], kc[:, cols]], axis=0)
            v2 = jnp.concatenate([vp[:, cols], vc[:, cols]], axis=0)
            bias = jnp.concatenate([jnp.where(valid, -slopes[2 * pair + half] * steps, NEG)
                                    for half in range(2)], axis=0)
            sc = _mm(q2, k2, 1, 1) * (DA_DIM ** -0.5) + bias
            mx = jnp.max(sc, axis=-1, keepdims=True)
            p = jnp.exp(sc - mx)
            l = jnp.sum(p, axis=-1, keepdims=True)
            o2 = _mm(p / l, v2, 1, 0)
            lse2 = mx + jnp.log(l)
            o_parts.append(jnp.where(top, o2[:DA_BLOCK], o2[DA_BLOCK:]))
            lse_parts.append(jnp.where(top, lse2[:DA_BLOCK], lse2[DA_BLOCK:]))
        return jnp.concatenate(o_parts, axis=1), jnp.concatenate(lse_parts, axis=1)

    blk = (DA_BLOCK, width)
    cur_map = lambda j: (j, 0)
    prev_map = lambda j: (jnp.maximum(j - 1, 0), 0)
    op = blockwise(
        name, f, (nblk,),
        [(blk, cur_map, "tile"), (blk, prev_map, "tile"), (blk, cur_map, "tile"),
         (blk, prev_map, "tile"), (blk, cur_map, "tile")],
        [((s, width), F32, blk, cur_map), ((s, width), F32, blk, cur_map)])

    @jax.custom_vjp
    def attn(q, k, v):
        return op(q, k, k, v, v)

    def attn_fwd(q, k, v):
        return attn(q, k, v), (q, k, v)

    def attn_bwd(res, cts):
        q, k, v = res
        _, vjp = jax.vjp(op, q, k, k, v, v)
        dq, dkp, dkc, dvp, dvc = vjp(cts)
        fill = lambda t: t.at[s - DA_BLOCK:].set(0.0)
        return dq, dkc + fill(dkp), dvc + fill(dvp)

    attn.defvjp(attn_fwd, attn_bwd)
    return attn(q, k, v)


def combine_patterns(name, outs, lses):
    s, width = outs[0].shape
    n = len(outs)

    def f(*vals):
        o, lse = vals[:n], vals[n:]
        mx = functools.reduce(jnp.maximum, lse)
        e = [jnp.exp(t - mx) for t in lse]
        return (sum(ei * oi for ei, oi in zip(e, o)) / sum(e),)

    tile = ROW_TILE // 2
    op = blockwise(name, f, (s // tile,), [((tile, width), _row, "tile")] * (2 * n),
                   [((s, width), BF16, (tile, width), _row)])
    return op(*outs, *lses)[0]


def _sub_order(t, r):
    if r == 1:
        return t
    s, c = t.shape
    return t.reshape(s // r, r, c).transpose(1, 0, 2).reshape(s, c)


def _seq_order(t, r):
    if r == 1:
        return t
    s, c = t.shape
    return t.reshape(r, s // r, c).transpose(1, 0, 2).reshape(s, c)


def _raw_bmm(a, b, ca, cb):
    return lax.dot_general(a.astype(BF16), b.astype(BF16), (((ca,), (cb,)), ((0,), (0,))),
                           preferred_element_type=F32)


def _split(a):
    hi = a.astype(BF16)
    return hi, (a - hi.astype(F32)).astype(BF16)


def _passes_bmm(a, b, ca, cb, passes):
    if passes == 1:
        return _raw_bmm(a, b, ca, cb)
    (a_hi, a_lo), (b_hi, b_lo) = _split(a), _split(b)
    return _raw_bmm(a_hi, b_hi, ca, cb) + (_raw_bmm(a_hi, b_lo, ca, cb) + _raw_bmm(a_lo, b_hi, ca, cb))


def _bmm(a, b, ca, cb, passes=1):
    fa, fb = 3 - ca, 3 - cb

    @jax.custom_vjp
    def mm(a, b):
        return _passes_bmm(a, b, ca, cb, passes)

    def mm_fwd(a, b):
        return mm(a, b), (a, b)

    def mm_bwd(res, ct):
        a, b = res
        da = (_passes_bmm(ct, b, 2, fb, passes) if ca == 2
              else _passes_bmm(b, ct, fb, 2, passes))
        db = (_passes_bmm(a, ct, fa, 1, passes) if cb == 1
              else _passes_bmm(ct, a, 1, fa, passes))
        return da.astype(a.dtype), db.astype(b.dtype)

    mm.defvjp(mm_fwd, mm_bwd)
    return mm(a, b)


def _delta_chunk(q, k, v, gcol, grow, bcol, state):
    c = q.shape[1]
    ii = lax.broadcasted_iota(jnp.int32, (1, c, c), 1)
    jj = lax.broadcasted_iota(jnp.int32, (1, c, c), 2)
    incl, strict = ii >= jj, ii > jj
    gc_col = jnp.sum(jnp.where(incl, grow, 0.0), axis=2, keepdims=True)
    gc_row = jnp.sum(jnp.where(ii <= jj, gcol, 0.0), axis=1, keepdims=True)
    decay = jnp.where(incl, jnp.exp(jnp.where(incl, gc_col - gc_row, 0.0)), 0.0)
    kb, vb = k * bcol, v * bcol
    m = jnp.where(strict, _bmm(kb, k, 2, 2) * decay, 0.0)
    eye = (ii == jj).astype(F32)
    p = -m
    inv = eye + p
    for _ in range(int(math.log2(c)) - 1):
        p = _bmm(p, p, 2, 1, 3)
        inv = inv + _bmm(inv, p, 2, 1, 3)
    e_col = jnp.exp(gc_col)
    u = _bmm(inv, vb, 2, 1)
    w = _bmm(inv, kb * e_col, 2, 1)
    qk = _bmm(q, k, 2, 2) * decay
    v_new = u - _bmm(w, state, 2, 1)
    o = _bmm(q * e_col, state, 2, 1) + _bmm(qk, v_new, 2, 1)
    g_last = jnp.sum(grow, axis=2, keepdims=True)
    new_state = state * jnp.exp(g_last) + _bmm(k * jnp.exp(g_last - gc_col), v_new, 1, 1)
    return o, new_state


def _delta_chunk_packed(q, k, v, gb, state):
    nh, c = q.shape[0], q.shape[1]
    lane = lax.broadcasted_iota(jnp.int32, gb.shape, 1)
    eye = (lax.broadcasted_iota(jnp.int32, (c, c), 0) == lax.broadcasted_iota(jnp.int32, (c, c), 1))
    column = lambda l: jnp.sum(jnp.where(lane == l, gb, 0.0), axis=1, keepdims=True)
    heads = lambda parts: jnp.concatenate([t[None] for t in parts], axis=0)
    bcol = heads([column(h) for h in range(nh)])
    gcols = [column(nh + h) for h in range(nh)]
    gcol = heads(gcols)
    grow = heads([jnp.sum(jnp.where(eye, g, 0.0), axis=0, keepdims=True) for g in gcols])
    return _delta_chunk(q, k, v, gcol, grow, bcol, state)


def delta_rule(name, q, k, v, gb):
    nh, s, dh = q.shape
    c = DN_CHUNK
    n = s // c

    def specs(rev):
        t = (lambda i: n - 1 - i) if rev else (lambda i: i)
        seq = pl.BlockSpec((nh, c, dh), lambda i: (0, t(i), 0))
        gate = pl.BlockSpec((c, LANES), lambda i: (t(i), 0))
        st = pl.BlockSpec((nh, 1, dh, dh), lambda i: (0, t(i), 0, 0))
        return seq, gate, st

    def fwd_call(q, k, v, gb):
        seq, gate, st = specs(False)

        def body(q_ref, k_ref, v_ref, gb_ref, o_ref, st_ref, state):
            @pl.when(pl.program_id(0) == 0)
            def _():
                state[...] = jnp.zeros_like(state)

            st_ref[:, 0] = state[...]
            o, new_state = _delta_chunk_packed(q_ref[...], k_ref[...], v_ref[...], gb_ref[...],
                                               state[...])
            o_ref[...] = o
            state[...] = new_state

        return pl.pallas_call(
            body, name=name + "_fwd", grid=(n,),
            in_specs=[seq, seq, seq, gate],
            out_specs=[seq, st],
            out_shape=[jax.ShapeDtypeStruct((nh, s, dh), F32),
                       jax.ShapeDtypeStruct((nh, n, dh, dh), F32)],
            scratch_shapes=[pltpu.VMEM((nh, dh, dh), F32)],
            compiler_params=_cparams(1),
        )(q, k, v, gb)

    def bwd_call(q, k, v, gb, states, do):
        seq, gate, st = specs(True)

        def body(q_ref, k_ref, v_ref, gb_ref, st_ref, do_ref,
                 dq_ref, dk_ref, dv_ref, dgb_ref, dstate):
            @pl.when(pl.program_id(0) == 0)
            def _():
                dstate[...] = jnp.zeros_like(dstate)

            _, vjp = jax.vjp(_delta_chunk_packed, q_ref[...], k_ref[...], v_ref[...], gb_ref[...],
                             st_ref[:, 0])
            dq, dk, dv, dgb, dst = vjp((do_ref[...], dstate[...]))
            dq_ref[...] = dq
            dk_ref[...] = dk
            dv_ref[...] = dv
            dgb_ref[...] = dgb
            dstate[...] = dst

        return pl.pallas_call(
            body, name=name + "_bwd", grid=(n,),
            in_specs=[seq, seq, seq, gate, st, seq],
            out_specs=[seq, seq, seq, gate],
            out_shape=[jax.ShapeDtypeStruct((nh, s, dh), F32)] * 3
            + [jax.ShapeDtypeStruct((s, LANES), F32)],
            scratch_shapes=[pltpu.VMEM((nh, dh, dh), F32)],
            compiler_params=_cparams(1),
        )(q, k, v, gb, states, do)

    @jax.custom_vjp
    def op(q, k, v, gb):
        return fwd_call(q, k, v, gb)[0]

    def op_fwd(q, k, v, gb):
        o, states = fwd_call(q, k, v, gb)
        return o, (q, k, v, gb, states)

    def op_bwd(res, do):
        return tuple(bwd_call(*res, do))

    op.defvjp(op_fwd, op_bwd)
    return op(q, k, v, gb)


def _my_place():
    return lax.axis_index("x"), lax.axis_index("y"), lax.axis_index("c")


def all_gather(name, shard):
    return all_gather_many(name, [shard])[0]


def all_gather_many(name, shards):
    n = len(shards)

    def body(*refs):
        x_refs, out_refs = refs[:n], refs[n:2 * n]
        send_sems, recv_sems, local_sems = refs[2 * n:]
        x, y, c = _my_place()
        me, sibling = (x, y, c), (x, y, 1 - c)
        chips = [(1 - x, y), (x, 1 - y), (1 - x, 1 - y)]

        def copy(i, k, block, to, own=False):
            px, py, pc = block
            slot = out_refs[i].at[4 * px + 2 * py + pc]
            return pltpu.make_async_remote_copy(
                src_ref=x_refs[i] if own else slot, dst_ref=slot,
                send_sem=send_sems.at[7 * i + k], recv_sem=recv_sems.at[7 * i + k],
                device_id=to, device_id_type=MESH)

        mine = [pltpu.make_async_copy(x_refs[i], out_refs[i].at[4 * x + 2 * y + c], local_sems.at[i])
                for i in range(n)]
        for cp in mine:
            cp.start()
        first = []
        for i in range(n):
            first.append(copy(i, 0, me, sibling, own=True))
            first += [copy(i, 1 + j, me, (*chip, c), own=True) for j, chip in enumerate(chips)]
        for cp in first:
            cp.start()
        passed = []
        for j, chip in enumerate(chips):
            for i in range(n):
                copy(i, 1 + j, (*chip, c), me).wait_recv()
                passed.append(copy(i, 4 + j, (*chip, c), sibling))
                passed[-1].start()
        for i in range(n):
            copy(i, 0, sibling, me).wait_recv()
        for j, chip in enumerate(chips):
            for i in range(n):
                copy(i, 4 + j, (*chip, 1 - c), me).wait_recv()
        for cp in first + passed:
            cp.wait_send()
        for cp in mine:
            cp.wait()

    any_spec = pl.BlockSpec(memory_space=pl.ANY)
    return pl.pallas_call(
        body, name=name,
        out_shape=[jax.ShapeDtypeStruct((N_DEV,) + t.shape, t.dtype) for t in shards],
        in_specs=[any_spec] * n, out_specs=[any_spec] * n,
        scratch_shapes=[pltpu.SemaphoreType.DMA((7 * n,)), pltpu.SemaphoreType.DMA((7 * n,)),
                        pltpu.SemaphoreType.DMA((n,))],
    )(*shards)


def sibling_exchange(name, parts):
    n = len(parts)

    def body(*refs):
        p_refs, out_refs = refs[:n], refs[n:2 * n]
        send_sems, recv_sems = refs[2 * n:]
        x, y, c = _my_place()
        copies = []
        for i in range(n):
            for chip in range(4):
                copies.append(pltpu.make_async_remote_copy(
                    src_ref=p_refs[i].at[2 * chip + (1 - c)], dst_ref=out_refs[i].at[chip],
                    send_sem=send_sems.at[4 * i + chip], recv_sem=recv_sems.at[4 * i + chip],
                    device_id=(x, y, 1 - c), device_id_type=MESH))
        for cp in copies:
            cp.start()
        for cp in copies:
            cp.wait_recv()
        for cp in copies:
            cp.wait_send()

    any_spec = pl.BlockSpec(memory_space=pl.ANY)
    return pl.pallas_call(
        body, name=name,
        out_shape=[jax.ShapeDtypeStruct((4,) + t.shape[1:], t.dtype) for t in parts],
        in_specs=[any_spec] * n, out_specs=[any_spec] * n,
        scratch_shapes=[pltpu.SemaphoreType.DMA((4 * n,)), pltpu.SemaphoreType.DMA((4 * n,))],
    )(*parts)


def chip_exchange(name, parts):
    n = len(parts)

    def body(*refs):
        p_refs, out_refs = refs[:n], refs[n:2 * n]
        send_sems, recv_sems, local_sems = refs[2 * n:]
        x, y, c = _my_place()
        me = 2 * x + y
        peers = [(1 - x, y), (x, 1 - y), (1 - x, 1 - y)]

        def copy(i, j, landing):
            px, py = peers[j]
            return pltpu.make_async_remote_copy(
                src_ref=p_refs[i].at[2 * px + py],
                dst_ref=out_refs[i].at[(2 * px + py) if landing else me],
                send_sem=send_sems.at[3 * i + j], recv_sem=recv_sems.at[3 * i + j],
                device_id=(px, py, c), device_id_type=MESH)

        mine = [pltpu.make_async_copy(p_refs[i].at[me], out_refs[i].at[me], local_sems.at[i])
                for i in range(n)]
        for cp in mine:
            cp.start()
        copies = [copy(i, j, False) for j in range(3) for i in range(n)]
        for cp in copies:
            cp.start()
        for j in range(3):
            for i in range(n):
                copy(i, j, True).wait_recv()
        for cp in copies:
            cp.wait_send()
        for cp in mine:
            cp.wait()

    any_spec = pl.BlockSpec(memory_space=pl.ANY)
    return pl.pallas_call(
        body, name=name,
        out_shape=[jax.ShapeDtypeStruct(t.shape, t.dtype) for t in parts],
        in_specs=[any_spec] * n, out_specs=[any_spec] * n,
        scratch_shapes=[pltpu.SemaphoreType.DMA((3 * n,)), pltpu.SemaphoreType.DMA((3 * n,)),
                        pltpu.SemaphoreType.DMA((n,))],
    )(*parts)


def pair_sum(name, mine, got, core):
    _, rows, cols = got.shape
    row_bytes = 4 * LANES * (-(-cols // LANES))
    tile = _pick(rows, [t for t in (512, 352, 256, 128, 64, 32, 16, 8)
                        if t * row_bytes <= ADAMW_BLOCK_BYTES])

    def body(core_ref, a_ref, b_ref, o_ref):
        o_ref[...] = (a_ref[...].astype(F32) + b_ref[...].astype(F32)).astype(o_ref.dtype)

    spec = pl.BlockSpec((4, tile, cols), lambda i, core_ref: (0, i, 0))
    return pl.pallas_call(
        body, name=name,
        grid_spec=pltpu.PrefetchScalarGridSpec(
            num_scalar_prefetch=1, grid=(rows // tile,),
            in_specs=[pl.BlockSpec((4, None, tile, cols), lambda i, core_ref: (0, core_ref[0], i, 0)),
                      spec],
            out_specs=spec),
        out_shape=jax.ShapeDtypeStruct(got.shape, got.dtype), compiler_params=_cparams(1),
    )(core, mine, got)


ADAMW_BLOCK_BYTES = 3 * 512 * 1024


def adamw(name, grad, w, m, v):
    rows, cols = w.shape
    stacked = grad.ndim == 3
    row_bytes = 4 * LANES * (-(-cols // LANES))
    tile = _pick(rows, [t for t in (512, 352, 256, 128, 64, 32, 16, 8)
                        if t * row_bytes <= ADAMW_BLOCK_BYTES])

    def body(g_ref, w_ref, m_ref, v_ref, go_ref, d_ref, mo_ref, vo_ref):
        if stacked:
            g = g_ref[0].astype(F32)
            for s in range(1, grad.shape[0]):
                g = g + g_ref[s].astype(F32)
        else:
            g = g_ref[...]
        m = ADAM_B1 * m_ref[...] + (1.0 - ADAM_B1) * g
        v = ADAM_B2 * v_ref[...] + (1.0 - ADAM_B2) * jnp.square(g)
        m_hat = m / (1.0 - ADAM_B1 ** ADAM_STEP)
        v_hat = v / (1.0 - ADAM_B2 ** ADAM_STEP)
        go_ref[...] = g
        d_ref[...] = -ADAM_LR * (m_hat / (jnp.sqrt(v_hat) + ADAM_EPS) + ADAM_WD * w_ref[...])
        mo_ref[...] = m
        vo_ref[...] = v

    flat = pl.BlockSpec((tile, cols), lambda i: (i, 0))
    g_spec = pl.BlockSpec((grad.shape[0], tile, cols), lambda i: (0, i, 0)) if stacked else flat
    return pl.pallas_call(
        body, name=name, grid=(rows // tile,),
        in_specs=[g_spec, flat, flat, flat], out_specs=[flat] * 4,
        out_shape=[jax.ShapeDtypeStruct((rows, cols), F32)] * 4,
        compiler_params=_cparams(1),
    )(grad, w, m, v)


def silu_rows(name, x):
    def body(x_ref, o_ref):
        o_ref[...] = _silu(x_ref[...])

    return pl.pallas_call(body, name=name, out_shape=jax.ShapeDtypeStruct(x.shape, F32))(x)


BIG = (("ffn1_wg", (D, D_FF // N_DEV), 1), ("ffn1_wu", (D, D_FF // N_DEV), 1),
       ("ffn1_wd", (D_FF // N_DEV, D), 0), ("w_in", (D, IN_COLS // N_DEV), 1),
       ("conv_w", (DN_CONV, 3 * D // N_DEV), 1), ("w_a", (D // N_DEV, D), 0),
       ("w_b", (DA_HEADS * DA_DIM, D // N_DEV), 1), ("w_o", (D // N_DEV, D), 0),
       ("ffn2_wg", (D, D_FF // N_DEV), 1), ("ffn2_wu", (D, D_FF // N_DEV), 1),
       ("ffn2_wd", (D_FF // N_DEV, D), 0))
SMALL = (("ada_b", (DEPTH, N_ADA * D)), ("ln_ffn1", (DEPTH, D)), ("ln_mix", (DEPTH, D)),
         ("ln_ffn2", (DEPTH, D)), ("a_log", (DEPTH, DN_HEADS)), ("dt_bias", (DEPTH, DN_HEADS)),
         ("dn_norm", (DEPTH, DN_DIM)), ("final_norm", (D,)))
SMALL_ROWS = 32


def _pack(arrays, rows):
    flat = jnp.concatenate([a.reshape(-1) for a in arrays])
    return jnp.pad(flat, (0, rows * D - flat.shape[0])).reshape(rows, D)


def _unpack(buf, shapes):
    flat = buf.reshape(-1)
    out, off = [], 0
    for shp in shapes:
        n = int(np.prod(shp))
        out.append(flat[off:off + n].reshape(shp))
        off += n
    return out


def _full_weights(gathered):
    full = {}
    for (name, (a, b), axis), t in zip(BIG, gathered):
        if axis == 1:
            full[name] = t.transpose(1, 2, 0, 3).reshape(DEPTH, a, N_DEV * b)
        else:
            full[name] = t.transpose(1, 0, 2, 3).reshape(DEPTH, N_DEV * a, b)
    return full


_Z0, _B0, _A0, _DQ0, _GA0 = 3072, 4096, 4104, 4112, 6416


def _reorder_in_proj(w):
    pad = jnp.zeros((w.shape[0], IN_COLS_PAD - IN_COLS), w.dtype)
    return jnp.concatenate([w[:, :_B0], w[:, _DQ0:], w[:, _B0:_DQ0], pad], axis=1)


def _ffn(tag, h, ln, shift, scale, gate, w_gu, w_d):
    n = norm_mod(tag + "_norm", h, ln, shift, scale)
    gu = linear(tag + "_gu", n, w_gu, BF16)
    a = swiglu_act(tag + "_act", gu)
    f = linear(tag + "_down", a, w_d, F32)
    return residual(tag + "_res", h, f, gate, 0.5)


def _mixer(tag, u, w_in, conv_w, a_log, dt_bias, dn_norm, w_a, w_b, w_o):
    s = u.shape[0]
    (q_pre, k_pre, v_pre, z, da_q, da_k, da_v, gate_a, gate_b, ab) = linear_split(
        tag + "_in", u, w_in, (4, 4, 4, 4, 3, 3, 3, 4, 4, 1), (BF16,) * 9 + (F32,))
    qkv_pre, da = (q_pre, k_pre, v_pre), (da_q, da_k, da_v)

    cw = conv_w.astype(F32).reshape(DN_CONV, 1, 3 * D)
    q, k, v = [conv_heads(f"{tag}_conv_{m}", qkv_pre[i], cw[:, :, i * D:(i + 1) * D], m)
               for i, m in enumerate("qkv")]
    pad = lambda t: jnp.pad(t, (DN_HEADS, ab.shape[1] - 2 * DN_HEADS))[None]
    gb = decay_beta(tag + "_decay", ab, pad(a_log), pad(dt_bias))
    o = delta_rule(tag + "_delta", q, k, v, gb[:, :LANES])
    o_a = gated_head_norm(tag + "_gnorm", o, z, dn_norm[None])
    y_a = linear(tag + "_wa", o_a, w_a, F32)

    outs, lses = [], []
    for r in DA_DILATIONS:
        o_r, lse_r = dilated_attention(f"{tag}_attn{r}", *[_sub_order(t, r) for t in da], r)
        outs.append(_seq_order(o_r, r))
        lses.append(_seq_order(lse_r, r))
    o_b = combine_patterns(tag + "_comb", outs, lses)
    y_b = linear(tag + "_wb", o_b, w_b, F32)

    merged = merge_gates(tag + "_merge", gate_a, gate_b, y_a, y_b)
    return linear(tag + "_wo", merged, w_o, F32)


def _local_loss(x, gathered, small, mod, target):
    w = _full_weights(gathered)
    h = x
    for l in range(DEPTH):
        tag = f"l{l}"
        sh1, sc1, gt1, sh2, sc2, gt2, sh3, sc3, gt3 = [mod[l, i * D:(i + 1) * D][None]
                                                       for i in range(N_ADA)]
        w_gu1 = _interleave_gate_up(w["ffn1_wg"][l], w["ffn1_wu"][l])
        w_gu2 = _interleave_gate_up(w["ffn2_wg"][l], w["ffn2_wu"][l])
        h = _ffn(tag + "_ffn1", h, small["ln_ffn1"][l][None], sh1, sc1, gt1, w_gu1, w["ffn1_wd"][l])
        u = norm_mod(tag + "_mixnorm", h, small["ln_mix"][l][None], sh2, sc2)
        m = _mixer(tag + "_mix", u, _reorder_in_proj(w["w_in"][l]), w["conv_w"][l],
                   small["a_log"][l], small["dt_bias"][l], small["dn_norm"][l],
                   w["w_a"][l], w["w_b"][l], w["w_o"][l])
        h = residual(tag + "_mixres", h, m, gt2, 1.0)
        h = _ffn(tag + "_ffn2", h, small["ln_ffn2"][l][None], sh3, sc3, gt3, w_gu2, w["ffn2_wd"][l])
    rows = loss_rows("loss", h, small["final_norm"][None], target)
    return 0.5 * jnp.sum(rows)


def kernel(x, c, ada_w, ada_b, ln_ffn1, ln_mix, ln_ffn2, ffn1_wg, ffn1_wu, ffn1_wd, w_in, conv_w, a_log, dt_bias, dn_norm, w_a, w_b, w_o, ffn2_wg, ffn2_wu, ffn2_wd, final_norm, loss_target, m_ada_w, m_ada_b, m_ln_ffn1, m_ln_mix, m_ln_ffn2, m_ffn1_wg, m_ffn1_wu, m_ffn1_wd, m_w_in, m_conv_w, m_a_log, m_dt_bias, m_dn_norm, m_w_a, m_w_b, m_w_o, m_ffn2_wg, m_ffn2_wu, m_ffn2_wd, m_final_norm, v_ada_w, v_ada_b, v_ln_ffn1, v_ln_mix, v_ln_ffn2, v_ffn1_wg, v_ffn1_wu, v_ffn1_wd, v_w_in, v_conv_w, v_a_log, v_dt_bias, v_dn_norm, v_w_a, v_w_b, v_w_o, v_ffn2_wg, v_ffn2_wu, v_ffn2_wd, v_final_norm):
    args = dict(locals())
    big_names = [n for n, _, _ in BIG]
    small_names = [n for n, _ in SMALL]
    me = 4 * lax.axis_index("x") + 2 * lax.axis_index("y") + lax.axis_index("c")
    cols = N_ADA * D // N_DEV

    c_all = all_gather("gather_c", jnp.pad(silu_rows("silu_c", c), ((0, 7), (0, 0))))[:, 0]
    mod_cols = jnp.stack([matmul(f"ada{l}", c_all, ada_w[l], "nn", F32) for l in range(DEPTH)])
    mod_cols = mod_cols + lax.dynamic_slice_in_dim(ada_b, me * cols, cols, axis=1)[:, None, :]
    mod_all = all_gather("gather_mod", mod_cols.reshape(DEPTH * N_DEV, cols))
    mod_all = mod_all.reshape(N_DEV, DEPTH, N_DEV, cols)
    mod = lax.dynamic_index_in_dim(mod_all, me, axis=2, keepdims=False)
    mod = mod.transpose(1, 0, 2).reshape(DEPTH, N_ADA * D)

    gathered = all_gather_many("gather_w", [args[n].astype(BF16) for n in big_names])

    small = {n: args[n] for n in small_names if n != "ada_b"}
    loss, (dx, dgathered, dsmall, dmod) = jax.value_and_grad(_local_loss, argnums=(0, 1, 2, 3))(
        x[0], gathered, small, mod, loss_target[0])

    part = _pack([dmod] + [dsmall[n] for n in small_names[1:]], SMALL_ROWS)
    parts = all_gather("gather_small", part)
    sm_out = adamw("adamw_small", parts, _pack([args[n] for n in small_names], SMALL_ROWS),
                   _pack([args["m_" + n] for n in small_names], SMALL_ROWS),
                   _pack([args["v_" + n] for n in small_names], SMALL_ROWS))

    dmod_all = parts.reshape(N_DEV, -1)[:, :DEPTH * N_ADA * D].reshape(N_DEV, DEPTH, N_ADA * D)
    dmod_mine = lax.dynamic_slice_in_dim(dmod_all, me * cols, cols, axis=2)
    g_ada = jnp.stack([matmul(f"ada{l}_dw", c_all, dmod_mine[:, l], "tn", F32) for l in range(DEPTH)])
    flat2 = lambda t: t.reshape(-1, t.shape[-1])
    ada_out = adamw("adamw_ada_w", flat2(g_ada), flat2(ada_w), flat2(m_ada_w), flat2(v_ada_w))

    my_core = lax.axis_index("c").astype(jnp.int32).reshape(1)
    from_sibling = sibling_exchange("pair_grads", dgathered)
    pair = []
    for n, t, got in zip(big_names, dgathered, from_sibling):
        cols_n = t.shape[-1]
        pair.append(pair_sum("pair_sum_" + n, t.reshape(4, 2, -1, cols_n),
                             got.reshape(4, -1, cols_n), my_core))
    landed = chip_exchange("scatter_grads", pair)
    big_out = {}
    for n, t in zip(big_names, landed):
        big_out[n] = adamw("adamw_" + n, t, flat2(args[n]), flat2(args["m_" + n]), flat2(args["v_" + n]))

    small_shapes = [shp for _, shp in SMALL]
    names = ["ada_w", "ada_b", "ln_ffn1", "ln_mix", "ln_ffn2", "ffn1_wg", "ffn1_wu", "ffn1_wd", "w_in",
             "conv_w", "a_log", "dt_bias", "dn_norm", "w_a", "w_b", "w_o", "ffn2_wg", "ffn2_wu",
             "ffn2_wd", "final_norm"]
    outs = [lax.psum(loss, ("x", "y", "c")), dx[None]]
    for kind in range(4):
        table = {n: big_out[n][kind].reshape(args[n].shape) for n in big_names}
        table.update(zip(small_names, _unpack(sm_out[kind], small_shapes)))
        table["ada_w"] = ada_out[kind].reshape(ada_w.shape)
        outs += [table[n] for n in names]
    return tuple(outs)
```

```python
import functools
import math

import numpy as np
import jax
import jax.numpy as jnp
from jax import lax
from jax.experimental import pallas as pl
from jax.experimental.pallas import tpu as pltpu

F32 = jnp.float32
BF16 = jnp.bfloat16

D = 1024
SEQ = 4096
DEPTH = 2
N_DEV = 8
DN_HEADS = 8
DN_DIM = 128
DN_CHUNK = 64
DN_CONV = 4
DA_HEADS = 12
DA_DIM = 64
DA_BLOCK = 128
DA_DILATIONS = (1, 4, 16)
ALIBI_MAX_EXP = 8.0
D_FF = 2816
N_ADA = 9
NORM_EPS = 1e-6
IN_COLS = 8464
IN_COLS_PAD = 8704
ADAM_LR, ADAM_B1, ADAM_B2, ADAM_EPS, ADAM_WD, ADAM_STEP = 0.001, 0.9, 0.999, 1e-08, 0.01, 10
NEG = -1e30

VMEM_LIMIT = 56 * 1024 * 1024
LANES = 128

MESH = pl.DeviceIdType.MESH


def _cparams(n_grid):
    return pltpu.CompilerParams(dimension_semantics=("arbitrary",) * n_grid,
                                vmem_limit_bytes=VMEM_LIMIT)


def blockwise(name, f, grid, ins, outs):
    n_in, n_out = len(ins), len(outs)
    diff = [i for i, (_, _, kind) in enumerate(ins) if kind != "const"]

    def apply(*vals):
        res = f(*vals)
        return tuple(r.astype(dt) for r, (_, dt, _, _) in zip(res, outs))

    def fwd_call(*arrays):
        def body(*refs):
            res = apply(*[r[...] for r in refs[:n_in]])
            for r, v in zip(refs[n_in:], res):
                r[...] = v

        return pl.pallas_call(
            body, name=name + "_fwd", grid=grid,
            in_specs=[pl.BlockSpec(b, im) for (b, im, _) in ins],
            out_specs=[pl.BlockSpec(b, im) for (_, _, b, im) in outs],
            out_shape=[jax.ShapeDtypeStruct(s, dt) for (s, dt, _, _) in outs],
            compiler_params=_cparams(len(grid)),
        )(*arrays)

    def bwd_call(arrays, cts):
        def body(*refs):
            in_refs, ct_refs = refs[:n_in], refs[n_in:n_in + n_out]
            g_refs = refs[n_in + n_out:]
            vals = [r[...] for r in in_refs]

            def fd(*dvals):
                full = list(vals)
                for i, v in zip(diff, dvals):
                    full[i] = v
                return apply(*full)

            _, vjp = jax.vjp(fd, *[vals[i] for i in diff])
            grads = vjp(tuple(r[...] for r in ct_refs))
            first = functools.reduce(jnp.logical_and,
                                     [pl.program_id(a) == 0 for a in range(len(grid))])
            for g_ref, g, i in zip(g_refs, grads, diff):
                if ins[i][2] == "acc":
                    @pl.when(first)
                    def _(g_ref=g_ref):
                        g_ref[...] = jnp.zeros_like(g_ref)
                    g_ref[...] += g.astype(F32)
                else:
                    g_ref[...] = g.astype(g_ref.dtype)

        g_shapes = [jax.ShapeDtypeStruct(arrays[i].shape,
                                         F32 if ins[i][2] == "acc" else arrays[i].dtype)
                    for i in diff]
        return pl.pallas_call(
            body, name=name + "_bwd", grid=grid,
            in_specs=([pl.BlockSpec(b, im) for (b, im, _) in ins]
                      + [pl.BlockSpec(b, im) for (_, _, b, im) in outs]),
            out_specs=[pl.BlockSpec(ins[i][0], ins[i][1]) for i in diff],
            out_shape=g_shapes,
            compiler_params=_cparams(len(grid)),
        )(*arrays, *cts)

    @jax.custom_vjp
    def op(*arrays):
        return tuple(fwd_call(*arrays))

    def op_fwd(*arrays):
        return tuple(fwd_call(*arrays)), arrays

    def op_bwd(arrays, cts):
        grads = bwd_call(arrays, cts)
        full = [None] * n_in
        for i, g in zip(diff, grads):
            full[i] = g.astype(arrays[i].dtype)
        return tuple(full)

    op.defvjp(op_fwd, op_bwd)
    return op


def _pick(n, cands):
    for c in cands:
        if n % c == 0:
            return c
    return n


def matmul(name, a, b, form, out_dtype):
    if form == "nn":
        (m, k), (_, n) = a.shape, b.shape
    elif form == "nt":
        (m, k), (n, _) = a.shape, b.shape
    else:
        (k, m), (_, n) = a.shape, b.shape
    tm = _pick(m, (1408, 1024, 512, 256, 128, 8))
    tn = _pick(n, (1408, 1024, 512, 384, 256, 128))
    tk = _pick(k, (1024, 1408, 512, 384, 256, 128, 8))
    nk = k // tk
    a_spec = (pl.BlockSpec((tk, tm), lambda i, j, kk: (kk, i)) if form == "tn"
              else pl.BlockSpec((tm, tk), lambda i, j, kk: (i, kk)))
    b_spec = (pl.BlockSpec((tn, tk), lambda i, j, kk: (j, kk)) if form == "nt"
              else pl.BlockSpec((tk, tn), lambda i, j, kk: (kk, j)))
    dims = {"nn": (((1,), (0,)), ((), ())), "nt": (((1,), (1,)), ((), ())),
            "tn": (((0,), (0,)), ((), ()))}[form]

    def body(a_ref, b_ref, o_ref, acc_ref):
        kk = pl.program_id(2)
        part = lax.dot_general(a_ref[...].astype(BF16), b_ref[...].astype(BF16), dims,
                               preferred_element_type=F32)

        @pl.when(kk == 0)
        def _():
            acc_ref[...] = part

        @pl.when(kk > 0)
        def _():
            acc_ref[...] += part

        @pl.when(kk == nk - 1)
        def _():
            o_ref[...] = acc_ref[...].astype(o_ref.dtype)

    return pl.pallas_call(
        body, name=name, grid=(m // tm, n // tn, nk),
        in_specs=[a_spec, b_spec],
        out_specs=pl.BlockSpec((tm, tn), lambda i, j, kk: (i, j)),
        out_shape=jax.ShapeDtypeStruct((m, n), out_dtype),
        scratch_shapes=[pltpu.VMEM((tm, tn), F32)],
        compiler_params=_cparams(3),
    )(a, b)


def linear(name, x, w, out_dtype):
    @jax.custom_vjp
    def op(x, w):
        return matmul(name + "_y", x, w, "nn", out_dtype)

    def op_fwd(x, w):
        return op(x, w), (x, w)

    def op_bwd(res, dy):
        x, w = res
        dx = matmul(name + "_dx", dy, w, "nt", x.dtype)
        dw = matmul(name + "_dw", x, dy, "tn", w.dtype)
        return dx, dw

    op.defvjp(op_fwd, op_bwd)
    return op(x, w)


SPLIT_TILE = 256


def linear_split(name, x, w, seg_tiles, out_dtypes):
    m, k = x.shape
    n = w.shape[1]
    nt = n // SPLIT_TILE
    starts = [sum(seg_tiles[:s]) for s in range(len(seg_tiles))]
    assert sum(seg_tiles) == nt
    tm = _pick(m, (2048, 1024, 512, 256, 128, 8))

    def inside(j, s):
        return jnp.logical_and(j >= starts[s], j < starts[s] + seg_tiles[s])

    def local(j, s):
        return jnp.clip(j - starts[s], 0, seg_tiles[s] - 1)

    def fwd_call(x, w):
        def body(x_ref, w_ref, *o_refs):
            j = pl.program_id(1)
            y = jnp.dot(x_ref[...].astype(BF16), w_ref[...].astype(BF16), preferred_element_type=F32)
            for s, o_ref in enumerate(o_refs):
                @pl.when(inside(j, s))
                def _(o_ref=o_ref):
                    o_ref[...] = y.astype(o_ref.dtype)

        return pl.pallas_call(
            body, name=name + "_y", grid=(m // tm, nt),
            in_specs=[pl.BlockSpec((tm, k), lambda i, j: (i, 0)),
                      pl.BlockSpec((k, SPLIT_TILE), lambda i, j: (0, j))],
            out_specs=[pl.BlockSpec((tm, SPLIT_TILE), lambda i, j, s=s: (i, local(j, s)))
                       for s in range(len(seg_tiles))],
            out_shape=[jax.ShapeDtypeStruct((m, t * SPLIT_TILE), dt)
                       for t, dt in zip(seg_tiles, out_dtypes)],
            compiler_params=_cparams(2),
        )(x, w)

    def dx_call(dys, w):
        def body(*refs):
            dy_refs, w_ref, o_ref, acc_ref = refs[:-3], refs[-3], refs[-2], refs[-1]
            j = pl.program_id(1)
            for s, dy_ref in enumerate(dy_refs):
                @pl.when(inside(j, s))
                def _(dy_ref=dy_ref):
                    part = lax.dot_general(dy_ref[...].astype(BF16), w_ref[...].astype(BF16),
                                           (((1,), (1,)), ((), ())), preferred_element_type=F32)

                    @pl.when(j == 0)
                    def _():
                        acc_ref[...] = part

                    @pl.when(j > 0)
                    def _():
                        acc_ref[...] += part

            @pl.when(j == nt - 1)
            def _():
                o_ref[...] = acc_ref[...].astype(o_ref.dtype)

        return pl.pallas_call(
            body, name=name + "_dx", grid=(m // tm, nt),
            in_specs=[pl.BlockSpec((tm, SPLIT_TILE), lambda i, j, s=s: (i, local(j, s)))
                      for s in range(len(seg_tiles))]
            + [pl.BlockSpec((k, SPLIT_TILE), lambda i, j: (0, j))],
            out_specs=pl.BlockSpec((tm, k), lambda i, j: (i, 0)),
            out_shape=jax.ShapeDtypeStruct((m, k), x.dtype),
            scratch_shapes=[pltpu.VMEM((tm, k), F32)],
            compiler_params=_cparams(2),
        )(*dys, w)

    def dw_call(x, dys):
        nm = m // tm

        def body(*refs):
            x_ref, dy_refs, o_ref, acc_ref = refs[0], refs[1:-2], refs[-2], refs[-1]
            j, kk = pl.program_id(0), pl.program_id(1)
            for s, dy_ref in enumerate(dy_refs):
                @pl.when(inside(j, s))
                def _(dy_ref=dy_ref):
                    rows = x_ref[pl.ds(pl.multiple_of(kk * tm, tm), tm), :]
                    part = lax.dot_general(rows.astype(BF16), dy_ref[...].astype(BF16),
                                           (((0,), (0,)), ((), ())), preferred_element_type=F32)

                    @pl.when(kk == 0)
                    def _():
                        acc_ref[...] = part

                    @pl.when(kk > 0)
                    def _():
                        acc_ref[...] += part

            @pl.when(kk == nm - 1)
            def _():
                o_ref[...] = acc_ref[...].astype(o_ref.dtype)

        return pl.pallas_call(
            body, name=name + "_dw", grid=(nt, nm),
            in_specs=[pl.BlockSpec((m, k), lambda j, kk: (0, 0))]
            + [pl.BlockSpec((tm, SPLIT_TILE),
                            lambda j, kk, s=s: (jnp.where(inside(j, s), kk, 0), local(j, s)))
               for s in range(len(seg_tiles))],
            out_specs=pl.BlockSpec((k, SPLIT_TILE), lambda j, kk: (0, j)),
            out_shape=jax.ShapeDtypeStruct((k, n), w.dtype),
            scratch_shapes=[pltpu.VMEM((k, SPLIT_TILE), F32)],
            compiler_params=_cparams(2),
        )(x, *dys)

    @jax.custom_vjp
    def op(x, w):
        return tuple(fwd_call(x, w))

    def op_fwd(x, w):
        return op(x, w), (x, w)

    def op_bwd(res, dys):
        x, w = res
        return dx_call(dys, w), dw_call(x, dys)

    op.defvjp(op_fwd, op_bwd)
    return op(x, w)


@jax.custom_vjp
def _sigmoid(x):
    return 1.0 / (1.0 + jnp.exp(-x))


def _sigmoid_fwd(x):
    s = _sigmoid(x)
    return s, s


def _sigmoid_bwd(s, ct):
    return (ct * (s * (1.0 - s)),)


_sigmoid.defvjp(_sigmoid_fwd, _sigmoid_bwd)


@jax.custom_vjp
def _silu(x):
    return x * _sigmoid(x)


def _silu_fwd(x):
    s = _sigmoid(x)
    return x * s, (x, s)


def _silu_bwd(res, ct):
    x, s = res
    return (ct * (s * (1.0 + x * (1.0 - s))),)


_silu.defvjp(_silu_fwd, _silu_bwd)


def _softplus(x):
    return jnp.maximum(x, 0.0) + jnp.log(1.0 + jnp.exp(-jnp.abs(x)))


def _rms(x):
    return x * lax.rsqrt(jnp.mean(x * x, axis=-1, keepdims=True) + NORM_EPS)


ROW_TILE = 512


def _row(i):
    return (i, 0)


def _fixed(*_):
    return (0, 0)


def norm_mod(name, h, ln, shift, scale):
    s, d = h.shape

    def f(h, ln, sh, sc):
        return ((_rms(h) * ln) * (1.0 + sc) + sh,)

    op = blockwise(name, f, (s // ROW_TILE,),
                   [((ROW_TILE, d), _row, "tile")] + [((1, d), _fixed, "acc")] * 3,
                   [((s, d), BF16, (ROW_TILE, d), _row)])
    return op(h, ln, shift, scale)[0]


FF_TILE = 1408


def _interleave_gate_up(wg, wu):
    parts = []
    for j in range(wg.shape[1] // FF_TILE):
        parts += [wg[:, j * FF_TILE:(j + 1) * FF_TILE], wu[:, j * FF_TILE:(j + 1) * FF_TILE]]
    return jnp.concatenate(parts, axis=1)


def swiglu_act(name, gu):
    s, f2 = gu.shape
    ff = f2 // 2

    def f(gu):
        g, u = gu[:, :FF_TILE].astype(F32), gu[:, FF_TILE:].astype(F32)
        return (_silu(g) * u,)

    op = blockwise(name, f, (s // ROW_TILE, ff // FF_TILE),
                   [((ROW_TILE, 2 * FF_TILE), lambda i, j: (i, j), "tile")],
                   [((s, ff), BF16, (ROW_TILE, FF_TILE), lambda i, j: (i, j))])
    return op(gu)[0]


def residual(name, h, y, gate, weight):
    s, d = h.shape

    def f(h, y, gate):
        return (h + (weight * gate) * y,)

    op = blockwise(name, f, (s // ROW_TILE,),
                   [((ROW_TILE, d), _row, "tile"), ((ROW_TILE, d), _row, "tile"),
                    ((1, d), _fixed, "acc")],
                   [((s, d), F32, (ROW_TILE, d), _row)])
    return op(h, y, gate)[0]


def merge_gates(name, ga, gb, ya, yb):
    s, d = ya.shape

    def f(ga, gb, ya, yb):
        return (_sigmoid(ga.astype(F32)) * ya + _sigmoid(gb.astype(F32)) * yb,)

    op = blockwise(name, f, (s // ROW_TILE,), [((ROW_TILE, d), _row, "tile")] * 4,
                   [((s, d), BF16, (ROW_TILE, d), _row)])
    return op(ga, gb, ya, yb)[0]


def loss_rows(name, h, g, target):
    s, d = h.shape

    def f(h, g, t):
        err = _rms(h) * g - t
        return (jnp.mean(err * err, axis=-1, keepdims=True),)

    op = blockwise(name, f, (s // ROW_TILE,),
                   [((ROW_TILE, d), _row, "tile"), ((1, d), _fixed, "acc"),
                    ((ROW_TILE, d), _row, "const")],
                   [((s, 1), F32, (ROW_TILE, 1), _row)])
    return op(h, g, target)[0]


def decay_beta(name, ab, a_log_pad, dt_bias_pad):
    s, n = ab.shape

    def f(ab, a_log, dt_bias):
        lane = lax.broadcasted_iota(jnp.int32, ab.shape, 1)
        beta = _sigmoid(ab)
        g = -jnp.exp(a_log) * _softplus(ab + dt_bias)
        return (jnp.where(lane < DN_HEADS, beta, jnp.where(lane < 2 * DN_HEADS, g, 0.0)),)

    op = blockwise(name, f, (s // ROW_TILE,),
                   [((ROW_TILE, n), _row, "tile"), ((1, n), _fixed, "acc"), ((1, n), _fixed, "acc")],
                   [((s, n), F32, (ROW_TILE, n), _row)])
    return op(ab, a_log_pad, dt_bias_pad)[0]


def _shift_rows(x, k):
    n = x.shape[0]

    @jax.custom_vjp
    def shift(x):
        row = lax.broadcasted_iota(jnp.int32, x.shape, 0)
        return jnp.where(row >= k, pltpu.roll(x, k, 0), 0.0)

    def shift_fwd(x):
        return shift(x), None

    def shift_bwd(_, g):
        row = lax.broadcasted_iota(jnp.int32, g.shape, 0)
        return (jnp.where(row < n - k, pltpu.roll(g, n - k, 0), 0.0),)

    shift.defvjp(shift_fwd, shift_bwd)
    return shift(x)


def conv_heads(name, x, w, mode):
    s, width = x.shape
    nh = width // LANES

    def f(x, w):
        x = x.astype(F32)
        y = w[DN_CONV - 1] * x
        for j in range(DN_CONV - 1):
            y = y + w[j] * _shift_rows(x, DN_CONV - 1 - j)
        y = _silu(y)
        if mode != "v":
            y = y * lax.rsqrt(jnp.sum(y * y, axis=-1, keepdims=True) + NORM_EPS)
        if mode == "q":
            y = y * (DN_DIM ** -0.5)
        return (y[None],)

    op = blockwise(name, f, (nh,),
                   [((s, LANES), lambda j: (0, j), "tile"),
                    ((DN_CONV, 1, LANES), lambda j: (0, 0, j), "tile")],
                   [((nh, s, LANES), F32, (1, s, LANES), lambda j: (j, 0, 0))])
    return op(x, w)[0]


def gated_head_norm(name, o, z, w):
    nh, s, dh = o.shape

    def f(o, z, w):
        return (_rms(o[0]) * w * _silu(z.astype(F32)),)

    op = blockwise(name, f, (s // ROW_TILE, nh),
                   [((1, ROW_TILE, dh), lambda i, h: (h, i, 0), "tile"),
                    ((ROW_TILE, dh), lambda i, h: (i, h), "tile"),
                    ((1, dh), lambda i, h: (0, 0), "acc")],
                   [((s, nh * dh), BF16, (ROW_TILE, dh), lambda i, h: (i, h))])
    return op(o, z, w)[0]


def _mm(a, b, ca, cb):
    return _bmm(a[None], b[None], ca + 1, cb + 1)[0]


def dilated_attention(name, q, k, v, dilation):
    s, width = q.shape
    nblk = s // DA_BLOCK
    per_sub = nblk // dilation
    slopes = [dilation * 2.0 ** (-ALIBI_MAX_EXP * (h + 1) / DA_HEADS) for h in range(DA_HEADS)]

    def f(q, kp, kc, vp, vc):
        first = (pl.program_id(0) % per_sub) == 0
        qi = lax.broadcasted_iota(jnp.int32, (DA_BLOCK, 2 * DA_BLOCK), 0)
        ki = lax.broadcasted_iota(jnp.int32, (DA_BLOCK, 2 * DA_BLOCK), 1)
        steps = (qi + DA_BLOCK - ki).astype(F32)
        lowest = qi + first.astype(jnp.int32) * (DA_BLOCK - qi)
        valid = jnp.logical_and(ki >= lowest, ki <= qi + DA_BLOCK)
        top = lax.broadcasted_iota(jnp.int32, (DA_BLOCK, LANES), 1) < DA_DIM
        o_parts, lse_parts = [], []
        for pair in range(width // LANES):
            cols = slice(pair * LANES, (pair + 1) * LANES)
            q2 = jnp.concatenate([jnp.where(top, q[:, cols], 0.0), jnp.where(top, 0.0, q[:, cols])], axis=0)
            k2 = jnp.concatenate([kp[:, cols], kc[:, cols]], axis=0)
            v2 = jnp.concatenate([vp[:, cols], vc[:, cols]], axis=0)
            bias = jnp.concatenate([jnp.where(valid, -slopes[2 * pair + half] * steps, NEG)
                                    for half in range(2)], axis=0)
            sc = _mm(q2, k2, 1, 1) * (DA_DIM ** -0.5) + bias
            mx = jnp.max(sc, axis=-1, keepdims=True)
            p = jnp.exp(sc - mx)
            l = jnp.sum(p, axis=-1, keepdims=True)
            o2 = _mm(p / l, v2, 1, 0)
            lse2 = mx + jnp.log(l)
            o_parts.append(jnp.where(top, o2[:DA_BLOCK], o2[DA_BLOCK:]))
            lse_parts.append(jnp.where(top, lse2[:DA_BLOCK], lse2[DA_BLOCK:]))
        return jnp.concatenate(o_parts, axis=1), jnp.concatenate(lse_parts, axis=1)

    blk = (DA_BLOCK, width)
    cur_map = lambda j: (j, 0)
    prev_map = lambda j: (jnp.maximum(j - 1, 0), 0)
    op = blockwise(
        name, f, (nblk,),
        [(blk, cur_map, "tile"), (blk, prev_map, "tile"), (blk, cur_map, "tile"),
         (blk, prev_map, "tile"), (blk, cur_map, "tile")],
        [((s, width), F32, blk, cur_map), ((s, width), F32, blk, cur_map)])

    @jax.custom_vjp
    def attn(q, k, v):
        return op(q, k, k, v, v)

    def attn_fwd(q, k, v):
        return attn(q, k, v), (q, k, v)

    def attn_bwd(res, cts):
        q, k, v = res
        _, vjp = jax.vjp(op, q, k, k, v, v)
        dq, dkp, dkc, dvp, dvc = vjp(cts)
        fill = lambda t: t.at[s - DA_BLOCK:].set(0.0)
        return dq, dkc + fill(dkp), dvc + fill(dvp)

    attn.defvjp(attn_fwd, attn_bwd)
    return attn(q, k, v)


def combine_patterns(name, outs, lses):
    s, width = outs[0].shape
    n = len(outs)

    def f(*vals):
        o, lse = vals[:n], vals[n:]
        mx = functools.reduce(jnp.maximum, lse)
        e = [jnp.exp(t - mx) for t in lse]
        return (sum(ei * oi for ei, oi in zip(e, o)) / sum(e),)

    tile = ROW_TILE // 2
    op = blockwise(name, f, (s // tile,), [((tile, width), _row, "tile")] * (2 * n),
                   [((s, width), BF16, (tile, width), _row)])
    return op(*outs, *lses)[0]


def _sub_order(t, r):
    if r == 1:
        return t
    s, c = t.shape
    return t.reshape(s // r, r, c).transpose(1, 0, 2).reshape(s, c)


def _seq_order(t, r):
    if r == 1:
        return t
    s, c = t.shape
    return t.reshape(r, s // r, c).transpose(1, 0, 2).reshape(s, c)


def _raw_bmm(a, b, ca, cb):
    return lax.dot_general(a.astype(BF16), b.astype(BF16), (((ca,), (cb,)), ((0,), (0,))),
                           preferred_element_type=F32)


def _split(a):
    hi = a.astype(BF16)
    return hi, (a - hi.astype(F32)).astype(BF16)


def _passes_bmm(a, b, ca, cb, passes):
    if passes == 1:
        return _raw_bmm(a, b, ca, cb)
    (a_hi, a_lo), (b_hi, b_lo) = _split(a), _split(b)
    return _raw_bmm(a_hi, b_hi, ca, cb) + (_raw_bmm(a_hi, b_lo, ca, cb) + _raw_bmm(a_lo, b_hi, ca, cb))


def _bmm(a, b, ca, cb, passes=1):
    fa, fb = 3 - ca, 3 - cb

    @jax.custom_vjp
    def mm(a, b):
        return _passes_bmm(a, b, ca, cb, passes)

    def mm_fwd(a, b):
        return mm(a, b), (a, b)

    def mm_bwd(res, ct):
        a, b = res
        da = (_passes_bmm(ct, b, 2, fb, passes) if ca == 2
              else _passes_bmm(b, ct, fb, 2, passes))
        db = (_passes_bmm(a, ct, fa, 1, passes) if cb == 1
              else _passes_bmm(ct, a, 1, fa, passes))
        return da.astype(a.dtype), db.astype(b.dtype)

    mm.defvjp(mm_fwd, mm_bwd)
    return mm(a, b)


def _delta_chunk(q, k, v, gcol, grow, bcol, state):
    c = q.shape[1]
    ii = lax.broadcasted_iota(jnp.int32, (1, c, c), 1)
    jj = lax.broadcasted_iota(jnp.int32, (1, c, c), 2)
    incl, strict = ii >= jj, ii > jj
    gc_col = jnp.sum(jnp.where(incl, grow, 0.0), axis=2, keepdims=True)
    gc_row = jnp.sum(jnp.where(ii <= jj, gcol, 0.0), axis=1, keepdims=True)
    decay = jnp.where(incl, jnp.exp(jnp.where(incl, gc_col - gc_row, 0.0)), 0.0)
    kb, vb = k * bcol, v * bcol
    m = jnp.where(strict, _bmm(kb, k, 2, 2) * decay, 0.0)
    eye = (ii == jj).astype(F32)
    p = -m
    inv = eye + p
    for _ in range(int(math.log2(c)) - 1):
        p = _bmm(p, p, 2, 1, 3)
        inv = inv + _bmm(inv, p, 2, 1, 3)
    e_col = jnp.exp(gc_col)
    u = _bmm(inv, vb, 2, 1)
    w = _bmm(inv, kb * e_col, 2, 1)
    qk = _bmm(q, k, 2, 2) * decay
    v_new = u - _bmm(w, state, 2, 1)
    o = _bmm(q * e_col, state, 2, 1) + _bmm(qk, v_new, 2, 1)
    g_last = jnp.sum(grow, axis=2, keepdims=True)
    new_state = state * jnp.exp(g_last) + _bmm(k * jnp.exp(g_last - gc_col), v_new, 1, 1)
    return o, new_state


def _delta_chunk_packed(q, k, v, gb, state):
    nh, c = q.shape[0], q.shape[1]
    lane = lax.broadcasted_iota(jnp.int32, gb.shape, 1)
    eye = (lax.broadcasted_iota(jnp.int32, (c, c), 0) == lax.broadcasted_iota(jnp.int32, (c, c), 1))
    column = lambda l: jnp.sum(jnp.where(lane == l, gb, 0.0), axis=1, keepdims=True)
    heads = lambda parts: jnp.concatenate([t[None] for t in parts], axis=0)
    bcol = heads([column(h) for h in range(nh)])
    gcols = [column(nh + h) for h in range(nh)]
    gcol = heads(gcols)
    grow = heads([jnp.sum(jnp.where(eye, g, 0.0), axis=0, keepdims=True) for g in gcols])
    return _delta_chunk(q, k, v, gcol, grow, bcol, state)


def _delta_specs(nh, s, dh, rev):
    c, n = DN_CHUNK, s // DN_CHUNK
    t = (lambda i: n - 1 - i) if rev else (lambda i: i)
    seq = pl.BlockSpec((nh, c, dh), lambda i: (0, t(i), 0))
    gate = pl.BlockSpec((c, LANES), lambda i: (t(i), 0))
    st = pl.BlockSpec((nh, 1, dh, dh), lambda i: (0, t(i), 0, 0))
    return seq, gate, st


def delta_fwd(name, q, k, v, gb, gather=None):
    nh, s, dh = q.shape
    n = s // DN_CHUNK
    seq, gate, st = _delta_specs(nh, s, dh, False)
    extra = list(gather or [])
    ne = len(extra)

    def body(*refs):
        q_ref, k_ref, v_ref, gb_ref = refs[:4]
        x_refs = refs[4:4 + ne]
        o_ref, st_ref = refs[4 + ne:6 + ne]
        out_refs = refs[6 + ne:6 + 2 * ne]
        state = refs[6 + 2 * ne]
        i = pl.program_id(0)
        if ne:
            begin, finish = _gather_phases(x_refs, out_refs, refs[7 + 2 * ne:])
            pl.when(i == 0)(begin)

        @pl.when(i == 0)
        def _():
            state[...] = jnp.zeros_like(state)

        st_ref[:, 0] = state[...]
        o, new_state = _delta_chunk_packed(q_ref[...], k_ref[...], v_ref[...], gb_ref[...], state[...])
        o_ref[...] = o
        state[...] = new_state
        if ne:
            pl.when(i == n - 1)(finish)

    any_spec = pl.BlockSpec(memory_space=pl.ANY)
    out = pl.pallas_call(
        body, name=name + "_fwd", grid=(n,),
        in_specs=[seq, seq, seq, gate] + [any_spec] * ne,
        out_specs=[seq, st] + [any_spec] * ne,
        out_shape=[jax.ShapeDtypeStruct((nh, s, dh), F32),
                   jax.ShapeDtypeStruct((nh, n, dh, dh), F32)]
        + [jax.ShapeDtypeStruct((N_DEV,) + t.shape, t.dtype) for t in extra],
        scratch_shapes=[pltpu.VMEM((nh, dh, dh), F32)]
        + ([pltpu.SemaphoreType.DMA((7 * ne,)), pltpu.SemaphoreType.DMA((7 * ne,)),
            pltpu.SemaphoreType.DMA((ne,))] if ne else []),
        compiler_params=_cparams(1),
    )(q, k, v, gb, *extra)
    return out[0], out[1], list(out[2:])


def delta_bwd(name, q, k, v, gb, states, do, exchange=None):
    nh, s, dh = q.shape
    n = s // DN_CHUNK
    seq, gate, st = _delta_specs(nh, s, dh, True)
    extra = list(exchange or [])
    ne = len(extra)

    def body(*refs):
        q_ref, k_ref, v_ref, gb_ref, st_ref, do_ref = refs[:6]
        p_refs = refs[6:6 + ne]
        dq_ref, dk_ref, dv_ref, dgb_ref = refs[6 + ne:10 + ne]
        land_refs = refs[10 + ne:10 + 2 * ne]
        dstate = refs[10 + 2 * ne]
        i = pl.program_id(0)
        if ne:
            begin, finish = _chip_exchange_phases(p_refs, land_refs, refs[11 + 2 * ne:])
            pl.when(i == 0)(begin)

        @pl.when(i == 0)
        def _():
            dstate[...] = jnp.zeros_like(dstate)

        _, vjp = jax.vjp(_delta_chunk_packed, q_ref[...], k_ref[...], v_ref[...], gb_ref[...],
                         st_ref[:, 0])
        dq, dk, dv, dgb, dst = vjp((do_ref[...], dstate[...]))
        dq_ref[...] = dq
        dk_ref[...] = dk
        dv_ref[...] = dv
        dgb_ref[...] = dgb
        dstate[...] = dst
        if ne:
            pl.when(i == n - 1)(finish)

    any_spec = pl.BlockSpec(memory_space=pl.ANY)
    out = pl.pallas_call(
        body, name=name + "_bwd", grid=(n,),
        in_specs=[seq, seq, seq, gate, st, seq] + [any_spec] * ne,
        out_specs=[seq, seq, seq, gate] + [any_spec] * ne,
        out_shape=[jax.ShapeDtypeStruct((nh, s, dh), F32)] * 3
        + [jax.ShapeDtypeStruct((s, LANES), F32)]
        + [jax.ShapeDtypeStruct(t.shape, t.dtype) for t in extra],
        scratch_shapes=[pltpu.VMEM((nh, dh, dh), F32)]
        + ([pltpu.SemaphoreType.DMA((3 * ne,)), pltpu.SemaphoreType.DMA((3 * ne,)),
            pltpu.SemaphoreType.DMA((ne,))] if ne else []),
        compiler_params=_cparams(1),
    )(q, k, v, gb, states, do, *extra)
    return tuple(out[:4]), list(out[4:])


def delta_rule(name, q, k, v, gb):
    @jax.custom_vjp
    def op(q, k, v, gb):
        return delta_fwd(name, q, k, v, gb)[0]

    def op_fwd(q, k, v, gb):
        o, states, _ = delta_fwd(name, q, k, v, gb)
        return o, (q, k, v, gb, states)

    def op_bwd(res, do):
        return delta_bwd(name, *res, do)[0]

    op.defvjp(op_fwd, op_bwd)
    return op(q, k, v, gb)


def _my_place():
    return lax.axis_index("x"), lax.axis_index("y"), lax.axis_index("c")


def all_gather(name, shard):
    return all_gather_many(name, [shard])[0]


def _gather_phases(x_refs, out_refs, sems):
    n = len(x_refs)
    send_sems, recv_sems, local_sems = sems
    x, y, c = _my_place()
    me, sibling = (x, y, c), (x, y, 1 - c)
    chips = [(1 - x, y), (x, 1 - y), (1 - x, 1 - y)]

    def copy(i, k, block, to, own=False):
        px, py, pc = block
        slot = out_refs[i].at[4 * px + 2 * py + pc]
        return pltpu.make_async_remote_copy(
            src_ref=x_refs[i] if own else slot, dst_ref=slot,
            send_sem=send_sems.at[7 * i + k], recv_sem=recv_sems.at[7 * i + k],
            device_id=to, device_id_type=MESH)

    mine = [pltpu.make_async_copy(x_refs[i], out_refs[i].at[4 * x + 2 * y + c], local_sems.at[i])
            for i in range(n)]
    first = []
    for i in range(n):
        first.append(copy(i, 0, me, sibling, own=True))
        first += [copy(i, 1 + j, me, (*chip, c), own=True) for j, chip in enumerate(chips)]

    def begin():
        for cp in mine + first:
            cp.start()

    def finish():
        passed = []
        for j, chip in enumerate(chips):
            for i in range(n):
                copy(i, 1 + j, (*chip, c), me).wait_recv()
                passed.append(copy(i, 4 + j, (*chip, c), sibling))
                passed[-1].start()
        for i in range(n):
            copy(i, 0, sibling, me).wait_recv()
        for j, chip in enumerate(chips):
            for i in range(n):
                copy(i, 4 + j, (*chip, 1 - c), me).wait_recv()
        for cp in first + passed:
            cp.wait_send()
        for cp in mine:
            cp.wait()

    return begin, finish


def all_gather_many(name, shards):
    n = len(shards)

    def body(*refs):
        begin, finish = _gather_phases(refs[:n], refs[n:2 * n], refs[2 * n:])
        begin()
        finish()

    any_spec = pl.BlockSpec(memory_space=pl.ANY)
    return pl.pallas_call(
        body, name=name,
        out_shape=[jax.ShapeDtypeStruct((N_DEV,) + t.shape, t.dtype) for t in shards],
        in_specs=[any_spec] * n, out_specs=[any_spec] * n,
        scratch_shapes=[pltpu.SemaphoreType.DMA((7 * n,)), pltpu.SemaphoreType.DMA((7 * n,)),
                        pltpu.SemaphoreType.DMA((n,))],
    )(*shards)


def sibling_exchange(name, parts):
    n = len(parts)

    def body(*refs):
        p_refs, out_refs = refs[:n], refs[n:2 * n]
        send_sems, recv_sems = refs[2 * n:]
        x, y, c = _my_place()
        copies = []
        for i in range(n):
            for chip in range(4):
                copies.append(pltpu.make_async_remote_copy(
                    src_ref=p_refs[i].at[2 * chip + (1 - c)], dst_ref=out_refs[i].at[chip],
                    send_sem=send_sems.at[4 * i + chip], recv_sem=recv_sems.at[4 * i + chip],
                    device_id=(x, y, 1 - c), device_id_type=MESH))
        for cp in copies:
            cp.start()
        for cp in copies:
            cp.wait_recv()
        for cp in copies:
            cp.wait_send()

    any_spec = pl.BlockSpec(memory_space=pl.ANY)
    return pl.pallas_call(
        body, name=name,
        out_shape=[jax.ShapeDtypeStruct((4,) + t.shape[1:], t.dtype) for t in parts],
        in_specs=[any_spec] * n, out_specs=[any_spec] * n,
        scratch_shapes=[pltpu.SemaphoreType.DMA((4 * n,)), pltpu.SemaphoreType.DMA((4 * n,))],
    )(*parts)


def _chip_exchange_phases(p_refs, out_refs, sems):
    n = len(p_refs)
    send_sems, recv_sems, local_sems = sems
    x, y, c = _my_place()
    me = 2 * x + y
    peers = [(1 - x, y), (x, 1 - y), (1 - x, 1 - y)]

    def copy(i, j, landing):
        px, py = peers[j]
        return pltpu.make_async_remote_copy(
            src_ref=p_refs[i].at[2 * px + py],
            dst_ref=out_refs[i].at[(2 * px + py) if landing else me],
            send_sem=send_sems.at[3 * i + j], recv_sem=recv_sems.at[3 * i + j],
            device_id=(px, py, c), device_id_type=MESH)

    mine = [pltpu.make_async_copy(p_refs[i].at[me], out_refs[i].at[me], local_sems.at[i])
            for i in range(n)]
    copies = [copy(i, j, False) for j in range(3) for i in range(n)]

    def begin():
        for cp in mine + copies:
            cp.start()

    def finish():
        for j in range(3):
            for i in range(n):
                copy(i, j, True).wait_recv()
        for cp in copies:
            cp.wait_send()
        for cp in mine:
            cp.wait()

    return begin, finish


def chip_exchange(name, parts):
    n = len(parts)

    def body(*refs):
        begin, finish = _chip_exchange_phases(refs[:n], refs[n:2 * n], refs[2 * n:])
        begin()
        finish()

    any_spec = pl.BlockSpec(memory_space=pl.ANY)
    return pl.pallas_call(
        body, name=name,
        out_shape=[jax.ShapeDtypeStruct(t.shape, t.dtype) for t in parts],
        in_specs=[any_spec] * n, out_specs=[any_spec] * n,
        scratch_shapes=[pltpu.SemaphoreType.DMA((3 * n,)), pltpu.SemaphoreType.DMA((3 * n,)),
                        pltpu.SemaphoreType.DMA((n,))],
    )(*parts)


def pair_sum(name, mine, got, core):
    _, rows, cols = got.shape
    row_bytes = 4 * LANES * (-(-cols // LANES))
    tile = _pick(rows, [t for t in (512, 352, 256, 128, 64, 32, 16, 8)
                        if t * row_bytes <= ADAMW_BLOCK_BYTES])

    def body(core_ref, a_ref, b_ref, o_ref):
        o_ref[...] = (a_ref[...].astype(F32) + b_ref[...].astype(F32)).astype(o_ref.dtype)

    spec = pl.BlockSpec((4, tile, cols), lambda i, core_ref: (0, i, 0))
    return pl.pallas_call(
        body, name=name,
        grid_spec=pltpu.PrefetchScalarGridSpec(
            num_scalar_prefetch=1, grid=(rows // tile,),
            in_specs=[pl.BlockSpec((4, None, tile, cols), lambda i, core_ref: (0, core_ref[0], i, 0)),
                      spec],
            out_specs=spec),
        out_shape=jax.ShapeDtypeStruct(got.shape, got.dtype), compiler_params=_cparams(1),
    )(core, mine, got)


ADAMW_BLOCK_BYTES = 3 * 512 * 1024


def adamw(name, grad, w, m, v):
    rows, cols = w.shape
    stacked = grad.ndim == 3
    row_bytes = 4 * LANES * (-(-cols // LANES))
    tile = _pick(rows, [t for t in (512, 352, 256, 128, 64, 32, 16, 8)
                        if t * row_bytes <= ADAMW_BLOCK_BYTES])

    def body(g_ref, w_ref, m_ref, v_ref, go_ref, d_ref, mo_ref, vo_ref):
        if stacked:
            g = g_ref[0].astype(F32)
            for s in range(1, grad.shape[0]):
                g = g + g_ref[s].astype(F32)
        else:
            g = g_ref[...]
        m = ADAM_B1 * m_ref[...] + (1.0 - ADAM_B1) * g
        v = ADAM_B2 * v_ref[...] + (1.0 - ADAM_B2) * jnp.square(g)
        m_hat = m / (1.0 - ADAM_B1 ** ADAM_STEP)
        v_hat = v / (1.0 - ADAM_B2 ** ADAM_STEP)
        go_ref[...] = g
        d_ref[...] = -ADAM_LR * (m_hat / (jnp.sqrt(v_hat) + ADAM_EPS) + ADAM_WD * w_ref[...])
        mo_ref[...] = m
        vo_ref[...] = v

    flat = pl.BlockSpec((tile, cols), lambda i: (i, 0))
    g_spec = pl.BlockSpec((grad.shape[0], tile, cols), lambda i: (0, i, 0)) if stacked else flat
    return pl.pallas_call(
        body, name=name, grid=(rows // tile,),
        in_specs=[g_spec, flat, flat, flat], out_specs=[flat] * 4,
        out_shape=[jax.ShapeDtypeStruct((rows, cols), F32)] * 4,
        compiler_params=_cparams(1),
    )(grad, w, m, v)


def silu_rows(name, x):
    def body(x_ref, o_ref):
        o_ref[...] = _silu(x_ref[...])

    return pl.pallas_call(body, name=name, out_shape=jax.ShapeDtypeStruct(x.shape, F32))(x)


BIG = (("ffn1_wg", (D, D_FF // N_DEV), 1), ("ffn1_wu", (D, D_FF // N_DEV), 1),
       ("ffn1_wd", (D_FF // N_DEV, D), 0), ("w_in", (D, IN_COLS // N_DEV), 1),
       ("conv_w", (DN_CONV, 3 * D // N_DEV), 1), ("w_a", (D // N_DEV, D), 0),
       ("w_b", (DA_HEADS * DA_DIM, D // N_DEV), 1), ("w_o", (D // N_DEV, D), 0),
       ("ffn2_wg", (D, D_FF // N_DEV), 1), ("ffn2_wu", (D, D_FF // N_DEV), 1),
       ("ffn2_wd", (D_FF // N_DEV, D), 0))
SMALL = (("ada_b", (DEPTH, N_ADA * D)), ("ln_ffn1", (DEPTH, D)), ("ln_mix", (DEPTH, D)),
         ("ln_ffn2", (DEPTH, D)), ("a_log", (DEPTH, DN_HEADS)), ("dt_bias", (DEPTH, DN_HEADS)),
         ("dn_norm", (DEPTH, DN_DIM)), ("final_norm", (D,)))
SMALL_ROWS = 32


def _pack(arrays, rows):
    flat = jnp.concatenate([a.reshape(-1) for a in arrays])
    return jnp.pad(flat, (0, rows * D - flat.shape[0])).reshape(rows, D)


def _unpack(buf, shapes):
    flat = buf.reshape(-1)
    out, off = [], 0
    for shp in shapes:
        n = int(np.prod(shp))
        out.append(flat[off:off + n].reshape(shp))
        off += n
    return out


def _full_weights(gathered, entries):
    full = {}
    for (name, (a, b), axis), t in zip(entries, gathered):
        if axis == 1:
            full[name] = t.transpose(1, 0, 2).reshape(a, N_DEV * b)
        else:
            full[name] = t.reshape(N_DEV * a, b)
    return full


_Z0, _B0, _A0, _DQ0, _GA0 = 3072, 4096, 4104, 4112, 6416


def _reorder_in_proj(w):
    pad = jnp.zeros((w.shape[0], IN_COLS_PAD - IN_COLS), w.dtype)
    return jnp.concatenate([w[:, :_B0], w[:, _DQ0:], w[:, _B0:_DQ0], pad], axis=1)


def _ffn(tag, h, ln, shift, scale, gate, w_gu, w_d):
    n = norm_mod(tag + "_norm", h, ln, shift, scale)
    gu = linear(tag + "_gu", n, w_gu, BF16)
    a = swiglu_act(tag + "_act", gu)
    f = linear(tag + "_down", a, w_d, F32)
    return residual(tag + "_res", h, f, gate, 0.5)


N_PRE = 5


def _mods(mod, l):
    return [mod[l, i * D:(i + 1) * D][None] for i in range(N_ADA)]


def _layer_pre(l, h, gathered, small, mod):
    w = _full_weights(gathered, BIG[:N_PRE])
    tag = f"l{l}"
    sh1, sc1, gt1, sh2, sc2 = _mods(mod, l)[:5]
    w_gu1 = _interleave_gate_up(w["ffn1_wg"], w["ffn1_wu"])
    h = _ffn(tag + "_ffn1", h, small["ln_ffn1"][l][None], sh1, sc1, gt1, w_gu1, w["ffn1_wd"])
    u = norm_mod(tag + "_mixnorm", h, small["ln_mix"][l][None], sh2, sc2)
    (q_pre, k_pre, v_pre, z, da_q, da_k, da_v, gate_a, gate_b, ab) = linear_split(
        tag + "_mix_in", u, _reorder_in_proj(w["w_in"]), (4, 4, 4, 4, 3, 3, 3, 4, 4, 1),
        (BF16,) * 9 + (F32,))
    cw = w["conv_w"].astype(F32).reshape(DN_CONV, 1, 3 * D)
    q, k, v = [conv_heads(f"{tag}_mix_conv_{m}", t, cw[:, :, i * D:(i + 1) * D], m)
               for i, (m, t) in enumerate(zip("qkv", (q_pre, k_pre, v_pre)))]
    pad = lambda t: jnp.pad(t, (DN_HEADS, ab.shape[1] - 2 * DN_HEADS))[None]
    gb = decay_beta(tag + "_mix_decay", ab, pad(small["a_log"][l]), pad(small["dt_bias"][l]))
    return (q, k, v, gb[:, :LANES]), (h, z, da_q, da_k, da_v, gate_a, gate_b)


def _layer_post(l, o, carry, gathered, small, mod):
    w = _full_weights(gathered, BIG[N_PRE:])
    tag = f"l{l}"
    h, z, da_q, da_k, da_v, gate_a, gate_b = carry
    gt2, sh3, sc3, gt3 = _mods(mod, l)[5:]
    o_a = gated_head_norm(tag + "_mix_gnorm", o, z, small["dn_norm"][l][None])
    y_a = linear(tag + "_mix_wa", o_a, w["w_a"], F32)

    outs, lses = [], []
    for r in DA_DILATIONS:
        o_r, lse_r = dilated_attention(f"{tag}_mix_attn{r}",
                                       *[_sub_order(t, r) for t in (da_q, da_k, da_v)], r)
        outs.append(_seq_order(o_r, r))
        lses.append(_seq_order(lse_r, r))
    o_b = combine_patterns(tag + "_mix_comb", outs, lses)
    y_b = linear(tag + "_mix_wb", o_b, w["w_b"], F32)

    merged = merge_gates(tag + "_mix_merge", gate_a, gate_b, y_a, y_b)
    m = linear(tag + "_mix_wo", merged, w["w_o"], F32)
    h = residual(tag + "_mixres", h, m, gt2, 1.0)
    w_gu2 = _interleave_gate_up(w["ffn2_wg"], w["ffn2_wu"])
    return _ffn(tag + "_ffn2", h, small["ln_ffn2"][l][None], sh3, sc3, gt3, w_gu2, w["ffn2_wd"])


def _layer(l, h, gathered, small, mod):
    (q, k, v, gb), carry = _layer_pre(l, h, gathered[:N_PRE], small, mod)
    o = delta_rule(f"l{l}_mix_delta", q, k, v, gb)
    return _layer_post(l, o, carry, gathered[N_PRE:], small, mod)


def _head(h, small, target):
    rows = loss_rows("loss", h, small["final_norm"][None], target)
    return 0.5 * jnp.sum(rows)


def _local_loss(x, gathered, small, mod, target):
    h = x
    for l in range(DEPTH):
        h = _layer(l, h, gathered[l], small, mod)
    return _head(h, small, target)


def kernel(x, c, ada_w, ada_b, ln_ffn1, ln_mix, ln_ffn2, ffn1_wg, ffn1_wu, ffn1_wd, w_in, conv_w, a_log, dt_bias, dn_norm, w_a, w_b, w_o, ffn2_wg, ffn2_wu, ffn2_wd, final_norm, loss_target, m_ada_w, m_ada_b, m_ln_ffn1, m_ln_mix, m_ln_ffn2, m_ffn1_wg, m_ffn1_wu, m_ffn1_wd, m_w_in, m_conv_w, m_a_log, m_dt_bias, m_dn_norm, m_w_a, m_w_b, m_w_o, m_ffn2_wg, m_ffn2_wu, m_ffn2_wd, m_final_norm, v_ada_w, v_ada_b, v_ln_ffn1, v_ln_mix, v_ln_ffn2, v_ffn1_wg, v_ffn1_wu, v_ffn1_wd, v_w_in, v_conv_w, v_a_log, v_dt_bias, v_dn_norm, v_w_a, v_w_b, v_w_o, v_ffn2_wg, v_ffn2_wu, v_ffn2_wd, v_final_norm):
    args = dict(locals())
    big_names = [n for n, _, _ in BIG]
    small_names = [n for n, _ in SMALL]
    me = 4 * lax.axis_index("x") + 2 * lax.axis_index("y") + lax.axis_index("c")
    cols = N_ADA * D // N_DEV

    c_all = all_gather("gather_c", jnp.pad(silu_rows("silu_c", c), ((0, 7), (0, 0))))[:, 0]
    mod_cols = jnp.stack([matmul(f"ada{l}", c_all, ada_w[l], "nn", F32) for l in range(DEPTH)])
    mod_cols = mod_cols + lax.dynamic_slice_in_dim(ada_b, me * cols, cols, axis=1)[:, None, :]
    mod_all = all_gather("gather_mod", mod_cols.reshape(DEPTH * N_DEV, cols))
    mod_all = mod_all.reshape(N_DEV, DEPTH, N_DEV, cols)
    mod = lax.dynamic_index_in_dim(mod_all, me, axis=2, keepdims=False)
    mod = mod.transpose(1, 0, 2).reshape(DEPTH, N_ADA * D)

    shards = [[args[n][l].astype(BF16) for n in big_names] for l in range(DEPTH)]
    gathered0 = all_gather_many("gather_w0", shards[0])

    small = {n: args[n] for n in small_names if n != "ada_b"}
    pre_out, vjp_pre = jax.vjp(functools.partial(_layer_pre, 0), x[0], gathered0[:N_PRE], small, mod)
    (q0, k0, v0, gb0), carry0 = pre_out
    o0, states0, gathered1 = delta_fwd("l0_mix_delta", q0, k0, v0, gb0, gather=shards[1])
    h1, vjp_post = jax.vjp(functools.partial(_layer_post, 0), o0, carry0, gathered0[N_PRE:], small, mod)
    loss, vjp1 = jax.vjp(lambda h, g, sm, md: _head(_layer(1, h, g, sm, md), sm, loss_target[0]),
                         h1, gathered1, small, mod)
    dh1, dgathered1, dsmall1, dmod1 = vjp1(jnp.ones((), F32))

    my_core = lax.axis_index("c").astype(jnp.int32).reshape(1)

    def pair_sums(l, dgathered):
        from_sibling = sibling_exchange(f"pair_grads{l}", dgathered)
        return [pair_sum(f"pair_sum{l}_{n}", t.reshape(4, 2, -1, t.shape[-1]), got, my_core)
                for n, t, got in zip(big_names, dgathered, from_sibling)]

    pair1 = pair_sums(1, dgathered1)
    do0, dcarry0, dpost_w, dsmall_post, dmod_post = vjp_post(dh1)
    ddelta0, landed1 = delta_bwd("l0_mix_delta", q0, k0, v0, gb0, states0, do0, exchange=pair1)
    dx, dpre_w, dsmall_pre, dmod_pre = vjp_pre((ddelta0, dcarry0))
    landed0 = chip_exchange("scatter_grads0", pair_sums(0, list(dpre_w) + list(dpost_w)))
    dsmall = {n: dsmall_pre[n] + dsmall_post[n] + dsmall1[n] for n in dsmall1}
    dmod = dmod_pre + dmod_post + dmod1

    part = _pack([dmod] + [dsmall[n] for n in small_names[1:]], SMALL_ROWS)
    parts = all_gather("gather_small", part)
    sm_out = adamw("adamw_small", parts, _pack([args[n] for n in small_names], SMALL_ROWS),
                   _pack([args["m_" + n] for n in small_names], SMALL_ROWS),
                   _pack([args["v_" + n] for n in small_names], SMALL_ROWS))

    dmod_all = parts.reshape(N_DEV, -1)[:, :DEPTH * N_ADA * D].reshape(N_DEV, DEPTH, N_ADA * D)
    dmod_mine = lax.dynamic_slice_in_dim(dmod_all, me * cols, cols, axis=2)
    g_ada = jnp.stack([matmul(f"ada{l}_dw", c_all, dmod_mine[:, l], "tn", F32) for l in range(DEPTH)])
    flat2 = lambda t: t.reshape(-1, t.shape[-1])
    ada_out = adamw("adamw_ada_w", flat2(g_ada), flat2(ada_w), flat2(m_ada_w), flat2(v_ada_w))

    big_out = {}
    for n, t0, t1 in zip(big_names, landed0, landed1):
        both = jnp.concatenate([t0, t1], axis=1)
        big_out[n] = adamw("adamw_" + n, both, flat2(args[n]), flat2(args["m_" + n]), flat2(args["v_" + n]))

    small_shapes = [shp for _, shp in SMALL]
    names = ["ada_w", "ada_b", "ln_ffn1", "ln_mix", "ln_ffn2", "ffn1_wg", "ffn1_wu", "ffn1_wd", "w_in",
             "conv_w", "a_log", "dt_bias", "dn_norm", "w_a", "w_b", "w_o", "ffn2_wg", "ffn2_wu",
             "ffn2_wd", "final_norm"]
    outs = [lax.psum(loss, ("x", "y", "c")), dx[None]]
    for kind in range(4):
        table = {n: big_out[n][kind].reshape(args[n].shape) for n in big_names}
        table.update(zip(small_names, _unpack(sm_out[kind], small_shapes)))
        table["ada_w"] = ada_out[kind].reshape(ada_w.shape)
        outs += [table[n] for n in names]
    return tuple(outs)
```

```python
import functools
import math

import numpy as np
import jax
import jax.numpy as jnp
from jax import lax
from jax.experimental import pallas as pl
from jax.experimental.pallas import tpu as pltpu

F32 = jnp.float32
BF16 = jnp.bfloat16

D = 1024
SEQ = 4096
DEPTH = 2
N_DEV = 8
DN_HEADS = 8
DN_DIM = 128
DN_CHUNK = 64
DN_CONV = 4
DA_HEADS = 12
DA_DIM = 64
DA_BLOCK = 128
DA_DILATIONS = (1, 4, 16)
ALIBI_MAX_EXP = 8.0
D_FF = 2816
N_ADA = 9
NORM_EPS = 1e-6
IN_COLS = 8464
IN_COLS_PAD = 8704
ADAM_LR, ADAM_B1, ADAM_B2, ADAM_EPS, ADAM_WD, ADAM_STEP = 0.001, 0.9, 0.999, 1e-08, 0.01, 10
NEG = -1e30

VMEM_LIMIT = 56 * 1024 * 1024
LANES = 128

MESH = pl.DeviceIdType.MESH


def _cparams(n_grid):
    return pltpu.CompilerParams(dimension_semantics=("arbitrary",) * n_grid,
                                vmem_limit_bytes=VMEM_LIMIT)


def blockwise(name, f, grid, ins, outs):
    n_in, n_out = len(ins), len(outs)
    diff = [i for i, (_, _, kind) in enumerate(ins) if kind != "const"]

    def apply(*vals):
        res = f(*vals)
        return tuple(r.astype(dt) for r, (_, dt, _, _) in zip(res, outs))

    def fwd_call(*arrays):
        def body(*refs):
            res = apply(*[r[...] for r in refs[:n_in]])
            for r, v in zip(refs[n_in:], res):
                r[...] = v

        return pl.pallas_call(
            body, name=name + "_fwd", grid=grid,
            in_specs=[pl.BlockSpec(b, im) for (b, im, _) in ins],
            out_specs=[pl.BlockSpec(b, im) for (_, _, b, im) in outs],
            out_shape=[jax.ShapeDtypeStruct(s, dt) for (s, dt, _, _) in outs],
            compiler_params=_cparams(len(grid)),
        )(*arrays)

    def bwd_call(arrays, cts):
        def body(*refs):
            in_refs, ct_refs = refs[:n_in], refs[n_in:n_in + n_out]
            g_refs = refs[n_in + n_out:]
            vals = [r[...] for r in in_refs]

            def fd(*dvals):
                full = list(vals)
                for i, v in zip(diff, dvals):
                    full[i] = v
                return apply(*full)

            _, vjp = jax.vjp(fd, *[vals[i] for i in diff])
            grads = vjp(tuple(r[...] for r in ct_refs))
            first = functools.reduce(jnp.logical_and,
                                     [pl.program_id(a) == 0 for a in range(len(grid))])
            for g_ref, g, i in zip(g_refs, grads, diff):
                if ins[i][2] == "acc":
                    @pl.when(first)
                    def _(g_ref=g_ref):
                        g_ref[...] = jnp.zeros_like(g_ref)
                    g_ref[...] += g.astype(F32)
                else:
                    g_ref[...] = g.astype(g_ref.dtype)

        g_shapes = [jax.ShapeDtypeStruct(arrays[i].shape,
                                         F32 if ins[i][2] == "acc" else arrays[i].dtype)
                    for i in diff]
        return pl.pallas_call(
            body, name=name + "_bwd", grid=grid,
            in_specs=([pl.BlockSpec(b, im) for (b, im, _) in ins]
                      + [pl.BlockSpec(b, im) for (_, _, b, im) in outs]),
            out_specs=[pl.BlockSpec(ins[i][0], ins[i][1]) for i in diff],
            out_shape=g_shapes,
            compiler_params=_cparams(len(grid)),
        )(*arrays, *cts)

    @jax.custom_vjp
    def op(*arrays):
        return tuple(fwd_call(*arrays))

    def op_fwd(*arrays):
        return tuple(fwd_call(*arrays)), arrays

    def op_bwd(arrays, cts):
        grads = bwd_call(arrays, cts)
        full = [None] * n_in
        for i, g in zip(diff, grads):
            full[i] = g.astype(arrays[i].dtype)
        return tuple(full)

    op.defvjp(op_fwd, op_bwd)
    return op


def _pick(n, cands):
    for c in cands:
        if n % c == 0:
            return c
    return n


def matmul(name, a, b, form, out_dtype):
    if form == "nn":
        (m, k), (_, n) = a.shape, b.shape
    elif form == "nt":
        (m, k), (n, _) = a.shape, b.shape
    else:
        (k, m), (_, n) = a.shape, b.shape
    tm = _pick(m, (1408, 1024, 512, 256, 128, 8))
    tn = _pick(n, (1408, 1024, 512, 384, 256, 128))
    tk = _pick(k, (1024, 1408, 512, 384, 256, 128, 8))
    nk = k // tk
    a_spec = (pl.BlockSpec((tk, tm), lambda i, j, kk: (kk, i)) if form == "tn"
              else pl.BlockSpec((tm, tk), lambda i, j, kk: (i, kk)))
    b_spec = (pl.BlockSpec((tn, tk), lambda i, j, kk: (j, kk)) if form == "nt"
              else pl.BlockSpec((tk, tn), lambda i, j, kk: (kk, j)))
    dims = {"nn": (((1,), (0,)), ((), ())), "nt": (((1,), (1,)), ((), ())),
            "tn": (((0,), (0,)), ((), ()))}[form]

    def body(a_ref, b_ref, o_ref, acc_ref):
        kk = pl.program_id(2)
        part = lax.dot_general(a_ref[...].astype(BF16), b_ref[...].astype(BF16), dims,
                               preferred_element_type=F32)

        @pl.when(kk == 0)
        def _():
            acc_ref[...] = part

        @pl.when(kk > 0)
        def _():
            acc_ref[...] += part

        @pl.when(kk == nk - 1)
        def _():
            o_ref[...] = acc_ref[...].astype(o_ref.dtype)

    return pl.pallas_call(
        body, name=name, grid=(m // tm, n // tn, nk),
        in_specs=[a_spec, b_spec],
        out_specs=pl.BlockSpec((tm, tn), lambda i, j, kk: (i, j)),
        out_shape=jax.ShapeDtypeStruct((m, n), out_dtype),
        scratch_shapes=[pltpu.VMEM((tm, tn), F32)],
        compiler_params=_cparams(3),
    )(a, b)


def linear(name, x, w, out_dtype):
    @jax.custom_vjp
    def op(x, w):
        return matmul(name + "_y", x, w, "nn", out_dtype)

    def op_fwd(x, w):
        return op(x, w), (x, w)

    def op_bwd(res, dy):
        x, w = res
        dx = matmul(name + "_dx", dy, w, "nt", x.dtype)
        dw = matmul(name + "_dw", x, dy, "tn", w.dtype)
        return dx, dw

    op.defvjp(op_fwd, op_bwd)
    return op(x, w)


SPLIT_TILE = 256


def linear_split(name, x, w, seg_tiles, out_dtypes):
    m, k = x.shape
    n = w.shape[1]
    nt = n // SPLIT_TILE
    starts = [sum(seg_tiles[:s]) for s in range(len(seg_tiles))]
    assert sum(seg_tiles) == nt
    tm = _pick(m, (2048, 1024, 512, 256, 128, 8))

    def inside(j, s):
        return jnp.logical_and(j >= starts[s], j < starts[s] + seg_tiles[s])

    def local(j, s):
        return jnp.clip(j - starts[s], 0, seg_tiles[s] - 1)

    def fwd_call(x, w):
        def body(x_ref, w_ref, *o_refs):
            j = pl.program_id(1)
            y = jnp.dot(x_ref[...].astype(BF16), w_ref[...].astype(BF16), preferred_element_type=F32)
            for s, o_ref in enumerate(o_refs):
                @pl.when(inside(j, s))
                def _(o_ref=o_ref):
                    o_ref[...] = y.astype(o_ref.dtype)

        return pl.pallas_call(
            body, name=name + "_y", grid=(m // tm, nt),
            in_specs=[pl.BlockSpec((tm, k), lambda i, j: (i, 0)),
                      pl.BlockSpec((k, SPLIT_TILE), lambda i, j: (0, j))],
            out_specs=[pl.BlockSpec((tm, SPLIT_TILE), lambda i, j, s=s: (i, local(j, s)))
                       for s in range(len(seg_tiles))],
            out_shape=[jax.ShapeDtypeStruct((m, t * SPLIT_TILE), dt)
                       for t, dt in zip(seg_tiles, out_dtypes)],
            compiler_params=_cparams(2),
        )(x, w)

    def dx_call(dys, w):
        def body(*refs):
            dy_refs, w_ref, o_ref, acc_ref = refs[:-3], refs[-3], refs[-2], refs[-1]
            j = pl.program_id(1)
            for s, dy_ref in enumerate(dy_refs):
                @pl.when(inside(j, s))
                def _(dy_ref=dy_ref):
                    part = lax.dot_general(dy_ref[...].astype(BF16), w_ref[...].astype(BF16),
                                           (((1,), (1,)), ((), ())), preferred_element_type=F32)

                    @pl.when(j == 0)
                    def _():
                        acc_ref[...] = part

                    @pl.when(j > 0)
                    def _():
                        acc_ref[...] += part

            @pl.when(j == nt - 1)
            def _():
                o_ref[...] = acc_ref[...].astype(o_ref.dtype)

        return pl.pallas_call(
            body, name=name + "_dx", grid=(m // tm, nt),
            in_specs=[pl.BlockSpec((tm, SPLIT_TILE), lambda i, j, s=s: (i, local(j, s)))
                      for s in range(len(seg_tiles))]
            + [pl.BlockSpec((k, SPLIT_TILE), lambda i, j: (0, j))],
            out_specs=pl.BlockSpec((tm, k), lambda i, j: (i, 0)),
            out_shape=jax.ShapeDtypeStruct((m, k), x.dtype),
            scratch_shapes=[pltpu.VMEM((tm, k), F32)],
            compiler_params=_cparams(2),
        )(*dys, w)

    def dw_call(x, dys):
        nm = m // tm

        def body(*refs):
            x_ref, dy_refs, o_ref, acc_ref = refs[0], refs[1:-2], refs[-2], refs[-1]
            j, kk = pl.program_id(0), pl.program_id(1)
            for s, dy_ref in enumerate(dy_refs):
                @pl.when(inside(j, s))
                def _(dy_ref=dy_ref):
                    rows = x_ref[pl.ds(pl.multiple_of(kk * tm, tm), tm), :]
                    part = lax.dot_general(rows.astype(BF16), dy_ref[...].astype(BF16),
                                           (((0,), (0,)), ((), ())), preferred_element_type=F32)

                    @pl.when(kk == 0)
                    def _():
                        acc_ref[...] = part

                    @pl.when(kk > 0)
                    def _():
                        acc_ref[...] += part

            @pl.when(kk == nm - 1)
            def _():
                o_ref[...] = acc_ref[...].astype(o_ref.dtype)

        return pl.pallas_call(
            body, name=name + "_dw", grid=(nt, nm),
            in_specs=[pl.BlockSpec((m, k), lambda j, kk: (0, 0))]
            + [pl.BlockSpec((tm, SPLIT_TILE),
                            lambda j, kk, s=s: (jnp.where(inside(j, s), kk, 0), local(j, s)))
               for s in range(len(seg_tiles))],
            out_specs=pl.BlockSpec((k, SPLIT_TILE), lambda j, kk: (0, j)),
            out_shape=jax.ShapeDtypeStruct((k, n), w.dtype),
            scratch_shapes=[pltpu.VMEM((k, SPLIT_TILE), F32)],
            compiler_params=_cparams(2),
        )(x, *dys)

    @jax.custom_vjp
    def op(x, w):
        return tuple(fwd_call(x, w))

    def op_fwd(x, w):
        return op(x, w), (x, w)

    def op_bwd(res, dys):
        x, w = res
        return dx_call(dys, w), dw_call(x, dys)

    op.defvjp(op_fwd, op_bwd)
    return op(x, w)


@jax.custom_vjp
def _sigmoid(x):
    return 1.0 / (1.0 + jnp.exp(-x))


def _sigmoid_fwd(x):
    s = _sigmoid(x)
    return s, s


def _sigmoid_bwd(s, ct):
    return (ct * (s * (1.0 - s)),)


_sigmoid.defvjp(_sigmoid_fwd, _sigmoid_bwd)


@jax.custom_vjp
def _silu(x):
    return x * _sigmoid(x)


def _silu_fwd(x):
    s = _sigmoid(x)
    return x * s, (x, s)


def _silu_bwd(res, ct):
    x, s = res
    return (ct * (s * (1.0 + x * (1.0 - s))),)


_silu.defvjp(_silu_fwd, _silu_bwd)


def _softplus(x):
    return jnp.maximum(x, 0.0) + jnp.log(1.0 + jnp.exp(-jnp.abs(x)))


def _rms(x):
    return x * lax.rsqrt(jnp.mean(x * x, axis=-1, keepdims=True) + NORM_EPS)


ROW_TILE = 512


def _row(i):
    return (i, 0)


def _fixed(*_):
    return (0, 0)


def norm_mod(name, h, ln, shift, scale):
    s, d = h.shape

    def f(h, ln, sh, sc):
        return ((_rms(h) * ln) * (1.0 + sc) + sh,)

    op = blockwise(name, f, (s // ROW_TILE,),
                   [((ROW_TILE, d), _row, "tile")] + [((1, d), _fixed, "acc")] * 3,
                   [((s, d), BF16, (ROW_TILE, d), _row)])
    return op(h, ln, shift, scale)[0]


FF_TILE = 1408


def _interleave_gate_up(wg, wu):
    parts = []
    for j in range(wg.shape[1] // FF_TILE):
        parts += [wg[:, j * FF_TILE:(j + 1) * FF_TILE], wu[:, j * FF_TILE:(j + 1) * FF_TILE]]
    return jnp.concatenate(parts, axis=1)


def swiglu_act(name, gu):
    s, f2 = gu.shape
    ff = f2 // 2

    def f(gu):
        g, u = gu[:, :FF_TILE].astype(F32), gu[:, FF_TILE:].astype(F32)
        return (_silu(g) * u,)

    op = blockwise(name, f, (s // ROW_TILE, ff // FF_TILE),
                   [((ROW_TILE, 2 * FF_TILE), lambda i, j: (i, j), "tile")],
                   [((s, ff), BF16, (ROW_TILE, FF_TILE), lambda i, j: (i, j))])
    return op(gu)[0]


def residual(name, h, y, gate, weight):
    s, d = h.shape

    def f(h, y, gate):
        return (h + (weight * gate) * y,)

    op = blockwise(name, f, (s // ROW_TILE,),
                   [((ROW_TILE, d), _row, "tile"), ((ROW_TILE, d), _row, "tile"),
                    ((1, d), _fixed, "acc")],
                   [((s, d), F32, (ROW_TILE, d), _row)])
    return op(h, y, gate)[0]


def merge_gates(name, ga, gb, ya, yb):
    s, d = ya.shape

    def f(ga, gb, ya, yb):
        return (_sigmoid(ga.astype(F32)) * ya + _sigmoid(gb.astype(F32)) * yb,)

    op = blockwise(name, f, (s // ROW_TILE,), [((ROW_TILE, d), _row, "tile")] * 4,
                   [((s, d), BF16, (ROW_TILE, d), _row)])
    return op(ga, gb, ya, yb)[0]


def loss_rows(name, h, g, target):
    s, d = h.shape

    def f(h, g, t):
        err = _rms(h) * g - t
        return (jnp.mean(err * err, axis=-1, keepdims=True),)

    op = blockwise(name, f, (s // ROW_TILE,),
                   [((ROW_TILE, d), _row, "tile"), ((1, d), _fixed, "acc"),
                    ((ROW_TILE, d), _row, "const")],
                   [((s, 1), F32, (ROW_TILE, 1), _row)])
    return op(h, g, target)[0]


def decay_beta(name, ab, a_log_pad, dt_bias_pad):
    s, n = ab.shape

    def f(ab, a_log, dt_bias):
        lane = lax.broadcasted_iota(jnp.int32, ab.shape, 1)
        beta = _sigmoid(ab)
        g = -jnp.exp(a_log) * _softplus(ab + dt_bias)
        return (jnp.where(lane < DN_HEADS, beta, jnp.where(lane < 2 * DN_HEADS, g, 0.0)),)

    op = blockwise(name, f, (s // ROW_TILE,),
                   [((ROW_TILE, n), _row, "tile"), ((1, n), _fixed, "acc"), ((1, n), _fixed, "acc")],
                   [((s, n), F32, (ROW_TILE, n), _row)])
    return op(ab, a_log_pad, dt_bias_pad)[0]


def _shift_rows(x, k):
    n = x.shape[0]

    @jax.custom_vjp
    def shift(x):
        row = lax.broadcasted_iota(jnp.int32, x.shape, 0)
        return jnp.where(row >= k, pltpu.roll(x, k, 0), 0.0)

    def shift_fwd(x):
        return shift(x), None

    def shift_bwd(_, g):
        row = lax.broadcasted_iota(jnp.int32, g.shape, 0)
        return (jnp.where(row < n - k, pltpu.roll(g, n - k, 0), 0.0),)

    shift.defvjp(shift_fwd, shift_bwd)
    return shift(x)


def conv_heads(name, x, w, mode):
    s, width = x.shape
    nh = width // LANES

    def f(x, w):
        x = x.astype(F32)
        y = w[DN_CONV - 1] * x
        for j in range(DN_CONV - 1):
            y = y + w[j] * _shift_rows(x, DN_CONV - 1 - j)
        y = _silu(y)
        if mode != "v":
            y = y * lax.rsqrt(jnp.sum(y * y, axis=-1, keepdims=True) + NORM_EPS)
        if mode == "q":
            y = y * (DN_DIM ** -0.5)
        return (y[None],)

    op = blockwise(name, f, (nh,),
                   [((s, LANES), lambda j: (0, j), "tile"),
                    ((DN_CONV, 1, LANES), lambda j: (0, 0, j), "tile")],
                   [((nh, s, LANES), F32, (1, s, LANES), lambda j: (j, 0, 0))])
    return op(x, w)[0]


def gated_head_norm(name, o, z, w):
    nh, s, dh = o.shape

    def f(o, z, w):
        return (_rms(o[0]) * w * _silu(z.astype(F32)),)

    op = blockwise(name, f, (s // ROW_TILE, nh),
                   [((1, ROW_TILE, dh), lambda i, h: (h, i, 0), "tile"),
                    ((ROW_TILE, dh), lambda i, h: (i, h), "tile"),
                    ((1, dh), lambda i, h: (0, 0), "acc")],
                   [((s, nh * dh), BF16, (ROW_TILE, dh), lambda i, h: (i, h))])
    return op(o, z, w)[0]


def _mm(a, b, ca, cb):
    return _bmm(a[None], b[None], ca + 1, cb + 1)[0]


def dilated_attention(name, q, k, v, dilation):
    s, width = q.shape
    nblk = s // DA_BLOCK
    per_sub = nblk // dilation
    slopes = [dilation * 2.0 ** (-ALIBI_MAX_EXP * (h + 1) / DA_HEADS) for h in range(DA_HEADS)]

    def f(q, kp, kc, vp, vc):
        first = (pl.program_id(0) % per_sub) == 0
        qi = lax.broadcasted_iota(jnp.int32, (DA_BLOCK, 2 * DA_BLOCK), 0)
        ki = lax.broadcasted_iota(jnp.int32, (DA_BLOCK, 2 * DA_BLOCK), 1)
        steps = (qi + DA_BLOCK - ki).astype(F32)
        lowest = qi + first.astype(jnp.int32) * (DA_BLOCK - qi)
        valid = jnp.logical_and(ki >= lowest, ki <= qi + DA_BLOCK)
        top = lax.broadcasted_iota(jnp.int32, (DA_BLOCK, LANES), 1) < DA_DIM
        o_parts, lse_parts = [], []
        for pair in range(width // LANES):
            cols = slice(pair * LANES, (pair + 1) * LANES)
            q2 = jnp.concatenate([jnp.where(top, q[:, cols], 0.0), jnp.where(top, 0.0, q[:, cols])], axis=0)
            k2 = jnp.concatenate([kp[:, cols], kc[:, cols]], axis=0)
            v2 = jnp.concatenate([vp[:, cols], vc[:, cols]], axis=0)
            bias = jnp.concatenate([jnp.where(valid, -slopes[2 * pair + half] * steps, NEG)
                                    for half in range(2)], axis=0)
            sc = _mm(q2, k2, 1, 1) * (DA_DIM ** -0.5) + bias
            mx = jnp.max(sc, axis=-1, keepdims=True)
            p = jnp.exp(sc - mx)
            l = jnp.sum(p, axis=-1, keepdims=True)
            o2 = _mm(p / l, v2, 1, 0)
            lse2 = mx + jnp.log(l)
            o_parts.append(jnp.where(top, o2[:DA_BLOCK], o2[DA_BLOCK:]))
            lse_parts.append(jnp.where(top, lse2[:DA_BLOCK], lse2[DA_BLOCK:]))
        return jnp.concatenate(o_parts, axis=1), jnp.concatenate(lse_parts, axis=1)

    blk = (DA_BLOCK, width)
    cur_map = lambda j: (j, 0)
    prev_map = lambda j: (jnp.maximum(j - 1, 0), 0)
    op = blockwise(
        name, f, (nblk,),
        [(blk, cur_map, "tile"), (blk, prev_map, "tile"), (blk, cur_map, "tile"),
         (blk, prev_map, "tile"), (blk, cur_map, "tile")],
        [((s, width), F32, blk, cur_map), ((s, width), F32, blk, cur_map)])

    @jax.custom_vjp
    def attn(q, k, v):
        return op(q, k, k, v, v)

    def attn_fwd(q, k, v):
        return attn(q, k, v), (q, k, v)

    def attn_bwd(res, cts):
        q, k, v = res
        _, vjp = jax.vjp(op, q, k, k, v, v)
        dq, dkp, dkc, dvp, dvc = vjp(cts)
        fill = lambda t: t.at[s - DA_BLOCK:].set(0.0)
        return dq, dkc + fill(dkp), dvc + fill(dvp)

    attn.defvjp(attn_fwd, attn_bwd)
    return attn(q, k, v)


def combine_patterns(name, outs, lses):
    s, width = outs[0].shape
    n = len(outs)

    def f(*vals):
        o, lse = vals[:n], vals[n:]
        mx = functools.reduce(jnp.maximum, lse)
        e = [jnp.exp(t - mx) for t in lse]
        return (sum(ei * oi for ei, oi in zip(e, o)) / sum(e),)

    tile = ROW_TILE // 2
    op = blockwise(name, f, (s // tile,), [((tile, width), _row, "tile")] * (2 * n),
                   [((s, width), BF16, (tile, width), _row)])
    return op(*outs, *lses)[0]


def _sub_order(t, r):
    if r == 1:
        return t
    s, c = t.shape
    return t.reshape(s // r, r, c).transpose(1, 0, 2).reshape(s, c)


def _seq_order(t, r):
    if r == 1:
        return t
    s, c = t.shape
    return t.reshape(r, s // r, c).transpose(1, 0, 2).reshape(s, c)


def _raw_bmm(a, b, ca, cb):
    return lax.dot_general(a.astype(BF16), b.astype(BF16), (((ca,), (cb,)), ((0,), (0,))),
                           preferred_element_type=F32)


def _split(a):
    hi = a.astype(BF16)
    return hi, (a - hi.astype(F32)).astype(BF16)


def _passes_bmm(a, b, ca, cb, passes):
    if passes == 1:
        return _raw_bmm(a, b, ca, cb)
    (a_hi, a_lo), (b_hi, b_lo) = _split(a), _split(b)
    return _raw_bmm(a_hi, b_hi, ca, cb) + (_raw_bmm(a_hi, b_lo, ca, cb) + _raw_bmm(a_lo, b_hi, ca, cb))


def _bmm(a, b, ca, cb, passes=1):
    fa, fb = 3 - ca, 3 - cb

    @jax.custom_vjp
    def mm(a, b):
        return _passes_bmm(a, b, ca, cb, passes)

    def mm_fwd(a, b):
        return mm(a, b), (a, b)

    def mm_bwd(res, ct):
        a, b = res
        da = (_passes_bmm(ct, b, 2, fb, passes) if ca == 2
              else _passes_bmm(b, ct, fb, 2, passes))
        db = (_passes_bmm(a, ct, fa, 1, passes) if cb == 1
              else _passes_bmm(ct, a, 1, fa, passes))
        return da.astype(a.dtype), db.astype(b.dtype)

    mm.defvjp(mm_fwd, mm_bwd)
    return mm(a, b)


def _delta_chunk(q, k, v, gcol, grow, bcol, state):
    c = q.shape[1]
    ii = lax.broadcasted_iota(jnp.int32, (1, c, c), 1)
    jj = lax.broadcasted_iota(jnp.int32, (1, c, c), 2)
    incl, strict = ii >= jj, ii > jj
    gc_col = jnp.sum(jnp.where(incl, grow, 0.0), axis=2, keepdims=True)
    gc_row = jnp.sum(jnp.where(ii <= jj, gcol, 0.0), axis=1, keepdims=True)
    decay = jnp.where(incl, jnp.exp(jnp.where(incl, gc_col - gc_row, 0.0)), 0.0)
    kb, vb = k * bcol, v * bcol
    m = jnp.where(strict, _bmm(kb, k, 2, 2) * decay, 0.0)
    eye = (ii == jj).astype(F32)
    p = -m
    inv = eye + p
    for _ in range(int(math.log2(c)) - 1):
        p = _bmm(p, p, 2, 1, 3)
        inv = inv + _bmm(inv, p, 2, 1, 3)
    e_col = jnp.exp(gc_col)
    u = _bmm(inv, vb, 2, 1)
    w = _bmm(inv, kb * e_col, 2, 1)
    qk = _bmm(q, k, 2, 2) * decay
    v_new = u - _bmm(w, state, 2, 1)
    o = _bmm(q * e_col, state, 2, 1) + _bmm(qk, v_new, 2, 1)
    g_last = jnp.sum(grow, axis=2, keepdims=True)
    new_state = state * jnp.exp(g_last) + _bmm(k * jnp.exp(g_last - gc_col), v_new, 1, 1)
    return o, new_state


def _delta_chunk_packed(q, k, v, gb, state):
    nh, c = q.shape[0], q.shape[1]
    lane = lax.broadcasted_iota(jnp.int32, gb.shape, 1)
    eye = (lax.broadcasted_iota(jnp.int32, (c, c), 0) == lax.broadcasted_iota(jnp.int32, (c, c), 1))
    column = lambda l: jnp.sum(jnp.where(lane == l, gb, 0.0), axis=1, keepdims=True)
    heads = lambda parts: jnp.concatenate([t[None] for t in parts], axis=0)
    bcol = heads([column(h) for h in range(nh)])
    gcols = [column(nh + h) for h in range(nh)]
    gcol = heads(gcols)
    grow = heads([jnp.sum(jnp.where(eye, g, 0.0), axis=0, keepdims=True) for g in gcols])
    return _delta_chunk(q, k, v, gcol, grow, bcol, state)


def _delta_specs(nh, s, dh, rev):
    c, n = DN_CHUNK, s // DN_CHUNK
    t = (lambda i: n - 1 - i) if rev else (lambda i: i)
    seq = pl.BlockSpec((nh, c, dh), lambda i: (0, t(i), 0))
    gate = pl.BlockSpec((c, LANES), lambda i: (t(i), 0))
    st = pl.BlockSpec((nh, 1, dh, dh), lambda i: (0, t(i), 0, 0))
    return seq, gate, st


def delta_fwd(name, q, k, v, gb, gather=None):
    nh, s, dh = q.shape
    n = s // DN_CHUNK
    seq, gate, st = _delta_specs(nh, s, dh, False)
    extra = list(gather or [])
    ne = len(extra)

    def body(*refs):
        q_ref, k_ref, v_ref, gb_ref = refs[:4]
        x_refs = refs[4:4 + ne]
        o_ref, st_ref = refs[4 + ne:6 + ne]
        out_refs = refs[6 + ne:6 + 2 * ne]
        state = refs[6 + 2 * ne]
        i = pl.program_id(0)
        if ne:
            begin, finish = _gather_phases(x_refs, out_refs, refs[7 + 2 * ne:])
            pl.when(i == 0)(begin)

        @pl.when(i == 0)
        def _():
            state[...] = jnp.zeros_like(state)

        st_ref[:, 0] = state[...]
        o, new_state = _delta_chunk_packed(q_ref[...], k_ref[...], v_ref[...], gb_ref[...], state[...])
        o_ref[...] = o
        state[...] = new_state
        if ne:
            pl.when(i == n - 1)(finish)

    any_spec = pl.BlockSpec(memory_space=pl.ANY)
    out = pl.pallas_call(
        body, name=name + "_fwd", grid=(n,),
        in_specs=[seq, seq, seq, gate] + [any_spec] * ne,
        out_specs=[seq, st] + [any_spec] * ne,
        out_shape=[jax.ShapeDtypeStruct((nh, s, dh), F32),
                   jax.ShapeDtypeStruct((nh, n, dh, dh), F32)]
        + [jax.ShapeDtypeStruct((N_DEV,) + t.shape, t.dtype) for t in extra],
        scratch_shapes=[pltpu.VMEM((nh, dh, dh), F32)]
        + ([pltpu.SemaphoreType.DMA((7 * ne,)), pltpu.SemaphoreType.DMA((7 * ne,)),
            pltpu.SemaphoreType.DMA((ne,))] if ne else []),
        compiler_params=_cparams(1),
    )(q, k, v, gb, *extra)
    return out[0], out[1], list(out[2:])


def delta_bwd(name, q, k, v, gb, states, do, exchange=None):
    nh, s, dh = q.shape
    n = s // DN_CHUNK
    seq, gate, st = _delta_specs(nh, s, dh, True)
    extra = list(exchange or [])
    ne = len(extra)

    def body(*refs):
        q_ref, k_ref, v_ref, gb_ref, st_ref, do_ref = refs[:6]
        p_refs = refs[6:6 + ne]
        dq_ref, dk_ref, dv_ref, dgb_ref = refs[6 + ne:10 + ne]
        land_refs = refs[10 + ne:10 + 2 * ne]
        dstate = refs[10 + 2 * ne]
        i = pl.program_id(0)
        if ne:
            begin, finish = _chip_exchange_phases(p_refs, land_refs, refs[11 + 2 * ne:])
            pl.when(i == 0)(begin)

        @pl.when(i == 0)
        def _():
            dstate[...] = jnp.zeros_like(dstate)

        _, vjp = jax.vjp(_delta_chunk_packed, q_ref[...], k_ref[...], v_ref[...], gb_ref[...],
                         st_ref[:, 0])
        dq, dk, dv, dgb, dst = vjp((do_ref[...], dstate[...]))
        dq_ref[...] = dq
        dk_ref[...] = dk
        dv_ref[...] = dv
        dgb_ref[...] = dgb
        dstate[...] = dst
        if ne:
            pl.when(i == n - 1)(finish)

    any_spec = pl.BlockSpec(memory_space=pl.ANY)
    out = pl.pallas_call(
        body, name=name + "_bwd", grid=(n,),
        in_specs=[seq, seq, seq, gate, st, seq] + [any_spec] * ne,
        out_specs=[seq, seq, seq, gate] + [any_spec] * ne,
        out_shape=[jax.ShapeDtypeStruct((nh, s, dh), F32)] * 3
        + [jax.ShapeDtypeStruct((s, LANES), F32)]
        + [jax.ShapeDtypeStruct(t.shape, t.dtype) for t in extra],
        scratch_shapes=[pltpu.VMEM((nh, dh, dh), F32)]
        + ([pltpu.SemaphoreType.DMA((3 * ne,)), pltpu.SemaphoreType.DMA((3 * ne,)),
            pltpu.SemaphoreType.DMA((ne,))] if ne else []),
        compiler_params=_cparams(1),
    )(q, k, v, gb, states, do, *extra)
    return tuple(out[:4]), list(out[4:])


def delta_rule(name, q, k, v, gb):
    @jax.custom_vjp
    def op(q, k, v, gb):
        return delta_fwd(name, q, k, v, gb)[0]

    def op_fwd(q, k, v, gb):
        o, states, _ = delta_fwd(name, q, k, v, gb)
        return o, (q, k, v, gb, states)

    def op_bwd(res, do):
        return delta_bwd(name, *res, do)[0]

    op.defvjp(op_fwd, op_bwd)
    return op(q, k, v, gb)


def _my_place():
    return lax.axis_index("x"), lax.axis_index("y"), lax.axis_index("c")


def all_gather(name, shard):
    return all_gather_many(name, [shard])[0]


def _gather_phases(x_refs, out_refs, sems):
    n = len(x_refs)
    send_sems, recv_sems, local_sems = sems
    x, y, c = _my_place()
    me, sibling = (x, y, c), (x, y, 1 - c)
    chips = [(1 - x, y), (x, 1 - y), (1 - x, 1 - y)]

    def copy(i, k, block, to, own=False):
        px, py, pc = block
        slot = out_refs[i].at[4 * px + 2 * py + pc]
        return pltpu.make_async_remote_copy(
            src_ref=x_refs[i] if own else slot, dst_ref=slot,
            send_sem=send_sems.at[7 * i + k], recv_sem=recv_sems.at[7 * i + k],
            device_id=to, device_id_type=MESH)

    mine = [pltpu.make_async_copy(x_refs[i], out_refs[i].at[4 * x + 2 * y + c], local_sems.at[i])
            for i in range(n)]
    first = []
    for i in range(n):
        first.append(copy(i, 0, me, sibling, own=True))
        first += [copy(i, 1 + j, me, (*chip, c), own=True) for j, chip in enumerate(chips)]

    def begin():
        for cp in mine + first:
            cp.start()

    def finish():
        passed = []
        for j, chip in enumerate(chips):
            for i in range(n):
                copy(i, 1 + j, (*chip, c), me).wait_recv()
                passed.append(copy(i, 4 + j, (*chip, c), sibling))
                passed[-1].start()
        for i in range(n):
            copy(i, 0, sibling, me).wait_recv()
        for j, chip in enumerate(chips):
            for i in range(n):
                copy(i, 4 + j, (*chip, 1 - c), me).wait_recv()
        for cp in first + passed:
            cp.wait_send()
        for cp in mine:
            cp.wait()

    return begin, finish


def all_gather_many(name, shards):
    n = len(shards)

    def body(*refs):
        begin, finish = _gather_phases(refs[:n], refs[n:2 * n], refs[2 * n:])
        begin()
        finish()

    any_spec = pl.BlockSpec(memory_space=pl.ANY)
    return pl.pallas_call(
        body, name=name,
        out_shape=[jax.ShapeDtypeStruct((N_DEV,) + t.shape, t.dtype) for t in shards],
        in_specs=[any_spec] * n, out_specs=[any_spec] * n,
        scratch_shapes=[pltpu.SemaphoreType.DMA((7 * n,)), pltpu.SemaphoreType.DMA((7 * n,)),
                        pltpu.SemaphoreType.DMA((n,))],
    )(*shards)


def sibling_exchange(name, parts):
    n = len(parts)

    def body(*refs):
        p_refs, out_refs = refs[:n], refs[n:2 * n]
        send_sems, recv_sems = refs[2 * n:]
        x, y, c = _my_place()
        copies = []
        for i in range(n):
            for chip in range(4):
                copies.append(pltpu.make_async_remote_copy(
                    src_ref=p_refs[i].at[2 * chip + (1 - c)], dst_ref=out_refs[i].at[chip],
                    send_sem=send_sems.at[4 * i + chip], recv_sem=recv_sems.at[4 * i + chip],
                    device_id=(x, y, 1 - c), device_id_type=MESH))
        for cp in copies:
            cp.start()
        for cp in copies:
            cp.wait_recv()
        for cp in copies:
            cp.wait_send()

    any_spec = pl.BlockSpec(memory_space=pl.ANY)
    return pl.pallas_call(
        body, name=name,
        out_shape=[jax.ShapeDtypeStruct((4,) + t.shape[1:], t.dtype) for t in parts],
        in_specs=[any_spec] * n, out_specs=[any_spec] * n,
        scratch_shapes=[pltpu.SemaphoreType.DMA((4 * n,)), pltpu.SemaphoreType.DMA((4 * n,))],
    )(*parts)


def _chip_exchange_phases(p_refs, out_refs, sems):
    n = len(p_refs)
    send_sems, recv_sems, local_sems = sems
    x, y, c = _my_place()
    me = 2 * x + y
    peers = [(1 - x, y), (x, 1 - y), (1 - x, 1 - y)]

    def copy(i, j, landing):
        px, py = peers[j]
        return pltpu.make_async_remote_copy(
            src_ref=p_refs[i].at[2 * px + py],
            dst_ref=out_refs[i].at[(2 * px + py) if landing else me],
            send_sem=send_sems.at[3 * i + j], recv_sem=recv_sems.at[3 * i + j],
            device_id=(px, py, c), device_id_type=MESH)

    mine = [pltpu.make_async_copy(p_refs[i].at[me], out_refs[i].at[me], local_sems.at[i])
            for i in range(n)]
    copies = [copy(i, j, False) for j in range(3) for i in range(n)]

    def begin():
        for cp in mine + copies:
            cp.start()

    def finish():
        for j in range(3):
            for i in range(n):
                copy(i, j, True).wait_recv()
        for cp in copies:
            cp.wait_send()
        for cp in mine:
            cp.wait()

    return begin, finish


def chip_exchange(name, parts):
    n = len(parts)

    def body(*refs):
        begin, finish = _chip_exchange_phases(refs[:n], refs[n:2 * n], refs[2 * n:])
        begin()
        finish()

    any_spec = pl.BlockSpec(memory_space=pl.ANY)
    return pl.pallas_call(
        body, name=name,
        out_shape=[jax.ShapeDtypeStruct(t.shape, t.dtype) for t in parts],
        in_specs=[any_spec] * n, out_specs=[any_spec] * n,
        scratch_shapes=[pltpu.SemaphoreType.DMA((3 * n,)), pltpu.SemaphoreType.DMA((3 * n,)),
                        pltpu.SemaphoreType.DMA((n,))],
    )(*parts)


def pair_sum(name, mine, got, core):
    _, rows, cols = got.shape
    row_bytes = 4 * LANES * (-(-cols // LANES))
    tile = _pick(rows, [t for t in (512, 352, 256, 128, 64, 32, 16, 8)
                        if t * row_bytes <= ADAMW_BLOCK_BYTES])

    def body(core_ref, a_ref, b_ref, o_ref):
        o_ref[...] = (a_ref[...].astype(F32) + b_ref[...].astype(F32)).astype(o_ref.dtype)

    spec = pl.BlockSpec((4, tile, cols), lambda i, core_ref: (0, i, 0))
    return pl.pallas_call(
        body, name=name,
        grid_spec=pltpu.PrefetchScalarGridSpec(
            num_scalar_prefetch=1, grid=(rows // tile,),
            in_specs=[pl.BlockSpec((4, None, tile, cols), lambda i, core_ref: (0, core_ref[0], i, 0)),
                      spec],
            out_specs=spec),
        out_shape=jax.ShapeDtypeStruct(got.shape, got.dtype), compiler_params=_cparams(1),
    )(core, mine, got)


ADAMW_BLOCK_BYTES = 3 * 512 * 1024


def adamw(name, grad, w, m, v):
    rows, cols = w.shape
    stacked = grad.ndim == 3
    row_bytes = 4 * LANES * (-(-cols // LANES))
    tile = _pick(rows, [t for t in (512, 352, 256, 128, 64, 32, 16, 8)
                        if t * row_bytes <= ADAMW_BLOCK_BYTES])

    def body(g_ref, w_ref, m_ref, v_ref, go_ref, d_ref, mo_ref, vo_ref):
        if stacked:
            g = g_ref[0].astype(F32)
            for s in range(1, grad.shape[0]):
                g = g + g_ref[s].astype(F32)
        else:
            g = g_ref[...]
        m = ADAM_B1 * m_ref[...] + (1.0 - ADAM_B1) * g
        v = ADAM_B2 * v_ref[...] + (1.0 - ADAM_B2) * jnp.square(g)
        m_hat = m / (1.0 - ADAM_B1 ** ADAM_STEP)
        v_hat = v / (1.0 - ADAM_B2 ** ADAM_STEP)
        go_ref[...] = g
        d_ref[...] = -ADAM_LR * (m_hat / (jnp.sqrt(v_hat) + ADAM_EPS) + ADAM_WD * w_ref[...])
        mo_ref[...] = m
        vo_ref[...] = v

    flat = pl.BlockSpec((tile, cols), lambda i: (i, 0))
    g_spec = pl.BlockSpec((grad.shape[0], tile, cols), lambda i: (0, i, 0)) if stacked else flat
    return pl.pallas_call(
        body, name=name, grid=(rows // tile,),
        in_specs=[g_spec, flat, flat, flat], out_specs=[flat] * 4,
        out_shape=[jax.ShapeDtypeStruct((rows, cols), F32)] * 4,
        compiler_params=_cparams(1),
    )(grad, w, m, v)


def silu_rows(name, x):
    def body(x_ref, o_ref):
        o_ref[...] = _silu(x_ref[...])

    return pl.pallas_call(body, name=name, out_shape=jax.ShapeDtypeStruct(x.shape, F32))(x)


BIG = (("ffn1_wg", (D, D_FF // N_DEV), 1), ("ffn1_wu", (D, D_FF // N_DEV), 1),
       ("ffn1_wd", (D_FF // N_DEV, D), 0), ("w_in", (D, IN_COLS // N_DEV), 1),
       ("conv_w", (DN_CONV, 3 * D // N_DEV), 1), ("w_a", (D // N_DEV, D), 0),
       ("w_b", (DA_HEADS * DA_DIM, D // N_DEV), 1), ("w_o", (D // N_DEV, D), 0),
       ("ffn2_wg", (D, D_FF // N_DEV), 1), ("ffn2_wu", (D, D_FF // N_DEV), 1),
       ("ffn2_wd", (D_FF // N_DEV, D), 0))
SMALL = (("ada_b", (DEPTH, N_ADA * D)), ("ln_ffn1", (DEPTH, D)), ("ln_mix", (DEPTH, D)),
         ("ln_ffn2", (DEPTH, D)), ("a_log", (DEPTH, DN_HEADS)), ("dt_bias", (DEPTH, DN_HEADS)),
         ("dn_norm", (DEPTH, DN_DIM)), ("final_norm", (D,)))
SMALL_ROWS = 32


def _pack(arrays, rows):
    flat = jnp.concatenate([a.reshape(-1) for a in arrays])
    return jnp.pad(flat, (0, rows * D - flat.shape[0])).reshape(rows, D)


def _unpack(buf, shapes):
    flat = buf.reshape(-1)
    out, off = [], 0
    for shp in shapes:
        n = int(np.prod(shp))
        out.append(flat[off:off + n].reshape(shp))
        off += n
    return out


def _full_weights(gathered, entries):
    full = {}
    for (name, (a, b), axis), t in zip(entries, gathered):
        if axis == 1:
            full[name] = t.transpose(1, 0, 2).reshape(a, N_DEV * b)
        else:
            full[name] = t.reshape(N_DEV * a, b)
    return full


_Z0, _B0, _A0, _DQ0, _GA0 = 3072, 4096, 4104, 4112, 6416


def _reorder_in_proj(w):
    pad = jnp.zeros((w.shape[0], IN_COLS_PAD - IN_COLS), w.dtype)
    return jnp.concatenate([w[:, :_B0], w[:, _DQ0:], w[:, _B0:_DQ0], pad], axis=1)


def _ffn(tag, h, ln, shift, scale, gate, w_gu, w_d):
    n = norm_mod(tag + "_norm", h, ln, shift, scale)
    gu = linear(tag + "_gu", n, w_gu, BF16)
    a = swiglu_act(tag + "_act", gu)
    f = linear(tag + "_down", a, w_d, F32)
    return residual(tag + "_res", h, f, gate, 0.5)


N_PRE = 5


def _mods(mod, l):
    return [mod[l, i * D:(i + 1) * D][None] for i in range(N_ADA)]


def _layer_pre(l, h, gathered, small, mod):
    w = _full_weights(gathered, BIG[:N_PRE])
    tag = f"l{l}"
    sh1, sc1, gt1, sh2, sc2 = _mods(mod, l)[:5]
    w_gu1 = _interleave_gate_up(w["ffn1_wg"], w["ffn1_wu"])
    h = _ffn(tag + "_ffn1", h, small["ln_ffn1"][l][None], sh1, sc1, gt1, w_gu1, w["ffn1_wd"])
    u = norm_mod(tag + "_mixnorm", h, small["ln_mix"][l][None], sh2, sc2)
    (q_pre, k_pre, v_pre, z, da_q, da_k, da_v, gate_a, gate_b, ab) = linear_split(
        tag + "_mix_in", u, _reorder_in_proj(w["w_in"]), (4, 4, 4, 4, 3, 3, 3, 4, 4, 1),
        (BF16,) * 9 + (F32,))
    cw = w["conv_w"].astype(F32).reshape(DN_CONV, 1, 3 * D)
    q, k, v = [conv_heads(f"{tag}_mix_conv_{m}", t, cw[:, :, i * D:(i + 1) * D], m)
               for i, (m, t) in enumerate(zip("qkv", (q_pre, k_pre, v_pre)))]
    pad = lambda t: jnp.pad(t, (DN_HEADS, ab.shape[1] - 2 * DN_HEADS))[None]
    gb = decay_beta(tag + "_mix_decay", ab, pad(small["a_log"][l]), pad(small["dt_bias"][l]))
    return (q, k, v, gb[:, :LANES]), (h, z, da_q, da_k, da_v, gate_a, gate_b)


def _layer_post(l, o, carry, gathered, small, mod):
    w = _full_weights(gathered, BIG[N_PRE:])
    tag = f"l{l}"
    h, z, da_q, da_k, da_v, gate_a, gate_b = carry
    gt2, sh3, sc3, gt3 = _mods(mod, l)[5:]
    o_a = gated_head_norm(tag + "_mix_gnorm", o, z, small["dn_norm"][l][None])
    y_a = linear(tag + "_mix_wa", o_a, w["w_a"], F32)

    outs, lses = [], []
    for r in DA_DILATIONS:
        o_r, lse_r = dilated_attention(f"{tag}_mix_attn{r}",
                                       *[_sub_order(t, r) for t in (da_q, da_k, da_v)], r)
        outs.append(_seq_order(o_r, r))
        lses.append(_seq_order(lse_r, r))
    o_b = combine_patterns(tag + "_mix_comb", outs, lses)
    y_b = linear(tag + "_mix_wb", o_b, w["w_b"], F32)

    merged = merge_gates(tag + "_mix_merge", gate_a, gate_b, y_a, y_b)
    m = linear(tag + "_mix_wo", merged, w["w_o"], F32)
    h = residual(tag + "_mixres", h, m, gt2, 1.0)
    w_gu2 = _interleave_gate_up(w["ffn2_wg"], w["ffn2_wu"])
    return _ffn(tag + "_ffn2", h, small["ln_ffn2"][l][None], sh3, sc3, gt3, w_gu2, w["ffn2_wd"])


def _layer(l, h, gathered, small, mod):
    (q, k, v, gb), carry = _layer_pre(l, h, gathered[:N_PRE], small, mod)
    o = delta_rule(f"l{l}_mix_delta", q, k, v, gb)
    return _layer_post(l, o, carry, gathered[N_PRE:], small, mod)


def _head(h, small, target):
    rows = loss_rows("loss", h, small["final_norm"][None], target)
    return 0.5 * jnp.sum(rows)


def _local_loss(x, gathered, small, mod, target):
    h = x
    for l in range(DEPTH):
        h = _layer(l, h, gathered[l], small, mod)
    return _head(h, small, target)


def kernel(x, c, ada_w, ada_b, ln_ffn1, ln_mix, ln_ffn2, ffn1_wg, ffn1_wu, ffn1_wd, w_in, conv_w, a_log, dt_bias, dn_norm, w_a, w_b, w_o, ffn2_wg, ffn2_wu, ffn2_wd, final_norm, loss_target, m_ada_w, m_ada_b, m_ln_ffn1, m_ln_mix, m_ln_ffn2, m_ffn1_wg, m_ffn1_wu, m_ffn1_wd, m_w_in, m_conv_w, m_a_log, m_dt_bias, m_dn_norm, m_w_a, m_w_b, m_w_o, m_ffn2_wg, m_ffn2_wu, m_ffn2_wd, m_final_norm, v_ada_w, v_ada_b, v_ln_ffn1, v_ln_mix, v_ln_ffn2, v_ffn1_wg, v_ffn1_wu, v_ffn1_wd, v_w_in, v_conv_w, v_a_log, v_dt_bias, v_dn_norm, v_w_a, v_w_b, v_w_o, v_ffn2_wg, v_ffn2_wu, v_ffn2_wd, v_final_norm):
    args = dict(locals())
    big_names = [n for n, _, _ in BIG]
    small_names = [n for n, _ in SMALL]
    me = 4 * lax.axis_index("x") + 2 * lax.axis_index("y") + lax.axis_index("c")
    cols = N_ADA * D // N_DEV

    c_all = all_gather("gather_c", jnp.pad(silu_rows("silu_c", c), ((0, 7), (0, 0))))[:, 0]
    mod_cols = jnp.stack([matmul(f"ada{l}", c_all, ada_w[l], "nn", F32) for l in range(DEPTH)])
    mod_cols = mod_cols + lax.dynamic_slice_in_dim(ada_b, me * cols, cols, axis=1)[:, None, :]
    mod_all = all_gather("gather_mod", mod_cols.reshape(DEPTH * N_DEV, cols))
    mod_all = mod_all.reshape(N_DEV, DEPTH, N_DEV, cols)
    mod = lax.dynamic_index_in_dim(mod_all, me, axis=2, keepdims=False)
    mod = mod.transpose(1, 0, 2).reshape(DEPTH, N_ADA * D)

    shards = [[args[n][l].astype(BF16) for n in big_names] for l in range(DEPTH)]
    pre_names, post_names = big_names[:N_PRE], big_names[N_PRE:]
    small = {n: args[n] for n in small_names if n != "ada_b"}
    target = loss_target[0]

    w0_pre = all_gather_many("gather_w0", shards[0][:N_PRE])
    pre0, vjp_pre0 = jax.vjp(functools.partial(_layer_pre, 0), x[0], w0_pre, small, mod)
    (q0, k0, v0, gb0), carry0 = pre0
    o0, states0, got = delta_fwd("l0_mix_delta", q0, k0, v0, gb0,
                                 gather=shards[0][N_PRE:] + shards[1][:N_PRE])
    w0_post, w1_pre = got[:len(post_names)], got[len(post_names):]
    h1, vjp_post0 = jax.vjp(functools.partial(_layer_post, 0), o0, carry0, w0_post, small, mod)
    pre1, vjp_pre1 = jax.vjp(functools.partial(_layer_pre, 1), h1, w1_pre, small, mod)
    (q1, k1, v1, gb1), carry1 = pre1
    o1, states1, w1_post = delta_fwd("l1_mix_delta", q1, k1, v1, gb1, gather=shards[1][N_PRE:])
    loss, vjp_post1 = jax.vjp(
        lambda o, carry, w, sm, md: _head(_layer_post(1, o, carry, w, sm, md), sm, target),
        o1, carry1, w1_post, small, mod)

    my_core = lax.axis_index("c").astype(jnp.int32).reshape(1)

    def pair_sums(tag, names, grads):
        from_sibling = sibling_exchange(f"pair_grads_{tag}", grads)
        return [pair_sum(f"pair_sum_{tag}_{n}", t.reshape(4, 2, -1, t.shape[-1]), got_n, my_core)
                for n, t, got_n in zip(names, grads, from_sibling)]

    do1, dcarry1, dw1_post, dsmall_a, dmod_a = vjp_post1(jnp.ones((), F32))
    ddelta1, landed1_post = delta_bwd("l1_mix_delta", q1, k1, v1, gb1, states1, do1,
                                      exchange=pair_sums("l1_post", post_names, list(dw1_post)))
    dh1, dw1_pre, dsmall_b, dmod_b = vjp_pre1((ddelta1, dcarry1))
    do0, dcarry0, dw0_post, dsmall_c, dmod_c = vjp_post0(dh1)
    ddelta0, landed_mid = delta_bwd(
        "l0_mix_delta", q0, k0, v0, gb0, states0, do0,
        exchange=pair_sums("mid", pre_names + post_names, list(dw1_pre) + list(dw0_post)))
    dx, dw0_pre, dsmall_d, dmod_d = vjp_pre0((ddelta0, dcarry0))
    landed0_pre = chip_exchange("scatter_grads0", pair_sums("l0_pre", pre_names, list(dw0_pre)))
    landed0 = list(landed0_pre) + list(landed_mid[N_PRE:])
    landed1 = list(landed_mid[:N_PRE]) + list(landed1_post)
    dsmall = {n: dsmall_a[n] + dsmall_b[n] + dsmall_c[n] + dsmall_d[n] for n in dsmall_a}
    dmod = dmod_a + dmod_b + dmod_c + dmod_d

    part = _pack([dmod] + [dsmall[n] for n in small_names[1:]], SMALL_ROWS)
    parts = all_gather("gather_small", part)
    sm_out = adamw("adamw_small", parts, _pack([args[n] for n in small_names], SMALL_ROWS),
                   _pack([args["m_" + n] for n in small_names], SMALL_ROWS),
                   _pack([args["v_" + n] for n in small_names], SMALL_ROWS))

    dmod_all = parts.reshape(N_DEV, -1)[:, :DEPTH * N_ADA * D].reshape(N_DEV, DEPTH, N_ADA * D)
    dmod_mine = lax.dynamic_slice_in_dim(dmod_all, me * cols, cols, axis=2)
    g_ada = jnp.stack([matmul(f"ada{l}_dw", c_all, dmod_mine[:, l], "tn", F32) for l in range(DEPTH)])
    flat2 = lambda t: t.reshape(-1, t.shape[-1])
    ada_out = adamw("adamw_ada_w", flat2(g_ada), flat2(ada_w), flat2(m_ada_w), flat2(v_ada_w))

    big_out = {}
    for n, t0, t1 in zip(big_names, landed0, landed1):
        both = jnp.concatenate([t0, t1], axis=1)
        big_out[n] = adamw("adamw_" + n, both, flat2(args[n]), flat2(args["m_" + n]), flat2(args["v_" + n]))

    small_shapes = [shp for _, shp in SMALL]
    names = ["ada_w", "ada_b", "ln_ffn1", "ln_mix", "ln_ffn2", "ffn1_wg", "ffn1_wu", "ffn1_wd", "w_in",
             "conv_w", "a_log", "dt_bias", "dn_norm", "w_a", "w_b", "w_o", "ffn2_wg", "ffn2_wu",
             "ffn2_wd", "final_norm"]
    outs = [lax.psum(loss, ("x", "y", "c")), dx[None]]
    for kind in range(4):
        table = {n: big_out[n][kind].reshape(args[n].shape) for n in big_names}
        table.update(zip(small_names, _unpack(sm_out[kind], small_shapes)))
        table["ada_w"] = ada_out[kind].reshape(ada_w.shape)
        outs += [table[n] for n in names]
    return tuple(outs)
```

```python
import functools
import math

import numpy as np
import jax
import jax.numpy as jnp
from jax import lax
from jax.experimental import pallas as pl
from jax.experimental.pallas import tpu as pltpu

F32 = jnp.float32
BF16 = jnp.bfloat16

D = 1024
SEQ = 4096
DEPTH = 2
N_DEV = 8
DN_HEADS = 8
DN_DIM = 128
DN_CHUNK = 64
DN_CONV = 4
DA_HEADS = 12
DA_DIM = 64
DA_BLOCK = 128
DA_DILATIONS = (1, 4, 16)
ALIBI_MAX_EXP = 8.0
D_FF = 2816
N_ADA = 9
NORM_EPS = 1e-6
IN_COLS = 8464
IN_COLS_PAD = 8704
ADAM_LR, ADAM_B1, ADAM_B2, ADAM_EPS, ADAM_WD, ADAM_STEP = 0.001, 0.9, 0.999, 1e-08, 0.01, 10
NEG = -1e30

VMEM_LIMIT = 56 * 1024 * 1024
LANES = 128

MESH = pl.DeviceIdType.MESH


def _cparams(n_grid):
    return pltpu.CompilerParams(dimension_semantics=("arbitrary",) * n_grid,
                                vmem_limit_bytes=VMEM_LIMIT)


def blockwise(name, f, grid, ins, outs):
    n_in, n_out = len(ins), len(outs)
    diff = [i for i, (_, _, kind) in enumerate(ins) if kind != "const"]

    def apply(*vals):
        res = f(*vals)
        return tuple(r.astype(dt) for r, (_, dt, _, _) in zip(res, outs))

    def fwd_call(*arrays):
        def body(*refs):
            res = apply(*[r[...] for r in refs[:n_in]])
            for r, v in zip(refs[n_in:], res):
                r[...] = v

        return pl.pallas_call(
            body, name=name + "_fwd", grid=grid,
            in_specs=[pl.BlockSpec(b, im) for (b, im, _) in ins],
            out_specs=[pl.BlockSpec(b, im) for (_, _, b, im) in outs],
            out_shape=[jax.ShapeDtypeStruct(s, dt) for (s, dt, _, _) in outs],
            compiler_params=_cparams(len(grid)),
        )(*arrays)

    def bwd_call(arrays, cts):
        def body(*refs):
            in_refs, ct_refs = refs[:n_in], refs[n_in:n_in + n_out]
            g_refs = refs[n_in + n_out:]
            vals = [r[...] for r in in_refs]

            def fd(*dvals):
                full = list(vals)
                for i, v in zip(diff, dvals):
                    full[i] = v
                return apply(*full)

            _, vjp = jax.vjp(fd, *[vals[i] for i in diff])
            grads = vjp(tuple(r[...] for r in ct_refs))
            first = functools.reduce(jnp.logical_and,
                                     [pl.program_id(a) == 0 for a in range(len(grid))])
            for g_ref, g, i in zip(g_refs, grads, diff):
                if ins[i][2] == "acc":
                    @pl.when(first)
                    def _(g_ref=g_ref):
                        g_ref[...] = jnp.zeros_like(g_ref)
                    g_ref[...] += g.astype(F32)
                else:
                    g_ref[...] = g.astype(g_ref.dtype)

        g_shapes = [jax.ShapeDtypeStruct(arrays[i].shape,
                                         F32 if ins[i][2] == "acc" else arrays[i].dtype)
                    for i in diff]
        return pl.pallas_call(
            body, name=name + "_bwd", grid=grid,
            in_specs=([pl.BlockSpec(b, im) for (b, im, _) in ins]
                      + [pl.BlockSpec(b, im) for (_, _, b, im) in outs]),
            out_specs=[pl.BlockSpec(ins[i][0], ins[i][1]) for i in diff],
            out_shape=g_shapes,
            compiler_params=_cparams(len(grid)),
        )(*arrays, *cts)

    @jax.custom_vjp
    def op(*arrays):
        return tuple(fwd_call(*arrays))

    def op_fwd(*arrays):
        return tuple(fwd_call(*arrays)), arrays

    def op_bwd(arrays, cts):
        grads = bwd_call(arrays, cts)
        full = [None] * n_in
        for i, g in zip(diff, grads):
            full[i] = g.astype(arrays[i].dtype)
        return tuple(full)

    op.defvjp(op_fwd, op_bwd)
    return op


def _pick(n, cands):
    for c in cands:
        if n % c == 0:
            return c
    return n


def matmul(name, a, b, form, out_dtype):
    if form == "nn":
        (m, k), (_, n) = a.shape, b.shape
    elif form == "nt":
        (m, k), (n, _) = a.shape, b.shape
    else:
        (k, m), (_, n) = a.shape, b.shape
    tm = _pick(m, (1408, 1024, 512, 256, 128, 8))
    tn = _pick(n, (1408, 1024, 512, 384, 256, 128))
    tk = _pick(k, (1024, 1408, 512, 384, 256, 128, 8))
    nk = k // tk
    a_spec = (pl.BlockSpec((tk, tm), lambda i, j, kk: (kk, i)) if form == "tn"
              else pl.BlockSpec((tm, tk), lambda i, j, kk: (i, kk)))
    b_spec = (pl.BlockSpec((tn, tk), lambda i, j, kk: (j, kk)) if form == "nt"
              else pl.BlockSpec((tk, tn), lambda i, j, kk: (kk, j)))
    dims = {"nn": (((1,), (0,)), ((), ())), "nt": (((1,), (1,)), ((), ())),
            "tn": (((0,), (0,)), ((), ()))}[form]

    def body(a_ref, b_ref, o_ref, acc_ref):
        kk = pl.program_id(2)
        part = lax.dot_general(a_ref[...].astype(BF16), b_ref[...].astype(BF16), dims,
                               preferred_element_type=F32)

        @pl.when(kk == 0)
        def _():
            acc_ref[...] = part

        @pl.when(kk > 0)
        def _():
            acc_ref[...] += part

        @pl.when(kk == nk - 1)
        def _():
            o_ref[...] = acc_ref[...].astype(o_ref.dtype)

    return pl.pallas_call(
        body, name=name, grid=(m // tm, n // tn, nk),
        in_specs=[a_spec, b_spec],
        out_specs=pl.BlockSpec((tm, tn), lambda i, j, kk: (i, j)),
        out_shape=jax.ShapeDtypeStruct((m, n), out_dtype),
        scratch_shapes=[pltpu.VMEM((tm, tn), F32)],
        compiler_params=_cparams(3),
    )(a, b)


def linear(name, x, w, out_dtype):
    @jax.custom_vjp
    def op(x, w):
        return matmul(name + "_y", x, w, "nn", out_dtype)

    def op_fwd(x, w):
        return op(x, w), (x, w)

    def op_bwd(res, dy):
        x, w = res
        dx = matmul(name + "_dx", dy, w, "nt", x.dtype)
        dw = matmul(name + "_dw", x, dy, "tn", w.dtype)
        return dx, dw

    op.defvjp(op_fwd, op_bwd)
    return op(x, w)


def linear_split(name, x, w, tile, seg_tiles, out_dtypes):
    SPLIT_TILE = tile
    m, k = x.shape
    n = w.shape[1]
    nt = n // SPLIT_TILE
    starts = [sum(seg_tiles[:s]) for s in range(len(seg_tiles))]
    assert sum(seg_tiles) == nt
    tm = _pick(m, (1024, 512, 256, 128, 8))

    def inside(j, s):
        return jnp.logical_and(j >= starts[s], j < starts[s] + seg_tiles[s])

    def local(j, s):
        return jnp.clip(j - starts[s], 0, seg_tiles[s] - 1)

    def fwd_call(x, w):
        def body(x_ref, w_ref, *o_refs):
            j = pl.program_id(1)
            y = jnp.dot(x_ref[...].astype(BF16), w_ref[...].astype(BF16), preferred_element_type=F32)
            for s, o_ref in enumerate(o_refs):
                @pl.when(inside(j, s))
                def _(o_ref=o_ref):
                    o_ref[...] = y.astype(o_ref.dtype)

        return pl.pallas_call(
            body, name=name + "_y", grid=(m // tm, nt),
            in_specs=[pl.BlockSpec((tm, k), lambda i, j: (i, 0)),
                      pl.BlockSpec((k, SPLIT_TILE), lambda i, j: (0, j))],
            out_specs=[pl.BlockSpec((tm, SPLIT_TILE), lambda i, j, s=s: (i, local(j, s)))
                       for s in range(len(seg_tiles))],
            out_shape=[jax.ShapeDtypeStruct((m, t * SPLIT_TILE), dt)
                       for t, dt in zip(seg_tiles, out_dtypes)],
            compiler_params=_cparams(2),
        )(x, w)

    def dx_call(dys, w):
        def body(*refs):
            dy_refs, w_ref, o_ref, acc_ref = refs[:-3], refs[-3], refs[-2], refs[-1]
            j = pl.program_id(1)
            for s, dy_ref in enumerate(dy_refs):
                @pl.when(inside(j, s))
                def _(dy_ref=dy_ref):
                    part = lax.dot_general(dy_ref[...].astype(BF16), w_ref[...].astype(BF16),
                                           (((1,), (1,)), ((), ())), preferred_element_type=F32)

                    @pl.when(j == 0)
                    def _():
                        acc_ref[...] = part

                    @pl.when(j > 0)
                    def _():
                        acc_ref[...] += part

            @pl.when(j == nt - 1)
            def _():
                o_ref[...] = acc_ref[...].astype(o_ref.dtype)

        return pl.pallas_call(
            body, name=name + "_dx", grid=(m // tm, nt),
            in_specs=[pl.BlockSpec((tm, SPLIT_TILE), lambda i, j, s=s: (i, local(j, s)))
                      for s in range(len(seg_tiles))]
            + [pl.BlockSpec((k, SPLIT_TILE), lambda i, j: (0, j))],
            out_specs=pl.BlockSpec((tm, k), lambda i, j: (i, 0)),
            out_shape=jax.ShapeDtypeStruct((m, k), x.dtype),
            scratch_shapes=[pltpu.VMEM((tm, k), F32)],
            compiler_params=_cparams(2),
        )(*dys, w)

    def dw_call(x, dys):
        nm = m // tm

        def body(*refs):
            x_ref, dy_refs, o_ref, acc_ref = refs[0], refs[1:-2], refs[-2], refs[-1]
            j, kk = pl.program_id(0), pl.program_id(1)
            for s, dy_ref in enumerate(dy_refs):
                @pl.when(inside(j, s))
                def _(dy_ref=dy_ref):
                    rows = x_ref[pl.ds(pl.multiple_of(kk * tm, tm), tm), :]
                    part = lax.dot_general(rows.astype(BF16), dy_ref[...].astype(BF16),
                                           (((0,), (0,)), ((), ())), preferred_element_type=F32)

                    @pl.when(kk == 0)
                    def _():
                        acc_ref[...] = part

                    @pl.when(kk > 0)
                    def _():
                        acc_ref[...] += part

            @pl.when(kk == nm - 1)
            def _():
                o_ref[...] = acc_ref[...].astype(o_ref.dtype)

        return pl.pallas_call(
            body, name=name + "_dw", grid=(nt, nm),
            in_specs=[pl.BlockSpec((m, k), lambda j, kk: (0, 0))]
            + [pl.BlockSpec((tm, SPLIT_TILE),
                            lambda j, kk, s=s: (jnp.where(inside(j, s), kk, 0), local(j, s)))
               for s in range(len(seg_tiles))],
            out_specs=pl.BlockSpec((k, SPLIT_TILE), lambda j, kk: (0, j)),
            out_shape=jax.ShapeDtypeStruct((k, n), w.dtype),
            scratch_shapes=[pltpu.VMEM((k, SPLIT_TILE), F32)],
            compiler_params=_cparams(2),
        )(x, *dys)

    @jax.custom_vjp
    def op(x, w):
        return tuple(fwd_call(x, w))

    def op_fwd(x, w):
        return op(x, w), (x, w)

    def op_bwd(res, dys):
        x, w = res
        return dx_call(dys, w), dw_call(x, dys)

    op.defvjp(op_fwd, op_bwd)
    return op(x, w)


@jax.custom_vjp
def _sigmoid(x):
    return 1.0 / (1.0 + jnp.exp(-x))


def _sigmoid_fwd(x):
    s = _sigmoid(x)
    return s, s


def _sigmoid_bwd(s, ct):
    return (ct * (s * (1.0 - s)),)


_sigmoid.defvjp(_sigmoid_fwd, _sigmoid_bwd)


@jax.custom_vjp
def _silu(x):
    return x * _sigmoid(x)


def _silu_fwd(x):
    s = _sigmoid(x)
    return x * s, (x, s)


def _silu_bwd(res, ct):
    x, s = res
    return (ct * (s * (1.0 + x * (1.0 - s))),)


_silu.defvjp(_silu_fwd, _silu_bwd)


def _softplus(x):
    return jnp.maximum(x, 0.0) + jnp.log(1.0 + jnp.exp(-jnp.abs(x)))


def _rms(x):
    return x * lax.rsqrt(jnp.mean(x * x, axis=-1, keepdims=True) + NORM_EPS)


ROW_TILE = 512


def _row(i):
    return (i, 0)


def _fixed(*_):
    return (0, 0)


def norm_mod(name, h, ln, shift, scale):
    s, d = h.shape

    def f(h, ln, sh, sc):
        return ((_rms(h) * ln) * (1.0 + sc) + sh,)

    op = blockwise(name, f, (s // ROW_TILE,),
                   [((ROW_TILE, d), _row, "tile")] + [((1, d), _fixed, "acc")] * 3,
                   [((s, d), BF16, (ROW_TILE, d), _row)])
    return op(h, ln, shift, scale)[0]


FF_TILE = 1408


def _interleave_gate_up(wg, wu):
    parts = []
    for j in range(wg.shape[1] // FF_TILE):
        parts += [wg[:, j * FF_TILE:(j + 1) * FF_TILE], wu[:, j * FF_TILE:(j + 1) * FF_TILE]]
    return jnp.concatenate(parts, axis=1)


def swiglu_act(name, gu):
    s, f2 = gu.shape
    ff = f2 // 2

    def f(gu):
        g, u = gu[:, :FF_TILE].astype(F32), gu[:, FF_TILE:].astype(F32)
        return (_silu(g) * u,)

    op = blockwise(name, f, (s // ROW_TILE, ff // FF_TILE),
                   [((ROW_TILE, 2 * FF_TILE), lambda i, j: (i, j), "tile")],
                   [((s, ff), BF16, (ROW_TILE, FF_TILE), lambda i, j: (i, j))])
    return op(gu)[0]


def residual(name, h, y, gate, weight):
    s, d = h.shape

    def f(h, y, gate):
        return (h + (weight * gate) * y,)

    op = blockwise(name, f, (s // ROW_TILE,),
                   [((ROW_TILE, d), _row, "tile"), ((ROW_TILE, d), _row, "tile"),
                    ((1, d), _fixed, "acc")],
                   [((s, d), F32, (ROW_TILE, d), _row)])
    return op(h, y, gate)[0]


def merge_gates(name, ga, gb, ya, yb):
    s, d = ya.shape

    def f(ga, gb, ya, yb):
        return (_sigmoid(ga.astype(F32)) * ya + _sigmoid(gb.astype(F32)) * yb,)

    op = blockwise(name, f, (s // ROW_TILE,), [((ROW_TILE, d), _row, "tile")] * 4,
                   [((s, d), BF16, (ROW_TILE, d), _row)])
    return op(ga, gb, ya, yb)[0]


def loss_rows(name, h, g, target):
    s, d = h.shape

    def f(h, g, t):
        err = _rms(h) * g - t
        return (jnp.mean(err * err, axis=-1, keepdims=True),)

    op = blockwise(name, f, (s // ROW_TILE,),
                   [((ROW_TILE, d), _row, "tile"), ((1, d), _fixed, "acc"),
                    ((ROW_TILE, d), _row, "const")],
                   [((s, 1), F32, (ROW_TILE, 1), _row)])
    return op(h, g, target)[0]


def decay_beta(name, ab, a_log_pad, dt_bias_pad):
    s, n = ab.shape

    def f(ab, a_log, dt_bias):
        lane = lax.broadcasted_iota(jnp.int32, ab.shape, 1)
        beta = _sigmoid(ab)
        g = -jnp.exp(a_log) * _softplus(ab + dt_bias)
        return (jnp.where(lane < DN_HEADS, beta, jnp.where(lane < 2 * DN_HEADS, g, 0.0)),)

    op = blockwise(name, f, (s // ROW_TILE,),
                   [((ROW_TILE, n), _row, "tile"), ((1, n), _fixed, "acc"), ((1, n), _fixed, "acc")],
                   [((s, n), F32, (ROW_TILE, n), _row)])
    return op(ab, a_log_pad, dt_bias_pad)[0]


def _shift_rows(x, k):
    n = x.shape[0]

    @jax.custom_vjp
    def shift(x):
        row = lax.broadcasted_iota(jnp.int32, x.shape, 0)
        return jnp.where(row >= k, pltpu.roll(x, k, 0), 0.0)

    def shift_fwd(x):
        return shift(x), None

    def shift_bwd(_, g):
        row = lax.broadcasted_iota(jnp.int32, g.shape, 0)
        return (jnp.where(row < n - k, pltpu.roll(g, n - k, 0), 0.0),)

    shift.defvjp(shift_fwd, shift_bwd)
    return shift(x)


def conv_heads(name, x, w, mode):
    s, width = x.shape
    nh = width // LANES

    def f(x, w):
        x = x.astype(F32)
        y = w[DN_CONV - 1] * x
        for j in range(DN_CONV - 1):
            y = y + w[j] * _shift_rows(x, DN_CONV - 1 - j)
        y = _silu(y)
        if mode != "v":
            y = y * lax.rsqrt(jnp.sum(y * y, axis=-1, keepdims=True) + NORM_EPS)
        if mode == "q":
            y = y * (DN_DIM ** -0.5)
        return (y[None],)

    op = blockwise(name, f, (nh,),
                   [((s, LANES), lambda j: (0, j), "tile"),
                    ((DN_CONV, 1, LANES), lambda j: (0, 0, j), "tile")],
                   [((nh, s, LANES), F32, (1, s, LANES), lambda j: (j, 0, 0))])
    return op(x, w)[0]


def gated_head_norm(name, o, z, w):
    nh, s, dh = o.shape

    def f(o, z, w):
        return (_rms(o[0]) * w * _silu(z.astype(F32)),)

    op = blockwise(name, f, (s // ROW_TILE, nh),
                   [((1, ROW_TILE, dh), lambda i, h: (h, i, 0), "tile"),
                    ((ROW_TILE, dh), lambda i, h: (i, h), "tile"),
                    ((1, dh), lambda i, h: (0, 0), "acc")],
                   [((s, nh * dh), BF16, (ROW_TILE, dh), lambda i, h: (i, h))])
    return op(o, z, w)[0]


def _mm(a, b, ca, cb):
    return _bmm(a[None], b[None], ca + 1, cb + 1)[0]


def dilated_attention(name, q, k, v, dilation):
    s, width = q.shape
    nblk = s // DA_BLOCK
    per_sub = nblk // dilation
    slopes = [dilation * 2.0 ** (-ALIBI_MAX_EXP * (h + 1) / DA_HEADS) for h in range(DA_HEADS)]

    def f(q, kp, kc, vp, vc):
        first = (pl.program_id(0) % per_sub) == 0
        qi = lax.broadcasted_iota(jnp.int32, (DA_BLOCK, 2 * DA_BLOCK), 0)
        ki = lax.broadcasted_iota(jnp.int32, (DA_BLOCK, 2 * DA_BLOCK), 1)
        steps = (qi + DA_BLOCK - ki).astype(F32)
        lowest = qi + first.astype(jnp.int32) * (DA_BLOCK - qi)
        valid = jnp.logical_and(ki >= lowest, ki <= qi + DA_BLOCK)
        top = lax.broadcasted_iota(jnp.int32, (DA_BLOCK, LANES), 1) < DA_DIM
        o_parts, lse_parts = [], []
        for pair in range(width // LANES):
            cols = slice(pair * LANES, (pair + 1) * LANES)
            q2 = jnp.concatenate([jnp.where(top, q[:, cols], 0.0), jnp.where(top, 0.0, q[:, cols])], axis=0)
            k2 = jnp.concatenate([kp[:, cols], kc[:, cols]], axis=0)
            v2 = jnp.concatenate([vp[:, cols], vc[:, cols]], axis=0)
            bias = jnp.concatenate([jnp.where(valid, -slopes[2 * pair + half] * steps, NEG)
                                    for half in range(2)], axis=0)
            sc = _mm(q2, k2, 1, 1) * (DA_DIM ** -0.5) + bias
            mx = jnp.max(sc, axis=-1, keepdims=True)
            p = jnp.exp(sc - mx)
            l = jnp.sum(p, axis=-1, keepdims=True)
            o2 = _mm(p / l, v2, 1, 0)
            lse2 = mx + jnp.log(l)
            o_parts.append(jnp.where(top, o2[:DA_BLOCK], o2[DA_BLOCK:]))
            lse_parts.append(jnp.where(top, lse2[:DA_BLOCK], lse2[DA_BLOCK:]))
        return jnp.concatenate(o_parts, axis=1), jnp.concatenate(lse_parts, axis=1)

    blk = (DA_BLOCK, width)
    cur_map = lambda j: (j, 0)
    prev_map = lambda j: (jnp.maximum(j - 1, 0), 0)
    op = blockwise(
        name, f, (nblk,),
        [(blk, cur_map, "tile"), (blk, prev_map, "tile"), (blk, cur_map, "tile"),
         (blk, prev_map, "tile"), (blk, cur_map, "tile")],
        [((s, width), F32, blk, cur_map), ((s, width), F32, blk, cur_map)])

    @jax.custom_vjp
    def attn(q, k, v):
        return op(q, k, k, v, v)

    def attn_fwd(q, k, v):
        return attn(q, k, v), (q, k, v)

    def attn_bwd(res, cts):
        q, k, v = res
        _, vjp = jax.vjp(op, q, k, k, v, v)
        dq, dkp, dkc, dvp, dvc = vjp(cts)
        fill = lambda t: t.at[s - DA_BLOCK:].set(0.0)
        return dq, dkc + fill(dkp), dvc + fill(dvp)

    attn.defvjp(attn_fwd, attn_bwd)
    return attn(q, k, v)


def combine_patterns(name, outs, lses):
    s, width = outs[0].shape
    n = len(outs)

    def f(*vals):
        o, lse = vals[:n], vals[n:]
        mx = functools.reduce(jnp.maximum, lse)
        e = [jnp.exp(t - mx) for t in lse]
        return (sum(ei * oi for ei, oi in zip(e, o)) / sum(e),)

    tile = ROW_TILE // 2
    op = blockwise(name, f, (s // tile,), [((tile, width), _row, "tile")] * (2 * n),
                   [((s, width), BF16, (tile, width), _row)])
    return op(*outs, *lses)[0]


def _sub_order(t, r):
    if r == 1:
        return t
    s, c = t.shape
    return t.reshape(s // r, r, c).transpose(1, 0, 2).reshape(s, c)


def _seq_order(t, r):
    if r == 1:
        return t
    s, c = t.shape
    return t.reshape(r, s // r, c).transpose(1, 0, 2).reshape(s, c)


def _raw_bmm(a, b, ca, cb):
    return lax.dot_general(a.astype(BF16), b.astype(BF16), (((ca,), (cb,)), ((0,), (0,))),
                           preferred_element_type=F32)


def _split(a):
    hi = a.astype(BF16)
    return hi, (a - hi.astype(F32)).astype(BF16)


def _passes_bmm(a, b, ca, cb, passes):
    if passes == 1:
        return _raw_bmm(a, b, ca, cb)
    (a_hi, a_lo), (b_hi, b_lo) = _split(a), _split(b)
    return _raw_bmm(a_hi, b_hi, ca, cb) + (_raw_bmm(a_hi, b_lo, ca, cb) + _raw_bmm(a_lo, b_hi, ca, cb))


def _bmm(a, b, ca, cb, passes=1):
    fa, fb = 3 - ca, 3 - cb

    @jax.custom_vjp
    def mm(a, b):
        return _passes_bmm(a, b, ca, cb, passes)

    def mm_fwd(a, b):
        return mm(a, b), (a, b)

    def mm_bwd(res, ct):
        a, b = res
        da = (_passes_bmm(ct, b, 2, fb, passes) if ca == 2
              else _passes_bmm(b, ct, fb, 2, passes))
        db = (_passes_bmm(a, ct, fa, 1, passes) if cb == 1
              else _passes_bmm(ct, a, 1, fa, passes))
        return da.astype(a.dtype), db.astype(b.dtype)

    mm.defvjp(mm_fwd, mm_bwd)
    return mm(a, b)


def _delta_chunk(q, k, v, gcol, grow, bcol, state):
    c = q.shape[1]
    ii = lax.broadcasted_iota(jnp.int32, (1, c, c), 1)
    jj = lax.broadcasted_iota(jnp.int32, (1, c, c), 2)
    incl, strict = ii >= jj, ii > jj
    gc_col = jnp.sum(jnp.where(incl, grow, 0.0), axis=2, keepdims=True)
    gc_row = jnp.sum(jnp.where(ii <= jj, gcol, 0.0), axis=1, keepdims=True)
    decay = jnp.where(incl, jnp.exp(jnp.where(incl, gc_col - gc_row, 0.0)), 0.0)
    kb, vb = k * bcol, v * bcol
    m = jnp.where(strict, _bmm(kb, k, 2, 2) * decay, 0.0)
    eye = (ii == jj).astype(F32)
    p = -m
    inv = eye + p
    for _ in range(int(math.log2(c)) - 1):
        p = _bmm(p, p, 2, 1, 3)
        inv = inv + _bmm(inv, p, 2, 1, 3)
    e_col = jnp.exp(gc_col)
    u = _bmm(inv, vb, 2, 1)
    w = _bmm(inv, kb * e_col, 2, 1)
    qk = _bmm(q, k, 2, 2) * decay
    v_new = u - _bmm(w, state, 2, 1)
    o = _bmm(q * e_col, state, 2, 1) + _bmm(qk, v_new, 2, 1)
    g_last = jnp.sum(grow, axis=2, keepdims=True)
    new_state = state * jnp.exp(g_last) + _bmm(k * jnp.exp(g_last - gc_col), v_new, 1, 1)
    return o, new_state


def _delta_chunk_packed(q, k, v, gb, state):
    nh, c = q.shape[0], q.shape[1]
    lane = lax.broadcasted_iota(jnp.int32, gb.shape, 1)
    eye = (lax.broadcasted_iota(jnp.int32, (c, c), 0) == lax.broadcasted_iota(jnp.int32, (c, c), 1))
    column = lambda l: jnp.sum(jnp.where(lane == l, gb, 0.0), axis=1, keepdims=True)
    heads = lambda parts: jnp.concatenate([t[None] for t in parts], axis=0)
    bcol = heads([column(h) for h in range(nh)])
    gcols = [column(nh + h) for h in range(nh)]
    gcol = heads(gcols)
    grow = heads([jnp.sum(jnp.where(eye, g, 0.0), axis=0, keepdims=True) for g in gcols])
    return _delta_chunk(q, k, v, gcol, grow, bcol, state)


def _delta_specs(nh, s, dh, rev):
    c, n = DN_CHUNK, s // DN_CHUNK
    t = (lambda i: n - 1 - i) if rev else (lambda i: i)
    seq = pl.BlockSpec((nh, c, dh), lambda i: (0, t(i), 0))
    gate = pl.BlockSpec((c, LANES), lambda i: (t(i), 0))
    st = pl.BlockSpec((nh, 1, dh, dh), lambda i: (0, t(i), 0, 0))
    return seq, gate, st


def delta_fwd(name, q, k, v, gb, gather=None):
    nh, s, dh = q.shape
    n = s // DN_CHUNK
    seq, gate, st = _delta_specs(nh, s, dh, False)
    extra = list(gather or [])
    ne = len(extra)

    def body(*refs):
        q_ref, k_ref, v_ref, gb_ref = refs[:4]
        x_refs = refs[4:4 + ne]
        o_ref, st_ref = refs[4 + ne:6 + ne]
        out_refs = refs[6 + ne:6 + 2 * ne]
        state = refs[6 + 2 * ne]
        i = pl.program_id(0)
        if ne:
            begin, finish = _gather_phases(x_refs, out_refs, refs[7 + 2 * ne:])
            pl.when(i == 0)(begin)

        @pl.when(i == 0)
        def _():
            state[...] = jnp.zeros_like(state)

        st_ref[:, 0] = state[...]
        o, new_state = _delta_chunk_packed(q_ref[...], k_ref[...], v_ref[...], gb_ref[...], state[...])
        o_ref[...] = o
        state[...] = new_state
        if ne:
            pl.when(i == n - 1)(finish)

    any_spec = pl.BlockSpec(memory_space=pl.ANY)
    out = pl.pallas_call(
        body, name=name + "_fwd", grid=(n,),
        in_specs=[seq, seq, seq, gate] + [any_spec] * ne,
        out_specs=[seq, st] + [any_spec] * ne,
        out_shape=[jax.ShapeDtypeStruct((nh, s, dh), F32),
                   jax.ShapeDtypeStruct((nh, n, dh, dh), F32)]
        + [jax.ShapeDtypeStruct((N_DEV,) + t.shape, t.dtype) for t in extra],
        scratch_shapes=[pltpu.VMEM((nh, dh, dh), F32)]
        + ([pltpu.SemaphoreType.DMA((7 * ne,)), pltpu.SemaphoreType.DMA((7 * ne,)),
            pltpu.SemaphoreType.DMA((ne,))] if ne else []),
        compiler_params=_cparams(1),
    )(q, k, v, gb, *extra)
    return out[0], out[1], list(out[2:])


def delta_bwd(name, q, k, v, gb, states, do, exchange=None):
    nh, s, dh = q.shape
    n = s // DN_CHUNK
    seq, gate, st = _delta_specs(nh, s, dh, True)
    extra = list(exchange or [])
    ne = len(extra)

    def body(*refs):
        q_ref, k_ref, v_ref, gb_ref, st_ref, do_ref = refs[:6]
        p_refs = refs[6:6 + ne]
        dq_ref, dk_ref, dv_ref, dgb_ref = refs[6 + ne:10 + ne]
        land_refs = refs[10 + ne:10 + 2 * ne]
        dstate = refs[10 + 2 * ne]
        i = pl.program_id(0)
        if ne:
            begin, finish = _chip_exchange_phases(p_refs, land_refs, refs[11 + 2 * ne:])
            pl.when(i == 0)(begin)

        @pl.when(i == 0)
        def _():
            dstate[...] = jnp.zeros_like(dstate)

        _, vjp = jax.vjp(_delta_chunk_packed, q_ref[...], k_ref[...], v_ref[...], gb_ref[...],
                         st_ref[:, 0])
        dq, dk, dv, dgb, dst = vjp((do_ref[...], dstate[...]))
        dq_ref[...] = dq
        dk_ref[...] = dk
        dv_ref[...] = dv
        dgb_ref[...] = dgb
        dstate[...] = dst
        if ne:
            pl.when(i == n - 1)(finish)

    any_spec = pl.BlockSpec(memory_space=pl.ANY)
    out = pl.pallas_call(
        body, name=name + "_bwd", grid=(n,),
        in_specs=[seq, seq, seq, gate, st, seq] + [any_spec] * ne,
        out_specs=[seq, seq, seq, gate] + [any_spec] * ne,
        out_shape=[jax.ShapeDtypeStruct((nh, s, dh), F32)] * 3
        + [jax.ShapeDtypeStruct((s, LANES), F32)]
        + [jax.ShapeDtypeStruct(t.shape, t.dtype) for t in extra],
        scratch_shapes=[pltpu.VMEM((nh, dh, dh), F32)]
        + ([pltpu.SemaphoreType.DMA((3 * ne,)), pltpu.SemaphoreType.DMA((3 * ne,)),
            pltpu.SemaphoreType.DMA((ne,))] if ne else []),
        compiler_params=_cparams(1),
    )(q, k, v, gb, states, do, *extra)
    return tuple(out[:4]), list(out[4:])


def delta_rule(name, q, k, v, gb):
    @jax.custom_vjp
    def op(q, k, v, gb):
        return delta_fwd(name, q, k, v, gb)[0]

    def op_fwd(q, k, v, gb):
        o, states, _ = delta_fwd(name, q, k, v, gb)
        return o, (q, k, v, gb, states)

    def op_bwd(res, do):
        return delta_bwd(name, *res, do)[0]

    op.defvjp(op_fwd, op_bwd)
    return op(q, k, v, gb)


def _my_place():
    return lax.axis_index("x"), lax.axis_index("y"), lax.axis_index("c")


def all_gather(name, shard):
    return all_gather_many(name, [shard])[0]


def _gather_phases(x_refs, out_refs, sems):
    n = len(x_refs)
    send_sems, recv_sems, local_sems = sems
    x, y, c = _my_place()
    me, sibling = (x, y, c), (x, y, 1 - c)
    chips = [(1 - x, y), (x, 1 - y), (1 - x, 1 - y)]

    def copy(i, k, block, to, own=False):
        px, py, pc = block
        slot = out_refs[i].at[4 * px + 2 * py + pc]
        return pltpu.make_async_remote_copy(
            src_ref=x_refs[i] if own else slot, dst_ref=slot,
            send_sem=send_sems.at[7 * i + k], recv_sem=recv_sems.at[7 * i + k],
            device_id=to, device_id_type=MESH)

    mine = [pltpu.make_async_copy(x_refs[i], out_refs[i].at[4 * x + 2 * y + c], local_sems.at[i])
            for i in range(n)]
    first = []
    for i in range(n):
        first.append(copy(i, 0, me, sibling, own=True))
        first += [copy(i, 1 + j, me, (*chip, c), own=True) for j, chip in enumerate(chips)]

    def begin():
        for cp in mine + first:
            cp.start()

    def finish():
        passed = []
        for j, chip in enumerate(chips):
            for i in range(n):
                copy(i, 1 + j, (*chip, c), me).wait_recv()
                passed.append(copy(i, 4 + j, (*chip, c), sibling))
                passed[-1].start()
        for i in range(n):
            copy(i, 0, sibling, me).wait_recv()
        for j, chip in enumerate(chips):
            for i in range(n):
                copy(i, 4 + j, (*chip, 1 - c), me).wait_recv()
        for cp in first + passed:
            cp.wait_send()
        for cp in mine:
            cp.wait()

    return begin, finish


def all_gather_many(name, shards):
    n = len(shards)

    def body(*refs):
        begin, finish = _gather_phases(refs[:n], refs[n:2 * n], refs[2 * n:])
        begin()
        finish()

    any_spec = pl.BlockSpec(memory_space=pl.ANY)
    return pl.pallas_call(
        body, name=name,
        out_shape=[jax.ShapeDtypeStruct((N_DEV,) + t.shape, t.dtype) for t in shards],
        in_specs=[any_spec] * n, out_specs=[any_spec] * n,
        scratch_shapes=[pltpu.SemaphoreType.DMA((7 * n,)), pltpu.SemaphoreType.DMA((7 * n,)),
                        pltpu.SemaphoreType.DMA((n,))],
    )(*shards)


def sibling_exchange(name, parts):
    n = len(parts)

    def body(*refs):
        p_refs, out_refs = refs[:n], refs[n:2 * n]
        send_sems, recv_sems = refs[2 * n:]
        x, y, c = _my_place()
        copies = []
        for i in range(n):
            for chip in range(4):
                copies.append(pltpu.make_async_remote_copy(
                    src_ref=p_refs[i].at[2 * chip + (1 - c)], dst_ref=out_refs[i].at[chip],
                    send_sem=send_sems.at[4 * i + chip], recv_sem=recv_sems.at[4 * i + chip],
                    device_id=(x, y, 1 - c), device_id_type=MESH))
        for cp in copies:
            cp.start()
        for cp in copies:
            cp.wait_recv()
        for cp in copies:
            cp.wait_send()

    any_spec = pl.BlockSpec(memory_space=pl.ANY)
    return pl.pallas_call(
        body, name=name,
        out_shape=[jax.ShapeDtypeStruct((4,) + t.shape[1:], t.dtype) for t in parts],
        in_specs=[any_spec] * n, out_specs=[any_spec] * n,
        scratch_shapes=[pltpu.SemaphoreType.DMA((4 * n,)), pltpu.SemaphoreType.DMA((4 * n,))],
    )(*parts)


def _chip_exchange_phases(p_refs, out_refs, sems):
    n = len(p_refs)
    send_sems, recv_sems, local_sems = sems
    x, y, c = _my_place()
    me = 2 * x + y
    peers = [(1 - x, y), (x, 1 - y), (1 - x, 1 - y)]

    def copy(i, j, landing):
        px, py = peers[j]
        return pltpu.make_async_remote_copy(
            src_ref=p_refs[i].at[2 * px + py],
            dst_ref=out_refs[i].at[(2 * px + py) if landing else me],
            send_sem=send_sems.at[3 * i + j], recv_sem=recv_sems.at[3 * i + j],
            device_id=(px, py, c), device_id_type=MESH)

    mine = [pltpu.make_async_copy(p_refs[i].at[me], out_refs[i].at[me], local_sems.at[i])
            for i in range(n)]
    copies = [copy(i, j, False) for j in range(3) for i in range(n)]

    def begin():
        for cp in mine + copies:
            cp.start()

    def finish():
        for j in range(3):
            for i in range(n):
                copy(i, j, True).wait_recv()
        for cp in copies:
            cp.wait_send()
        for cp in mine:
            cp.wait()

    return begin, finish


def chip_exchange(name, parts):
    n = len(parts)

    def body(*refs):
        begin, finish = _chip_exchange_phases(refs[:n], refs[n:2 * n], refs[2 * n:])
        begin()
        finish()

    any_spec = pl.BlockSpec(memory_space=pl.ANY)
    return pl.pallas_call(
        body, name=name,
        out_shape=[jax.ShapeDtypeStruct(t.shape, t.dtype) for t in parts],
        in_specs=[any_spec] * n, out_specs=[any_spec] * n,
        scratch_shapes=[pltpu.SemaphoreType.DMA((3 * n,)), pltpu.SemaphoreType.DMA((3 * n,)),
                        pltpu.SemaphoreType.DMA((n,))],
    )(*parts)


def pair_sum(name, mine, got, core):
    _, rows, cols = got.shape
    row_bytes = 4 * LANES * (-(-cols // LANES))
    tile = _pick(rows, [t for t in (512, 352, 256, 128, 64, 32, 16, 8)
                        if t * row_bytes <= ADAMW_BLOCK_BYTES])

    def body(core_ref, a_ref, b_ref, o_ref):
        o_ref[...] = (a_ref[...].astype(F32) + b_ref[...].astype(F32)).astype(o_ref.dtype)

    spec = pl.BlockSpec((4, tile, cols), lambda i, core_ref: (0, i, 0))
    return pl.pallas_call(
        body, name=name,
        grid_spec=pltpu.PrefetchScalarGridSpec(
            num_scalar_prefetch=1, grid=(rows // tile,),
            in_specs=[pl.BlockSpec((4, None, tile, cols), lambda i, core_ref: (0, core_ref[0], i, 0)),
                      spec],
            out_specs=spec),
        out_shape=jax.ShapeDtypeStruct(got.shape, got.dtype), compiler_params=_cparams(1),
    )(core, mine, got)


ADAMW_BLOCK_BYTES = 3 * 512 * 1024


def adamw(name, grad, w, m, v):
    rows, cols = w.shape
    stacked = grad.ndim == 3
    row_bytes = 4 * LANES * (-(-cols // LANES))
    tile = _pick(rows, [t for t in (512, 352, 256, 128, 64, 32, 16, 8)
                        if t * row_bytes <= ADAMW_BLOCK_BYTES])

    def body(g_ref, w_ref, m_ref, v_ref, go_ref, d_ref, mo_ref, vo_ref):
        if stacked:
            g = g_ref[0].astype(F32)
            for s in range(1, grad.shape[0]):
                g = g + g_ref[s].astype(F32)
        else:
            g = g_ref[...]
        m = ADAM_B1 * m_ref[...] + (1.0 - ADAM_B1) * g
        v = ADAM_B2 * v_ref[...] + (1.0 - ADAM_B2) * jnp.square(g)
        m_hat = m / (1.0 - ADAM_B1 ** ADAM_STEP)
        v_hat = v / (1.0 - ADAM_B2 ** ADAM_STEP)
        go_ref[...] = g
        d_ref[...] = -ADAM_LR * (m_hat / (jnp.sqrt(v_hat) + ADAM_EPS) + ADAM_WD * w_ref[...])
        mo_ref[...] = m
        vo_ref[...] = v

    flat = pl.BlockSpec((tile, cols), lambda i: (i, 0))
    g_spec = pl.BlockSpec((grad.shape[0], tile, cols), lambda i: (0, i, 0)) if stacked else flat
    return pl.pallas_call(
        body, name=name, grid=(rows // tile,),
        in_specs=[g_spec, flat, flat, flat], out_specs=[flat] * 4,
        out_shape=[jax.ShapeDtypeStruct((rows, cols), F32)] * 4,
        compiler_params=_cparams(1),
    )(grad, w, m, v)


def silu_rows(name, x):
    def body(x_ref, o_ref):
        o_ref[...] = _silu(x_ref[...])

    return pl.pallas_call(body, name=name, out_shape=jax.ShapeDtypeStruct(x.shape, F32))(x)


BIG = (("ffn1_wg", (D, D_FF // N_DEV), 1), ("ffn1_wu", (D, D_FF // N_DEV), 1),
       ("ffn1_wd", (D_FF // N_DEV, D), 0), ("w_in", (D, IN_COLS // N_DEV), 1),
       ("conv_w", (DN_CONV, 3 * D // N_DEV), 1), ("w_a", (D // N_DEV, D), 0),
       ("w_b", (DA_HEADS * DA_DIM, D // N_DEV), 1), ("w_o", (D // N_DEV, D), 0),
       ("ffn2_wg", (D, D_FF // N_DEV), 1), ("ffn2_wu", (D, D_FF // N_DEV), 1),
       ("ffn2_wd", (D_FF // N_DEV, D), 0))
SMALL = (("ada_b", (DEPTH, N_ADA * D)), ("ln_ffn1", (DEPTH, D)), ("ln_mix", (DEPTH, D)),
         ("ln_ffn2", (DEPTH, D)), ("a_log", (DEPTH, DN_HEADS)), ("dt_bias", (DEPTH, DN_HEADS)),
         ("dn_norm", (DEPTH, DN_DIM)), ("final_norm", (D,)))
SMALL_ROWS = 32


def _pack(arrays, rows):
    flat = jnp.concatenate([a.reshape(-1) for a in arrays])
    return jnp.pad(flat, (0, rows * D - flat.shape[0])).reshape(rows, D)


def _unpack(buf, shapes):
    flat = buf.reshape(-1)
    out, off = [], 0
    for shp in shapes:
        n = int(np.prod(shp))
        out.append(flat[off:off + n].reshape(shp))
        off += n
    return out


def _full_weights(gathered, entries):
    full = {}
    for (name, (a, b), axis), t in zip(entries, gathered):
        if axis == 1:
            full[name] = t.transpose(1, 0, 2).reshape(a, N_DEV * b)
        else:
            full[name] = t.reshape(N_DEV * a, b)
    return full


_B0, _DQ0, _GA0 = 4096, 4112, 6416
_N_WIDE = 6 * D


def _reorder_in_proj(w):
    pad = jnp.zeros((w.shape[0], IN_COLS_PAD - IN_COLS), w.dtype)
    return jnp.concatenate([w[:, :_B0], w[:, _GA0:], w[:, _DQ0:_GA0], w[:, _B0:_DQ0], pad], axis=1)


def _ffn(tag, h, ln, shift, scale, gate, w_gu, w_d):
    n = norm_mod(tag + "_norm", h, ln, shift, scale)
    gu = linear(tag + "_gu", n, w_gu, BF16)
    a = swiglu_act(tag + "_act", gu)
    f = linear(tag + "_down", a, w_d, F32)
    return residual(tag + "_res", h, f, gate, 0.5)


N_PRE = 5


def _mods(mod, l):
    return [mod[l, i * D:(i + 1) * D][None] for i in range(N_ADA)]


def _layer_pre(l, h, gathered, small, mod):
    w = _full_weights(gathered, BIG[:N_PRE])
    tag = f"l{l}"
    sh1, sc1, gt1, sh2, sc2 = _mods(mod, l)[:5]
    w_gu1 = _interleave_gate_up(w["ffn1_wg"], w["ffn1_wu"])
    h = _ffn(tag + "_ffn1", h, small["ln_ffn1"][l][None], sh1, sc1, gt1, w_gu1, w["ffn1_wd"])
    u = norm_mod(tag + "_mixnorm", h, small["ln_mix"][l][None], sh2, sc2)
    w_in = _reorder_in_proj(w["w_in"])
    (q_pre, k_pre, v_pre, z, gate_a, gate_b) = linear_split(
        tag + "_mix_in", u, w_in[:, :_N_WIDE], D, (1,) * 6, (BF16,) * 6)
    (da_q, da_k, da_v) = linear_split(
        tag + "_mix_in_da", u, w_in[:, _N_WIDE:_N_WIDE + 3 * DA_HEADS * DA_DIM], DA_HEADS * DA_DIM,
        (1,) * 3, (BF16,) * 3)
    ab = linear(tag + "_mix_in_ab", u, w_in[:, _N_WIDE + 3 * DA_HEADS * DA_DIM:], F32)
    cw = w["conv_w"].astype(F32).reshape(DN_CONV, 1, 3 * D)
    q, k, v = [conv_heads(f"{tag}_mix_conv_{m}", t, cw[:, :, i * D:(i + 1) * D], m)
               for i, (m, t) in enumerate(zip("qkv", (q_pre, k_pre, v_pre)))]
    pad = lambda t: jnp.pad(t, (DN_HEADS, ab.shape[1] - 2 * DN_HEADS))[None]
    gb = decay_beta(tag + "_mix_decay", ab, pad(small["a_log"][l]), pad(small["dt_bias"][l]))
    return (q, k, v, gb[:, :LANES]), (h, z, da_q, da_k, da_v, gate_a, gate_b)


def _layer_post(l, o, carry, gathered, small, mod):
    w = _full_weights(gathered, BIG[N_PRE:])
    tag = f"l{l}"
    h, z, da_q, da_k, da_v, gate_a, gate_b = carry
    gt2, sh3, sc3, gt3 = _mods(mod, l)[5:]
    o_a = gated_head_norm(tag + "_mix_gnorm", o, z, small["dn_norm"][l][None])
    y_a = linear(tag + "_mix_wa", o_a, w["w_a"], F32)

    outs, lses = [], []
    for r in DA_DILATIONS:
        o_r, lse_r = dilated_attention(f"{tag}_mix_attn{r}",
                                       *[_sub_order(t, r) for t in (da_q, da_k, da_v)], r)
        outs.append(_seq_order(o_r, r))
        lses.append(_seq_order(lse_r, r))
    o_b = combine_patterns(tag + "_mix_comb", outs, lses)
    y_b = linear(tag + "_mix_wb", o_b, w["w_b"], F32)

    merged = merge_gates(tag + "_mix_merge", gate_a, gate_b, y_a, y_b)
    m = linear(tag + "_mix_wo", merged, w["w_o"], F32)
    h = residual(tag + "_mixres", h, m, gt2, 1.0)
    w_gu2 = _interleave_gate_up(w["ffn2_wg"], w["ffn2_wu"])
    return _ffn(tag + "_ffn2", h, small["ln_ffn2"][l][None], sh3, sc3, gt3, w_gu2, w["ffn2_wd"])


def _layer(l, h, gathered, small, mod):
    (q, k, v, gb), carry = _layer_pre(l, h, gathered[:N_PRE], small, mod)
    o = delta_rule(f"l{l}_mix_delta", q, k, v, gb)
    return _layer_post(l, o, carry, gathered[N_PRE:], small, mod)


def _head(h, small, target):
    rows = loss_rows("loss", h, small["final_norm"][None], target)
    return 0.5 * jnp.sum(rows)


def _local_loss(x, gathered, small, mod, target):
    h = x
    for l in range(DEPTH):
        h = _layer(l, h, gathered[l], small, mod)
    return _head(h, small, target)


def kernel(x, c, ada_w, ada_b, ln_ffn1, ln_mix, ln_ffn2, ffn1_wg, ffn1_wu, ffn1_wd, w_in, conv_w, a_log, dt_bias, dn_norm, w_a, w_b, w_o, ffn2_wg, ffn2_wu, ffn2_wd, final_norm, loss_target, m_ada_w, m_ada_b, m_ln_ffn1, m_ln_mix, m_ln_ffn2, m_ffn1_wg, m_ffn1_wu, m_ffn1_wd, m_w_in, m_conv_w, m_a_log, m_dt_bias, m_dn_norm, m_w_a, m_w_b, m_w_o, m_ffn2_wg, m_ffn2_wu, m_ffn2_wd, m_final_norm, v_ada_w, v_ada_b, v_ln_ffn1, v_ln_mix, v_ln_ffn2, v_ffn1_wg, v_ffn1_wu, v_ffn1_wd, v_w_in, v_conv_w, v_a_log, v_dt_bias, v_dn_norm, v_w_a, v_w_b, v_w_o, v_ffn2_wg, v_ffn2_wu, v_ffn2_wd, v_final_norm):
    args = dict(locals())
    big_names = [n for n, _, _ in BIG]
    small_names = [n for n, _ in SMALL]
    me = 4 * lax.axis_index("x") + 2 * lax.axis_index("y") + lax.axis_index("c")
    cols = N_ADA * D // N_DEV

    c_all = all_gather("gather_c", jnp.pad(silu_rows("silu_c", c), ((0, 7), (0, 0))))[:, 0]
    mod_cols = jnp.stack([matmul(f"ada{l}", c_all, ada_w[l], "nn", F32) for l in range(DEPTH)])
    mod_cols = mod_cols + lax.dynamic_slice_in_dim(ada_b, me * cols, cols, axis=1)[:, None, :]
    mod_all = all_gather("gather_mod", mod_cols.reshape(DEPTH * N_DEV, cols))
    mod_all = mod_all.reshape(N_DEV, DEPTH, N_DEV, cols)
    mod = lax.dynamic_index_in_dim(mod_all, me, axis=2, keepdims=False)
    mod = mod.transpose(1, 0, 2).reshape(DEPTH, N_ADA * D)

    shards = [[args[n][l].astype(BF16) for n in big_names] for l in range(DEPTH)]
    pre_names, post_names = big_names[:N_PRE], big_names[N_PRE:]
    small = {n: args[n] for n in small_names if n != "ada_b"}
    target = loss_target[0]

    w0_pre = all_gather_many("gather_w0", shards[0][:N_PRE])
    pre0, vjp_pre0 = jax.vjp(functools.partial(_layer_pre, 0), x[0], w0_pre, small, mod)
    (q0, k0, v0, gb0), carry0 = pre0
    o0, states0, got = delta_fwd("l0_mix_delta", q0, k0, v0, gb0,
                                 gather=shards[0][N_PRE:] + shards[1][:N_PRE])
    w0_post, w1_pre = got[:len(post_names)], got[len(post_names):]
    h1, vjp_post0 = jax.vjp(functools.partial(_layer_post, 0), o0, carry0, w0_post, small, mod)
    pre1, vjp_pre1 = jax.vjp(functools.partial(_layer_pre, 1), h1, w1_pre, small, mod)
    (q1, k1, v1, gb1), carry1 = pre1
    o1, states1, w1_post = delta_fwd("l1_mix_delta", q1, k1, v1, gb1, gather=shards[1][N_PRE:])
    loss, vjp_post1 = jax.vjp(
        lambda o, carry, w, sm, md: _head(_layer_post(1, o, carry, w, sm, md), sm, target),
        o1, carry1, w1_post, small, mod)

    my_core = lax.axis_index("c").astype(jnp.int32).reshape(1)

    def pair_sums(tag, names, grads):
        from_sibling = sibling_exchange(f"pair_grads_{tag}", grads)
        return [pair_sum(f"pair_sum_{tag}_{n}", t.reshape(4, 2, -1, t.shape[-1]), got_n, my_core)
                for n, t, got_n in zip(names, grads, from_sibling)]

    do1, dcarry1, dw1_post, dsmall_a, dmod_a = vjp_post1(jnp.ones((), F32))
    ddelta1, landed1_post = delta_bwd("l1_mix_delta", q1, k1, v1, gb1, states1, do1,
                                      exchange=pair_sums("l1_post", post_names, list(dw1_post)))
    dh1, dw1_pre, dsmall_b, dmod_b = vjp_pre1((ddelta1, dcarry1))
    do0, dcarry0, dw0_post, dsmall_c, dmod_c = vjp_post0(dh1)
    ddelta0, landed_mid = delta_bwd(
        "l0_mix_delta", q0, k0, v0, gb0, states0, do0,
        exchange=pair_sums("mid", pre_names + post_names, list(dw1_pre) + list(dw0_post)))
    dx, dw0_pre, dsmall_d, dmod_d = vjp_pre0((ddelta0, dcarry0))
    landed0_pre = chip_exchange("scatter_grads0", pair_sums("l0_pre", pre_names, list(dw0_pre)))
    landed0 = list(landed0_pre) + list(landed_mid[N_PRE:])
    landed1 = list(landed_mid[:N_PRE]) + list(landed1_post)
    dsmall = {n: dsmall_a[n] + dsmall_b[n] + dsmall_c[n] + dsmall_d[n] for n in dsmall_a}
    dmod = dmod_a + dmod_b + dmod_c + dmod_d

    part = _pack([dmod] + [dsmall[n] for n in small_names[1:]], SMALL_ROWS)
    parts = all_gather("gather_small", part)
    sm_out = adamw("adamw_small", parts, _pack([args[n] for n in small_names], SMALL_ROWS),
                   _pack([args["m_" + n] for n in small_names], SMALL_ROWS),
                   _pack([args["v_" + n] for n in small_names], SMALL_ROWS))

    dmod_all = parts.reshape(N_DEV, -1)[:, :DEPTH * N_ADA * D].reshape(N_DEV, DEPTH, N_ADA * D)
    dmod_mine = lax.dynamic_slice_in_dim(dmod_all, me * cols, cols, axis=2)
    g_ada = jnp.stack([matmul(f"ada{l}_dw", c_all, dmod_mine[:, l], "tn", F32) for l in range(DEPTH)])
    flat2 = lambda t: t.reshape(-1, t.shape[-1])
    ada_out = adamw("adamw_ada_w", flat2(g_ada), flat2(ada_w), flat2(m_ada_w), flat2(v_ada_w))

    big_out = {}
    for n, t0, t1 in zip(big_names, landed0, landed1):
        both = jnp.concatenate([t0, t1], axis=1)
        big_out[n] = adamw("adamw_" + n, both, flat2(args[n]), flat2(args["m_" + n]), flat2(args["v_" + n]))

    small_shapes = [shp for _, shp in SMALL]
    names = ["ada_w", "ada_b", "ln_ffn1", "ln_mix", "ln_ffn2", "ffn1_wg", "ffn1_wu", "ffn1_wd", "w_in",
             "conv_w", "a_log", "dt_bias", "dn_norm", "w_a", "w_b", "w_o", "ffn2_wg", "ffn2_wu",
             "ffn2_wd", "final_norm"]
    outs = [lax.psum(loss, ("x", "y", "c")), dx[None]]
    for kind in range(4):
        table = {n: big_out[n][kind].reshape(args[n].shape) for n in big_names}
        table.update(zip(small_names, _unpack(sm_out[kind], small_shapes)))
        table["ada_w"] = ada_out[kind].reshape(ada_w.shape)
        outs += [table[n] for n in names]
    return tuple(outs)
```

```python
import functools
import math

import numpy as np
import jax
import jax.numpy as jnp
from jax import lax
from jax.experimental import pallas as pl
from jax.experimental.pallas import tpu as pltpu

F32 = jnp.float32
BF16 = jnp.bfloat16

D = 1024
SEQ = 4096
DEPTH = 2
N_DEV = 8
DN_HEADS = 8
DN_DIM = 128
DN_CHUNK = 64
DN_CONV = 4
DA_HEADS = 12
DA_DIM = 64
DA_BLOCK = 128
DA_DILATIONS = (1, 4, 16)
ALIBI_MAX_EXP = 8.0
D_FF = 2816
N_ADA = 9
NORM_EPS = 1e-6
IN_COLS = 8464
IN_COLS_PAD = 8704
ADAM_LR, ADAM_B1, ADAM_B2, ADAM_EPS, ADAM_WD, ADAM_STEP = 0.001, 0.9, 0.999, 1e-08, 0.01, 10
NEG = -1e30

VMEM_LIMIT = 56 * 1024 * 1024
LANES = 128

MESH = pl.DeviceIdType.MESH


def _cparams(n_grid):
    return pltpu.CompilerParams(dimension_semantics=("arbitrary",) * n_grid,
                                vmem_limit_bytes=VMEM_LIMIT)


def blockwise(name, f, grid, ins, outs):
    n_in, n_out = len(ins), len(outs)
    diff = [i for i, (_, _, kind) in enumerate(ins) if kind != "const"]

    def apply(*vals):
        res = f(*vals)
        return tuple(r.astype(dt) for r, (_, dt, _, _) in zip(res, outs))

    def fwd_call(*arrays):
        def body(*refs):
            res = apply(*[r[...] for r in refs[:n_in]])
            for r, v in zip(refs[n_in:], res):
                r[...] = v

        return pl.pallas_call(
            body, name=name + "_fwd", grid=grid,
            in_specs=[pl.BlockSpec(b, im) for (b, im, _) in ins],
            out_specs=[pl.BlockSpec(b, im) for (_, _, b, im) in outs],
            out_shape=[jax.ShapeDtypeStruct(s, dt) for (s, dt, _, _) in outs],
            compiler_params=_cparams(len(grid)),
        )(*arrays)

    def bwd_call(arrays, cts):
        def body(*refs):
            in_refs, ct_refs = refs[:n_in], refs[n_in:n_in + n_out]
            g_refs = refs[n_in + n_out:]
            vals = [r[...] for r in in_refs]

            def fd(*dvals):
                full = list(vals)
                for i, v in zip(diff, dvals):
                    full[i] = v
                return apply(*full)

            _, vjp = jax.vjp(fd, *[vals[i] for i in diff])
            grads = vjp(tuple(r[...] for r in ct_refs))
            first = functools.reduce(jnp.logical_and,
                                     [pl.program_id(a) == 0 for a in range(len(grid))])
            for g_ref, g, i in zip(g_refs, grads, diff):
                if ins[i][2] == "acc":
                    @pl.when(first)
                    def _(g_ref=g_ref):
                        g_ref[...] = jnp.zeros_like(g_ref)
                    g_ref[...] += g.astype(F32)
                else:
                    g_ref[...] = g.astype(g_ref.dtype)

        g_shapes = [jax.ShapeDtypeStruct(arrays[i].shape,
                                         F32 if ins[i][2] == "acc" else arrays[i].dtype)
                    for i in diff]
        return pl.pallas_call(
            body, name=name + "_bwd", grid=grid,
            in_specs=([pl.BlockSpec(b, im) for (b, im, _) in ins]
                      + [pl.BlockSpec(b, im) for (_, _, b, im) in outs]),
            out_specs=[pl.BlockSpec(ins[i][0], ins[i][1]) for i in diff],
            out_shape=g_shapes,
            compiler_params=_cparams(len(grid)),
        )(*arrays, *cts)

    @jax.custom_vjp
    def op(*arrays):
        return tuple(fwd_call(*arrays))

    def op_fwd(*arrays):
        return tuple(fwd_call(*arrays)), arrays

    def op_bwd(arrays, cts):
        grads = bwd_call(arrays, cts)
        full = [None] * n_in
        for i, g in zip(diff, grads):
            full[i] = g.astype(arrays[i].dtype)
        return tuple(full)

    op.defvjp(op_fwd, op_bwd)
    return op


def _pick(n, cands):
    for c in cands:
        if n % c == 0:
            return c
    return n


MIN_MATMUL_STEPS = 8


def matmul(name, a, b, form, out_dtype):
    if form == "nn":
        (m, k), (_, n) = a.shape, b.shape
    elif form == "nt":
        (m, k), (n, _) = a.shape, b.shape
    else:
        (k, m), (_, n) = a.shape, b.shape
    tm = _pick(m, (1408, 1024, 512, 256, 128, 8))
    tn = _pick(n, (1408, 1024, 512, 384, 256, 128))
    tk = _pick(k, (1024, 1408, 768, 512, 384, 256, 128, 8))
    nk = k // tk
    while (m // tm) * (n // tn) * nk < MIN_MATMUL_STEPS and tm % 512 == 0:
        tm //= 2
    a_spec = (pl.BlockSpec((tk, tm), lambda i, j, kk: (kk, i)) if form == "tn"
              else pl.BlockSpec((tm, tk), lambda i, j, kk: (i, kk)))
    b_spec = (pl.BlockSpec((tn, tk), lambda i, j, kk: (j, kk)) if form == "nt"
              else pl.BlockSpec((tk, tn), lambda i, j, kk: (kk, j)))
    dims = {"nn": (((1,), (0,)), ((), ())), "nt": (((1,), (1,)), ((), ())),
            "tn": (((0,), (0,)), ((), ()))}[form]

    def body(a_ref, b_ref, o_ref, acc_ref):
        kk = pl.program_id(2)
        part = lax.dot_general(a_ref[...].astype(BF16), b_ref[...].astype(BF16), dims,
                               preferred_element_type=F32)
        if nk == 1:
            o_ref[...] = part.astype(o_ref.dtype)
            return

        @pl.when(kk == 0)
        def _():
            acc_ref[...] = part

        @pl.when(kk > 0)
        def _():
            acc_ref[...] += part

        @pl.when(kk == nk - 1)
        def _():
            o_ref[...] = acc_ref[...].astype(o_ref.dtype)

    return pl.pallas_call(
        body, name=name, grid=(m // tm, n // tn, nk),
        in_specs=[a_spec, b_spec],
        out_specs=pl.BlockSpec((tm, tn), lambda i, j, kk: (i, j)),
        out_shape=jax.ShapeDtypeStruct((m, n), out_dtype),
        scratch_shapes=[pltpu.VMEM((tm, tn), F32)],
        compiler_params=_cparams(3),
    )(a, b)


def linear(name, x, w, out_dtype):
    @jax.custom_vjp
    def op(x, w):
        return matmul(name + "_y", x, w, "nn", out_dtype)

    def op_fwd(x, w):
        return op(x, w), (x, w)

    def op_bwd(res, dy):
        x, w = res
        dx = matmul(name + "_dx", dy, w, "nt", x.dtype)
        dw = matmul(name + "_dw", x, dy, "tn", w.dtype)
        return dx, dw

    op.defvjp(op_fwd, op_bwd)
    return op(x, w)


def linear_split(name, x, w, tile, seg_tiles, out_dtypes):
    SPLIT_TILE = tile
    m, k = x.shape
    n = w.shape[1]
    nt = n // SPLIT_TILE
    starts = [sum(seg_tiles[:s]) for s in range(len(seg_tiles))]
    assert sum(seg_tiles) == nt
    tm = _pick(m, (1024, 512, 256, 128, 8))

    def inside(j, s):
        return jnp.logical_and(j >= starts[s], j < starts[s] + seg_tiles[s])

    def local(j, s):
        return jnp.clip(j - starts[s], 0, seg_tiles[s] - 1)

    def fwd_call(x, w):
        def body(x_ref, w_ref, *o_refs):
            j = pl.program_id(1)
            y = jnp.dot(x_ref[...].astype(BF16), w_ref[...].astype(BF16), preferred_element_type=F32)
            for s, o_ref in enumerate(o_refs):
                @pl.when(inside(j, s))
                def _(o_ref=o_ref):
                    o_ref[...] = y.astype(o_ref.dtype)

        return pl.pallas_call(
            body, name=name + "_y", grid=(m // tm, nt),
            in_specs=[pl.BlockSpec((tm, k), lambda i, j: (i, 0)),
                      pl.BlockSpec((k, SPLIT_TILE), lambda i, j: (0, j))],
            out_specs=[pl.BlockSpec((tm, SPLIT_TILE), lambda i, j, s=s: (i, local(j, s)))
                       for s in range(len(seg_tiles))],
            out_shape=[jax.ShapeDtypeStruct((m, t * SPLIT_TILE), dt)
                       for t, dt in zip(seg_tiles, out_dtypes)],
            compiler_params=_cparams(2),
        )(x, w)

    def dx_call(dys, w):
        def body(*refs):
            dy_refs, w_ref, o_ref, acc_ref = refs[:-3], refs[-3], refs[-2], refs[-1]
            j = pl.program_id(1)
            for s, dy_ref in enumerate(dy_refs):
                @pl.when(inside(j, s))
                def _(dy_ref=dy_ref):
                    part = lax.dot_general(dy_ref[...].astype(BF16), w_ref[...].astype(BF16),
                                           (((1,), (1,)), ((), ())), preferred_element_type=F32)

                    @pl.when(j == 0)
                    def _():
                        acc_ref[...] = part

                    @pl.when(j > 0)
                    def _():
                        acc_ref[...] += part

            @pl.when(j == nt - 1)
            def _():
                o_ref[...] = acc_ref[...].astype(o_ref.dtype)

        return pl.pallas_call(
            body, name=name + "_dx", grid=(m // tm, nt),
            in_specs=[pl.BlockSpec((tm, SPLIT_TILE), lambda i, j, s=s: (i, local(j, s)))
                      for s in range(len(seg_tiles))]
            + [pl.BlockSpec((k, SPLIT_TILE), lambda i, j: (0, j))],
            out_specs=pl.BlockSpec((tm, k), lambda i, j: (i, 0)),
            out_shape=jax.ShapeDtypeStruct((m, k), x.dtype),
            scratch_shapes=[pltpu.VMEM((tm, k), F32)],
            compiler_params=_cparams(2),
        )(*dys, w)

    def dw_call(x, dys):
        nm = m // tm

        def body(*refs):
            x_ref, dy_refs, o_ref, acc_ref = refs[0], refs[1:-2], refs[-2], refs[-1]
            j, kk = pl.program_id(0), pl.program_id(1)
            for s, dy_ref in enumerate(dy_refs):
                @pl.when(inside(j, s))
                def _(dy_ref=dy_ref):
                    rows = x_ref[pl.ds(pl.multiple_of(kk * tm, tm), tm), :]
                    part = lax.dot_general(rows.astype(BF16), dy_ref[...].astype(BF16),
                                           (((0,), (0,)), ((), ())), preferred_element_type=F32)

                    @pl.when(kk == 0)
                    def _():
                        acc_ref[...] = part

                    @pl.when(kk > 0)
                    def _():
                        acc_ref[...] += part

            @pl.when(kk == nm - 1)
            def _():
                o_ref[...] = acc_ref[...].astype(o_ref.dtype)

        return pl.pallas_call(
            body, name=name + "_dw", grid=(nt, nm),
            in_specs=[pl.BlockSpec((m, k), lambda j, kk: (0, 0))]
            + [pl.BlockSpec((tm, SPLIT_TILE),
                            lambda j, kk, s=s: (jnp.where(inside(j, s), kk, 0), local(j, s)))
               for s in range(len(seg_tiles))],
            out_specs=pl.BlockSpec((k, SPLIT_TILE), lambda j, kk: (0, j)),
            out_shape=jax.ShapeDtypeStruct((k, n), w.dtype),
            scratch_shapes=[pltpu.VMEM((k, SPLIT_TILE), F32)],
            compiler_params=_cparams(2),
        )(x, *dys)

    @jax.custom_vjp
    def op(x, w):
        return tuple(fwd_call(x, w))

    def op_fwd(x, w):
        return op(x, w), (x, w)

    def op_bwd(res, dys):
        x, w = res
        return dx_call(dys, w), dw_call(x, dys)

    op.defvjp(op_fwd, op_bwd)
    return op(x, w)


@jax.custom_vjp
def _sigmoid(x):
    return 1.0 / (1.0 + jnp.exp(-x))


def _sigmoid_fwd(x):
    s = _sigmoid(x)
    return s, s


def _sigmoid_bwd(s, ct):
    return (ct * (s * (1.0 - s)),)


_sigmoid.defvjp(_sigmoid_fwd, _sigmoid_bwd)


@jax.custom_vjp
def _silu(x):
    return x * _sigmoid(x)


def _silu_fwd(x):
    s = _sigmoid(x)
    return x * s, (x, s)


def _silu_bwd(res, ct):
    x, s = res
    return (ct * (s * (1.0 + x * (1.0 - s))),)


_silu.defvjp(_silu_fwd, _silu_bwd)


def _softplus(x):
    return jnp.maximum(x, 0.0) + jnp.log(1.0 + jnp.exp(-jnp.abs(x)))


def _rms(x):
    return x * lax.rsqrt(jnp.mean(x * x, axis=-1, keepdims=True) + NORM_EPS)


ROW_TILE = 512


def _row(i):
    return (i, 0)


def _fixed(*_):
    return (0, 0)


def norm_mod(name, h, ln, shift, scale):
    s, d = h.shape

    def f(h, ln, sh, sc):
        return ((_rms(h) * ln) * (1.0 + sc) + sh,)

    op = blockwise(name, f, (s // ROW_TILE,),
                   [((ROW_TILE, d), _row, "tile")] + [((1, d), _fixed, "acc")] * 3,
                   [((s, d), BF16, (ROW_TILE, d), _row)])
    return op(h, ln, shift, scale)[0]


FF_TILE = 1408


def _interleave_gate_up(wg, wu):
    parts = []
    for j in range(wg.shape[1] // FF_TILE):
        parts += [wg[:, j * FF_TILE:(j + 1) * FF_TILE], wu[:, j * FF_TILE:(j + 1) * FF_TILE]]
    return jnp.concatenate(parts, axis=1)


def swiglu_act(name, gu):
    s, f2 = gu.shape
    ff = f2 // 2

    def f(gu):
        g, u = gu[:, :FF_TILE].astype(F32), gu[:, FF_TILE:].astype(F32)
        return (_silu(g) * u,)

    op = blockwise(name, f, (s // ROW_TILE, ff // FF_TILE),
                   [((ROW_TILE, 2 * FF_TILE), lambda i, j: (i, j), "tile")],
                   [((s, ff), BF16, (ROW_TILE, FF_TILE), lambda i, j: (i, j))])
    return op(gu)[0]


def residual(name, h, y, gate, weight):
    s, d = h.shape

    def f(h, y, gate):
        return (h + (weight * gate) * y,)

    op = blockwise(name, f, (s // ROW_TILE,),
                   [((ROW_TILE, d), _row, "tile"), ((ROW_TILE, d), _row, "tile"),
                    ((1, d), _fixed, "acc")],
                   [((s, d), F32, (ROW_TILE, d), _row)])
    return op(h, y, gate)[0]


def merge_gates(name, ga, gb, ya, yb):
    s, d = ya.shape

    def f(ga, gb, ya, yb):
        return (_sigmoid(ga.astype(F32)) * ya + _sigmoid(gb.astype(F32)) * yb,)

    op = blockwise(name, f, (s // ROW_TILE,), [((ROW_TILE, d), _row, "tile")] * 4,
                   [((s, d), BF16, (ROW_TILE, d), _row)])
    return op(ga, gb, ya, yb)[0]


def loss_rows(name, h, g, target):
    s, d = h.shape

    def f(h, g, t):
        err = _rms(h) * g - t
        return (jnp.mean(err * err, axis=-1, keepdims=True),)

    op = blockwise(name, f, (s // ROW_TILE,),
                   [((ROW_TILE, d), _row, "tile"), ((1, d), _fixed, "acc"),
                    ((ROW_TILE, d), _row, "const")],
                   [((s, 1), F32, (ROW_TILE, 1), _row)])
    return op(h, g, target)[0]


def decay_beta(name, ab, a_log_pad, dt_bias_pad):
    s, n = ab.shape

    def f(ab, a_log, dt_bias):
        lane = lax.broadcasted_iota(jnp.int32, ab.shape, 1)
        beta = _sigmoid(ab)
        g = -jnp.exp(a_log) * _softplus(ab + dt_bias)
        return (jnp.where(lane < DN_HEADS, beta, jnp.where(lane < 2 * DN_HEADS, g, 0.0)),)

    op = blockwise(name, f, (s // ROW_TILE,),
                   [((ROW_TILE, n), _row, "tile"), ((1, n), _fixed, "acc"), ((1, n), _fixed, "acc")],
                   [((s, n), F32, (ROW_TILE, n), _row)])
    return op(ab, a_log_pad, dt_bias_pad)[0]


def _shift_rows(x, k):
    n = x.shape[0]

    @jax.custom_vjp
    def shift(x):
        row = lax.broadcasted_iota(jnp.int32, x.shape, 0)
        return jnp.where(row >= k, pltpu.roll(x, k, 0), 0.0)

    def shift_fwd(x):
        return shift(x), None

    def shift_bwd(_, g):
        row = lax.broadcasted_iota(jnp.int32, g.shape, 0)
        return (jnp.where(row < n - k, pltpu.roll(g, n - k, 0), 0.0),)

    shift.defvjp(shift_fwd, shift_bwd)
    return shift(x)


def conv_heads(name, x, w, mode):
    s, width = x.shape
    nh = width // LANES

    def f(x, w):
        x = x.astype(F32)
        y = w[DN_CONV - 1] * x
        for j in range(DN_CONV - 1):
            y = y + w[j] * _shift_rows(x, DN_CONV - 1 - j)
        y = _silu(y)
        if mode != "v":
            y = y * lax.rsqrt(jnp.sum(y * y, axis=-1, keepdims=True) + NORM_EPS)
        if mode == "q":
            y = y * (DN_DIM ** -0.5)
        return (y[None],)

    op = blockwise(name, f, (nh,),
                   [((s, LANES), lambda j: (0, j), "tile"),
                    ((DN_CONV, 1, LANES), lambda j: (0, 0, j), "tile")],
                   [((nh, s, LANES), F32, (1, s, LANES), lambda j: (j, 0, 0))])
    return op(x, w)[0]


def gated_head_norm(name, o, z, w):
    nh, s, dh = o.shape

    def f(o, z, w):
        return (_rms(o[0]) * w * _silu(z.astype(F32)),)

    op = blockwise(name, f, (s // ROW_TILE, nh),
                   [((1, ROW_TILE, dh), lambda i, h: (h, i, 0), "tile"),
                    ((ROW_TILE, dh), lambda i, h: (i, h), "tile"),
                    ((1, dh), lambda i, h: (0, 0), "acc")],
                   [((s, nh * dh), BF16, (ROW_TILE, dh), lambda i, h: (i, h))])
    return op(o, z, w)[0]


def _mm(a, b, ca, cb):
    return _bmm(a[None], b[None], ca + 1, cb + 1)[0]


def dilated_attention(name, q, k, v, dilation):
    s, width = q.shape
    nblk = s // DA_BLOCK
    per_sub = nblk // dilation
    slopes = [dilation * 2.0 ** (-ALIBI_MAX_EXP * (h + 1) / DA_HEADS) for h in range(DA_HEADS)]

    def f(q, kp, kc, vp, vc):
        first = (pl.program_id(0) % per_sub) == 0
        qi = lax.broadcasted_iota(jnp.int32, (DA_BLOCK, 2 * DA_BLOCK), 0)
        ki = lax.broadcasted_iota(jnp.int32, (DA_BLOCK, 2 * DA_BLOCK), 1)
        steps = (qi + DA_BLOCK - ki).astype(F32)
        lowest = qi + first.astype(jnp.int32) * (DA_BLOCK - qi)
        valid = jnp.logical_and(ki >= lowest, ki <= qi + DA_BLOCK)
        top = lax.broadcasted_iota(jnp.int32, (DA_BLOCK, LANES), 1) < DA_DIM
        o_parts, lse_parts = [], []
        for pair in range(width // LANES):
            cols = slice(pair * LANES, (pair + 1) * LANES)
            q2 = jnp.concatenate([jnp.where(top, q[:, cols], 0.0), jnp.where(top, 0.0, q[:, cols])], axis=0)
            k2 = jnp.concatenate([kp[:, cols], kc[:, cols]], axis=0)
            v2 = jnp.concatenate([vp[:, cols], vc[:, cols]], axis=0)
            bias = jnp.concatenate([jnp.where(valid, -slopes[2 * pair + half] * steps, NEG)
                                    for half in range(2)], axis=0)
            sc = _mm(q2, k2, 1, 1) * (DA_DIM ** -0.5) + bias
            mx = jnp.max(sc, axis=-1, keepdims=True)
            p = jnp.exp(sc - mx)
            l = jnp.sum(p, axis=-1, keepdims=True)
            o2 = _mm(p / l, v2, 1, 0)
            lse2 = mx + jnp.log(l)
            o_parts.append(jnp.where(top, o2[:DA_BLOCK], o2[DA_BLOCK:]))
            lse_parts.append(jnp.where(top, lse2[:DA_BLOCK], lse2[DA_BLOCK:]))
        return jnp.concatenate(o_parts, axis=1), jnp.concatenate(lse_parts, axis=1)

    blk = (DA_BLOCK, width)
    cur_map = lambda j: (j, 0)
    prev_map = lambda j: (jnp.maximum(j - 1, 0), 0)
    op = blockwise(
        name, f, (nblk,),
        [(blk, cur_map, "tile"), (blk, prev_map, "tile"), (blk, cur_map, "tile"),
         (blk, prev_map, "tile"), (blk, cur_map, "tile")],
        [((s, width), F32, blk, cur_map), ((s, width), F32, blk, cur_map)])

    @jax.custom_vjp
    def attn(q, k, v):
        return op(q, k, k, v, v)

    def attn_fwd(q, k, v):
        return attn(q, k, v), (q, k, v)

    def attn_bwd(res, cts):
        q, k, v = res
        _, vjp = jax.vjp(op, q, k, k, v, v)
        dq, dkp, dkc, dvp, dvc = vjp(cts)
        fill = lambda t: t.at[s - DA_BLOCK:].set(0.0)
        return dq, dkc + fill(dkp), dvc + fill(dvp)

    attn.defvjp(attn_fwd, attn_bwd)
    return attn(q, k, v)


def combine_patterns(name, outs, lses):
    s, width = outs[0].shape
    n = len(outs)

    def f(*vals):
        o, lse = vals[:n], vals[n:]
        mx = functools.reduce(jnp.maximum, lse)
        e = [jnp.exp(t - mx) for t in lse]
        return (sum(ei * oi for ei, oi in zip(e, o)) / sum(e),)

    tile = ROW_TILE // 2
    op = blockwise(name, f, (s // tile,), [((tile, width), _row, "tile")] * (2 * n),
                   [((s, width), BF16, (tile, width), _row)])
    return op(*outs, *lses)[0]


def _sub_order(t, r):
    if r == 1:
        return t
    s, c = t.shape
    return t.reshape(s // r, r, c).transpose(1, 0, 2).reshape(s, c)


def _seq_order(t, r):
    if r == 1:
        return t
    s, c = t.shape
    return t.reshape(r, s // r, c).transpose(1, 0, 2).reshape(s, c)


def _raw_bmm(a, b, ca, cb):
    return lax.dot_general(a.astype(BF16), b.astype(BF16), (((ca,), (cb,)), ((0,), (0,))),
                           preferred_element_type=F32)


def _split(a):
    hi = a.astype(BF16)
    return hi, (a - hi.astype(F32)).astype(BF16)


def _passes_bmm(a, b, ca, cb, passes):
    if passes == 1:
        return _raw_bmm(a, b, ca, cb)
    (a_hi, a_lo), (b_hi, b_lo) = _split(a), _split(b)
    return _raw_bmm(a_hi, b_hi, ca, cb) + (_raw_bmm(a_hi, b_lo, ca, cb) + _raw_bmm(a_lo, b_hi, ca, cb))


def _bmm(a, b, ca, cb, passes=1):
    fa, fb = 3 - ca, 3 - cb

    @jax.custom_vjp
    def mm(a, b):
        return _passes_bmm(a, b, ca, cb, passes)

    def mm_fwd(a, b):
        return mm(a, b), (a, b)

    def mm_bwd(res, ct):
        a, b = res
        da = (_passes_bmm(ct, b, 2, fb, passes) if ca == 2
              else _passes_bmm(b, ct, fb, 2, passes))
        db = (_passes_bmm(a, ct, fa, 1, passes) if cb == 1
              else _passes_bmm(ct, a, 1, fa, passes))
        return da.astype(a.dtype), db.astype(b.dtype)

    mm.defvjp(mm_fwd, mm_bwd)
    return mm(a, b)


def _delta_chunk(q, k, v, gcol, grow, bcol, state):
    c = q.shape[1]
    ii = lax.broadcasted_iota(jnp.int32, (1, c, c), 1)
    jj = lax.broadcasted_iota(jnp.int32, (1, c, c), 2)
    incl, strict = ii >= jj, ii > jj
    gc_col = jnp.sum(jnp.where(incl, grow, 0.0), axis=2, keepdims=True)
    gc_row = jnp.sum(jnp.where(ii <= jj, gcol, 0.0), axis=1, keepdims=True)
    decay = jnp.where(incl, jnp.exp(jnp.where(incl, gc_col - gc_row, 0.0)), 0.0)
    kb, vb = k * bcol, v * bcol
    m = jnp.where(strict, _bmm(kb, k, 2, 2) * decay, 0.0)
    eye = (ii == jj).astype(F32)
    p = -m
    inv = eye + p
    for _ in range(int(math.log2(c)) - 1):
        p = _bmm(p, p, 2, 1, 3)
        inv = inv + _bmm(inv, p, 2, 1, 3)
    e_col = jnp.exp(gc_col)
    u = _bmm(inv, vb, 2, 1)
    w = _bmm(inv, kb * e_col, 2, 1)
    qk = _bmm(q, k, 2, 2) * decay
    v_new = u - _bmm(w, state, 2, 1)
    o = _bmm(q * e_col, state, 2, 1) + _bmm(qk, v_new, 2, 1)
    g_last = jnp.sum(grow, axis=2, keepdims=True)
    new_state = state * jnp.exp(g_last) + _bmm(k * jnp.exp(g_last - gc_col), v_new, 1, 1)
    return o, new_state


def _delta_chunk_packed(q, k, v, gb, state):
    nh, c = q.shape[0], q.shape[1]
    lane = lax.broadcasted_iota(jnp.int32, gb.shape, 1)
    eye = (lax.broadcasted_iota(jnp.int32, (c, c), 0) == lax.broadcasted_iota(jnp.int32, (c, c), 1))
    column = lambda l: jnp.sum(jnp.where(lane == l, gb, 0.0), axis=1, keepdims=True)
    heads = lambda parts: jnp.concatenate([t[None] for t in parts], axis=0)
    bcol = heads([column(h) for h in range(nh)])
    gcols = [column(nh + h) for h in range(nh)]
    gcol = heads(gcols)
    grow = heads([jnp.sum(jnp.where(eye, g, 0.0), axis=0, keepdims=True) for g in gcols])
    return _delta_chunk(q, k, v, gcol, grow, bcol, state)


def _delta_specs(nh, s, dh, rev):
    c, n = DN_CHUNK, s // DN_CHUNK
    t = (lambda i: n - 1 - i) if rev else (lambda i: i)
    seq = pl.BlockSpec((nh, c, dh), lambda i: (0, t(i), 0))
    gate = pl.BlockSpec((c, LANES), lambda i: (t(i), 0))
    st = pl.BlockSpec((nh, 1, dh, dh), lambda i: (0, t(i), 0, 0))
    return seq, gate, st


def delta_fwd(name, q, k, v, gb, gather=None):
    nh, s, dh = q.shape
    n = s // DN_CHUNK
    seq, gate, st = _delta_specs(nh, s, dh, False)
    extra = list(gather or [])
    ne = len(extra)

    def body(*refs):
        q_ref, k_ref, v_ref, gb_ref = refs[:4]
        x_refs = refs[4:4 + ne]
        o_ref, st_ref = refs[4 + ne:6 + ne]
        out_refs = refs[6 + ne:6 + 2 * ne]
        state = refs[6 + 2 * ne]
        i = pl.program_id(0)
        if ne:
            begin, finish = _gather_phases(x_refs, out_refs, refs[7 + 2 * ne:])
            pl.when(i == 0)(begin)

        @pl.when(i == 0)
        def _():
            state[...] = jnp.zeros_like(state)

        st_ref[:, 0] = state[...]
        o, new_state = _delta_chunk_packed(q_ref[...], k_ref[...], v_ref[...], gb_ref[...], state[...])
        o_ref[...] = o
        state[...] = new_state
        if ne:
            pl.when(i == n - 1)(finish)

    any_spec = pl.BlockSpec(memory_space=pl.ANY)
    out = pl.pallas_call(
        body, name=name + "_fwd", grid=(n,),
        in_specs=[seq, seq, seq, gate] + [any_spec] * ne,
        out_specs=[seq, st] + [any_spec] * ne,
        out_shape=[jax.ShapeDtypeStruct((nh, s, dh), F32),
                   jax.ShapeDtypeStruct((nh, n, dh, dh), F32)]
        + [jax.ShapeDtypeStruct((N_DEV,) + t.shape, t.dtype) for t in extra],
        scratch_shapes=[pltpu.VMEM((nh, dh, dh), F32)]
        + ([pltpu.SemaphoreType.DMA((7 * ne,)), pltpu.SemaphoreType.DMA((7 * ne,)),
            pltpu.SemaphoreType.DMA((ne,))] if ne else []),
        compiler_params=_cparams(1),
    )(q, k, v, gb, *extra)
    return out[0], out[1], list(out[2:])


def delta_bwd(name, q, k, v, gb, states, do, exchange=None):
    nh, s, dh = q.shape
    n = s // DN_CHUNK
    seq, gate, st = _delta_specs(nh, s, dh, True)
    extra = list(exchange or [])
    ne = len(extra)

    def body(*refs):
        q_ref, k_ref, v_ref, gb_ref, st_ref, do_ref = refs[:6]
        p_refs = refs[6:6 + ne]
        dq_ref, dk_ref, dv_ref, dgb_ref = refs[6 + ne:10 + ne]
        land_refs = refs[10 + ne:10 + 2 * ne]
        dstate = refs[10 + 2 * ne]
        i = pl.program_id(0)
        if ne:
            begin, finish = _chip_exchange_phases(p_refs, land_refs, refs[11 + 2 * ne:])
            pl.when(i == 0)(begin)

        @pl.when(i == 0)
        def _():
            dstate[...] = jnp.zeros_like(dstate)

        _, vjp = jax.vjp(_delta_chunk_packed, q_ref[...], k_ref[...], v_ref[...], gb_ref[...],
                         st_ref[:, 0])
        dq, dk, dv, dgb, dst = vjp((do_ref[...], dstate[...]))
        dq_ref[...] = dq
        dk_ref[...] = dk
        dv_ref[...] = dv
        dgb_ref[...] = dgb
        dstate[...] = dst
        if ne:
            pl.when(i == n - 1)(finish)

    any_spec = pl.BlockSpec(memory_space=pl.ANY)
    out = pl.pallas_call(
        body, name=name + "_bwd", grid=(n,),
        in_specs=[seq, seq, seq, gate, st, seq] + [any_spec] * ne,
        out_specs=[seq, seq, seq, gate] + [any_spec] * ne,
        out_shape=[jax.ShapeDtypeStruct((nh, s, dh), F32)] * 3
        + [jax.ShapeDtypeStruct((s, LANES), F32)]
        + [jax.ShapeDtypeStruct(t.shape, t.dtype) for t in extra],
        scratch_shapes=[pltpu.VMEM((nh, dh, dh), F32)]
        + ([pltpu.SemaphoreType.DMA((3 * ne,)), pltpu.SemaphoreType.DMA((3 * ne,)),
            pltpu.SemaphoreType.DMA((ne,))] if ne else []),
        compiler_params=_cparams(1),
    )(q, k, v, gb, states, do, *extra)
    return tuple(out[:4]), list(out[4:])


def delta_rule(name, q, k, v, gb):
    @jax.custom_vjp
    def op(q, k, v, gb):
        return delta_fwd(name, q, k, v, gb)[0]

    def op_fwd(q, k, v, gb):
        o, states, _ = delta_fwd(name, q, k, v, gb)
        return o, (q, k, v, gb, states)

    def op_bwd(res, do):
        return delta_bwd(name, *res, do)[0]

    op.defvjp(op_fwd, op_bwd)
    return op(q, k, v, gb)


def _my_place():
    return lax.axis_index("x"), lax.axis_index("y"), lax.axis_index("c")


def all_gather(name, shard):
    return all_gather_many(name, [shard])[0]


def _gather_phases(x_refs, out_refs, sems):
    n = len(x_refs)
    send_sems, recv_sems, local_sems = sems
    x, y, c = _my_place()
    me, sibling = (x, y, c), (x, y, 1 - c)
    chips = [(1 - x, y), (x, 1 - y), (1 - x, 1 - y)]

    def copy(i, k, block, to, own=False):
        px, py, pc = block
        slot = out_refs[i].at[4 * px + 2 * py + pc]
        return pltpu.make_async_remote_copy(
            src_ref=x_refs[i] if own else slot, dst_ref=slot,
            send_sem=send_sems.at[7 * i + k], recv_sem=recv_sems.at[7 * i + k],
            device_id=to, device_id_type=MESH)

    mine = [pltpu.make_async_copy(x_refs[i], out_refs[i].at[4 * x + 2 * y + c], local_sems.at[i])
            for i in range(n)]
    first = []
    for i in range(n):
        first.append(copy(i, 0, me, sibling, own=True))
        first += [copy(i, 1 + j, me, (*chip, c), own=True) for j, chip in enumerate(chips)]

    def begin():
        for cp in mine + first:
            cp.start()

    def finish():
        passed = []
        for j, chip in enumerate(chips):
            for i in range(n):
                copy(i, 1 + j, (*chip, c), me).wait_recv()
                passed.append(copy(i, 4 + j, (*chip, c), sibling))
                passed[-1].start()
        for i in range(n):
            copy(i, 0, sibling, me).wait_recv()
        for j, chip in enumerate(chips):
            for i in range(n):
                copy(i, 4 + j, (*chip, 1 - c), me).wait_recv()
        for cp in first + passed:
            cp.wait_send()
        for cp in mine:
            cp.wait()

    return begin, finish


def all_gather_many(name, shards):
    n = len(shards)

    def body(*refs):
        begin, finish = _gather_phases(refs[:n], refs[n:2 * n], refs[2 * n:])
        begin()
        finish()

    any_spec = pl.BlockSpec(memory_space=pl.ANY)
    return pl.pallas_call(
        body, name=name,
        out_shape=[jax.ShapeDtypeStruct((N_DEV,) + t.shape, t.dtype) for t in shards],
        in_specs=[any_spec] * n, out_specs=[any_spec] * n,
        scratch_shapes=[pltpu.SemaphoreType.DMA((7 * n,)), pltpu.SemaphoreType.DMA((7 * n,)),
                        pltpu.SemaphoreType.DMA((n,))],
    )(*shards)


def sibling_exchange(name, parts):
    n = len(parts)

    def body(*refs):
        p_refs, out_refs = refs[:n], refs[n:2 * n]
        send_sems, recv_sems = refs[2 * n:]
        x, y, c = _my_place()
        copies = []
        for i in range(n):
            for chip in range(4):
                copies.append(pltpu.make_async_remote_copy(
                    src_ref=p_refs[i].at[2 * chip + (1 - c)], dst_ref=out_refs[i].at[chip],
                    send_sem=send_sems.at[4 * i + chip], recv_sem=recv_sems.at[4 * i + chip],
                    device_id=(x, y, 1 - c), device_id_type=MESH))
        for cp in copies:
            cp.start()
        for cp in copies:
            cp.wait_recv()
        for cp in copies:
            cp.wait_send()

    any_spec = pl.BlockSpec(memory_space=pl.ANY)
    return pl.pallas_call(
        body, name=name,
        out_shape=[jax.ShapeDtypeStruct((4,) + t.shape[1:], t.dtype) for t in parts],
        in_specs=[any_spec] * n, out_specs=[any_spec] * n,
        scratch_shapes=[pltpu.SemaphoreType.DMA((4 * n,)), pltpu.SemaphoreType.DMA((4 * n,))],
    )(*parts)


def _chip_exchange_phases(p_refs, out_refs, sems):
    n = len(p_refs)
    send_sems, recv_sems, local_sems = sems
    x, y, c = _my_place()
    me = 2 * x + y
    peers = [(1 - x, y), (x, 1 - y), (1 - x, 1 - y)]

    def copy(i, j, landing):
        px, py = peers[j]
        return pltpu.make_async_remote_copy(
            src_ref=p_refs[i].at[2 * px + py],
            dst_ref=out_refs[i].at[(2 * px + py) if landing else me],
            send_sem=send_sems.at[3 * i + j], recv_sem=recv_sems.at[3 * i + j],
            device_id=(px, py, c), device_id_type=MESH)

    mine = [pltpu.make_async_copy(p_refs[i].at[me], out_refs[i].at[me], local_sems.at[i])
            for i in range(n)]
    copies = [copy(i, j, False) for j in range(3) for i in range(n)]

    def begin():
        for cp in mine + copies:
            cp.start()

    def finish():
        for j in range(3):
            for i in range(n):
                copy(i, j, True).wait_recv()
        for cp in copies:
            cp.wait_send()
        for cp in mine:
            cp.wait()

    return begin, finish


def chip_exchange(name, parts):
    n = len(parts)

    def body(*refs):
        begin, finish = _chip_exchange_phases(refs[:n], refs[n:2 * n], refs[2 * n:])
        begin()
        finish()

    any_spec = pl.BlockSpec(memory_space=pl.ANY)
    return pl.pallas_call(
        body, name=name,
        out_shape=[jax.ShapeDtypeStruct(t.shape, t.dtype) for t in parts],
        in_specs=[any_spec] * n, out_specs=[any_spec] * n,
        scratch_shapes=[pltpu.SemaphoreType.DMA((3 * n,)), pltpu.SemaphoreType.DMA((3 * n,)),
                        pltpu.SemaphoreType.DMA((n,))],
    )(*parts)


def pair_sum(name, mine, got, core):
    _, rows, cols = got.shape
    row_bytes = 4 * LANES * (-(-cols // LANES))
    tile = _pick(rows, [t for t in (512, 352, 256, 128, 64, 32, 16, 8)
                        if t * row_bytes <= ADAMW_BLOCK_BYTES])

    def body(core_ref, a_ref, b_ref, o_ref):
        o_ref[...] = (a_ref[...].astype(F32) + b_ref[...].astype(F32)).astype(o_ref.dtype)

    spec = pl.BlockSpec((4, tile, cols), lambda i, core_ref: (0, i, 0))
    return pl.pallas_call(
        body, name=name,
        grid_spec=pltpu.PrefetchScalarGridSpec(
            num_scalar_prefetch=1, grid=(rows // tile,),
            in_specs=[pl.BlockSpec((4, None, tile, cols), lambda i, core_ref: (0, core_ref[0], i, 0)),
                      spec],
            out_specs=spec),
        out_shape=jax.ShapeDtypeStruct(got.shape, got.dtype), compiler_params=_cparams(1),
    )(core, mine, got)


ADAMW_BLOCK_BYTES = 3 * 512 * 1024


def adamw(name, grad, w, m, v):
    rows, cols = w.shape
    stacked = grad.ndim == 3
    row_bytes = 4 * LANES * (-(-cols // LANES))
    tile = _pick(rows, [t for t in (512, 352, 256, 128, 64, 32, 16, 8)
                        if t * row_bytes <= ADAMW_BLOCK_BYTES])

    def body(g_ref, w_ref, m_ref, v_ref, go_ref, d_ref, mo_ref, vo_ref):
        if stacked:
            g = g_ref[0].astype(F32)
            for s in range(1, grad.shape[0]):
                g = g + g_ref[s].astype(F32)
        else:
            g = g_ref[...]
        m = ADAM_B1 * m_ref[...] + (1.0 - ADAM_B1) * g
        v = ADAM_B2 * v_ref[...] + (1.0 - ADAM_B2) * jnp.square(g)
        m_hat = m / (1.0 - ADAM_B1 ** ADAM_STEP)
        v_hat = v / (1.0 - ADAM_B2 ** ADAM_STEP)
        go_ref[...] = g
        d_ref[...] = -ADAM_LR * (m_hat / (jnp.sqrt(v_hat) + ADAM_EPS) + ADAM_WD * w_ref[...])
        mo_ref[...] = m
        vo_ref[...] = v

    flat = pl.BlockSpec((tile, cols), lambda i: (i, 0))
    g_spec = pl.BlockSpec((grad.shape[0], tile, cols), lambda i: (0, i, 0)) if stacked else flat
    return pl.pallas_call(
        body, name=name, grid=(rows // tile,),
        in_specs=[g_spec, flat, flat, flat], out_specs=[flat] * 4,
        out_shape=[jax.ShapeDtypeStruct((rows, cols), F32)] * 4,
        compiler_params=_cparams(1),
    )(grad, w, m, v)


def silu_rows(name, x):
    def body(x_ref, o_ref):
        o_ref[...] = _silu(x_ref[...])

    return pl.pallas_call(body, name=name, out_shape=jax.ShapeDtypeStruct(x.shape, F32))(x)


BIG = (("ffn1_wg", (D, D_FF // N_DEV), 1), ("ffn1_wu", (D, D_FF // N_DEV), 1),
       ("ffn1_wd", (D_FF // N_DEV, D), 0), ("w_in", (D, IN_COLS // N_DEV), 1),
       ("conv_w", (DN_CONV, 3 * D // N_DEV), 1), ("w_a", (D // N_DEV, D), 0),
       ("w_b", (DA_HEADS * DA_DIM, D // N_DEV), 1), ("w_o", (D // N_DEV, D), 0),
       ("ffn2_wg", (D, D_FF // N_DEV), 1), ("ffn2_wu", (D, D_FF // N_DEV), 1),
       ("ffn2_wd", (D_FF // N_DEV, D), 0))
SMALL = (("ada_b", (DEPTH, N_ADA * D)), ("ln_ffn1", (DEPTH, D)), ("ln_mix", (DEPTH, D)),
         ("ln_ffn2", (DEPTH, D)), ("a_log", (DEPTH, DN_HEADS)), ("dt_bias", (DEPTH, DN_HEADS)),
         ("dn_norm", (DEPTH, DN_DIM)), ("final_norm", (D,)))
SMALL_ROWS = 32


def _pack(arrays, rows):
    flat = jnp.concatenate([a.reshape(-1) for a in arrays])
    return jnp.pad(flat, (0, rows * D - flat.shape[0])).reshape(rows, D)


def _unpack(buf, shapes):
    flat = buf.reshape(-1)
    out, off = [], 0
    for shp in shapes:
        n = int(np.prod(shp))
        out.append(flat[off:off + n].reshape(shp))
        off += n
    return out


def _full_weights(gathered, entries):
    full = {}
    for (name, (a, b), axis), t in zip(entries, gathered):
        if axis == 1:
            full[name] = t.transpose(1, 0, 2).reshape(a, N_DEV * b)
        else:
            full[name] = t.reshape(N_DEV * a, b)
    return full


_B0, _DQ0, _GA0 = 4096, 4112, 6416
_N_WIDE = 6 * D


def _reorder_in_proj(w):
    pad = jnp.zeros((w.shape[0], IN_COLS_PAD - IN_COLS), w.dtype)
    return jnp.concatenate([w[:, :_B0], w[:, _GA0:], w[:, _DQ0:_GA0], w[:, _B0:_DQ0], pad], axis=1)


def _ffn(tag, h, ln, shift, scale, gate, w_gu, w_d):
    n = norm_mod(tag + "_norm", h, ln, shift, scale)
    gu = linear(tag + "_gu", n, w_gu, BF16)
    a = swiglu_act(tag + "_act", gu)
    f = linear(tag + "_down", a, w_d, F32)
    return residual(tag + "_res", h, f, gate, 0.5)


N_PRE = 5


def _mods(mod, l):
    return [mod[l, i * D:(i + 1) * D][None] for i in range(N_ADA)]


def _layer_pre(l, h, gathered, small, mod):
    w = _full_weights(gathered, BIG[:N_PRE])
    tag = f"l{l}"
    sh1, sc1, gt1, sh2, sc2 = _mods(mod, l)[:5]
    w_gu1 = _interleave_gate_up(w["ffn1_wg"], w["ffn1_wu"])
    h = _ffn(tag + "_ffn1", h, small["ln_ffn1"][l][None], sh1, sc1, gt1, w_gu1, w["ffn1_wd"])
    u = norm_mod(tag + "_mixnorm", h, small["ln_mix"][l][None], sh2, sc2)
    w_in = _reorder_in_proj(w["w_in"])
    (q_pre, k_pre, v_pre, z, gate_a, gate_b) = linear_split(
        tag + "_mix_in", u, w_in[:, :_N_WIDE], D, (1,) * 6, (BF16,) * 6)
    (da_q, da_k, da_v) = linear_split(
        tag + "_mix_in_da", u, w_in[:, _N_WIDE:_N_WIDE + 3 * DA_HEADS * DA_DIM], DA_HEADS * DA_DIM,
        (1,) * 3, (BF16,) * 3)
    ab = linear(tag + "_mix_in_ab", u, w_in[:, _N_WIDE + 3 * DA_HEADS * DA_DIM:], F32)
    cw = w["conv_w"].astype(F32).reshape(DN_CONV, 1, 3 * D)
    q, k, v = [conv_heads(f"{tag}_mix_conv_{m}", t, cw[:, :, i * D:(i + 1) * D], m)
               for i, (m, t) in enumerate(zip("qkv", (q_pre, k_pre, v_pre)))]
    pad = lambda t: jnp.pad(t, (DN_HEADS, ab.shape[1] - 2 * DN_HEADS))[None]
    gb = decay_beta(tag + "_mix_decay", ab, pad(small["a_log"][l]), pad(small["dt_bias"][l]))
    return (q, k, v, gb[:, :LANES]), (h, z, da_q, da_k, da_v, gate_a, gate_b)


def _layer_post(l, o, carry, gathered, small, mod):
    w = _full_weights(gathered, BIG[N_PRE:])
    tag = f"l{l}"
    h, z, da_q, da_k, da_v, gate_a, gate_b = carry
    gt2, sh3, sc3, gt3 = _mods(mod, l)[5:]
    o_a = gated_head_norm(tag + "_mix_gnorm", o, z, small["dn_norm"][l][None])
    y_a = linear(tag + "_mix_wa", o_a, w["w_a"], F32)

    outs, lses = [], []
    for r in DA_DILATIONS:
        o_r, lse_r = dilated_attention(f"{tag}_mix_attn{r}",
                                       *[_sub_order(t, r) for t in (da_q, da_k, da_v)], r)
        outs.append(_seq_order(o_r, r))
        lses.append(_seq_order(lse_r, r))
    o_b = combine_patterns(tag + "_mix_comb", outs, lses)
    y_b = linear(tag + "_mix_wb", o_b, w["w_b"], F32)

    merged = merge_gates(tag + "_mix_merge", gate_a, gate_b, y_a, y_b)
    m = linear(tag + "_mix_wo", merged, w["w_o"], F32)
    h = residual(tag + "_mixres", h, m, gt2, 1.0)
    w_gu2 = _interleave_gate_up(w["ffn2_wg"], w["ffn2_wu"])
    return _ffn(tag + "_ffn2", h, small["ln_ffn2"][l][None], sh3, sc3, gt3, w_gu2, w["ffn2_wd"])


def _layer(l, h, gathered, small, mod):
    (q, k, v, gb), carry = _layer_pre(l, h, gathered[:N_PRE], small, mod)
    o = delta_rule(f"l{l}_mix_delta", q, k, v, gb)
    return _layer_post(l, o, carry, gathered[N_PRE:], small, mod)


def _head(h, small, target):
    rows = loss_rows("loss", h, small["final_norm"][None], target)
    return 0.5 * jnp.sum(rows)


def _local_loss(x, gathered, small, mod, target):
    h = x
    for l in range(DEPTH):
        h = _layer(l, h, gathered[l], small, mod)
    return _head(h, small, target)


def kernel(x, c, ada_w, ada_b, ln_ffn1, ln_mix, ln_ffn2, ffn1_wg, ffn1_wu, ffn1_wd, w_in, conv_w, a_log, dt_bias, dn_norm, w_a, w_b, w_o, ffn2_wg, ffn2_wu, ffn2_wd, final_norm, loss_target, m_ada_w, m_ada_b, m_ln_ffn1, m_ln_mix, m_ln_ffn2, m_ffn1_wg, m_ffn1_wu, m_ffn1_wd, m_w_in, m_conv_w, m_a_log, m_dt_bias, m_dn_norm, m_w_a, m_w_b, m_w_o, m_ffn2_wg, m_ffn2_wu, m_ffn2_wd, m_final_norm, v_ada_w, v_ada_b, v_ln_ffn1, v_ln_mix, v_ln_ffn2, v_ffn1_wg, v_ffn1_wu, v_ffn1_wd, v_w_in, v_conv_w, v_a_log, v_dt_bias, v_dn_norm, v_w_a, v_w_b, v_w_o, v_ffn2_wg, v_ffn2_wu, v_ffn2_wd, v_final_norm):
    args = dict(locals())
    big_names = [n for n, _, _ in BIG]
    small_names = [n for n, _ in SMALL]
    me = 4 * lax.axis_index("x") + 2 * lax.axis_index("y") + lax.axis_index("c")
    cols = N_ADA * D // N_DEV

    c_all = all_gather("gather_c", jnp.pad(silu_rows("silu_c", c), ((0, 7), (0, 0))))[:, 0]
    mod_cols = jnp.stack([matmul(f"ada{l}", c_all, ada_w[l], "nn", F32) for l in range(DEPTH)])
    mod_cols = mod_cols + lax.dynamic_slice_in_dim(ada_b, me * cols, cols, axis=1)[:, None, :]
    mod_all = all_gather("gather_mod", mod_cols.reshape(DEPTH * N_DEV, cols))
    mod_all = mod_all.reshape(N_DEV, DEPTH, N_DEV, cols)
    mod = lax.dynamic_index_in_dim(mod_all, me, axis=2, keepdims=False)
    mod = mod.transpose(1, 0, 2).reshape(DEPTH, N_ADA * D)

    shards = [[args[n][l].astype(BF16) for n in big_names] for l in range(DEPTH)]
    pre_names, post_names = big_names[:N_PRE], big_names[N_PRE:]
    small = {n: args[n] for n in small_names if n != "ada_b"}
    target = loss_target[0]

    w0_pre = all_gather_many("gather_w0", shards[0][:N_PRE])
    pre0, vjp_pre0 = jax.vjp(functools.partial(_layer_pre, 0), x[0], w0_pre, small, mod)
    (q0, k0, v0, gb0), carry0 = pre0
    o0, states0, got = delta_fwd("l0_mix_delta", q0, k0, v0, gb0,
                                 gather=shards[0][N_PRE:] + shards[1][:N_PRE])
    w0_post, w1_pre = got[:len(post_names)], got[len(post_names):]
    h1, vjp_post0 = jax.vjp(functools.partial(_layer_post, 0), o0, carry0, w0_post, small, mod)
    pre1, vjp_pre1 = jax.vjp(functools.partial(_layer_pre, 1), h1, w1_pre, small, mod)
    (q1, k1, v1, gb1), carry1 = pre1
    o1, states1, w1_post = delta_fwd("l1_mix_delta", q1, k1, v1, gb1, gather=shards[1][N_PRE:])
    loss, vjp_post1 = jax.vjp(
        lambda o, carry, w, sm, md: _head(_layer_post(1, o, carry, w, sm, md), sm, target),
        o1, carry1, w1_post, small, mod)

    my_core = lax.axis_index("c").astype(jnp.int32).reshape(1)

    def pair_sums(tag, names, grads):
        from_sibling = sibling_exchange(f"pair_grads_{tag}", grads)
        return [pair_sum(f"pair_sum_{tag}_{n}", t.reshape(4, 2, -1, t.shape[-1]), got_n, my_core)
                for n, t, got_n in zip(names, grads, from_sibling)]

    do1, dcarry1, dw1_post, dsmall_a, dmod_a = vjp_post1(jnp.ones((), F32))
    ddelta1, landed1_post = delta_bwd("l1_mix_delta", q1, k1, v1, gb1, states1, do1,
                                      exchange=pair_sums("l1_post", post_names, list(dw1_post)))
    dh1, dw1_pre, dsmall_b, dmod_b = vjp_pre1((ddelta1, dcarry1))
    do0, dcarry0, dw0_post, dsmall_c, dmod_c = vjp_post0(dh1)
    ddelta0, landed_mid = delta_bwd(
        "l0_mix_delta", q0, k0, v0, gb0, states0, do0,
        exchange=pair_sums("mid", pre_names + post_names, list(dw1_pre) + list(dw0_post)))
    dx, dw0_pre, dsmall_d, dmod_d = vjp_pre0((ddelta0, dcarry0))
    landed0_pre = chip_exchange("scatter_grads0", pair_sums("l0_pre", pre_names, list(dw0_pre)))
    landed0 = list(landed0_pre) + list(landed_mid[N_PRE:])
    landed1 = list(landed_mid[:N_PRE]) + list(landed1_post)
    dsmall = {n: dsmall_a[n] + dsmall_b[n] + dsmall_c[n] + dsmall_d[n] for n in dsmall_a}
    dmod = dmod_a + dmod_b + dmod_c + dmod_d

    part = _pack([dmod] + [dsmall[n] for n in small_names[1:]], SMALL_ROWS)
    parts = all_gather("gather_small", part)
    sm_out = adamw("adamw_small", parts, _pack([args[n] for n in small_names], SMALL_ROWS),
                   _pack([args["m_" + n] for n in small_names], SMALL_ROWS),
                   _pack([args["v_" + n] for n in small_names], SMALL_ROWS))

    dmod_all = parts.reshape(N_DEV, -1)[:, :DEPTH * N_ADA * D].reshape(N_DEV, DEPTH, N_ADA * D)
    dmod_mine = lax.dynamic_slice_in_dim(dmod_all, me * cols, cols, axis=2)
    g_ada = jnp.stack([matmul(f"ada{l}_dw", c_all, dmod_mine[:, l], "tn", F32) for l in range(DEPTH)])
    flat2 = lambda t: t.reshape(-1, t.shape[-1])
    ada_out = adamw("adamw_ada_w", flat2(g_ada), flat2(ada_w), flat2(m_ada_w), flat2(v_ada_w))

    big_out = {}
    for n, t0, t1 in zip(big_names, landed0, landed1):
        both = jnp.concatenate([t0, t1], axis=1)
        big_out[n] = adamw("adamw_" + n, both, flat2(args[n]), flat2(args["m_" + n]), flat2(args["v_" + n]))

    small_shapes = [shp for _, shp in SMALL]
    names = ["ada_w", "ada_b", "ln_ffn1", "ln_mix", "ln_ffn2", "ffn1_wg", "ffn1_wu", "ffn1_wd", "w_in",
             "conv_w", "a_log", "dt_bias", "dn_norm", "w_a", "w_b", "w_o", "ffn2_wg", "ffn2_wu",
             "ffn2_wd", "final_norm"]
    outs = [lax.psum(loss, ("x", "y", "c")), dx[None]]
    for kind in range(4):
        table = {n: big_out[n][kind].reshape(args[n].shape) for n in big_names}
        table.update(zip(small_names, _unpack(sm_out[kind], small_shapes)))
        table["ada_w"] = ada_out[kind].reshape(ada_w.shape)
        outs += [table[n] for n in names]
    return tuple(outs)
```

```python
import functools
import math

import numpy as np
import jax
import jax.numpy as jnp
from jax import lax
from jax.experimental import pallas as pl
from jax.experimental.pallas import tpu as pltpu

F32 = jnp.float32
BF16 = jnp.bfloat16

D = 1024
SEQ = 4096
DEPTH = 2
N_DEV = 8
DN_HEADS = 8
DN_DIM = 128
DN_CHUNK = 64
DN_CONV = 4
DA_HEADS = 12
DA_DIM = 64
DA_BLOCK = 128
DA_DILATIONS = (1, 4, 16)
ALIBI_MAX_EXP = 8.0
D_FF = 2816
N_ADA = 9
NORM_EPS = 1e-6
IN_COLS = 8464
IN_COLS_PAD = 8704
ADAM_LR, ADAM_B1, ADAM_B2, ADAM_EPS, ADAM_WD, ADAM_STEP = 0.001, 0.9, 0.999, 1e-08, 0.01, 10
NEG = -1e30

VMEM_LIMIT = 56 * 1024 * 1024
LANES = 128

MESH = pl.DeviceIdType.MESH


def _cparams(n_grid):
    return pltpu.CompilerParams(dimension_semantics=("arbitrary",) * n_grid,
                                vmem_limit_bytes=VMEM_LIMIT)


def blockwise(name, f, grid, ins, outs):
    n_in, n_out = len(ins), len(outs)
    diff = [i for i, (_, _, kind) in enumerate(ins) if kind != "const"]

    def apply(*vals):
        res = f(*vals)
        return tuple(r.astype(dt) for r, (_, dt, _, _) in zip(res, outs))

    def fwd_call(*arrays):
        def body(*refs):
            res = apply(*[r[...] for r in refs[:n_in]])
            for r, v in zip(refs[n_in:], res):
                r[...] = v

        return pl.pallas_call(
            body, name=name + "_fwd", grid=grid,
            in_specs=[pl.BlockSpec(b, im) for (b, im, _) in ins],
            out_specs=[pl.BlockSpec(b, im) for (_, _, b, im) in outs],
            out_shape=[jax.ShapeDtypeStruct(s, dt) for (s, dt, _, _) in outs],
            compiler_params=_cparams(len(grid)),
        )(*arrays)

    def bwd_call(arrays, cts):
        def body(*refs):
            in_refs, ct_refs = refs[:n_in], refs[n_in:n_in + n_out]
            g_refs = refs[n_in + n_out:]
            vals = [r[...] for r in in_refs]

            def fd(*dvals):
                full = list(vals)
                for i, v in zip(diff, dvals):
                    full[i] = v
                return apply(*full)

            _, vjp = jax.vjp(fd, *[vals[i] for i in diff])
            grads = vjp(tuple(r[...] for r in ct_refs))
            first = functools.reduce(jnp.logical_and,
                                     [pl.program_id(a) == 0 for a in range(len(grid))])
            for g_ref, g, i in zip(g_refs, grads, diff):
                if ins[i][2] == "acc":
                    @pl.when(first)
                    def _(g_ref=g_ref):
                        g_ref[...] = jnp.zeros_like(g_ref)
                    g_ref[...] += g.astype(F32)
                else:
                    g_ref[...] = g.astype(g_ref.dtype)

        g_shapes = [jax.ShapeDtypeStruct(arrays[i].shape,
                                         F32 if ins[i][2] == "acc" else arrays[i].dtype)
                    for i in diff]
        return pl.pallas_call(
            body, name=name + "_bwd", grid=grid,
            in_specs=([pl.BlockSpec(b, im) for (b, im, _) in ins]
                      + [pl.BlockSpec(b, im) for (_, _, b, im) in outs]),
            out_specs=[pl.BlockSpec(ins[i][0], ins[i][1]) for i in diff],
            out_shape=g_shapes,
            compiler_params=_cparams(len(grid)),
        )(*arrays, *cts)

    @jax.custom_vjp
    def op(*arrays):
        return tuple(fwd_call(*arrays))

    def op_fwd(*arrays):
        return tuple(fwd_call(*arrays)), arrays

    def op_bwd(arrays, cts):
        grads = bwd_call(arrays, cts)
        full = [None] * n_in
        for i, g in zip(diff, grads):
            full[i] = g.astype(arrays[i].dtype)
        return tuple(full)

    op.defvjp(op_fwd, op_bwd)
    op.backward = bwd_call
    return op


def _pick(n, cands):
    for c in cands:
        if n % c == 0:
            return c
    return n


def matmul(name, a, b, form, out_dtype):
    if form == "nn":
        (m, k), (_, n) = a.shape, b.shape
    elif form == "nt":
        (m, k), (n, _) = a.shape, b.shape
    else:
        (k, m), (_, n) = a.shape, b.shape
    tm = _pick(m, (1408, 1024, 512, 256, 128, 8))
    tn = _pick(n, (1408, 1024, 512, 384, 256, 128))
    tk = _pick(k, (1024, 1408, 768, 512, 384, 256, 128, 8))
    nk = k // tk
    a_spec = (pl.BlockSpec((tk, tm), lambda i, j, kk: (kk, i)) if form == "tn"
              else pl.BlockSpec((tm, tk), lambda i, j, kk: (i, kk)))
    b_spec = (pl.BlockSpec((tn, tk), lambda i, j, kk: (j, kk)) if form == "nt"
              else pl.BlockSpec((tk, tn), lambda i, j, kk: (kk, j)))
    dims = {"nn": (((1,), (0,)), ((), ())), "nt": (((1,), (1,)), ((), ())),
            "tn": (((0,), (0,)), ((), ()))}[form]

    def body(a_ref, b_ref, o_ref, acc_ref):
        kk = pl.program_id(2)
        part = lax.dot_general(a_ref[...].astype(BF16), b_ref[...].astype(BF16), dims,
                               preferred_element_type=F32)
        if nk == 1:
            o_ref[...] = part.astype(o_ref.dtype)
            return

        @pl.when(kk == 0)
        def _():
            acc_ref[...] = part

        @pl.when(kk > 0)
        def _():
            acc_ref[...] += part

        @pl.when(kk == nk - 1)
        def _():
            o_ref[...] = acc_ref[...].astype(o_ref.dtype)

    return pl.pallas_call(
        body, name=name, grid=(m // tm, n // tn, nk),
        in_specs=[a_spec, b_spec],
        out_specs=pl.BlockSpec((tm, tn), lambda i, j, kk: (i, j)),
        out_shape=jax.ShapeDtypeStruct((m, n), out_dtype),
        scratch_shapes=[pltpu.VMEM((tm, tn), F32)],
        compiler_params=_cparams(3),
    )(a, b)


def linear(name, x, w, out_dtype):
    @jax.custom_vjp
    def op(x, w):
        return matmul(name + "_y", x, w, "nn", out_dtype)

    def op_fwd(x, w):
        return op(x, w), (x, w)

    def op_bwd(res, dy):
        x, w = res
        dx = matmul(name + "_dx", dy, w, "nt", x.dtype)
        dw = matmul(name + "_dw", x, dy, "tn", w.dtype)
        return dx, dw

    op.defvjp(op_fwd, op_bwd)
    return op(x, w)


def linear_split(name, x, w, tile, seg_tiles, out_dtypes):
    SPLIT_TILE = tile
    m, k = x.shape
    n = w.shape[1]
    nt = n // SPLIT_TILE
    starts = [sum(seg_tiles[:s]) for s in range(len(seg_tiles))]
    assert sum(seg_tiles) == nt
    tm = _pick(m, (1024, 512, 256, 128, 8))

    def inside(j, s):
        return jnp.logical_and(j >= starts[s], j < starts[s] + seg_tiles[s])

    def local(j, s):
        return jnp.clip(j - starts[s], 0, seg_tiles[s] - 1)

    def fwd_call(x, w):
        def body(x_ref, w_ref, *o_refs):
            j = pl.program_id(1)
            y = jnp.dot(x_ref[...].astype(BF16), w_ref[...].astype(BF16), preferred_element_type=F32)
            for s, o_ref in enumerate(o_refs):
                @pl.when(inside(j, s))
                def _(o_ref=o_ref):
                    o_ref[...] = y.astype(o_ref.dtype)

        return pl.pallas_call(
            body, name=name + "_y", grid=(m // tm, nt),
            in_specs=[pl.BlockSpec((tm, k), lambda i, j: (i, 0)),
                      pl.BlockSpec((k, SPLIT_TILE), lambda i, j: (0, j))],
            out_specs=[pl.BlockSpec((tm, SPLIT_TILE), lambda i, j, s=s: (i, local(j, s)))
                       for s in range(len(seg_tiles))],
            out_shape=[jax.ShapeDtypeStruct((m, t * SPLIT_TILE), dt)
                       for t, dt in zip(seg_tiles, out_dtypes)],
            compiler_params=_cparams(2),
        )(x, w)

    def dx_call(dys, w):
        def body(*refs):
            dy_refs, w_ref, o_ref, acc_ref = refs[:-3], refs[-3], refs[-2], refs[-1]
            j = pl.program_id(1)
            for s, dy_ref in enumerate(dy_refs):
                @pl.when(inside(j, s))
                def _(dy_ref=dy_ref):
                    part = lax.dot_general(dy_ref[...].astype(BF16), w_ref[...].astype(BF16),
                                           (((1,), (1,)), ((), ())), preferred_element_type=F32)

                    @pl.when(j == 0)
                    def _():
                        acc_ref[...] = part

                    @pl.when(j > 0)
                    def _():
                        acc_ref[...] += part

            @pl.when(j == nt - 1)
            def _():
                o_ref[...] = acc_ref[...].astype(o_ref.dtype)

        return pl.pallas_call(
            body, name=name + "_dx", grid=(m // tm, nt),
            in_specs=[pl.BlockSpec((tm, SPLIT_TILE), lambda i, j, s=s: (i, local(j, s)))
                      for s in range(len(seg_tiles))]
            + [pl.BlockSpec((k, SPLIT_TILE), lambda i, j: (0, j))],
            out_specs=pl.BlockSpec((tm, k), lambda i, j: (i, 0)),
            out_shape=jax.ShapeDtypeStruct((m, k), x.dtype),
            scratch_shapes=[pltpu.VMEM((tm, k), F32)],
            compiler_params=_cparams(2),
        )(*dys, w)

    def dw_call(x, dys):
        nm = m // tm

        def body(*refs):
            x_ref, dy_refs, o_ref, acc_ref = refs[0], refs[1:-2], refs[-2], refs[-1]
            j, kk = pl.program_id(0), pl.program_id(1)
            for s, dy_ref in enumerate(dy_refs):
                @pl.when(inside(j, s))
                def _(dy_ref=dy_ref):
                    rows = x_ref[pl.ds(pl.multiple_of(kk * tm, tm), tm), :]
                    part = lax.dot_general(rows.astype(BF16), dy_ref[...].astype(BF16),
                                           (((0,), (0,)), ((), ())), preferred_element_type=F32)

                    @pl.when(kk == 0)
                    def _():
                        acc_ref[...] = part

                    @pl.when(kk > 0)
                    def _():
                        acc_ref[...] += part

            @pl.when(kk == nm - 1)
            def _():
                o_ref[...] = acc_ref[...].astype(o_ref.dtype)

        return pl.pallas_call(
            body, name=name + "_dw", grid=(nt, nm),
            in_specs=[pl.BlockSpec((m, k), lambda j, kk: (0, 0))]
            + [pl.BlockSpec((tm, SPLIT_TILE),
                            lambda j, kk, s=s: (jnp.where(inside(j, s), kk, 0), local(j, s)))
               for s in range(len(seg_tiles))],
            out_specs=pl.BlockSpec((k, SPLIT_TILE), lambda j, kk: (0, j)),
            out_shape=jax.ShapeDtypeStruct((k, n), w.dtype),
            scratch_shapes=[pltpu.VMEM((k, SPLIT_TILE), F32)],
            compiler_params=_cparams(2),
        )(x, *dys)

    @jax.custom_vjp
    def op(x, w):
        return tuple(fwd_call(x, w))

    def op_fwd(x, w):
        return op(x, w), (x, w)

    def op_bwd(res, dys):
        x, w = res
        return dx_call(dys, w), dw_call(x, dys)

    op.defvjp(op_fwd, op_bwd)
    return op(x, w)


@jax.custom_vjp
def _sigmoid(x):
    return 1.0 / (1.0 + jnp.exp(-x))


def _sigmoid_fwd(x):
    s = _sigmoid(x)
    return s, s


def _sigmoid_bwd(s, ct):
    return (ct * (s * (1.0 - s)),)


_sigmoid.defvjp(_sigmoid_fwd, _sigmoid_bwd)


@jax.custom_vjp
def _silu(x):
    return x * _sigmoid(x)


def _silu_fwd(x):
    s = _sigmoid(x)
    return x * s, (x, s)


def _silu_bwd(res, ct):
    x, s = res
    return (ct * (s * (1.0 + x * (1.0 - s))),)


_silu.defvjp(_silu_fwd, _silu_bwd)


def _softplus(x):
    return jnp.maximum(x, 0.0) + jnp.log(1.0 + jnp.exp(-jnp.abs(x)))


def _rms(x):
    return x * lax.rsqrt(jnp.mean(x * x, axis=-1, keepdims=True) + NORM_EPS)


ROW_TILE = 512


def _row(i):
    return (i, 0)


def _fixed(*_):
    return (0, 0)


def norm_mod(name, h, ln, shift, scale):
    s, d = h.shape

    def f(h, ln, sh, sc):
        return ((_rms(h) * ln) * (1.0 + sc) + sh,)

    op = blockwise(name, f, (s // ROW_TILE,),
                   [((ROW_TILE, d), _row, "tile")] + [((1, d), _fixed, "acc")] * 3,
                   [((s, d), BF16, (ROW_TILE, d), _row)])
    return op(h, ln, shift, scale)[0]


FF_TILE = 1408


def _interleave_gate_up(wg, wu):
    parts = []
    for j in range(wg.shape[1] // FF_TILE):
        parts += [wg[:, j * FF_TILE:(j + 1) * FF_TILE], wu[:, j * FF_TILE:(j + 1) * FF_TILE]]
    return jnp.concatenate(parts, axis=1)


def _swiglu(gu):
    g, u = gu[:, :FF_TILE].astype(F32), gu[:, FF_TILE:].astype(F32)
    return _silu(g) * u


def _swiglu_op(name, s, ff):
    return blockwise(name, lambda gu: (_swiglu(gu),), (s // ROW_TILE, ff // FF_TILE),
                     [((ROW_TILE, 2 * FF_TILE), lambda i, j: (i, j), "tile")],
                     [((s, ff), BF16, (ROW_TILE, FF_TILE), lambda i, j: (i, j))])


def gate_up_act(name, x, w):
    s, k = x.shape
    ff = w.shape[1] // 2
    tm = _pick(s, (512, 256, 128, 8))

    def fwd_call(x, w):
        def body(x_ref, w_ref, gu_ref, a_ref):
            gu = jnp.dot(x_ref[...], w_ref[...], preferred_element_type=F32).astype(BF16)
            gu_ref[...] = gu
            a_ref[...] = _swiglu(gu).astype(BF16)

        return pl.pallas_call(
            body, name=name + "_y", grid=(ff // FF_TILE, s // tm),
            in_specs=[pl.BlockSpec((tm, k), lambda j, i: (i, 0)),
                      pl.BlockSpec((k, 2 * FF_TILE), lambda j, i: (0, j))],
            out_specs=[pl.BlockSpec((tm, 2 * FF_TILE), lambda j, i: (i, j)),
                       pl.BlockSpec((tm, FF_TILE), lambda j, i: (i, j))],
            out_shape=[jax.ShapeDtypeStruct((s, 2 * ff), BF16), jax.ShapeDtypeStruct((s, ff), BF16)],
            compiler_params=_cparams(2),
        )(x, w)

    @jax.custom_vjp
    def op(x, w):
        return fwd_call(x, w)[1]

    def op_fwd(x, w):
        gu, a = fwd_call(x, w)
        return a, (x, w, gu)

    def op_bwd(res, da):
        x, w, gu = res
        dgu = _swiglu_op(name + "_act", s, ff).backward((gu,), (da,))[0]
        return (matmul(name + "_dx", dgu, w, "nt", x.dtype), matmul(name + "_dw", x, dgu, "tn", w.dtype))

    op.defvjp(op_fwd, op_bwd)
    return op(x, w)


def residual(name, h, y, gate, weight):
    s, d = h.shape

    def f(h, y, gate):
        return (h + (weight * gate) * y,)

    op = blockwise(name, f, (s // ROW_TILE,),
                   [((ROW_TILE, d), _row, "tile"), ((ROW_TILE, d), _row, "tile"),
                    ((1, d), _fixed, "acc")],
                   [((s, d), F32, (ROW_TILE, d), _row)])
    return op(h, y, gate)[0]


def merge_gates(name, ga, gb, ya, yb):
    s, d = ya.shape

    def f(ga, gb, ya, yb):
        return (_sigmoid(ga.astype(F32)) * ya + _sigmoid(gb.astype(F32)) * yb,)

    op = blockwise(name, f, (s // ROW_TILE,), [((ROW_TILE, d), _row, "tile")] * 4,
                   [((s, d), BF16, (ROW_TILE, d), _row)])
    return op(ga, gb, ya, yb)[0]


def loss_rows(name, h, g, target):
    s, d = h.shape

    def f(h, g, t):
        err = _rms(h) * g - t
        return (jnp.mean(err * err, axis=-1, keepdims=True),)

    op = blockwise(name, f, (s // ROW_TILE,),
                   [((ROW_TILE, d), _row, "tile"), ((1, d), _fixed, "acc"),
                    ((ROW_TILE, d), _row, "const")],
                   [((s, 1), F32, (ROW_TILE, 1), _row)])
    return op(h, g, target)[0]


def decay_beta(name, ab, a_log_pad, dt_bias_pad):
    s, n = ab.shape

    def f(ab, a_log, dt_bias):
        lane = lax.broadcasted_iota(jnp.int32, ab.shape, 1)
        beta = _sigmoid(ab)
        g = -jnp.exp(a_log) * _softplus(ab + dt_bias)
        return (jnp.where(lane < DN_HEADS, beta, jnp.where(lane < 2 * DN_HEADS, g, 0.0)),)

    op = blockwise(name, f, (s // ROW_TILE,),
                   [((ROW_TILE, n), _row, "tile"), ((1, n), _fixed, "acc"), ((1, n), _fixed, "acc")],
                   [((s, n), F32, (ROW_TILE, n), _row)])
    return op(ab, a_log_pad, dt_bias_pad)[0]


def _shift_rows(x, k):
    n = x.shape[0]

    @jax.custom_vjp
    def shift(x):
        row = lax.broadcasted_iota(jnp.int32, x.shape, 0)
        return jnp.where(row >= k, pltpu.roll(x, k, 0), 0.0)

    def shift_fwd(x):
        return shift(x), None

    def shift_bwd(_, g):
        row = lax.broadcasted_iota(jnp.int32, g.shape, 0)
        return (jnp.where(row < n - k, pltpu.roll(g, n - k, 0), 0.0),)

    shift.defvjp(shift_fwd, shift_bwd)
    return shift(x)


def conv_heads(name, x, w, mode):
    s, width = x.shape
    nh = width // LANES

    def f(x, w):
        x = x.astype(F32)
        y = w[DN_CONV - 1] * x
        for j in range(DN_CONV - 1):
            y = y + w[j] * _shift_rows(x, DN_CONV - 1 - j)
        y = _silu(y)
        if mode != "v":
            y = y * lax.rsqrt(jnp.sum(y * y, axis=-1, keepdims=True) + NORM_EPS)
        if mode == "q":
            y = y * (DN_DIM ** -0.5)
        return (y[None],)

    op = blockwise(name, f, (nh,),
                   [((s, LANES), lambda j: (0, j), "tile"),
                    ((DN_CONV, 1, LANES), lambda j: (0, 0, j), "tile")],
                   [((nh, s, LANES), F32, (1, s, LANES), lambda j: (j, 0, 0))])
    return op(x, w)[0]


def gated_head_norm(name, o, z, w):
    nh, s, dh = o.shape

    def f(o, z, w):
        return (_rms(o[0]) * w * _silu(z.astype(F32)),)

    op = blockwise(name, f, (s // ROW_TILE, nh),
                   [((1, ROW_TILE, dh), lambda i, h: (h, i, 0), "tile"),
                    ((ROW_TILE, dh), lambda i, h: (i, h), "tile"),
                    ((1, dh), lambda i, h: (0, 0), "acc")],
                   [((s, nh * dh), BF16, (ROW_TILE, dh), lambda i, h: (i, h))])
    return op(o, z, w)[0]


def _mm(a, b, ca, cb):
    return _bmm(a[None], b[None], ca + 1, cb + 1)[0]


def dilated_attention(name, q, k, v, dilation):
    s, width = q.shape
    nblk = s // DA_BLOCK
    per_sub = nblk // dilation
    slopes = [dilation * 2.0 ** (-ALIBI_MAX_EXP * (h + 1) / DA_HEADS) for h in range(DA_HEADS)]

    def f(q, kp, kc, vp, vc):
        first = (pl.program_id(0) % per_sub) == 0
        qi = lax.broadcasted_iota(jnp.int32, (DA_BLOCK, 2 * DA_BLOCK), 0)
        ki = lax.broadcasted_iota(jnp.int32, (DA_BLOCK, 2 * DA_BLOCK), 1)
        steps = (qi + DA_BLOCK - ki).astype(F32)
        lowest = qi + first.astype(jnp.int32) * (DA_BLOCK - qi)
        valid = jnp.logical_and(ki >= lowest, ki <= qi + DA_BLOCK)
        top = lax.broadcasted_iota(jnp.int32, (DA_BLOCK, LANES), 1) < DA_DIM
        o_parts, lse_parts = [], []
        for pair in range(width // LANES):
            cols = slice(pair * LANES, (pair + 1) * LANES)
            q2 = jnp.concatenate([jnp.where(top, q[:, cols], 0.0), jnp.where(top, 0.0, q[:, cols])], axis=0)
            k2 = jnp.concatenate([kp[:, cols], kc[:, cols]], axis=0)
            v2 = jnp.concatenate([vp[:, cols], vc[:, cols]], axis=0)
            bias = jnp.concatenate([jnp.where(valid, -slopes[2 * pair + half] * steps, NEG)
                                    for half in range(2)], axis=0)
            sc = _mm(q2, k2, 1, 1) * (DA_DIM ** -0.5) + bias
            mx = jnp.max(sc, axis=-1, keepdims=True)
            p = jnp.exp(sc - mx)
            l = jnp.sum(p, axis=-1, keepdims=True)
            o2 = _mm(p / l, v2, 1, 0)
            lse2 = mx + jnp.log(l)
            o_parts.append(jnp.where(top, o2[:DA_BLOCK], o2[DA_BLOCK:]))
            lse_parts.append(jnp.where(top, lse2[:DA_BLOCK], lse2[DA_BLOCK:]))
        return jnp.concatenate(o_parts, axis=1), jnp.concatenate(lse_parts, axis=1)

    blk = (DA_BLOCK, width)
    cur_map = lambda j: (j, 0)
    prev_map = lambda j: (jnp.maximum(j - 1, 0), 0)
    op = blockwise(
        name, f, (nblk,),
        [(blk, cur_map, "tile"), (blk, prev_map, "tile"), (blk, cur_map, "tile"),
         (blk, prev_map, "tile"), (blk, cur_map, "tile")],
        [((s, width), F32, blk, cur_map), ((s, width), F32, blk, cur_map)])

    @jax.custom_vjp
    def attn(q, k, v):
        return op(q, k, k, v, v)

    def attn_fwd(q, k, v):
        return attn(q, k, v), (q, k, v)

    def attn_bwd(res, cts):
        q, k, v = res
        _, vjp = jax.vjp(op, q, k, k, v, v)
        dq, dkp, dkc, dvp, dvc = vjp(cts)
        fill = lambda t: t.at[s - DA_BLOCK:].set(0.0)
        return dq, dkc + fill(dkp), dvc + fill(dvp)

    attn.defvjp(attn_fwd, attn_bwd)
    return attn(q, k, v)


def combine_patterns(name, outs, lses):
    s, width = outs[0].shape
    n = len(outs)

    def f(*vals):
        o, lse = vals[:n], vals[n:]
        mx = functools.reduce(jnp.maximum, lse)
        e = [jnp.exp(t - mx) for t in lse]
        return (sum(ei * oi for ei, oi in zip(e, o)) / sum(e),)

    tile = ROW_TILE // 2
    op = blockwise(name, f, (s // tile,), [((tile, width), _row, "tile")] * (2 * n),
                   [((s, width), BF16, (tile, width), _row)])
    return op(*outs, *lses)[0]


def _sub_order(t, r):
    if r == 1:
        return t
    s, c = t.shape
    return t.reshape(s // r, r, c).transpose(1, 0, 2).reshape(s, c)


def _seq_order(t, r):
    if r == 1:
        return t
    s, c = t.shape
    return t.reshape(r, s // r, c).transpose(1, 0, 2).reshape(s, c)


def _raw_bmm(a, b, ca, cb):
    return lax.dot_general(a.astype(BF16), b.astype(BF16), (((ca,), (cb,)), ((0,), (0,))),
                           preferred_element_type=F32)


def _split(a):
    hi = a.astype(BF16)
    return hi, (a - hi.astype(F32)).astype(BF16)


def _passes_bmm(a, b, ca, cb, passes):
    if passes == 1:
        return _raw_bmm(a, b, ca, cb)
    (a_hi, a_lo), (b_hi, b_lo) = _split(a), _split(b)
    return _raw_bmm(a_hi, b_hi, ca, cb) + (_raw_bmm(a_hi, b_lo, ca, cb) + _raw_bmm(a_lo, b_hi, ca, cb))


def _bmm(a, b, ca, cb, passes=1):
    fa, fb = 3 - ca, 3 - cb

    @jax.custom_vjp
    def mm(a, b):
        return _passes_bmm(a, b, ca, cb, passes)

    def mm_fwd(a, b):
        return mm(a, b), (a, b)

    def mm_bwd(res, ct):
        a, b = res
        da = (_passes_bmm(ct, b, 2, fb, passes) if ca == 2
              else _passes_bmm(b, ct, fb, 2, passes))
        db = (_passes_bmm(a, ct, fa, 1, passes) if cb == 1
              else _passes_bmm(ct, a, 1, fa, passes))
        return da.astype(a.dtype), db.astype(b.dtype)

    mm.defvjp(mm_fwd, mm_bwd)
    return mm(a, b)


def _delta_chunk(q, k, v, gcol, grow, bcol, state):
    c = q.shape[1]
    ii = lax.broadcasted_iota(jnp.int32, (1, c, c), 1)
    jj = lax.broadcasted_iota(jnp.int32, (1, c, c), 2)
    incl, strict = ii >= jj, ii > jj
    gc_col = jnp.sum(jnp.where(incl, grow, 0.0), axis=2, keepdims=True)
    gc_row = jnp.sum(jnp.where(ii <= jj, gcol, 0.0), axis=1, keepdims=True)
    decay = jnp.where(incl, jnp.exp(jnp.where(incl, gc_col - gc_row, 0.0)), 0.0)
    kb, vb = k * bcol, v * bcol
    m = jnp.where(strict, _bmm(kb, k, 2, 2) * decay, 0.0)
    eye = (ii == jj).astype(F32)
    p = -m
    inv = eye + p
    for _ in range(int(math.log2(c)) - 1):
        p = _bmm(p, p, 2, 1, 3)
        inv = inv + _bmm(inv, p, 2, 1, 3)
    e_col = jnp.exp(gc_col)
    u = _bmm(inv, vb, 2, 1)
    w = _bmm(inv, kb * e_col, 2, 1)
    qk = _bmm(q, k, 2, 2) * decay
    v_new = u - _bmm(w, state, 2, 1)
    o = _bmm(q * e_col, state, 2, 1) + _bmm(qk, v_new, 2, 1)
    g_last = jnp.sum(grow, axis=2, keepdims=True)
    new_state = state * jnp.exp(g_last) + _bmm(k * jnp.exp(g_last - gc_col), v_new, 1, 1)
    return o, new_state


def _delta_chunk_packed(q, k, v, gb, state):
    nh, c = q.shape[0], q.shape[1]
    lane = lax.broadcasted_iota(jnp.int32, gb.shape, 1)
    eye = (lax.broadcasted_iota(jnp.int32, (c, c), 0) == lax.broadcasted_iota(jnp.int32, (c, c), 1))
    column = lambda l: jnp.sum(jnp.where(lane == l, gb, 0.0), axis=1, keepdims=True)
    heads = lambda parts: jnp.concatenate([t[None] for t in parts], axis=0)
    bcol = heads([column(h) for h in range(nh)])
    gcols = [column(nh + h) for h in range(nh)]
    gcol = heads(gcols)
    grow = heads([jnp.sum(jnp.where(eye, g, 0.0), axis=0, keepdims=True) for g in gcols])
    return _delta_chunk(q, k, v, gcol, grow, bcol, state)


def _delta_specs(nh, s, dh, rev):
    c, n = DN_CHUNK, s // DN_CHUNK
    t = (lambda i: n - 1 - i) if rev else (lambda i: i)
    seq = pl.BlockSpec((nh, c, dh), lambda i: (0, t(i), 0))
    gate = pl.BlockSpec((c, LANES), lambda i: (t(i), 0))
    st = pl.BlockSpec((nh, 1, dh, dh), lambda i: (0, t(i), 0, 0))
    return seq, gate, st


def delta_fwd(name, q, k, v, gb, gather=None):
    nh, s, dh = q.shape
    n = s // DN_CHUNK
    seq, gate, st = _delta_specs(nh, s, dh, False)
    extra = list(gather or [])
    ne = len(extra)

    def body(*refs):
        q_ref, k_ref, v_ref, gb_ref = refs[:4]
        x_refs = refs[4:4 + ne]
        o_ref, st_ref = refs[4 + ne:6 + ne]
        out_refs = refs[6 + ne:6 + 2 * ne]
        state = refs[6 + 2 * ne]
        i = pl.program_id(0)
        if ne:
            begin, finish = _gather_phases(x_refs, out_refs, refs[7 + 2 * ne:])
            pl.when(i == 0)(begin)

        @pl.when(i == 0)
        def _():
            state[...] = jnp.zeros_like(state)

        st_ref[:, 0] = state[...]
        o, new_state = _delta_chunk_packed(q_ref[...], k_ref[...], v_ref[...], gb_ref[...], state[...])
        o_ref[...] = o
        state[...] = new_state
        if ne:
            pl.when(i == n - 1)(finish)

    any_spec = pl.BlockSpec(memory_space=pl.ANY)
    out = pl.pallas_call(
        body, name=name + "_fwd", grid=(n,),
        in_specs=[seq, seq, seq, gate] + [any_spec] * ne,
        out_specs=[seq, st] + [any_spec] * ne,
        out_shape=[jax.ShapeDtypeStruct((nh, s, dh), F32),
                   jax.ShapeDtypeStruct((nh, n, dh, dh), F32)]
        + [jax.ShapeDtypeStruct((N_DEV,) + t.shape, t.dtype) for t in extra],
        scratch_shapes=[pltpu.VMEM((nh, dh, dh), F32)]
        + ([pltpu.SemaphoreType.DMA((7 * ne,)), pltpu.SemaphoreType.DMA((7 * ne,)),
            pltpu.SemaphoreType.DMA((ne,))] if ne else []),
        compiler_params=_cparams(1),
    )(q, k, v, gb, *extra)
    return out[0], out[1], list(out[2:])


def delta_bwd(name, q, k, v, gb, states, do, exchange=None):
    nh, s, dh = q.shape
    n = s // DN_CHUNK
    seq, gate, st = _delta_specs(nh, s, dh, True)
    extra = list(exchange or [])
    ne = len(extra)

    def body(*refs):
        q_ref, k_ref, v_ref, gb_ref, st_ref, do_ref = refs[:6]
        p_refs = refs[6:6 + ne]
        dq_ref, dk_ref, dv_ref, dgb_ref = refs[6 + ne:10 + ne]
        land_refs = refs[10 + ne:10 + 2 * ne]
        dstate = refs[10 + 2 * ne]
        i = pl.program_id(0)
        if ne:
            begin, finish = _chip_exchange_phases(p_refs, land_refs, refs[11 + 2 * ne:])
            pl.when(i == 0)(begin)

        @pl.when(i == 0)
        def _():
            dstate[...] = jnp.zeros_like(dstate)

        _, vjp = jax.vjp(_delta_chunk_packed, q_ref[...], k_ref[...], v_ref[...], gb_ref[...],
                         st_ref[:, 0])
        dq, dk, dv, dgb, dst = vjp((do_ref[...], dstate[...]))
        dq_ref[...] = dq
        dk_ref[...] = dk
        dv_ref[...] = dv
        dgb_ref[...] = dgb
        dstate[...] = dst
        if ne:
            pl.when(i == n - 1)(finish)

    any_spec = pl.BlockSpec(memory_space=pl.ANY)
    out = pl.pallas_call(
        body, name=name + "_bwd", grid=(n,),
        in_specs=[seq, seq, seq, gate, st, seq] + [any_spec] * ne,
        out_specs=[seq, seq, seq, gate] + [any_spec] * ne,
        out_shape=[jax.ShapeDtypeStruct((nh, s, dh), F32)] * 3
        + [jax.ShapeDtypeStruct((s, LANES), F32)]
        + [jax.ShapeDtypeStruct(t.shape, t.dtype) for t in extra],
        scratch_shapes=[pltpu.VMEM((nh, dh, dh), F32)]
        + ([pltpu.SemaphoreType.DMA((3 * ne,)), pltpu.SemaphoreType.DMA((3 * ne,)),
            pltpu.SemaphoreType.DMA((ne,))] if ne else []),
        compiler_params=_cparams(1),
    )(q, k, v, gb, states, do, *extra)
    return tuple(out[:4]), list(out[4:])


def delta_rule(name, q, k, v, gb):
    @jax.custom_vjp
    def op(q, k, v, gb):
        return delta_fwd(name, q, k, v, gb)[0]

    def op_fwd(q, k, v, gb):
        o, states, _ = delta_fwd(name, q, k, v, gb)
        return o, (q, k, v, gb, states)

    def op_bwd(res, do):
        return delta_bwd(name, *res, do)[0]

    op.defvjp(op_fwd, op_bwd)
    return op(q, k, v, gb)


def _my_place():
    return lax.axis_index("x"), lax.axis_index("y"), lax.axis_index("c")


def all_gather(name, shard):
    return all_gather_many(name, [shard])[0]


def _gather_phases(x_refs, out_refs, sems):
    n = len(x_refs)
    send_sems, recv_sems, local_sems = sems
    x, y, c = _my_place()
    me, sibling = (x, y, c), (x, y, 1 - c)
    chips = [(1 - x, y), (x, 1 - y), (1 - x, 1 - y)]

    def copy(i, k, block, to, own=False):
        px, py, pc = block
        slot = out_refs[i].at[4 * px + 2 * py + pc]
        return pltpu.make_async_remote_copy(
            src_ref=x_refs[i] if own else slot, dst_ref=slot,
            send_sem=send_sems.at[7 * i + k], recv_sem=recv_sems.at[7 * i + k],
            device_id=to, device_id_type=MESH)

    mine = [pltpu.make_async_copy(x_refs[i], out_refs[i].at[4 * x + 2 * y + c], local_sems.at[i])
            for i in range(n)]
    first = []
    for i in range(n):
        first.append(copy(i, 0, me, sibling, own=True))
        first += [copy(i, 1 + j, me, (*chip, c), own=True) for j, chip in enumerate(chips)]

    def begin():
        for cp in mine + first:
            cp.start()

    def finish():
        passed = []
        for j, chip in enumerate(chips):
            for i in range(n):
                copy(i, 1 + j, (*chip, c), me).wait_recv()
                passed.append(copy(i, 4 + j, (*chip, c), sibling))
                passed[-1].start()
        for i in range(n):
            copy(i, 0, sibling, me).wait_recv()
        for j, chip in enumerate(chips):
            for i in range(n):
                copy(i, 4 + j, (*chip, 1 - c), me).wait_recv()
        for cp in first + passed:
            cp.wait_send()
        for cp in mine:
            cp.wait()

    return begin, finish


def all_gather_many(name, shards):
    n = len(shards)

    def body(*refs):
        begin, finish = _gather_phases(refs[:n], refs[n:2 * n], refs[2 * n:])
        begin()
        finish()

    any_spec = pl.BlockSpec(memory_space=pl.ANY)
    return pl.pallas_call(
        body, name=name,
        out_shape=[jax.ShapeDtypeStruct((N_DEV,) + t.shape, t.dtype) for t in shards],
        in_specs=[any_spec] * n, out_specs=[any_spec] * n,
        scratch_shapes=[pltpu.SemaphoreType.DMA((7 * n,)), pltpu.SemaphoreType.DMA((7 * n,)),
                        pltpu.SemaphoreType.DMA((n,))],
    )(*shards)


def sibling_exchange(name, parts):
    n = len(parts)

    def body(*refs):
        p_refs, out_refs = refs[:n], refs[n:2 * n]
        send_sems, recv_sems = refs[2 * n:]
        x, y, c = _my_place()
        copies = []
        for i in range(n):
            for chip in range(4):
                copies.append(pltpu.make_async_remote_copy(
                    src_ref=p_refs[i].at[2 * chip + (1 - c)], dst_ref=out_refs[i].at[chip],
                    send_sem=send_sems.at[4 * i + chip], recv_sem=recv_sems.at[4 * i + chip],
                    device_id=(x, y, 1 - c), device_id_type=MESH))
        for cp in copies:
            cp.start()
        for cp in copies:
            cp.wait_recv()
        for cp in copies:
            cp.wait_send()

    any_spec = pl.BlockSpec(memory_space=pl.ANY)
    return pl.pallas_call(
        body, name=name,
        out_shape=[jax.ShapeDtypeStruct((4,) + t.shape[1:], t.dtype) for t in parts],
        in_specs=[any_spec] * n, out_specs=[any_spec] * n,
        scratch_shapes=[pltpu.SemaphoreType.DMA((4 * n,)), pltpu.SemaphoreType.DMA((4 * n,))],
    )(*parts)


def _chip_exchange_phases(p_refs, out_refs, sems):
    n = len(p_refs)
    send_sems, recv_sems, local_sems = sems
    x, y, c = _my_place()
    me = 2 * x + y
    peers = [(1 - x, y), (x, 1 - y), (1 - x, 1 - y)]

    def copy(i, j, landing):
        px, py = peers[j]
        return pltpu.make_async_remote_copy(
            src_ref=p_refs[i].at[2 * px + py],
            dst_ref=out_refs[i].at[(2 * px + py) if landing else me],
            send_sem=send_sems.at[3 * i + j], recv_sem=recv_sems.at[3 * i + j],
            device_id=(px, py, c), device_id_type=MESH)

    mine = [pltpu.make_async_copy(p_refs[i].at[me], out_refs[i].at[me], local_sems.at[i])
            for i in range(n)]
    copies = [copy(i, j, False) for j in range(3) for i in range(n)]

    def begin():
        for cp in mine + copies:
            cp.start()

    def finish():
        for j in range(3):
            for i in range(n):
                copy(i, j, True).wait_recv()
        for cp in copies:
            cp.wait_send()
        for cp in mine:
            cp.wait()

    return begin, finish


def chip_exchange(name, parts):
    n = len(parts)

    def body(*refs):
        begin, finish = _chip_exchange_phases(refs[:n], refs[n:2 * n], refs[2 * n:])
        begin()
        finish()

    any_spec = pl.BlockSpec(memory_space=pl.ANY)
    return pl.pallas_call(
        body, name=name,
        out_shape=[jax.ShapeDtypeStruct(t.shape, t.dtype) for t in parts],
        in_specs=[any_spec] * n, out_specs=[any_spec] * n,
        scratch_shapes=[pltpu.SemaphoreType.DMA((3 * n,)), pltpu.SemaphoreType.DMA((3 * n,)),
                        pltpu.SemaphoreType.DMA((n,))],
    )(*parts)


def pair_sum(name, mine, got, core):
    _, rows, cols = got.shape
    row_bytes = 4 * LANES * (-(-cols // LANES))
    tile = _pick(rows, [t for t in (512, 352, 256, 128, 64, 32, 16, 8)
                        if t * row_bytes <= ADAMW_BLOCK_BYTES])

    def body(core_ref, a_ref, b_ref, o_ref):
        o_ref[...] = (a_ref[...].astype(F32) + b_ref[...].astype(F32)).astype(o_ref.dtype)

    spec = pl.BlockSpec((4, tile, cols), lambda i, core_ref: (0, i, 0))
    return pl.pallas_call(
        body, name=name,
        grid_spec=pltpu.PrefetchScalarGridSpec(
            num_scalar_prefetch=1, grid=(rows // tile,),
            in_specs=[pl.BlockSpec((4, None, tile, cols), lambda i, core_ref: (0, core_ref[0], i, 0)),
                      spec],
            out_specs=spec),
        out_shape=jax.ShapeDtypeStruct(got.shape, got.dtype), compiler_params=_cparams(1),
    )(core, mine, got)


ADAMW_BLOCK_BYTES = 3 * 512 * 1024


def adamw(name, grad, w, m, v):
    rows, cols = w.shape
    stacked = grad.ndim == 3
    row_bytes = 4 * LANES * (-(-cols // LANES))
    tile = _pick(rows, [t for t in (512, 352, 256, 128, 64, 32, 16, 8)
                        if t * row_bytes <= ADAMW_BLOCK_BYTES])

    def body(g_ref, w_ref, m_ref, v_ref, go_ref, d_ref, mo_ref, vo_ref):
        if stacked:
            g = g_ref[0].astype(F32)
            for s in range(1, grad.shape[0]):
                g = g + g_ref[s].astype(F32)
        else:
            g = g_ref[...]
        m = ADAM_B1 * m_ref[...] + (1.0 - ADAM_B1) * g
        v = ADAM_B2 * v_ref[...] + (1.0 - ADAM_B2) * jnp.square(g)
        m_hat = m / (1.0 - ADAM_B1 ** ADAM_STEP)
        v_hat = v / (1.0 - ADAM_B2 ** ADAM_STEP)
        go_ref[...] = g
        d_ref[...] = -ADAM_LR * (m_hat / (jnp.sqrt(v_hat) + ADAM_EPS) + ADAM_WD * w_ref[...])
        mo_ref[...] = m
        vo_ref[...] = v

    flat = pl.BlockSpec((tile, cols), lambda i: (i, 0))
    g_spec = pl.BlockSpec((grad.shape[0], tile, cols), lambda i: (0, i, 0)) if stacked else flat
    return pl.pallas_call(
        body, name=name, grid=(rows // tile,),
        in_specs=[g_spec, flat, flat, flat], out_specs=[flat] * 4,
        out_shape=[jax.ShapeDtypeStruct((rows, cols), F32)] * 4,
        compiler_params=_cparams(1),
    )(grad, w, m, v)


def silu_rows(name, x):
    def body(x_ref, o_ref):
        o_ref[...] = _silu(x_ref[...])

    return pl.pallas_call(body, name=name, out_shape=jax.ShapeDtypeStruct(x.shape, F32))(x)


BIG = (("ffn1_wg", (D, D_FF // N_DEV), 1), ("ffn1_wu", (D, D_FF // N_DEV), 1),
       ("ffn1_wd", (D_FF // N_DEV, D), 0), ("w_in", (D, IN_COLS // N_DEV), 1),
       ("conv_w", (DN_CONV, 3 * D // N_DEV), 1), ("w_a", (D // N_DEV, D), 0),
       ("w_b", (DA_HEADS * DA_DIM, D // N_DEV), 1), ("w_o", (D // N_DEV, D), 0),
       ("ffn2_wg", (D, D_FF // N_DEV), 1), ("ffn2_wu", (D, D_FF // N_DEV), 1),
       ("ffn2_wd", (D_FF // N_DEV, D), 0))
SMALL = (("ada_b", (DEPTH, N_ADA * D)), ("ln_ffn1", (DEPTH, D)), ("ln_mix", (DEPTH, D)),
         ("ln_ffn2", (DEPTH, D)), ("a_log", (DEPTH, DN_HEADS)), ("dt_bias", (DEPTH, DN_HEADS)),
         ("dn_norm", (DEPTH, DN_DIM)), ("final_norm", (D,)))
SMALL_ROWS = 32


def _pack(arrays, rows):
    flat = jnp.concatenate([a.reshape(-1) for a in arrays])
    return jnp.pad(flat, (0, rows * D - flat.shape[0])).reshape(rows, D)


def _unpack(buf, shapes):
    flat = buf.reshape(-1)
    out, off = [], 0
    for shp in shapes:
        n = int(np.prod(shp))
        out.append(flat[off:off + n].reshape(shp))
        off += n
    return out


def _full_weights(gathered, entries):
    full = {}
    for (name, (a, b), axis), t in zip(entries, gathered):
        if axis == 1:
            full[name] = t.transpose(1, 0, 2).reshape(a, N_DEV * b)
        else:
            full[name] = t.reshape(N_DEV * a, b)
    return full


_B0, _DQ0, _GA0 = 4096, 4112, 6416
_N_WIDE = 6 * D


def _reorder_in_proj(w):
    pad = jnp.zeros((w.shape[0], IN_COLS_PAD - IN_COLS), w.dtype)
    return jnp.concatenate([w[:, :_B0], w[:, _GA0:], w[:, _DQ0:_GA0], w[:, _B0:_DQ0], pad], axis=1)


def _ffn(tag, h, ln, shift, scale, gate, w_gu, w_d):
    n = norm_mod(tag + "_norm", h, ln, shift, scale)
    a = gate_up_act(tag + "_gu", n, w_gu)
    f = linear(tag + "_down", a, w_d, F32)
    return residual(tag + "_res", h, f, gate, 0.5)


N_PRE = 5


def _mods(mod, l):
    return [mod[l, i * D:(i + 1) * D][None] for i in range(N_ADA)]


def _layer_pre(l, h, gathered, small, mod):
    w = _full_weights(gathered, BIG[:N_PRE])
    tag = f"l{l}"
    sh1, sc1, gt1, sh2, sc2 = _mods(mod, l)[:5]
    w_gu1 = _interleave_gate_up(w["ffn1_wg"], w["ffn1_wu"])
    h = _ffn(tag + "_ffn1", h, small["ln_ffn1"][l][None], sh1, sc1, gt1, w_gu1, w["ffn1_wd"])
    u = norm_mod(tag + "_mixnorm", h, small["ln_mix"][l][None], sh2, sc2)
    w_in = _reorder_in_proj(w["w_in"])
    (q_pre, k_pre, v_pre, z, gate_a, gate_b) = linear_split(
        tag + "_mix_in", u, w_in[:, :_N_WIDE], D, (1,) * 6, (BF16,) * 6)
    (da_q, da_k, da_v) = linear_split(
        tag + "_mix_in_da", u, w_in[:, _N_WIDE:_N_WIDE + 3 * DA_HEADS * DA_DIM], DA_HEADS * DA_DIM,
        (1,) * 3, (BF16,) * 3)
    ab = linear(tag + "_mix_in_ab", u, w_in[:, _N_WIDE + 3 * DA_HEADS * DA_DIM:], F32)
    cw = w["conv_w"].astype(F32).reshape(DN_CONV, 1, 3 * D)
    q, k, v = [conv_heads(f"{tag}_mix_conv_{m}", t, cw[:, :, i * D:(i + 1) * D], m)
               for i, (m, t) in enumerate(zip("qkv", (q_pre, k_pre, v_pre)))]
    pad = lambda t: jnp.pad(t, (DN_HEADS, ab.shape[1] - 2 * DN_HEADS))[None]
    gb = decay_beta(tag + "_mix_decay", ab, pad(small["a_log"][l]), pad(small["dt_bias"][l]))
    return (q, k, v, gb[:, :LANES]), (h, z, da_q, da_k, da_v, gate_a, gate_b)


def _layer_post(l, o, carry, gathered, small, mod):
    w = _full_weights(gathered, BIG[N_PRE:])
    tag = f"l{l}"
    h, z, da_q, da_k, da_v, gate_a, gate_b = carry
    gt2, sh3, sc3, gt3 = _mods(mod, l)[5:]
    o_a = gated_head_norm(tag + "_mix_gnorm", o, z, small["dn_norm"][l][None])
    y_a = linear(tag + "_mix_wa", o_a, w["w_a"], F32)

    outs, lses = [], []
    for r in DA_DILATIONS:
        o_r, lse_r = dilated_attention(f"{tag}_mix_attn{r}",
                                       *[_sub_order(t, r) for t in (da_q, da_k, da_v)], r)
        outs.append(_seq_order(o_r, r))
        lses.append(_seq_order(lse_r, r))
    o_b = combine_patterns(tag + "_mix_comb", outs, lses)
    y_b = linear(tag + "_mix_wb", o_b, w["w_b"], F32)

    merged = merge_gates(tag + "_mix_merge", gate_a, gate_b, y_a, y_b)
    m = linear(tag + "_mix_wo", merged, w["w_o"], F32)
    h = residual(tag + "_mixres", h, m, gt2, 1.0)
    w_gu2 = _interleave_gate_up(w["ffn2_wg"], w["ffn2_wu"])
    return _ffn(tag + "_ffn2", h, small["ln_ffn2"][l][None], sh3, sc3, gt3, w_gu2, w["ffn2_wd"])


def _layer(l, h, gathered, small, mod):
    (q, k, v, gb), carry = _layer_pre(l, h, gathered[:N_PRE], small, mod)
    o = delta_rule(f"l{l}_mix_delta", q, k, v, gb)
    return _layer_post(l, o, carry, gathered[N_PRE:], small, mod)


def _head(h, small, target):
    rows = loss_rows("loss", h, small["final_norm"][None], target)
    return 0.5 * jnp.sum(rows)


def _local_loss(x, gathered, small, mod, target):
    h = x
    for l in range(DEPTH):
        h = _layer(l, h, gathered[l], small, mod)
    return _head(h, small, target)


def kernel(x, c, ada_w, ada_b, ln_ffn1, ln_mix, ln_ffn2, ffn1_wg, ffn1_wu, ffn1_wd, w_in, conv_w, a_log, dt_bias, dn_norm, w_a, w_b, w_o, ffn2_wg, ffn2_wu, ffn2_wd, final_norm, loss_target, m_ada_w, m_ada_b, m_ln_ffn1, m_ln_mix, m_ln_ffn2, m_ffn1_wg, m_ffn1_wu, m_ffn1_wd, m_w_in, m_conv_w, m_a_log, m_dt_bias, m_dn_norm, m_w_a, m_w_b, m_w_o, m_ffn2_wg, m_ffn2_wu, m_ffn2_wd, m_final_norm, v_ada_w, v_ada_b, v_ln_ffn1, v_ln_mix, v_ln_ffn2, v_ffn1_wg, v_ffn1_wu, v_ffn1_wd, v_w_in, v_conv_w, v_a_log, v_dt_bias, v_dn_norm, v_w_a, v_w_b, v_w_o, v_ffn2_wg, v_ffn2_wu, v_ffn2_wd, v_final_norm):
    args = dict(locals())
    big_names = [n for n, _, _ in BIG]
    small_names = [n for n, _ in SMALL]
    me = 4 * lax.axis_index("x") + 2 * lax.axis_index("y") + lax.axis_index("c")
    cols = N_ADA * D // N_DEV

    c_all = all_gather("gather_c", jnp.pad(silu_rows("silu_c", c), ((0, 7), (0, 0))))[:, 0]
    mod_cols = jnp.stack([matmul(f"ada{l}", c_all, ada_w[l], "nn", F32) for l in range(DEPTH)])
    mod_cols = mod_cols + lax.dynamic_slice_in_dim(ada_b, me * cols, cols, axis=1)[:, None, :]
    mod_all = all_gather("gather_mod", mod_cols.reshape(DEPTH * N_DEV, cols))
    mod_all = mod_all.reshape(N_DEV, DEPTH, N_DEV, cols)
    mod = lax.dynamic_index_in_dim(mod_all, me, axis=2, keepdims=False)
    mod = mod.transpose(1, 0, 2).reshape(DEPTH, N_ADA * D)

    shards = [[args[n][l].astype(BF16) for n in big_names] for l in range(DEPTH)]
    pre_names, post_names = big_names[:N_PRE], big_names[N_PRE:]
    small = {n: args[n] for n in small_names if n != "ada_b"}
    target = loss_target[0]

    w0_pre = all_gather_many("gather_w0", shards[0][:N_PRE])
    pre0, vjp_pre0 = jax.vjp(functools.partial(_layer_pre, 0), x[0], w0_pre, small, mod)
    (q0, k0, v0, gb0), carry0 = pre0
    o0, states0, got = delta_fwd("l0_mix_delta", q0, k0, v0, gb0,
                                 gather=shards[0][N_PRE:] + shards[1][:N_PRE])
    w0_post, w1_pre = got[:len(post_names)], got[len(post_names):]
    h1, vjp_post0 = jax.vjp(functools.partial(_layer_post, 0), o0, carry0, w0_post, small, mod)
    pre1, vjp_pre1 = jax.vjp(functools.partial(_layer_pre, 1), h1, w1_pre, small, mod)
    (q1, k1, v1, gb1), carry1 = pre1
    o1, states1, w1_post = delta_fwd("l1_mix_delta", q1, k1, v1, gb1, gather=shards[1][N_PRE:])
    loss, vjp_post1 = jax.vjp(
        lambda o, carry, w, sm, md: _head(_layer_post(1, o, carry, w, sm, md), sm, target),
        o1, carry1, w1_post, small, mod)

    my_core = lax.axis_index("c").astype(jnp.int32).reshape(1)

    def pair_sums(tag, names, grads):
        from_sibling = sibling_exchange(f"pair_grads_{tag}", grads)
        return [pair_sum(f"pair_sum_{tag}_{n}", t.reshape(4, 2, -1, t.shape[-1]), got_n, my_core)
                for n, t, got_n in zip(names, grads, from_sibling)]

    do1, dcarry1, dw1_post, dsmall_a, dmod_a = vjp_post1(jnp.ones((), F32))
    ddelta1, landed1_post = delta_bwd("l1_mix_delta", q1, k1, v1, gb1, states1, do1,
                                      exchange=pair_sums("l1_post", post_names, list(dw1_post)))
    dh1, dw1_pre, dsmall_b, dmod_b = vjp_pre1((ddelta1, dcarry1))
    do0, dcarry0, dw0_post, dsmall_c, dmod_c = vjp_post0(dh1)
    ddelta0, landed_mid = delta_bwd(
        "l0_mix_delta", q0, k0, v0, gb0, states0, do0,
        exchange=pair_sums("mid", pre_names + post_names, list(dw1_pre) + list(dw0_post)))
    dx, dw0_pre, dsmall_d, dmod_d = vjp_pre0((ddelta0, dcarry0))
    landed0_pre = chip_exchange("scatter_grads0", pair_sums("l0_pre", pre_names, list(dw0_pre)))
    landed0 = list(landed0_pre) + list(landed_mid[N_PRE:])
    landed1 = list(landed_mid[:N_PRE]) + list(landed1_post)
    dsmall = {n: dsmall_a[n] + dsmall_b[n] + dsmall_c[n] + dsmall_d[n] for n in dsmall_a}
    dmod = dmod_a + dmod_b + dmod_c + dmod_d

    part = _pack([dmod] + [dsmall[n] for n in small_names[1:]], SMALL_ROWS)
    parts = all_gather("gather_small", part)
    sm_out = adamw("adamw_small", parts, _pack([args[n] for n in small_names], SMALL_ROWS),
                   _pack([args["m_" + n] for n in small_names], SMALL_ROWS),
                   _pack([args["v_" + n] for n in small_names], SMALL_ROWS))

    dmod_all = parts.reshape(N_DEV, -1)[:, :DEPTH * N_ADA * D].reshape(N_DEV, DEPTH, N_ADA * D)
    dmod_mine = lax.dynamic_slice_in_dim(dmod_all, me * cols, cols, axis=2)
    g_ada = jnp.stack([matmul(f"ada{l}_dw", c_all, dmod_mine[:, l], "tn", F32) for l in range(DEPTH)])
    flat2 = lambda t: t.reshape(-1, t.shape[-1])
    ada_out = adamw("adamw_ada_w", flat2(g_ada), flat2(ada_w), flat2(m_ada_w), flat2(v_ada_w))

    big_out = {}
    for n, t0, t1 in zip(big_names, landed0, landed1):
        both = jnp.concatenate([t0, t1], axis=1)
        big_out[n] = adamw("adamw_" + n, both, flat2(args[n]), flat2(args["m_" + n]), flat2(args["v_" + n]))

    small_shapes = [shp for _, shp in SMALL]
    names = ["ada_w", "ada_b", "ln_ffn1", "ln_mix", "ln_ffn2", "ffn1_wg", "ffn1_wu", "ffn1_wd", "w_in",
             "conv_w", "a_log", "dt_bias", "dn_norm", "w_a", "w_b", "w_o", "ffn2_wg", "ffn2_wu",
             "ffn2_wd", "final_norm"]
    outs = [lax.psum(loss, ("x", "y", "c")), dx[None]]
    for kind in range(4):
        table = {n: big_out[n][kind].reshape(args[n].shape) for n in big_names}
        table.update(zip(small_names, _unpack(sm_out[kind], small_shapes)))
        table["ada_w"] = ada_out[kind].reshape(ada_w.shape)
        outs += [table[n] for n in names]
    return tuple(outs)
```

```python
import functools
import math

import numpy as np
import jax
import jax.numpy as jnp
from jax import lax
from jax.experimental import pallas as pl
from jax.experimental.pallas import tpu as pltpu

F32 = jnp.float32
BF16 = jnp.bfloat16

D = 1024
SEQ = 4096
DEPTH = 2
N_DEV = 8
DN_HEADS = 8
DN_DIM = 128
DN_CHUNK = 64
DN_CONV = 4
DA_HEADS = 12
DA_DIM = 64
DA_BLOCK = 128
DA_DILATIONS = (1, 4, 16)
ALIBI_MAX_EXP = 8.0
D_FF = 2816
N_ADA = 9
NORM_EPS = 1e-6
IN_COLS = 8464
IN_COLS_PAD = 8704
ADAM_LR, ADAM_B1, ADAM_B2, ADAM_EPS, ADAM_WD, ADAM_STEP = 0.001, 0.9, 0.999, 1e-08, 0.01, 10
NEG = -1e30

VMEM_LIMIT = 56 * 1024 * 1024
LANES = 128

MESH = pl.DeviceIdType.MESH


def _cparams(n_grid):
    return pltpu.CompilerParams(dimension_semantics=("arbitrary",) * n_grid,
                                vmem_limit_bytes=VMEM_LIMIT)


def blockwise(name, f, grid, ins, outs):
    n_in, n_out = len(ins), len(outs)
    diff = [i for i, (_, _, kind) in enumerate(ins) if kind != "const"]

    def apply(*vals):
        res = f(*vals)
        return tuple(r.astype(dt) for r, (_, dt, _, _) in zip(res, outs))

    def fwd_call(*arrays):
        def body(*refs):
            res = apply(*[r[...] for r in refs[:n_in]])
            for r, v in zip(refs[n_in:], res):
                r[...] = v

        return pl.pallas_call(
            body, name=name + "_fwd", grid=grid,
            in_specs=[pl.BlockSpec(b, im) for (b, im, _) in ins],
            out_specs=[pl.BlockSpec(b, im) for (_, _, b, im) in outs],
            out_shape=[jax.ShapeDtypeStruct(s, dt) for (s, dt, _, _) in outs],
            compiler_params=_cparams(len(grid)),
        )(*arrays)

    def bwd_call(arrays, cts):
        def body(*refs):
            in_refs, ct_refs = refs[:n_in], refs[n_in:n_in + n_out]
            g_refs = refs[n_in + n_out:]
            vals = [r[...] for r in in_refs]

            def fd(*dvals):
                full = list(vals)
                for i, v in zip(diff, dvals):
                    full[i] = v
                return apply(*full)

            _, vjp = jax.vjp(fd, *[vals[i] for i in diff])
            grads = vjp(tuple(r[...] for r in ct_refs))
            first = functools.reduce(jnp.logical_and,
                                     [pl.program_id(a) == 0 for a in range(len(grid))])
            for g_ref, g, i in zip(g_refs, grads, diff):
                if ins[i][2] == "acc":
                    @pl.when(first)
                    def _(g_ref=g_ref):
                        g_ref[...] = jnp.zeros_like(g_ref)
                    g_ref[...] += g.astype(F32)
                else:
                    g_ref[...] = g.astype(g_ref.dtype)

        g_shapes = [jax.ShapeDtypeStruct(arrays[i].shape,
                                         F32 if ins[i][2] == "acc" else arrays[i].dtype)
                    for i in diff]
        return pl.pallas_call(
            body, name=name + "_bwd", grid=grid,
            in_specs=([pl.BlockSpec(b, im) for (b, im, _) in ins]
                      + [pl.BlockSpec(b, im) for (_, _, b, im) in outs]),
            out_specs=[pl.BlockSpec(ins[i][0], ins[i][1]) for i in diff],
            out_shape=g_shapes,
            compiler_params=_cparams(len(grid)),
        )(*arrays, *cts)

    @jax.custom_vjp
    def op(*arrays):
        return tuple(fwd_call(*arrays))

    def op_fwd(*arrays):
        return tuple(fwd_call(*arrays)), arrays

    def op_bwd(arrays, cts):
        grads = bwd_call(arrays, cts)
        full = [None] * n_in
        for i, g in zip(diff, grads):
            full[i] = g.astype(arrays[i].dtype)
        return tuple(full)

    op.defvjp(op_fwd, op_bwd)
    op.backward = bwd_call
    return op


def _pick(n, cands):
    for c in cands:
        if n % c == 0:
            return c
    return n


def matmul(name, a, b, form, out_dtype):
    if form == "nn":
        (m, k), (_, n) = a.shape, b.shape
    elif form == "nt":
        (m, k), (n, _) = a.shape, b.shape
    else:
        (k, m), (_, n) = a.shape, b.shape
    tm = _pick(m, (1408, 1024, 512, 256, 128, 8))
    tn = _pick(n, (1408, 1024, 512, 384, 256, 128))
    tk = _pick(k, (1024, 1408, 768, 512, 384, 256, 128, 8))
    nk = k // tk
    a_spec = (pl.BlockSpec((tk, tm), lambda i, j, kk: (kk, i)) if form == "tn"
              else pl.BlockSpec((tm, tk), lambda i, j, kk: (i, kk)))
    b_spec = (pl.BlockSpec((tn, tk), lambda i, j, kk: (j, kk)) if form == "nt"
              else pl.BlockSpec((tk, tn), lambda i, j, kk: (kk, j)))
    dims = {"nn": (((1,), (0,)), ((), ())), "nt": (((1,), (1,)), ((), ())),
            "tn": (((0,), (0,)), ((), ()))}[form]

    def body(a_ref, b_ref, o_ref, acc_ref):
        kk = pl.program_id(2)
        part = lax.dot_general(a_ref[...].astype(BF16), b_ref[...].astype(BF16), dims,
                               preferred_element_type=F32)
        if nk == 1:
            o_ref[...] = part.astype(o_ref.dtype)
            return

        @pl.when(kk == 0)
        def _():
            acc_ref[...] = part

        @pl.when(kk > 0)
        def _():
            acc_ref[...] += part

        @pl.when(kk == nk - 1)
        def _():
            o_ref[...] = acc_ref[...].astype(o_ref.dtype)

    return pl.pallas_call(
        body, name=name, grid=(m // tm, n // tn, nk),
        in_specs=[a_spec, b_spec],
        out_specs=pl.BlockSpec((tm, tn), lambda i, j, kk: (i, j)),
        out_shape=jax.ShapeDtypeStruct((m, n), out_dtype),
        scratch_shapes=[pltpu.VMEM((tm, tn), F32)],
        compiler_params=_cparams(3),
    )(a, b)


def linear(name, x, w, out_dtype):
    @jax.custom_vjp
    def op(x, w):
        return matmul(name + "_y", x, w, "nn", out_dtype)

    def op_fwd(x, w):
        return op(x, w), (x, w)

    def op_bwd(res, dy):
        x, w = res
        dx = matmul(name + "_dx", dy, w, "nt", x.dtype)
        dw = matmul(name + "_dw", x, dy, "tn", w.dtype)
        return dx, dw

    op.defvjp(op_fwd, op_bwd)
    return op(x, w)


def linear_split(name, x, w, tile, seg_tiles, out_dtypes):
    SPLIT_TILE = tile
    m, k = x.shape
    n = w.shape[1]
    nt = n // SPLIT_TILE
    starts = [sum(seg_tiles[:s]) for s in range(len(seg_tiles))]
    assert sum(seg_tiles) == nt
    tm = _pick(m, (1024, 512, 256, 128, 8))

    def inside(j, s):
        return jnp.logical_and(j >= starts[s], j < starts[s] + seg_tiles[s])

    def local(j, s):
        return jnp.clip(j - starts[s], 0, seg_tiles[s] - 1)

    def fwd_call(x, w):
        def body(x_ref, w_ref, *o_refs):
            j = pl.program_id(1)
            y = jnp.dot(x_ref[...].astype(BF16), w_ref[...].astype(BF16), preferred_element_type=F32)
            for s, o_ref in enumerate(o_refs):
                @pl.when(inside(j, s))
                def _(o_ref=o_ref):
                    o_ref[...] = y.astype(o_ref.dtype)

        return pl.pallas_call(
            body, name=name + "_y", grid=(m // tm, nt),
            in_specs=[pl.BlockSpec((tm, k), lambda i, j: (i, 0)),
                      pl.BlockSpec((k, SPLIT_TILE), lambda i, j: (0, j))],
            out_specs=[pl.BlockSpec((tm, SPLIT_TILE), lambda i, j, s=s: (i, local(j, s)))
                       for s in range(len(seg_tiles))],
            out_shape=[jax.ShapeDtypeStruct((m, t * SPLIT_TILE), dt)
                       for t, dt in zip(seg_tiles, out_dtypes)],
            compiler_params=_cparams(2),
        )(x, w)

    def dx_call(dys, w):
        def body(*refs):
            dy_refs, w_ref, o_ref, acc_ref = refs[:-3], refs[-3], refs[-2], refs[-1]
            j = pl.program_id(1)
            for s, dy_ref in enumerate(dy_refs):
                @pl.when(inside(j, s))
                def _(dy_ref=dy_ref):
                    part = lax.dot_general(dy_ref[...].astype(BF16), w_ref[...].astype(BF16),
                                           (((1,), (1,)), ((), ())), preferred_element_type=F32)

                    @pl.when(j == 0)
                    def _():
                        acc_ref[...] = part

                    @pl.when(j > 0)
                    def _():
                        acc_ref[...] += part

            @pl.when(j == nt - 1)
            def _():
                o_ref[...] = acc_ref[...].astype(o_ref.dtype)

        return pl.pallas_call(
            body, name=name + "_dx", grid=(m // tm, nt),
            in_specs=[pl.BlockSpec((tm, SPLIT_TILE), lambda i, j, s=s: (i, local(j, s)))
                      for s in range(len(seg_tiles))]
            + [pl.BlockSpec((k, SPLIT_TILE), lambda i, j: (0, j))],
            out_specs=pl.BlockSpec((tm, k), lambda i, j: (i, 0)),
            out_shape=jax.ShapeDtypeStruct((m, k), x.dtype),
            scratch_shapes=[pltpu.VMEM((tm, k), F32)],
            compiler_params=_cparams(2),
        )(*dys, w)

    def dw_call(x, dys):
        nm = m // tm

        def body(*refs):
            x_ref, dy_refs, o_ref, acc_ref = refs[0], refs[1:-2], refs[-2], refs[-1]
            j, kk = pl.program_id(0), pl.program_id(1)
            for s, dy_ref in enumerate(dy_refs):
                @pl.when(inside(j, s))
                def _(dy_ref=dy_ref):
                    rows = x_ref[pl.ds(pl.multiple_of(kk * tm, tm), tm), :]
                    part = lax.dot_general(rows.astype(BF16), dy_ref[...].astype(BF16),
                                           (((0,), (0,)), ((), ())), preferred_element_type=F32)

                    @pl.when(kk == 0)
                    def _():
                        acc_ref[...] = part

                    @pl.when(kk > 0)
                    def _():
                        acc_ref[...] += part

            @pl.when(kk == nm - 1)
            def _():
                o_ref[...] = acc_ref[...].astype(o_ref.dtype)

        return pl.pallas_call(
            body, name=name + "_dw", grid=(nt, nm),
            in_specs=[pl.BlockSpec((m, k), lambda j, kk: (0, 0))]
            + [pl.BlockSpec((tm, SPLIT_TILE),
                            lambda j, kk, s=s: (jnp.where(inside(j, s), kk, 0), local(j, s)))
               for s in range(len(seg_tiles))],
            out_specs=pl.BlockSpec((k, SPLIT_TILE), lambda j, kk: (0, j)),
            out_shape=jax.ShapeDtypeStruct((k, n), w.dtype),
            scratch_shapes=[pltpu.VMEM((k, SPLIT_TILE), F32)],
            compiler_params=_cparams(2),
        )(x, *dys)

    @jax.custom_vjp
    def op(x, w):
        return tuple(fwd_call(x, w))

    def op_fwd(x, w):
        return op(x, w), (x, w)

    def op_bwd(res, dys):
        x, w = res
        return dx_call(dys, w), dw_call(x, dys)

    op.defvjp(op_fwd, op_bwd)
    return op(x, w)


@jax.custom_vjp
def _sigmoid(x):
    return 1.0 / (1.0 + jnp.exp(-x))


def _sigmoid_fwd(x):
    s = _sigmoid(x)
    return s, s


def _sigmoid_bwd(s, ct):
    return (ct * (s * (1.0 - s)),)


_sigmoid.defvjp(_sigmoid_fwd, _sigmoid_bwd)


@jax.custom_vjp
def _silu(x):
    return x * _sigmoid(x)


def _silu_fwd(x):
    s = _sigmoid(x)
    return x * s, (x, s)


def _silu_bwd(res, ct):
    x, s = res
    return (ct * (s * (1.0 + x * (1.0 - s))),)


_silu.defvjp(_silu_fwd, _silu_bwd)


def _softplus(x):
    return jnp.maximum(x, 0.0) + jnp.log(1.0 + jnp.exp(-jnp.abs(x)))


def _rms(x):
    return x * lax.rsqrt(jnp.mean(x * x, axis=-1, keepdims=True) + NORM_EPS)


ROW_TILE = 512


def _row(i):
    return (i, 0)


def _fixed(*_):
    return (0, 0)


def norm_mod(name, h, ln, shift, scale):
    s, d = h.shape

    def f(h, ln, sh, sc):
        return ((_rms(h) * ln) * (1.0 + sc) + sh,)

    op = blockwise(name, f, (s // ROW_TILE,),
                   [((ROW_TILE, d), _row, "tile")] + [((1, d), _fixed, "acc")] * 3,
                   [((s, d), BF16, (ROW_TILE, d), _row)])
    return op(h, ln, shift, scale)[0]


FF_TILE = 1408


def _interleave_gate_up(wg, wu):
    parts = []
    for j in range(wg.shape[1] // FF_TILE):
        parts += [wg[:, j * FF_TILE:(j + 1) * FF_TILE], wu[:, j * FF_TILE:(j + 1) * FF_TILE]]
    return jnp.concatenate(parts, axis=1)


def _swiglu(gu):
    g, u = gu[:, :FF_TILE].astype(F32), gu[:, FF_TILE:].astype(F32)
    return _silu(g) * u


def _swiglu_op(name, s, ff):
    return blockwise(name, lambda gu: (_swiglu(gu),), (s // ROW_TILE, ff // FF_TILE),
                     [((ROW_TILE, 2 * FF_TILE), lambda i, j: (i, j), "tile")],
                     [((s, ff), BF16, (ROW_TILE, FF_TILE), lambda i, j: (i, j))])


def gate_up_act(name, x, w):
    s, k = x.shape
    ff = w.shape[1] // 2
    tm = _pick(s, (512, 256, 128, 8))

    def fwd_call(x, w):
        def body(x_ref, w_ref, gu_ref, a_ref):
            gu = jnp.dot(x_ref[...], w_ref[...], preferred_element_type=F32).astype(BF16)
            gu_ref[...] = gu
            a_ref[...] = _swiglu(gu).astype(BF16)

        return pl.pallas_call(
            body, name=name + "_y", grid=(ff // FF_TILE, s // tm),
            in_specs=[pl.BlockSpec((tm, k), lambda j, i: (i, 0)),
                      pl.BlockSpec((k, 2 * FF_TILE), lambda j, i: (0, j))],
            out_specs=[pl.BlockSpec((tm, 2 * FF_TILE), lambda j, i: (i, j)),
                       pl.BlockSpec((tm, FF_TILE), lambda j, i: (i, j))],
            out_shape=[jax.ShapeDtypeStruct((s, 2 * ff), BF16), jax.ShapeDtypeStruct((s, ff), BF16)],
            compiler_params=_cparams(2),
        )(x, w)

    @jax.custom_vjp
    def op(x, w):
        return fwd_call(x, w)[1]

    def op_fwd(x, w):
        gu, a = fwd_call(x, w)
        return a, (x, w, gu)

    def op_bwd(res, da):
        x, w, gu = res
        dgu = _swiglu_op(name + "_act", s, ff).backward((gu,), (da,))[0]
        return (matmul(name + "_dx", dgu, w, "nt", x.dtype), matmul(name + "_dw", x, dgu, "tn", w.dtype))

    op.defvjp(op_fwd, op_bwd)
    return op(x, w)


def residual(name, h, y, gate, weight):
    s, d = h.shape

    def f(h, y, gate):
        return (h + (weight * gate) * y,)

    op = blockwise(name, f, (s // ROW_TILE,),
                   [((ROW_TILE, d), _row, "tile"), ((ROW_TILE, d), _row, "tile"),
                    ((1, d), _fixed, "acc")],
                   [((s, d), F32, (ROW_TILE, d), _row)])
    return op(h, y, gate)[0]


def merge_gates(name, ga, gb, ya, yb):
    s, d = ya.shape

    def f(ga, gb, ya, yb):
        return (_sigmoid(ga.astype(F32)) * ya + _sigmoid(gb.astype(F32)) * yb,)

    op = blockwise(name, f, (s // ROW_TILE,), [((ROW_TILE, d), _row, "tile")] * 4,
                   [((s, d), BF16, (ROW_TILE, d), _row)])
    return op(ga, gb, ya, yb)[0]


def loss_rows(name, h, g, target):
    s, d = h.shape

    def f(h, g, t):
        err = _rms(h) * g - t
        return (jnp.mean(err * err, axis=-1, keepdims=True),)

    op = blockwise(name, f, (s // ROW_TILE,),
                   [((ROW_TILE, d), _row, "tile"), ((1, d), _fixed, "acc"),
                    ((ROW_TILE, d), _row, "const")],
                   [((s, 1), F32, (ROW_TILE, 1), _row)])
    return op(h, g, target)[0]


def decay_beta(name, ab, a_log_pad, dt_bias_pad):
    s, n = ab.shape

    def f(ab, a_log, dt_bias):
        lane = lax.broadcasted_iota(jnp.int32, ab.shape, 1)
        beta = _sigmoid(ab)
        g = -jnp.exp(a_log) * _softplus(ab + dt_bias)
        return (jnp.where(lane < DN_HEADS, beta, jnp.where(lane < 2 * DN_HEADS, g, 0.0)),)

    op = blockwise(name, f, (s // ROW_TILE,),
                   [((ROW_TILE, n), _row, "tile"), ((1, n), _fixed, "acc"), ((1, n), _fixed, "acc")],
                   [((s, n), F32, (ROW_TILE, n), _row)])
    return op(ab, a_log_pad, dt_bias_pad)[0]


def _shift_rows(x, k):
    n = x.shape[0]

    @jax.custom_vjp
    def shift(x):
        row = lax.broadcasted_iota(jnp.int32, x.shape, 0)
        return jnp.where(row >= k, pltpu.roll(x, k, 0), 0.0)

    def shift_fwd(x):
        return shift(x), None

    def shift_bwd(_, g):
        row = lax.broadcasted_iota(jnp.int32, g.shape, 0)
        return (jnp.where(row < n - k, pltpu.roll(g, n - k, 0), 0.0),)

    shift.defvjp(shift_fwd, shift_bwd)
    return shift(x)


def conv_heads(name, x, w, mode):
    s, width = x.shape
    nh = width // LANES

    def f(x, w):
        x = x.astype(F32)
        y = w[DN_CONV - 1] * x
        for j in range(DN_CONV - 1):
            y = y + w[j] * _shift_rows(x, DN_CONV - 1 - j)
        y = _silu(y)
        if mode != "v":
            y = y * lax.rsqrt(jnp.sum(y * y, axis=-1, keepdims=True) + NORM_EPS)
        if mode == "q":
            y = y * (DN_DIM ** -0.5)
        return (y[None],)

    op = blockwise(name, f, (nh,),
                   [((s, LANES), lambda j: (0, j), "tile"),
                    ((DN_CONV, 1, LANES), lambda j: (0, 0, j), "tile")],
                   [((nh, s, LANES), F32, (1, s, LANES), lambda j: (j, 0, 0))])
    return op(x, w)[0]


def gated_head_norm(name, o, z, w):
    nh, s, dh = o.shape

    def f(o, z, w):
        return (_rms(o[0]) * w * _silu(z.astype(F32)),)

    op = blockwise(name, f, (s // ROW_TILE, nh),
                   [((1, ROW_TILE, dh), lambda i, h: (h, i, 0), "tile"),
                    ((ROW_TILE, dh), lambda i, h: (i, h), "tile"),
                    ((1, dh), lambda i, h: (0, 0), "acc")],
                   [((s, nh * dh), BF16, (ROW_TILE, dh), lambda i, h: (i, h))])
    return op(o, z, w)[0]


def _mm(a, b, ca, cb):
    return _bmm(a[None], b[None], ca + 1, cb + 1)[0]


def dilated_attention(name, q, k, v, dilation):
    s, width = q.shape
    nblk = s // DA_BLOCK
    per_sub = nblk // dilation
    slopes = [dilation * 2.0 ** (-ALIBI_MAX_EXP * (h + 1) / DA_HEADS) for h in range(DA_HEADS)]

    def f(q, kp, kc, vp, vc):
        first = (pl.program_id(0) % per_sub) == 0
        qi = lax.broadcasted_iota(jnp.int32, (DA_BLOCK, 2 * DA_BLOCK), 0)
        ki = lax.broadcasted_iota(jnp.int32, (DA_BLOCK, 2 * DA_BLOCK), 1)
        steps = (qi + DA_BLOCK - ki).astype(F32)
        lowest = qi + first.astype(jnp.int32) * (DA_BLOCK - qi)
        valid = jnp.logical_and(ki >= lowest, ki <= qi + DA_BLOCK)
        top = lax.broadcasted_iota(jnp.int32, (DA_BLOCK, LANES), 1) < DA_DIM
        o_parts, lse_parts = [], []
        for pair in range(width // LANES):
            cols = slice(pair * LANES, (pair + 1) * LANES)
            q2 = jnp.concatenate([jnp.where(top, q[:, cols], 0.0), jnp.where(top, 0.0, q[:, cols])], axis=0)
            k2 = jnp.concatenate([kp[:, cols], kc[:, cols]], axis=0)
            v2 = jnp.concatenate([vp[:, cols], vc[:, cols]], axis=0)
            bias = jnp.concatenate([jnp.where(valid, -slopes[2 * pair + half] * steps, NEG)
                                    for half in range(2)], axis=0)
            sc = _mm(q2, k2, 1, 1) * (DA_DIM ** -0.5) + bias
            mx = jnp.max(sc, axis=-1, keepdims=True)
            p = jnp.exp(sc - mx)
            l = jnp.sum(p, axis=-1, keepdims=True)
            o2 = _mm(p / l, v2, 1, 0)
            lse2 = mx + jnp.log(l)
            o_parts.append(jnp.where(top, o2[:DA_BLOCK], o2[DA_BLOCK:]))
            lse_parts.append(jnp.where(top, lse2[:DA_BLOCK], lse2[DA_BLOCK:]))
        return jnp.concatenate(o_parts, axis=1), jnp.concatenate(lse_parts, axis=1)

    blk = (DA_BLOCK, width)
    cur_map = lambda j: (j, 0)
    prev_map = lambda j: (jnp.maximum(j - 1, 0), 0)
    op = blockwise(
        name, f, (nblk,),
        [(blk, cur_map, "tile"), (blk, prev_map, "tile"), (blk, cur_map, "tile"),
         (blk, prev_map, "tile"), (blk, cur_map, "tile")],
        [((s, width), BF16, blk, cur_map), ((s, width), F32, blk, cur_map)])

    @jax.custom_vjp
    def attn(q, k, v):
        return op(q, k, k, v, v)

    def attn_fwd(q, k, v):
        return attn(q, k, v), (q, k, v)

    def attn_bwd(res, cts):
        q, k, v = res
        _, vjp = jax.vjp(op, q, k, k, v, v)
        dq, dkp, dkc, dvp, dvc = vjp(cts)
        fill = lambda t: t.at[s - DA_BLOCK:].set(0.0)
        return dq, dkc + fill(dkp), dvc + fill(dvp)

    attn.defvjp(attn_fwd, attn_bwd)
    return attn(q, k, v)


def combine_patterns(name, outs, lses):
    s, width = outs[0].shape
    n = len(outs)

    def f(*vals):
        o, lse = vals[:n], vals[n:]
        mx = functools.reduce(jnp.maximum, lse)
        e = [jnp.exp(t - mx) for t in lse]
        return (sum(ei * oi for ei, oi in zip(e, o)) / sum(e),)

    tile = ROW_TILE // 2
    op = blockwise(name, f, (s // tile,), [((tile, width), _row, "tile")] * (2 * n),
                   [((s, width), BF16, (tile, width), _row)])
    return op(*outs, *lses)[0]


def _sub_order(t, r):
    if r == 1:
        return t
    s, c = t.shape
    return t.reshape(s // r, r, c).transpose(1, 0, 2).reshape(s, c)


def _seq_order(t, r):
    if r == 1:
        return t
    s, c = t.shape
    return t.reshape(r, s // r, c).transpose(1, 0, 2).reshape(s, c)


def _raw_bmm(a, b, ca, cb):
    return lax.dot_general(a.astype(BF16), b.astype(BF16), (((ca,), (cb,)), ((0,), (0,))),
                           preferred_element_type=F32)


def _split(a):
    hi = a.astype(BF16)
    return hi, (a - hi.astype(F32)).astype(BF16)


def _passes_bmm(a, b, ca, cb, passes):
    if passes == 1:
        return _raw_bmm(a, b, ca, cb)
    (a_hi, a_lo), (b_hi, b_lo) = _split(a), _split(b)
    return _raw_bmm(a_hi, b_hi, ca, cb) + (_raw_bmm(a_hi, b_lo, ca, cb) + _raw_bmm(a_lo, b_hi, ca, cb))


def _bmm(a, b, ca, cb, passes=1):
    fa, fb = 3 - ca, 3 - cb

    @jax.custom_vjp
    def mm(a, b):
        return _passes_bmm(a, b, ca, cb, passes)

    def mm_fwd(a, b):
        return mm(a, b), (a, b)

    def mm_bwd(res, ct):
        a, b = res
        da = (_passes_bmm(ct, b, 2, fb, passes) if ca == 2
              else _passes_bmm(b, ct, fb, 2, passes))
        db = (_passes_bmm(a, ct, fa, 1, passes) if cb == 1
              else _passes_bmm(ct, a, 1, fa, passes))
        return da.astype(a.dtype), db.astype(b.dtype)

    mm.defvjp(mm_fwd, mm_bwd)
    return mm(a, b)


def _delta_chunk(q, k, v, gcol, grow, bcol, state):
    c = q.shape[1]
    ii = lax.broadcasted_iota(jnp.int32, (1, c, c), 1)
    jj = lax.broadcasted_iota(jnp.int32, (1, c, c), 2)
    incl, strict = ii >= jj, ii > jj
    gc_col = jnp.sum(jnp.where(incl, grow, 0.0), axis=2, keepdims=True)
    gc_row = jnp.sum(jnp.where(ii <= jj, gcol, 0.0), axis=1, keepdims=True)
    decay = jnp.where(incl, jnp.exp(jnp.where(incl, gc_col - gc_row, 0.0)), 0.0)
    kb, vb = k * bcol, v * bcol
    m = jnp.where(strict, _bmm(kb, k, 2, 2) * decay, 0.0)
    eye = (ii == jj).astype(F32)
    p = -m
    inv = eye + p
    for _ in range(int(math.log2(c)) - 1):
        p = _bmm(p, p, 2, 1, 3)
        inv = inv + _bmm(inv, p, 2, 1, 3)
    e_col = jnp.exp(gc_col)
    u = _bmm(inv, vb, 2, 1)
    w = _bmm(inv, kb * e_col, 2, 1)
    qk = _bmm(q, k, 2, 2) * decay
    v_new = u - _bmm(w, state, 2, 1)
    o = _bmm(q * e_col, state, 2, 1) + _bmm(qk, v_new, 2, 1)
    g_last = jnp.sum(grow, axis=2, keepdims=True)
    new_state = state * jnp.exp(g_last) + _bmm(k * jnp.exp(g_last - gc_col), v_new, 1, 1)
    return o, new_state


def _delta_chunk_packed(q, k, v, gb, state):
    nh, c = q.shape[0], q.shape[1]
    lane = lax.broadcasted_iota(jnp.int32, gb.shape, 1)
    eye = (lax.broadcasted_iota(jnp.int32, (c, c), 0) == lax.broadcasted_iota(jnp.int32, (c, c), 1))
    column = lambda l: jnp.sum(jnp.where(lane == l, gb, 0.0), axis=1, keepdims=True)
    heads = lambda parts: jnp.concatenate([t[None] for t in parts], axis=0)
    bcol = heads([column(h) for h in range(nh)])
    gcols = [column(nh + h) for h in range(nh)]
    gcol = heads(gcols)
    grow = heads([jnp.sum(jnp.where(eye, g, 0.0), axis=0, keepdims=True) for g in gcols])
    return _delta_chunk(q, k, v, gcol, grow, bcol, state)


def _delta_specs(nh, s, dh, rev):
    c, n = DN_CHUNK, s // DN_CHUNK
    t = (lambda i: n - 1 - i) if rev else (lambda i: i)
    seq = pl.BlockSpec((nh, c, dh), lambda i: (0, t(i), 0))
    gate = pl.BlockSpec((c, LANES), lambda i: (t(i), 0))
    st = pl.BlockSpec((nh, 1, dh, dh), lambda i: (0, t(i), 0, 0))
    return seq, gate, st


def delta_fwd(name, q, k, v, gb, gather=None):
    nh, s, dh = q.shape
    n = s // DN_CHUNK
    seq, gate, st = _delta_specs(nh, s, dh, False)
    extra = list(gather or [])
    ne = len(extra)

    def body(*refs):
        q_ref, k_ref, v_ref, gb_ref = refs[:4]
        x_refs = refs[4:4 + ne]
        o_ref, st_ref = refs[4 + ne:6 + ne]
        out_refs = refs[6 + ne:6 + 2 * ne]
        state = refs[6 + 2 * ne]
        i = pl.program_id(0)
        if ne:
            begin, finish = _gather_phases(x_refs, out_refs, refs[7 + 2 * ne:])
            pl.when(i == 0)(begin)

        @pl.when(i == 0)
        def _():
            state[...] = jnp.zeros_like(state)

        st_ref[:, 0] = state[...]
        o, new_state = _delta_chunk_packed(q_ref[...], k_ref[...], v_ref[...], gb_ref[...], state[...])
        o_ref[...] = o
        state[...] = new_state
        if ne:
            pl.when(i == n - 1)(finish)

    any_spec = pl.BlockSpec(memory_space=pl.ANY)
    out = pl.pallas_call(
        body, name=name + "_fwd", grid=(n,),
        in_specs=[seq, seq, seq, gate] + [any_spec] * ne,
        out_specs=[seq, st] + [any_spec] * ne,
        out_shape=[jax.ShapeDtypeStruct((nh, s, dh), F32),
                   jax.ShapeDtypeStruct((nh, n, dh, dh), F32)]
        + [jax.ShapeDtypeStruct((N_DEV,) + t.shape, t.dtype) for t in extra],
        scratch_shapes=[pltpu.VMEM((nh, dh, dh), F32)]
        + ([pltpu.SemaphoreType.DMA((7 * ne,)), pltpu.SemaphoreType.DMA((7 * ne,)),
            pltpu.SemaphoreType.DMA((ne,))] if ne else []),
        compiler_params=_cparams(1),
    )(q, k, v, gb, *extra)
    return out[0], out[1], list(out[2:])


def delta_bwd(name, q, k, v, gb, states, do, exchange=None):
    nh, s, dh = q.shape
    n = s // DN_CHUNK
    seq, gate, st = _delta_specs(nh, s, dh, True)
    extra = list(exchange or [])
    ne = len(extra)

    def body(*refs):
        q_ref, k_ref, v_ref, gb_ref, st_ref, do_ref = refs[:6]
        p_refs = refs[6:6 + ne]
        dq_ref, dk_ref, dv_ref, dgb_ref = refs[6 + ne:10 + ne]
        land_refs = refs[10 + ne:10 + 2 * ne]
        dstate = refs[10 + 2 * ne]
        i = pl.program_id(0)
        if ne:
            begin, finish = _chip_exchange_phases(p_refs, land_refs, refs[11 + 2 * ne:])
            pl.when(i == 0)(begin)

        @pl.when(i == 0)
        def _():
            dstate[...] = jnp.zeros_like(dstate)

        _, vjp = jax.vjp(_delta_chunk_packed, q_ref[...], k_ref[...], v_ref[...], gb_ref[...],
                         st_ref[:, 0])
        dq, dk, dv, dgb, dst = vjp((do_ref[...], dstate[...]))
        dq_ref[...] = dq
        dk_ref[...] = dk
        dv_ref[...] = dv
        dgb_ref[...] = dgb
        dstate[...] = dst
        if ne:
            pl.when(i == n - 1)(finish)

    any_spec = pl.BlockSpec(memory_space=pl.ANY)
    out = pl.pallas_call(
        body, name=name + "_bwd", grid=(n,),
        in_specs=[seq, seq, seq, gate, st, seq] + [any_spec] * ne,
        out_specs=[seq, seq, seq, gate] + [any_spec] * ne,
        out_shape=[jax.ShapeDtypeStruct((nh, s, dh), F32)] * 3
        + [jax.ShapeDtypeStruct((s, LANES), F32)]
        + [jax.ShapeDtypeStruct(t.shape, t.dtype) for t in extra],
        scratch_shapes=[pltpu.VMEM((nh, dh, dh), F32)]
        + ([pltpu.SemaphoreType.DMA((3 * ne,)), pltpu.SemaphoreType.DMA((3 * ne,)),
            pltpu.SemaphoreType.DMA((ne,))] if ne else []),
        compiler_params=_cparams(1),
    )(q, k, v, gb, states, do, *extra)
    return tuple(out[:4]), list(out[4:])


def delta_rule(name, q, k, v, gb):
    @jax.custom_vjp
    def op(q, k, v, gb):
        return delta_fwd(name, q, k, v, gb)[0]

    def op_fwd(q, k, v, gb):
        o, states, _ = delta_fwd(name, q, k, v, gb)
        return o, (q, k, v, gb, states)

    def op_bwd(res, do):
        return delta_bwd(name, *res, do)[0]

    op.defvjp(op_fwd, op_bwd)
    return op(q, k, v, gb)


def _my_place():
    return lax.axis_index("x"), lax.axis_index("y"), lax.axis_index("c")


def all_gather(name, shard):
    return all_gather_many(name, [shard])[0]


def _gather_phases(x_refs, out_refs, sems):
    n = len(x_refs)
    send_sems, recv_sems, local_sems = sems
    x, y, c = _my_place()
    me, sibling = (x, y, c), (x, y, 1 - c)
    chips = [(1 - x, y), (x, 1 - y), (1 - x, 1 - y)]

    def copy(i, k, block, to, own=False):
        px, py, pc = block
        slot = out_refs[i].at[4 * px + 2 * py + pc]
        return pltpu.make_async_remote_copy(
            src_ref=x_refs[i] if own else slot, dst_ref=slot,
            send_sem=send_sems.at[7 * i + k], recv_sem=recv_sems.at[7 * i + k],
            device_id=to, device_id_type=MESH)

    mine = [pltpu.make_async_copy(x_refs[i], out_refs[i].at[4 * x + 2 * y + c], local_sems.at[i])
            for i in range(n)]
    first = []
    for i in range(n):
        first.append(copy(i, 0, me, sibling, own=True))
        first += [copy(i, 1 + j, me, (*chip, c), own=True) for j, chip in enumerate(chips)]

    def begin():
        for cp in mine + first:
            cp.start()

    def finish():
        passed = []
        for j, chip in enumerate(chips):
            for i in range(n):
                copy(i, 1 + j, (*chip, c), me).wait_recv()
                passed.append(copy(i, 4 + j, (*chip, c), sibling))
                passed[-1].start()
        for i in range(n):
            copy(i, 0, sibling, me).wait_recv()
        for j, chip in enumerate(chips):
            for i in range(n):
                copy(i, 4 + j, (*chip, 1 - c), me).wait_recv()
        for cp in first + passed:
            cp.wait_send()
        for cp in mine:
            cp.wait()

    return begin, finish


def all_gather_many(name, shards):
    n = len(shards)

    def body(*refs):
        begin, finish = _gather_phases(refs[:n], refs[n:2 * n], refs[2 * n:])
        begin()
        finish()

    any_spec = pl.BlockSpec(memory_space=pl.ANY)
    return pl.pallas_call(
        body, name=name,
        out_shape=[jax.ShapeDtypeStruct((N_DEV,) + t.shape, t.dtype) for t in shards],
        in_specs=[any_spec] * n, out_specs=[any_spec] * n,
        scratch_shapes=[pltpu.SemaphoreType.DMA((7 * n,)), pltpu.SemaphoreType.DMA((7 * n,)),
                        pltpu.SemaphoreType.DMA((n,))],
    )(*shards)


def sibling_exchange(name, parts):
    n = len(parts)

    def body(*refs):
        p_refs, out_refs = refs[:n], refs[n:2 * n]
        send_sems, recv_sems = refs[2 * n:]
        x, y, c = _my_place()
        copies = []
        for i in range(n):
            for chip in range(4):
                copies.append(pltpu.make_async_remote_copy(
                    src_ref=p_refs[i].at[2 * chip + (1 - c)], dst_ref=out_refs[i].at[chip],
                    send_sem=send_sems.at[4 * i + chip], recv_sem=recv_sems.at[4 * i + chip],
                    device_id=(x, y, 1 - c), device_id_type=MESH))
        for cp in copies:
            cp.start()
        for cp in copies:
            cp.wait_recv()
        for cp in copies:
            cp.wait_send()

    any_spec = pl.BlockSpec(memory_space=pl.ANY)
    return pl.pallas_call(
        body, name=name,
        out_shape=[jax.ShapeDtypeStruct((4,) + t.shape[1:], t.dtype) for t in parts],
        in_specs=[any_spec] * n, out_specs=[any_spec] * n,
        scratch_shapes=[pltpu.SemaphoreType.DMA((4 * n,)), pltpu.SemaphoreType.DMA((4 * n,))],
    )(*parts)


def _chip_exchange_phases(p_refs, out_refs, sems):
    n = len(p_refs)
    send_sems, recv_sems, local_sems = sems
    x, y, c = _my_place()
    me = 2 * x + y
    peers = [(1 - x, y), (x, 1 - y), (1 - x, 1 - y)]

    def copy(i, j, landing):
        px, py = peers[j]
        return pltpu.make_async_remote_copy(
            src_ref=p_refs[i].at[2 * px + py],
            dst_ref=out_refs[i].at[(2 * px + py) if landing else me],
            send_sem=send_sems.at[3 * i + j], recv_sem=recv_sems.at[3 * i + j],
            device_id=(px, py, c), device_id_type=MESH)

    mine = [pltpu.make_async_copy(p_refs[i].at[me], out_refs[i].at[me], local_sems.at[i])
            for i in range(n)]
    copies = [copy(i, j, False) for j in range(3) for i in range(n)]

    def begin():
        for cp in mine + copies:
            cp.start()

    def finish():
        for j in range(3):
            for i in range(n):
                copy(i, j, True).wait_recv()
        for cp in copies:
            cp.wait_send()
        for cp in mine:
            cp.wait()

    return begin, finish


def chip_exchange(name, parts):
    n = len(parts)

    def body(*refs):
        begin, finish = _chip_exchange_phases(refs[:n], refs[n:2 * n], refs[2 * n:])
        begin()
        finish()

    any_spec = pl.BlockSpec(memory_space=pl.ANY)
    return pl.pallas_call(
        body, name=name,
        out_shape=[jax.ShapeDtypeStruct(t.shape, t.dtype) for t in parts],
        in_specs=[any_spec] * n, out_specs=[any_spec] * n,
        scratch_shapes=[pltpu.SemaphoreType.DMA((3 * n,)), pltpu.SemaphoreType.DMA((3 * n,)),
                        pltpu.SemaphoreType.DMA((n,))],
    )(*parts)


def pair_sum(name, mine, got, core):
    _, rows, cols = got.shape
    row_bytes = 4 * LANES * (-(-cols // LANES))
    tile = _pick(rows, [t for t in (512, 352, 256, 128, 64, 32, 16, 8)
                        if t * row_bytes <= ADAMW_BLOCK_BYTES])

    def body(core_ref, a_ref, b_ref, o_ref):
        o_ref[...] = (a_ref[...].astype(F32) + b_ref[...].astype(F32)).astype(o_ref.dtype)

    spec = pl.BlockSpec((4, tile, cols), lambda i, core_ref: (0, i, 0))
    return pl.pallas_call(
        body, name=name,
        grid_spec=pltpu.PrefetchScalarGridSpec(
            num_scalar_prefetch=1, grid=(rows // tile,),
            in_specs=[pl.BlockSpec((4, None, tile, cols), lambda i, core_ref: (0, core_ref[0], i, 0)),
                      spec],
            out_specs=spec),
        out_shape=jax.ShapeDtypeStruct(got.shape, got.dtype), compiler_params=_cparams(1),
    )(core, mine, got)


ADAMW_BLOCK_BYTES = 3 * 512 * 1024


def adamw(name, grad, w, m, v):
    rows, cols = w.shape
    stacked = grad.ndim == 3
    row_bytes = 4 * LANES * (-(-cols // LANES))
    tile = _pick(rows, [t for t in (512, 352, 256, 128, 64, 32, 16, 8)
                        if t * row_bytes <= ADAMW_BLOCK_BYTES])

    def body(g_ref, w_ref, m_ref, v_ref, go_ref, d_ref, mo_ref, vo_ref):
        if stacked:
            g = g_ref[0].astype(F32)
            for s in range(1, grad.shape[0]):
                g = g + g_ref[s].astype(F32)
        else:
            g = g_ref[...]
        m = ADAM_B1 * m_ref[...] + (1.0 - ADAM_B1) * g
        v = ADAM_B2 * v_ref[...] + (1.0 - ADAM_B2) * jnp.square(g)
        m_hat = m / (1.0 - ADAM_B1 ** ADAM_STEP)
        v_hat = v / (1.0 - ADAM_B2 ** ADAM_STEP)
        go_ref[...] = g
        d_ref[...] = -ADAM_LR * (m_hat / (jnp.sqrt(v_hat) + ADAM_EPS) + ADAM_WD * w_ref[...])
        mo_ref[...] = m
        vo_ref[...] = v

    flat = pl.BlockSpec((tile, cols), lambda i: (i, 0))
    g_spec = pl.BlockSpec((grad.shape[0], tile, cols), lambda i: (0, i, 0)) if stacked else flat
    return pl.pallas_call(
        body, name=name, grid=(rows // tile,),
        in_specs=[g_spec, flat, flat, flat], out_specs=[flat] * 4,
        out_shape=[jax.ShapeDtypeStruct((rows, cols), F32)] * 4,
        compiler_params=_cparams(1),
    )(grad, w, m, v)


def silu_rows(name, x):
    def body(x_ref, o_ref):
        o_ref[...] = _silu(x_ref[...])

    return pl.pallas_call(body, name=name, out_shape=jax.ShapeDtypeStruct(x.shape, F32))(x)


BIG = (("ffn1_wg", (D, D_FF // N_DEV), 1), ("ffn1_wu", (D, D_FF // N_DEV), 1),
       ("ffn1_wd", (D_FF // N_DEV, D), 0), ("w_in", (D, IN_COLS // N_DEV), 1),
       ("conv_w", (DN_CONV, 3 * D // N_DEV), 1), ("w_a", (D // N_DEV, D), 0),
       ("w_b", (DA_HEADS * DA_DIM, D // N_DEV), 1), ("w_o", (D // N_DEV, D), 0),
       ("ffn2_wg", (D, D_FF // N_DEV), 1), ("ffn2_wu", (D, D_FF // N_DEV), 1),
       ("ffn2_wd", (D_FF // N_DEV, D), 0))
SMALL = (("ada_b", (DEPTH, N_ADA * D)), ("ln_ffn1", (DEPTH, D)), ("ln_mix", (DEPTH, D)),
         ("ln_ffn2", (DEPTH, D)), ("a_log", (DEPTH, DN_HEADS)), ("dt_bias", (DEPTH, DN_HEADS)),
         ("dn_norm", (DEPTH, DN_DIM)), ("final_norm", (D,)))
SMALL_ROWS = 32


def _pack(arrays, rows):
    flat = jnp.concatenate([a.reshape(-1) for a in arrays])
    return jnp.pad(flat, (0, rows * D - flat.shape[0])).reshape(rows, D)


def _unpack(buf, shapes):
    flat = buf.reshape(-1)
    out, off = [], 0
    for shp in shapes:
        n = int(np.prod(shp))
        out.append(flat[off:off + n].reshape(shp))
        off += n
    return out


def _full_weights(gathered, entries):
    full = {}
    for (name, (a, b), axis), t in zip(entries, gathered):
        if axis == 1:
            full[name] = t.transpose(1, 0, 2).reshape(a, N_DEV * b)
        else:
            full[name] = t.reshape(N_DEV * a, b)
    return full


_B0, _DQ0, _GA0 = 4096, 4112, 6416
_N_WIDE = 6 * D


def _reorder_in_proj(w):
    pad = jnp.zeros((w.shape[0], IN_COLS_PAD - IN_COLS), w.dtype)
    return jnp.concatenate([w[:, :_B0], w[:, _GA0:], w[:, _DQ0:_GA0], w[:, _B0:_DQ0], pad], axis=1)


def _ffn(tag, h, ln, shift, scale, gate, w_gu, w_d):
    n = norm_mod(tag + "_norm", h, ln, shift, scale)
    a = gate_up_act(tag + "_gu", n, w_gu)
    f = linear(tag + "_down", a, w_d, F32)
    return residual(tag + "_res", h, f, gate, 0.5)


N_PRE = 5


def _mods(mod, l):
    return [mod[l, i * D:(i + 1) * D][None] for i in range(N_ADA)]


def _layer_pre(l, h, gathered, small, mod):
    w = _full_weights(gathered, BIG[:N_PRE])
    tag = f"l{l}"
    sh1, sc1, gt1, sh2, sc2 = _mods(mod, l)[:5]
    w_gu1 = _interleave_gate_up(w["ffn1_wg"], w["ffn1_wu"])
    h = _ffn(tag + "_ffn1", h, small["ln_ffn1"][l][None], sh1, sc1, gt1, w_gu1, w["ffn1_wd"])
    u = norm_mod(tag + "_mixnorm", h, small["ln_mix"][l][None], sh2, sc2)
    w_in = _reorder_in_proj(w["w_in"])
    (q_pre, k_pre, v_pre, z, gate_a, gate_b) = linear_split(
        tag + "_mix_in", u, w_in[:, :_N_WIDE], D, (1,) * 6, (BF16,) * 6)
    (da_q, da_k, da_v) = linear_split(
        tag + "_mix_in_da", u, w_in[:, _N_WIDE:_N_WIDE + 3 * DA_HEADS * DA_DIM], DA_HEADS * DA_DIM,
        (1,) * 3, (BF16,) * 3)
    ab = linear(tag + "_mix_in_ab", u, w_in[:, _N_WIDE + 3 * DA_HEADS * DA_DIM:], F32)
    cw = w["conv_w"].astype(F32).reshape(DN_CONV, 1, 3 * D)
    q, k, v = [conv_heads(f"{tag}_mix_conv_{m}", t, cw[:, :, i * D:(i + 1) * D], m)
               for i, (m, t) in enumerate(zip("qkv", (q_pre, k_pre, v_pre)))]
    pad = lambda t: jnp.pad(t, (DN_HEADS, ab.shape[1] - 2 * DN_HEADS))[None]
    gb = decay_beta(tag + "_mix_decay", ab, pad(small["a_log"][l]), pad(small["dt_bias"][l]))
    return (q, k, v, gb[:, :LANES]), (h, z, da_q, da_k, da_v, gate_a, gate_b)


def _layer_post(l, o, carry, gathered, small, mod):
    w = _full_weights(gathered, BIG[N_PRE:])
    tag = f"l{l}"
    h, z, da_q, da_k, da_v, gate_a, gate_b = carry
    gt2, sh3, sc3, gt3 = _mods(mod, l)[5:]
    o_a = gated_head_norm(tag + "_mix_gnorm", o, z, small["dn_norm"][l][None])
    y_a = linear(tag + "_mix_wa", o_a, w["w_a"], F32)

    outs, lses = [], []
    for r in DA_DILATIONS:
        o_r, lse_r = dilated_attention(f"{tag}_mix_attn{r}",
                                       *[_sub_order(t, r) for t in (da_q, da_k, da_v)], r)
        outs.append(_seq_order(o_r, r))
        lses.append(_seq_order(lse_r, r))
    o_b = combine_patterns(tag + "_mix_comb", outs, lses)
    y_b = linear(tag + "_mix_wb", o_b, w["w_b"], F32)

    merged = merge_gates(tag + "_mix_merge", gate_a, gate_b, y_a, y_b)
    m = linear(tag + "_mix_wo", merged, w["w_o"], F32)
    h = residual(tag + "_mixres", h, m, gt2, 1.0)
    w_gu2 = _interleave_gate_up(w["ffn2_wg"], w["ffn2_wu"])
    return _ffn(tag + "_ffn2", h, small["ln_ffn2"][l][None], sh3, sc3, gt3, w_gu2, w["ffn2_wd"])


def _layer(l, h, gathered, small, mod):
    (q, k, v, gb), carry = _layer_pre(l, h, gathered[:N_PRE], small, mod)
    o = delta_rule(f"l{l}_mix_delta", q, k, v, gb)
    return _layer_post(l, o, carry, gathered[N_PRE:], small, mod)


def _head(h, small, target):
    rows = loss_rows("loss", h, small["final_norm"][None], target)
    return 0.5 * jnp.sum(rows)


def _local_loss(x, gathered, small, mod, target):
    h = x
    for l in range(DEPTH):
        h = _layer(l, h, gathered[l], small, mod)
    return _head(h, small, target)


def kernel(x, c, ada_w, ada_b, ln_ffn1, ln_mix, ln_ffn2, ffn1_wg, ffn1_wu, ffn1_wd, w_in, conv_w, a_log, dt_bias, dn_norm, w_a, w_b, w_o, ffn2_wg, ffn2_wu, ffn2_wd, final_norm, loss_target, m_ada_w, m_ada_b, m_ln_ffn1, m_ln_mix, m_ln_ffn2, m_ffn1_wg, m_ffn1_wu, m_ffn1_wd, m_w_in, m_conv_w, m_a_log, m_dt_bias, m_dn_norm, m_w_a, m_w_b, m_w_o, m_ffn2_wg, m_ffn2_wu, m_ffn2_wd, m_final_norm, v_ada_w, v_ada_b, v_ln_ffn1, v_ln_mix, v_ln_ffn2, v_ffn1_wg, v_ffn1_wu, v_ffn1_wd, v_w_in, v_conv_w, v_a_log, v_dt_bias, v_dn_norm, v_w_a, v_w_b, v_w_o, v_ffn2_wg, v_ffn2_wu, v_ffn2_wd, v_final_norm):
    args = dict(locals())
    big_names = [n for n, _, _ in BIG]
    small_names = [n for n, _ in SMALL]
    me = 4 * lax.axis_index("x") + 2 * lax.axis_index("y") + lax.axis_index("c")
    cols = N_ADA * D // N_DEV

    c_all = all_gather("gather_c", jnp.pad(silu_rows("silu_c", c), ((0, 7), (0, 0))))[:, 0]
    mod_cols = jnp.stack([matmul(f"ada{l}", c_all, ada_w[l], "nn", F32) for l in range(DEPTH)])
    mod_cols = mod_cols + lax.dynamic_slice_in_dim(ada_b, me * cols, cols, axis=1)[:, None, :]
    mod_all = all_gather("gather_mod", mod_cols.reshape(DEPTH * N_DEV, cols))
    mod_all = mod_all.reshape(N_DEV, DEPTH, N_DEV, cols)
    mod = lax.dynamic_index_in_dim(mod_all, me, axis=2, keepdims=False)
    mod = mod.transpose(1, 0, 2).reshape(DEPTH, N_ADA * D)

    shards = [[args[n][l].astype(BF16) for n in big_names] for l in range(DEPTH)]
    pre_names, post_names = big_names[:N_PRE], big_names[N_PRE:]
    small = {n: args[n] for n in small_names if n != "ada_b"}
    target = loss_target[0]

    w0_pre = all_gather_many("gather_w0", shards[0][:N_PRE])
    pre0, vjp_pre0 = jax.vjp(functools.partial(_layer_pre, 0), x[0], w0_pre, small, mod)
    (q0, k0, v0, gb0), carry0 = pre0
    o0, states0, got = delta_fwd("l0_mix_delta", q0, k0, v0, gb0,
                                 gather=shards[0][N_PRE:] + shards[1][:N_PRE])
    w0_post, w1_pre = got[:len(post_names)], got[len(post_names):]
    h1, vjp_post0 = jax.vjp(functools.partial(_layer_post, 0), o0, carry0, w0_post, small, mod)
    pre1, vjp_pre1 = jax.vjp(functools.partial(_layer_pre, 1), h1, w1_pre, small, mod)
    (q1, k1, v1, gb1), carry1 = pre1
    o1, states1, w1_post = delta_fwd("l1_mix_delta", q1, k1, v1, gb1, gather=shards[1][N_PRE:])
    loss, vjp_post1 = jax.vjp(
        lambda o, carry, w, sm, md: _head(_layer_post(1, o, carry, w, sm, md), sm, target),
        o1, carry1, w1_post, small, mod)

    my_core = lax.axis_index("c").astype(jnp.int32).reshape(1)

    def pair_sums(tag, names, grads):
        from_sibling = sibling_exchange(f"pair_grads_{tag}", grads)
        return [pair_sum(f"pair_sum_{tag}_{n}", t.reshape(4, 2, -1, t.shape[-1]), got_n, my_core)
                for n, t, got_n in zip(names, grads, from_sibling)]

    do1, dcarry1, dw1_post, dsmall_a, dmod_a = vjp_post1(jnp.ones((), F32))
    ddelta1, landed1_post = delta_bwd("l1_mix_delta", q1, k1, v1, gb1, states1, do1,
                                      exchange=pair_sums("l1_post", post_names, list(dw1_post)))
    dh1, dw1_pre, dsmall_b, dmod_b = vjp_pre1((ddelta1, dcarry1))
    do0, dcarry0, dw0_post, dsmall_c, dmod_c = vjp_post0(dh1)
    ddelta0, landed_mid = delta_bwd(
        "l0_mix_delta", q0, k0, v0, gb0, states0, do0,
        exchange=pair_sums("mid", pre_names + post_names, list(dw1_pre) + list(dw0_post)))
    dx, dw0_pre, dsmall_d, dmod_d = vjp_pre0((ddelta0, dcarry0))
    landed0_pre = chip_exchange("scatter_grads0", pair_sums("l0_pre", pre_names, list(dw0_pre)))
    landed0 = list(landed0_pre) + list(landed_mid[N_PRE:])
    landed1 = list(landed_mid[:N_PRE]) + list(landed1_post)
    dsmall = {n: dsmall_a[n] + dsmall_b[n] + dsmall_c[n] + dsmall_d[n] for n in dsmall_a}
    dmod = dmod_a + dmod_b + dmod_c + dmod_d

    part = _pack([dmod] + [dsmall[n] for n in small_names[1:]], SMALL_ROWS)
    parts = all_gather("gather_small", part)
    sm_out = adamw("adamw_small", parts, _pack([args[n] for n in small_names], SMALL_ROWS),
                   _pack([args["m_" + n] for n in small_names], SMALL_ROWS),
                   _pack([args["v_" + n] for n in small_names], SMALL_ROWS))

    dmod_all = parts.reshape(N_DEV, -1)[:, :DEPTH * N_ADA * D].reshape(N_DEV, DEPTH, N_ADA * D)
    dmod_mine = lax.dynamic_slice_in_dim(dmod_all, me * cols, cols, axis=2)
    g_ada = jnp.stack([matmul(f"ada{l}_dw", c_all, dmod_mine[:, l], "tn", F32) for l in range(DEPTH)])
    flat2 = lambda t: t.reshape(-1, t.shape[-1])
    ada_out = adamw("adamw_ada_w", flat2(g_ada), flat2(ada_w), flat2(m_ada_w), flat2(v_ada_w))

    big_out = {}
    for n, t0, t1 in zip(big_names, landed0, landed1):
        both = jnp.concatenate([t0, t1], axis=1)
        big_out[n] = adamw("adamw_" + n, both, flat2(args[n]), flat2(args["m_" + n]), flat2(args["v_" + n]))

    small_shapes = [shp for _, shp in SMALL]
    names = ["ada_w", "ada_b", "ln_ffn1", "ln_mix", "ln_ffn2", "ffn1_wg", "ffn1_wu", "ffn1_wd", "w_in",
             "conv_w", "a_log", "dt_bias", "dn_norm", "w_a", "w_b", "w_o", "ffn2_wg", "ffn2_wu",
             "ffn2_wd", "final_norm"]
    outs = [lax.psum(loss, ("x", "y", "c")), dx[None]]
    for kind in range(4):
        table = {n: big_out[n][kind].reshape(args[n].shape) for n in big_names}
        table.update(zip(small_names, _unpack(sm_out[kind], small_shapes)))
        table["ada_w"] = ada_out[kind].reshape(ada_w.shape)
        outs += [table[n] for n in names]
    return tuple(outs)
```

```python
import functools
import math

import numpy as np
import jax
import jax.numpy as jnp
from jax import lax
from jax.experimental import pallas as pl
from jax.experimental.pallas import tpu as pltpu

F32 = jnp.float32
BF16 = jnp.bfloat16

D = 1024
SEQ = 4096
DEPTH = 2
N_DEV = 8
DN_HEADS = 8
DN_DIM = 128
DN_CHUNK = 64
DN_CONV = 4
DA_HEADS = 12
DA_DIM = 64
DA_BLOCK = 128
DA_DILATIONS = (1, 4, 16)
ALIBI_MAX_EXP = 8.0
D_FF = 2816
N_ADA = 9
NORM_EPS = 1e-6
IN_COLS = 8464
IN_COLS_PAD = 8704
ADAM_LR, ADAM_B1, ADAM_B2, ADAM_EPS, ADAM_WD, ADAM_STEP = 0.001, 0.9, 0.999, 1e-08, 0.01, 10
NEG = -1e30

VMEM_LIMIT = 56 * 1024 * 1024
LANES = 128

MESH = pl.DeviceIdType.MESH


def _cparams(n_grid):
    return pltpu.CompilerParams(dimension_semantics=("arbitrary",) * n_grid,
                                vmem_limit_bytes=VMEM_LIMIT)


def blockwise(name, f, grid, ins, outs):
    n_in, n_out = len(ins), len(outs)
    diff = [i for i, (_, _, kind) in enumerate(ins) if kind != "const"]

    def apply(*vals):
        res = f(*vals)
        return tuple(r.astype(dt) for r, (_, dt, _, _) in zip(res, outs))

    def fwd_call(*arrays):
        def body(*refs):
            res = apply(*[r[...] for r in refs[:n_in]])
            for r, v in zip(refs[n_in:], res):
                r[...] = v

        return pl.pallas_call(
            body, name=name + "_fwd", grid=grid,
            in_specs=[pl.BlockSpec(b, im) for (b, im, _) in ins],
            out_specs=[pl.BlockSpec(b, im) for (_, _, b, im) in outs],
            out_shape=[jax.ShapeDtypeStruct(s, dt) for (s, dt, _, _) in outs],
            compiler_params=_cparams(len(grid)),
        )(*arrays)

    def bwd_call(arrays, cts):
        def body(*refs):
            in_refs, ct_refs = refs[:n_in], refs[n_in:n_in + n_out]
            g_refs = refs[n_in + n_out:]
            vals = [r[...] for r in in_refs]

            def fd(*dvals):
                full = list(vals)
                for i, v in zip(diff, dvals):
                    full[i] = v
                return apply(*full)

            _, vjp = jax.vjp(fd, *[vals[i] for i in diff])
            grads = vjp(tuple(r[...] for r in ct_refs))
            first = functools.reduce(jnp.logical_and,
                                     [pl.program_id(a) == 0 for a in range(len(grid))])
            for g_ref, g, i in zip(g_refs, grads, diff):
                if ins[i][2] == "acc":
                    @pl.when(first)
                    def _(g_ref=g_ref):
                        g_ref[...] = jnp.zeros_like(g_ref)
                    g_ref[...] += g.astype(F32)
                else:
                    g_ref[...] = g.astype(g_ref.dtype)

        g_shapes = [jax.ShapeDtypeStruct(arrays[i].shape,
                                         F32 if ins[i][2] == "acc" else arrays[i].dtype)
                    for i in diff]
        return pl.pallas_call(
            body, name=name + "_bwd", grid=grid,
            in_specs=([pl.BlockSpec(b, im) for (b, im, _) in ins]
                      + [pl.BlockSpec(b, im) for (_, _, b, im) in outs]),
            out_specs=[pl.BlockSpec(ins[i][0], ins[i][1]) for i in diff],
            out_shape=g_shapes,
            compiler_params=_cparams(len(grid)),
        )(*arrays, *cts)

    @jax.custom_vjp
    def op(*arrays):
        return tuple(fwd_call(*arrays))

    def op_fwd(*arrays):
        return tuple(fwd_call(*arrays)), arrays

    def op_bwd(arrays, cts):
        grads = bwd_call(arrays, cts)
        full = [None] * n_in
        for i, g in zip(diff, grads):
            full[i] = g.astype(arrays[i].dtype)
        return tuple(full)

    op.defvjp(op_fwd, op_bwd)
    op.backward = bwd_call
    return op


def _pick(n, cands):
    for c in cands:
        if n % c == 0:
            return c
    return n


def matmul(name, a, b, form, out_dtype):
    if form == "nn":
        (m, k), (_, n) = a.shape, b.shape
    elif form == "nt":
        (m, k), (n, _) = a.shape, b.shape
    else:
        (k, m), (_, n) = a.shape, b.shape
    tm = _pick(m, (1408, 1024, 512, 256, 128, 8))
    tn = _pick(n, (1408, 1024, 512, 384, 256, 128))
    tk = _pick(k, (1024, 1408, 768, 512, 384, 256, 128, 8))
    nk = k // tk
    a_spec = (pl.BlockSpec((tk, tm), lambda i, j, kk: (kk, i)) if form == "tn"
              else pl.BlockSpec((tm, tk), lambda i, j, kk: (i, kk)))
    b_spec = (pl.BlockSpec((tn, tk), lambda i, j, kk: (j, kk)) if form == "nt"
              else pl.BlockSpec((tk, tn), lambda i, j, kk: (kk, j)))
    dims = {"nn": (((1,), (0,)), ((), ())), "nt": (((1,), (1,)), ((), ())),
            "tn": (((0,), (0,)), ((), ()))}[form]

    def body(a_ref, b_ref, o_ref, acc_ref):
        kk = pl.program_id(2)
        part = lax.dot_general(a_ref[...].astype(BF16), b_ref[...].astype(BF16), dims,
                               preferred_element_type=F32)
        if nk == 1:
            o_ref[...] = part.astype(o_ref.dtype)
            return

        @pl.when(kk == 0)
        def _():
            acc_ref[...] = part

        @pl.when(kk > 0)
        def _():
            acc_ref[...] += part

        @pl.when(kk == nk - 1)
        def _():
            o_ref[...] = acc_ref[...].astype(o_ref.dtype)

    return pl.pallas_call(
        body, name=name, grid=(m // tm, n // tn, nk),
        in_specs=[a_spec, b_spec],
        out_specs=pl.BlockSpec((tm, tn), lambda i, j, kk: (i, j)),
        out_shape=jax.ShapeDtypeStruct((m, n), out_dtype),
        scratch_shapes=[pltpu.VMEM((tm, tn), F32)],
        compiler_params=_cparams(3),
    )(a, b)


def linear(name, x, w, out_dtype):
    @jax.custom_vjp
    def op(x, w):
        return matmul(name + "_y", x, w, "nn", out_dtype)

    def op_fwd(x, w):
        return op(x, w), (x, w)

    def op_bwd(res, dy):
        x, w = res
        dx = matmul(name + "_dx", dy, w, "nt", x.dtype)
        dw = matmul(name + "_dw", x, dy, "tn", w.dtype)
        return dx, dw

    op.defvjp(op_fwd, op_bwd)
    return op(x, w)


def linear_split(name, x, w, tile, seg_tiles, out_dtypes):
    SPLIT_TILE = tile
    m, k = x.shape
    n = w.shape[1]
    nt = n // SPLIT_TILE
    starts = [sum(seg_tiles[:s]) for s in range(len(seg_tiles))]
    assert sum(seg_tiles) == nt
    tm = _pick(m, (1024, 512, 256, 128, 8))

    def inside(j, s):
        return jnp.logical_and(j >= starts[s], j < starts[s] + seg_tiles[s])

    def local(j, s):
        return jnp.clip(j - starts[s], 0, seg_tiles[s] - 1)

    def fwd_call(x, w):
        def body(x_ref, w_ref, *o_refs):
            j = pl.program_id(1)
            y = jnp.dot(x_ref[...].astype(BF16), w_ref[...].astype(BF16), preferred_element_type=F32)
            for s, o_ref in enumerate(o_refs):
                @pl.when(inside(j, s))
                def _(o_ref=o_ref):
                    o_ref[...] = y.astype(o_ref.dtype)

        return pl.pallas_call(
            body, name=name + "_y", grid=(m // tm, nt),
            in_specs=[pl.BlockSpec((tm, k), lambda i, j: (i, 0)),
                      pl.BlockSpec((k, SPLIT_TILE), lambda i, j: (0, j))],
            out_specs=[pl.BlockSpec((tm, SPLIT_TILE), lambda i, j, s=s: (i, local(j, s)))
                       for s in range(len(seg_tiles))],
            out_shape=[jax.ShapeDtypeStruct((m, t * SPLIT_TILE), dt)
                       for t, dt in zip(seg_tiles, out_dtypes)],
            compiler_params=_cparams(2),
        )(x, w)

    def dx_call(dys, w):
        def body(*refs):
            dy_refs, w_ref, o_ref, acc_ref = refs[:-3], refs[-3], refs[-2], refs[-1]
            j = pl.program_id(1)
            for s, dy_ref in enumerate(dy_refs):
                @pl.when(inside(j, s))
                def _(dy_ref=dy_ref):
                    part = lax.dot_general(dy_ref[...].astype(BF16), w_ref[...].astype(BF16),
                                           (((1,), (1,)), ((), ())), preferred_element_type=F32)

                    @pl.when(j == 0)
                    def _():
                        acc_ref[...] = part

                    @pl.when(j > 0)
                    def _():
                        acc_ref[...] += part

            @pl.when(j == nt - 1)
            def _():
                o_ref[...] = acc_ref[...].astype(o_ref.dtype)

        return pl.pallas_call(
            body, name=name + "_dx", grid=(m // tm, nt),
            in_specs=[pl.BlockSpec((tm, SPLIT_TILE), lambda i, j, s=s: (i, local(j, s)))
                      for s in range(len(seg_tiles))]
            + [pl.BlockSpec((k, SPLIT_TILE), lambda i, j: (0, j))],
            out_specs=pl.BlockSpec((tm, k), lambda i, j: (i, 0)),
            out_shape=jax.ShapeDtypeStruct((m, k), x.dtype),
            scratch_shapes=[pltpu.VMEM((tm, k), F32)],
            compiler_params=_cparams(2),
        )(*dys, w)

    def dw_call(x, dys):
        nm = m // tm

        def body(*refs):
            x_ref, dy_refs, o_ref, acc_ref = refs[0], refs[1:-2], refs[-2], refs[-1]
            j, kk = pl.program_id(0), pl.program_id(1)
            for s, dy_ref in enumerate(dy_refs):
                @pl.when(inside(j, s))
                def _(dy_ref=dy_ref):
                    rows = x_ref[pl.ds(pl.multiple_of(kk * tm, tm), tm), :]
                    part = lax.dot_general(rows.astype(BF16), dy_ref[...].astype(BF16),
                                           (((0,), (0,)), ((), ())), preferred_element_type=F32)

                    @pl.when(kk == 0)
                    def _():
                        acc_ref[...] = part

                    @pl.when(kk > 0)
                    def _():
                        acc_ref[...] += part

            @pl.when(kk == nm - 1)
            def _():
                o_ref[...] = acc_ref[...].astype(o_ref.dtype)

        return pl.pallas_call(
            body, name=name + "_dw", grid=(nt, nm),
            in_specs=[pl.BlockSpec((m, k), lambda j, kk: (0, 0))]
            + [pl.BlockSpec((tm, SPLIT_TILE),
                            lambda j, kk, s=s: (jnp.where(inside(j, s), kk, 0), local(j, s)))
               for s in range(len(seg_tiles))],
            out_specs=pl.BlockSpec((k, SPLIT_TILE), lambda j, kk: (0, j)),
            out_shape=jax.ShapeDtypeStruct((k, n), w.dtype),
            scratch_shapes=[pltpu.VMEM((k, SPLIT_TILE), F32)],
            compiler_params=_cparams(2),
        )(x, *dys)

    @jax.custom_vjp
    def op(x, w):
        return tuple(fwd_call(x, w))

    def op_fwd(x, w):
        return op(x, w), (x, w)

    def op_bwd(res, dys):
        x, w = res
        return dx_call(dys, w), dw_call(x, dys)

    op.defvjp(op_fwd, op_bwd)
    return op(x, w)


@jax.custom_vjp
def _sigmoid(x):
    return 1.0 / (1.0 + jnp.exp(-x))


def _sigmoid_fwd(x):
    s = _sigmoid(x)
    return s, s


def _sigmoid_bwd(s, ct):
    return (ct * (s * (1.0 - s)),)


_sigmoid.defvjp(_sigmoid_fwd, _sigmoid_bwd)


@jax.custom_vjp
def _silu(x):
    return x * _sigmoid(x)


def _silu_fwd(x):
    s = _sigmoid(x)
    return x * s, (x, s)


def _silu_bwd(res, ct):
    x, s = res
    return (ct * (s * (1.0 + x * (1.0 - s))),)


_silu.defvjp(_silu_fwd, _silu_bwd)


def _softplus(x):
    return jnp.maximum(x, 0.0) + jnp.log(1.0 + jnp.exp(-jnp.abs(x)))


def _rms(x):
    return x * lax.rsqrt(jnp.mean(x * x, axis=-1, keepdims=True) + NORM_EPS)


ROW_TILE = 512


def _row(i):
    return (i, 0)


def _fixed(*_):
    return (0, 0)


def norm_mod(name, h, ln, shift, scale):
    s, d = h.shape

    def f(h, ln, sh, sc):
        return ((_rms(h) * ln) * (1.0 + sc) + sh,)

    op = blockwise(name, f, (s // ROW_TILE,),
                   [((ROW_TILE, d), _row, "tile")] + [((1, d), _fixed, "acc")] * 3,
                   [((s, d), BF16, (ROW_TILE, d), _row)])
    return op(h, ln, shift, scale)[0]


FF_TILE = 1408


def _interleave_gate_up(wg, wu):
    parts = []
    for j in range(wg.shape[1] // FF_TILE):
        parts += [wg[:, j * FF_TILE:(j + 1) * FF_TILE], wu[:, j * FF_TILE:(j + 1) * FF_TILE]]
    return jnp.concatenate(parts, axis=1)


def _swiglu(gu):
    g, u = gu[:, :FF_TILE].astype(F32), gu[:, FF_TILE:].astype(F32)
    return _silu(g) * u


def _swiglu_op(name, s, ff):
    return blockwise(name, lambda gu: (_swiglu(gu),), (s // ROW_TILE, ff // FF_TILE),
                     [((ROW_TILE, 2 * FF_TILE), lambda i, j: (i, j), "tile")],
                     [((s, ff), BF16, (ROW_TILE, FF_TILE), lambda i, j: (i, j))])


def gate_up_act(name, x, w):
    s, k = x.shape
    ff = w.shape[1] // 2
    tm = _pick(s, (512, 256, 128, 8))

    def fwd_call(x, w):
        def body(x_ref, w_ref, gu_ref, a_ref):
            gu = jnp.dot(x_ref[...], w_ref[...], preferred_element_type=F32).astype(BF16)
            gu_ref[...] = gu
            a_ref[...] = _swiglu(gu).astype(BF16)

        return pl.pallas_call(
            body, name=name + "_y", grid=(ff // FF_TILE, s // tm),
            in_specs=[pl.BlockSpec((tm, k), lambda j, i: (i, 0)),
                      pl.BlockSpec((k, 2 * FF_TILE), lambda j, i: (0, j))],
            out_specs=[pl.BlockSpec((tm, 2 * FF_TILE), lambda j, i: (i, j)),
                       pl.BlockSpec((tm, FF_TILE), lambda j, i: (i, j))],
            out_shape=[jax.ShapeDtypeStruct((s, 2 * ff), BF16), jax.ShapeDtypeStruct((s, ff), BF16)],
            compiler_params=_cparams(2),
        )(x, w)

    @jax.custom_vjp
    def op(x, w):
        return fwd_call(x, w)[1]

    def op_fwd(x, w):
        gu, a = fwd_call(x, w)
        return a, (x, w, gu)

    def op_bwd(res, da):
        x, w, gu = res
        dgu = _swiglu_op(name + "_act", s, ff).backward((gu,), (da,))[0]
        return (matmul(name + "_dx", dgu, w, "nt", x.dtype), matmul(name + "_dw", x, dgu, "tn", w.dtype))

    op.defvjp(op_fwd, op_bwd)
    return op(x, w)


def residual(name, h, y, gate, weight):
    s, d = h.shape

    def f(h, y, gate):
        return (h + (weight * gate) * y,)

    op = blockwise(name, f, (s // ROW_TILE,),
                   [((ROW_TILE, d), _row, "tile"), ((ROW_TILE, d), _row, "tile"),
                    ((1, d), _fixed, "acc")],
                   [((s, d), F32, (ROW_TILE, d), _row)])
    return op(h, y, gate)[0]


def merge_gates(name, ga, gb, ya, yb):
    s, d = ya.shape

    def f(ga, gb, ya, yb):
        return (_sigmoid(ga.astype(F32)) * ya + _sigmoid(gb.astype(F32)) * yb,)

    op = blockwise(name, f, (s // ROW_TILE,), [((ROW_TILE, d), _row, "tile")] * 4,
                   [((s, d), BF16, (ROW_TILE, d), _row)])
    return op(ga, gb, ya, yb)[0]


def loss_rows(name, h, g, target):
    s, d = h.shape

    def f(h, g, t):
        err = _rms(h) * g - t
        return (jnp.mean(err * err, axis=-1, keepdims=True),)

    op = blockwise(name, f, (s // ROW_TILE,),
                   [((ROW_TILE, d), _row, "tile"), ((1, d), _fixed, "acc"),
                    ((ROW_TILE, d), _row, "const")],
                   [((s, 1), F32, (ROW_TILE, 1), _row)])
    return op(h, g, target)[0]


def decay_beta(name, ab, a_log_pad, dt_bias_pad):
    s, n = ab.shape

    def f(ab, a_log, dt_bias):
        lane = lax.broadcasted_iota(jnp.int32, ab.shape, 1)
        beta = _sigmoid(ab)
        g = -jnp.exp(a_log) * _softplus(ab + dt_bias)
        return (jnp.where(lane < DN_HEADS, beta, jnp.where(lane < 2 * DN_HEADS, g, 0.0)),)

    op = blockwise(name, f, (s // ROW_TILE,),
                   [((ROW_TILE, n), _row, "tile"), ((1, n), _fixed, "acc"), ((1, n), _fixed, "acc")],
                   [((s, n), F32, (ROW_TILE, n), _row)])
    return op(ab, a_log_pad, dt_bias_pad)[0]


def _shift_rows(x, k):
    n = x.shape[0]

    @jax.custom_vjp
    def shift(x):
        row = lax.broadcasted_iota(jnp.int32, x.shape, 0)
        return jnp.where(row >= k, pltpu.roll(x, k, 0), 0.0)

    def shift_fwd(x):
        return shift(x), None

    def shift_bwd(_, g):
        row = lax.broadcasted_iota(jnp.int32, g.shape, 0)
        return (jnp.where(row < n - k, pltpu.roll(g, n - k, 0), 0.0),)

    shift.defvjp(shift_fwd, shift_bwd)
    return shift(x)


def conv_heads(name, x, w, mode):
    s, width = x.shape
    nh = width // LANES

    def f(x, w):
        x = x.astype(F32)
        y = w[DN_CONV - 1] * x
        for j in range(DN_CONV - 1):
            y = y + w[j] * _shift_rows(x, DN_CONV - 1 - j)
        y = _silu(y)
        if mode != "v":
            y = y * lax.rsqrt(jnp.sum(y * y, axis=-1, keepdims=True) + NORM_EPS)
        if mode == "q":
            y = y * (DN_DIM ** -0.5)
        return (y[None],)

    op = blockwise(name, f, (nh,),
                   [((s, LANES), lambda j: (0, j), "tile"),
                    ((DN_CONV, 1, LANES), lambda j: (0, 0, j), "tile")],
                   [((nh, s, LANES), F32, (1, s, LANES), lambda j: (j, 0, 0))])
    return op(x, w)[0]


def gated_head_norm(name, o, z, w):
    nh, s, dh = o.shape

    def f(o, z, w):
        return (_rms(o[0]) * w * _silu(z.astype(F32)),)

    op = blockwise(name, f, (s // ROW_TILE, nh),
                   [((1, ROW_TILE, dh), lambda i, h: (h, i, 0), "tile"),
                    ((ROW_TILE, dh), lambda i, h: (i, h), "tile"),
                    ((1, dh), lambda i, h: (0, 0), "acc")],
                   [((s, nh * dh), BF16, (ROW_TILE, dh), lambda i, h: (i, h))])
    return op(o, z, w)[0]


def _mm(a, b, ca, cb):
    return _bmm(a[None], b[None], ca + 1, cb + 1)[0]


def dilated_attention(name, q, k, v, dilation):
    s, width = q.shape
    nblk = s // DA_BLOCK
    per_sub = nblk // dilation
    slopes = [dilation * 2.0 ** (-ALIBI_MAX_EXP * (h + 1) / DA_HEADS) for h in range(DA_HEADS)]

    def f(q, kp, kc, vp, vc):
        first = (pl.program_id(0) % per_sub) == 0
        qi = lax.broadcasted_iota(jnp.int32, (DA_BLOCK, 2 * DA_BLOCK), 0)
        ki = lax.broadcasted_iota(jnp.int32, (DA_BLOCK, 2 * DA_BLOCK), 1)
        steps = (qi + DA_BLOCK - ki).astype(F32)
        lowest = qi + first.astype(jnp.int32) * (DA_BLOCK - qi)
        valid = jnp.logical_and(ki >= lowest, ki <= qi + DA_BLOCK)
        top = lax.broadcasted_iota(jnp.int32, (DA_BLOCK, LANES), 1) < DA_DIM
        o_parts, lse_parts = [], []
        for pair in range(width // LANES):
            cols = slice(pair * LANES, (pair + 1) * LANES)
            q2 = jnp.concatenate([jnp.where(top, q[:, cols], 0.0), jnp.where(top, 0.0, q[:, cols])], axis=0)
            k2 = jnp.concatenate([kp[:, cols], kc[:, cols]], axis=0)
            v2 = jnp.concatenate([vp[:, cols], vc[:, cols]], axis=0)
            bias = jnp.concatenate([jnp.where(valid, -slopes[2 * pair + half] * steps, NEG)
                                    for half in range(2)], axis=0)
            sc = _mm(q2, k2, 1, 1) * (DA_DIM ** -0.5) + bias
            mx = jnp.max(sc, axis=-1, keepdims=True)
            p = jnp.exp(sc - mx)
            l = jnp.sum(p, axis=-1, keepdims=True)
            o2 = _mm(p / l, v2, 1, 0)
            lse2 = mx + jnp.log(l)
            o_parts.append(jnp.where(top, o2[:DA_BLOCK], o2[DA_BLOCK:]))
            lse_parts.append(jnp.where(top, lse2[:DA_BLOCK], lse2[DA_BLOCK:]))
        return jnp.concatenate(o_parts, axis=1), jnp.concatenate(lse_parts, axis=1)

    blk = (DA_BLOCK, width)
    cur_map = lambda j: (j, 0)
    prev_map = lambda j: (jnp.maximum(j - 1, 0), 0)
    op = blockwise(
        name, f, (nblk,),
        [(blk, cur_map, "tile"), (blk, prev_map, "tile"), (blk, cur_map, "tile"),
         (blk, prev_map, "tile"), (blk, cur_map, "tile")],
        [((s, width), BF16, blk, cur_map), ((s, width), F32, blk, cur_map)])

    @jax.custom_vjp
    def attn(q, k, v):
        return op(q, k, k, v, v)

    def attn_fwd(q, k, v):
        return attn(q, k, v), (q, k, v)

    def attn_bwd(res, cts):
        q, k, v = res
        _, vjp = jax.vjp(op, q, k, k, v, v)
        dq, dkp, dkc, dvp, dvc = vjp(cts)
        fill = lambda t: t.at[s - DA_BLOCK:].set(0.0)
        return dq, dkc + fill(dkp), dvc + fill(dvp)

    attn.defvjp(attn_fwd, attn_bwd)
    return attn(q, k, v)


def combine_patterns(name, outs, lses):
    s, width = outs[0].shape
    n = len(outs)

    def f(*vals):
        o, lse = vals[:n], vals[n:]
        mx = functools.reduce(jnp.maximum, lse)
        e = [jnp.exp(t - mx) for t in lse]
        return (sum(ei * oi for ei, oi in zip(e, o)) / sum(e),)

    tile = ROW_TILE // 2
    op = blockwise(name, f, (s // tile,), [((tile, width), _row, "tile")] * (2 * n),
                   [((s, width), BF16, (tile, width), _row)])
    return op(*outs, *lses)[0]


def _sub_order(t, r):
    if r == 1:
        return t
    s, c = t.shape
    return t.reshape(s // r, r, c).transpose(1, 0, 2).reshape(s, c)


def _seq_order(t, r):
    if r == 1:
        return t
    s, c = t.shape
    return t.reshape(r, s // r, c).transpose(1, 0, 2).reshape(s, c)


def _raw_bmm(a, b, ca, cb):
    return lax.dot_general(a.astype(BF16), b.astype(BF16), (((ca,), (cb,)), ((0,), (0,))),
                           preferred_element_type=F32)


def _split(a):
    hi = a.astype(BF16)
    return hi, (a - hi.astype(F32)).astype(BF16)


def _passes_bmm(a, b, ca, cb, passes):
    if passes == 1:
        return _raw_bmm(a, b, ca, cb)
    (a_hi, a_lo), (b_hi, b_lo) = _split(a), _split(b)
    return _raw_bmm(a_hi, b_hi, ca, cb) + (_raw_bmm(a_hi, b_lo, ca, cb) + _raw_bmm(a_lo, b_hi, ca, cb))


def _bmm(a, b, ca, cb, passes=1):
    fa, fb = 3 - ca, 3 - cb

    @jax.custom_vjp
    def mm(a, b):
        return _passes_bmm(a, b, ca, cb, passes)

    def mm_fwd(a, b):
        return mm(a, b), (a, b)

    def mm_bwd(res, ct):
        a, b = res
        da = (_passes_bmm(ct, b, 2, fb, passes) if ca == 2
              else _passes_bmm(b, ct, fb, 2, passes))
        db = (_passes_bmm(a, ct, fa, 1, passes) if cb == 1
              else _passes_bmm(ct, a, 1, fa, passes))
        return da.astype(a.dtype), db.astype(b.dtype)

    mm.defvjp(mm_fwd, mm_bwd)
    return mm(a, b)


def _delta_chunk(q, k, v, gcol, grow, bcol, state):
    c = q.shape[1]
    ii = lax.broadcasted_iota(jnp.int32, (1, c, c), 1)
    jj = lax.broadcasted_iota(jnp.int32, (1, c, c), 2)
    incl, strict = ii >= jj, ii > jj
    gc_col = jnp.sum(jnp.where(incl, grow, 0.0), axis=2, keepdims=True)
    gc_row = jnp.sum(jnp.where(ii <= jj, gcol, 0.0), axis=1, keepdims=True)
    decay = jnp.where(incl, jnp.exp(jnp.where(incl, gc_col - gc_row, 0.0)), 0.0)
    kb, vb = k * bcol, v * bcol
    m = jnp.where(strict, _bmm(kb, k, 2, 2) * decay, 0.0)
    eye = (ii == jj).astype(F32)
    p = -m
    inv = eye + p
    for _ in range(int(math.log2(c)) - 1):
        p = _bmm(p, p, 2, 1, 3)
        inv = inv + _bmm(inv, p, 2, 1, 3)
    e_col = jnp.exp(gc_col)
    u = _bmm(inv, vb, 2, 1)
    w = _bmm(inv, kb * e_col, 2, 1)
    qk = _bmm(q, k, 2, 2) * decay
    v_new = u - _bmm(w, state, 2, 1)
    o = _bmm(q * e_col, state, 2, 1) + _bmm(qk, v_new, 2, 1)
    g_last = jnp.sum(grow, axis=2, keepdims=True)
    new_state = state * jnp.exp(g_last) + _bmm(k * jnp.exp(g_last - gc_col), v_new, 1, 1)
    return o, new_state


def _delta_chunk_packed(q, k, v, gb, state):
    nh, c = q.shape[0], q.shape[1]
    lane = lax.broadcasted_iota(jnp.int32, gb.shape, 1)
    eye = (lax.broadcasted_iota(jnp.int32, (c, c), 0) == lax.broadcasted_iota(jnp.int32, (c, c), 1))
    column = lambda l: jnp.sum(jnp.where(lane == l, gb, 0.0), axis=1, keepdims=True)
    heads = lambda parts: jnp.concatenate([t[None] for t in parts], axis=0)
    bcol = heads([column(h) for h in range(nh)])
    gcols = [column(nh + h) for h in range(nh)]
    gcol = heads(gcols)
    grow = heads([jnp.sum(jnp.where(eye, g, 0.0), axis=0, keepdims=True) for g in gcols])
    return _delta_chunk(q, k, v, gcol, grow, bcol, state)


def _delta_specs(nh, s, dh, rev):
    c, n = DN_CHUNK, s // DN_CHUNK
    t = (lambda i: n - 1 - i) if rev else (lambda i: i)
    seq = pl.BlockSpec((nh, c, dh), lambda i: (0, t(i), 0))
    gate = pl.BlockSpec((c, LANES), lambda i: (t(i), 0))
    st = pl.BlockSpec((nh, 1, dh, dh), lambda i: (0, t(i), 0, 0))
    return seq, gate, st


def delta_fwd(name, q, k, v, gb, gather=None):
    nh, s, dh = q.shape
    n = s // DN_CHUNK
    seq, gate, st = _delta_specs(nh, s, dh, False)
    extra = list(gather or [])
    ne = len(extra)

    def body(*refs):
        q_ref, k_ref, v_ref, gb_ref = refs[:4]
        x_refs = refs[4:4 + ne]
        o_ref, st_ref = refs[4 + ne:6 + ne]
        out_refs = refs[6 + ne:6 + 2 * ne]
        state = refs[6 + 2 * ne]
        i = pl.program_id(0)
        if ne:
            begin, finish = _gather_phases(x_refs, out_refs, refs[7 + 2 * ne:])
            pl.when(i == 0)(begin)

        @pl.when(i == 0)
        def _():
            state[...] = jnp.zeros_like(state)

        st_ref[:, 0] = state[...]
        o, new_state = _delta_chunk_packed(q_ref[...], k_ref[...], v_ref[...], gb_ref[...], state[...])
        o_ref[...] = o
        state[...] = new_state
        if ne:
            pl.when(i == n - 1)(finish)

    any_spec = pl.BlockSpec(memory_space=pl.ANY)
    out = pl.pallas_call(
        body, name=name + "_fwd", grid=(n,),
        in_specs=[seq, seq, seq, gate] + [any_spec] * ne,
        out_specs=[seq, st] + [any_spec] * ne,
        out_shape=[jax.ShapeDtypeStruct((nh, s, dh), F32),
                   jax.ShapeDtypeStruct((nh, n, dh, dh), F32)]
        + [jax.ShapeDtypeStruct((N_DEV,) + t.shape, t.dtype) for t in extra],
        scratch_shapes=[pltpu.VMEM((nh, dh, dh), F32)]
        + ([pltpu.SemaphoreType.DMA((7 * ne,)), pltpu.SemaphoreType.DMA((7 * ne,)),
            pltpu.SemaphoreType.DMA((ne,))] if ne else []),
        compiler_params=_cparams(1),
    )(q, k, v, gb, *extra)
    return out[0], out[1], list(out[2:])


def delta_bwd(name, q, k, v, gb, states, do, exchange=None):
    nh, s, dh = q.shape
    n = s // DN_CHUNK
    seq, gate, st = _delta_specs(nh, s, dh, True)
    extra = list(exchange or [])
    ne = len(extra)

    def body(*refs):
        q_ref, k_ref, v_ref, gb_ref, st_ref, do_ref = refs[:6]
        p_refs = refs[6:6 + ne]
        dq_ref, dk_ref, dv_ref, dgb_ref = refs[6 + ne:10 + ne]
        land_refs = refs[10 + ne:10 + 2 * ne]
        dstate = refs[10 + 2 * ne]
        i = pl.program_id(0)
        if ne:
            begin, finish = _chip_exchange_phases(p_refs, land_refs, refs[11 + 2 * ne:])
            pl.when(i == 0)(begin)

        @pl.when(i == 0)
        def _():
            dstate[...] = jnp.zeros_like(dstate)

        _, vjp = jax.vjp(_delta_chunk_packed, q_ref[...], k_ref[...], v_ref[...], gb_ref[...],
                         st_ref[:, 0])
        dq, dk, dv, dgb, dst = vjp((do_ref[...], dstate[...]))
        dq_ref[...] = dq
        dk_ref[...] = dk
        dv_ref[...] = dv
        dgb_ref[...] = dgb
        dstate[...] = dst
        if ne:
            pl.when(i == n - 1)(finish)

    any_spec = pl.BlockSpec(memory_space=pl.ANY)
    out = pl.pallas_call(
        body, name=name + "_bwd", grid=(n,),
        in_specs=[seq, seq, seq, gate, st, seq] + [any_spec] * ne,
        out_specs=[seq, seq, seq, gate] + [any_spec] * ne,
        out_shape=[jax.ShapeDtypeStruct((nh, s, dh), F32)] * 3
        + [jax.ShapeDtypeStruct((s, LANES), F32)]
        + [jax.ShapeDtypeStruct(t.shape, t.dtype) for t in extra],
        scratch_shapes=[pltpu.VMEM((nh, dh, dh), F32)]
        + ([pltpu.SemaphoreType.DMA((3 * ne,)), pltpu.SemaphoreType.DMA((3 * ne,)),
            pltpu.SemaphoreType.DMA((ne,))] if ne else []),
        compiler_params=_cparams(1),
    )(q, k, v, gb, states, do, *extra)
    return tuple(out[:4]), list(out[4:])


def delta_rule(name, q, k, v, gb):
    @jax.custom_vjp
    def op(q, k, v, gb):
        return delta_fwd(name, q, k, v, gb)[0]

    def op_fwd(q, k, v, gb):
        o, states, _ = delta_fwd(name, q, k, v, gb)
        return o, (q, k, v, gb, states)

    def op_bwd(res, do):
        return delta_bwd(name, *res, do)[0]

    op.defvjp(op_fwd, op_bwd)
    return op(q, k, v, gb)


def _my_place():
    return lax.axis_index("x"), lax.axis_index("y"), lax.axis_index("c")


def all_gather(name, shard):
    return all_gather_many(name, [shard])[0]


def _gather_phases(x_refs, out_refs, sems):
    n = len(x_refs)
    send_sems, recv_sems, local_sems = sems
    x, y, c = _my_place()
    me, sibling = (x, y, c), (x, y, 1 - c)
    chips = [(1 - x, y), (x, 1 - y), (1 - x, 1 - y)]

    def copy(i, k, block, to, own=False):
        px, py, pc = block
        slot = out_refs[i].at[4 * px + 2 * py + pc]
        return pltpu.make_async_remote_copy(
            src_ref=x_refs[i] if own else slot, dst_ref=slot,
            send_sem=send_sems.at[7 * i + k], recv_sem=recv_sems.at[7 * i + k],
            device_id=to, device_id_type=MESH)

    mine = [pltpu.make_async_copy(x_refs[i], out_refs[i].at[4 * x + 2 * y + c], local_sems.at[i])
            for i in range(n)]
    first = []
    for i in range(n):
        first.append(copy(i, 0, me, sibling, own=True))
        first += [copy(i, 1 + j, me, (*chip, c), own=True) for j, chip in enumerate(chips)]

    def begin():
        for cp in mine + first:
            cp.start()

    def finish():
        passed = []
        for j, chip in enumerate(chips):
            for i in range(n):
                copy(i, 1 + j, (*chip, c), me).wait_recv()
                passed.append(copy(i, 4 + j, (*chip, c), sibling))
                passed[-1].start()
        for i in range(n):
            copy(i, 0, sibling, me).wait_recv()
        for j, chip in enumerate(chips):
            for i in range(n):
                copy(i, 4 + j, (*chip, 1 - c), me).wait_recv()
        for cp in first + passed:
            cp.wait_send()
        for cp in mine:
            cp.wait()

    return begin, finish


def all_gather_many(name, shards):
    n = len(shards)

    def body(*refs):
        begin, finish = _gather_phases(refs[:n], refs[n:2 * n], refs[2 * n:])
        begin()
        finish()

    any_spec = pl.BlockSpec(memory_space=pl.ANY)
    return pl.pallas_call(
        body, name=name,
        out_shape=[jax.ShapeDtypeStruct((N_DEV,) + t.shape, t.dtype) for t in shards],
        in_specs=[any_spec] * n, out_specs=[any_spec] * n,
        scratch_shapes=[pltpu.SemaphoreType.DMA((7 * n,)), pltpu.SemaphoreType.DMA((7 * n,)),
                        pltpu.SemaphoreType.DMA((n,))],
    )(*shards)


def sibling_exchange(name, parts):
    n = len(parts)

    def body(*refs):
        p_refs, out_refs = refs[:n], refs[n:2 * n]
        send_sems, recv_sems = refs[2 * n:]
        x, y, c = _my_place()
        copies = []
        for i in range(n):
            for chip in range(4):
                copies.append(pltpu.make_async_remote_copy(
                    src_ref=p_refs[i].at[2 * chip + (1 - c)], dst_ref=out_refs[i].at[chip],
                    send_sem=send_sems.at[4 * i + chip], recv_sem=recv_sems.at[4 * i + chip],
                    device_id=(x, y, 1 - c), device_id_type=MESH))
        for cp in copies:
            cp.start()
        for cp in copies:
            cp.wait_recv()
        for cp in copies:
            cp.wait_send()

    any_spec = pl.BlockSpec(memory_space=pl.ANY)
    return pl.pallas_call(
        body, name=name,
        out_shape=[jax.ShapeDtypeStruct((4,) + t.shape[1:], t.dtype) for t in parts],
        in_specs=[any_spec] * n, out_specs=[any_spec] * n,
        scratch_shapes=[pltpu.SemaphoreType.DMA((4 * n,)), pltpu.SemaphoreType.DMA((4 * n,))],
    )(*parts)


def _chip_exchange_phases(p_refs, out_refs, sems):
    n = len(p_refs)
    send_sems, recv_sems, local_sems = sems
    x, y, c = _my_place()
    me = 2 * x + y
    peers = [(1 - x, y), (x, 1 - y), (1 - x, 1 - y)]

    def copy(i, j, landing):
        px, py = peers[j]
        return pltpu.make_async_remote_copy(
            src_ref=p_refs[i].at[2 * px + py],
            dst_ref=out_refs[i].at[(2 * px + py) if landing else me],
            send_sem=send_sems.at[3 * i + j], recv_sem=recv_sems.at[3 * i + j],
            device_id=(px, py, c), device_id_type=MESH)

    mine = [pltpu.make_async_copy(p_refs[i].at[me], out_refs[i].at[me], local_sems.at[i])
            for i in range(n)]
    copies = [copy(i, j, False) for j in range(3) for i in range(n)]

    def begin():
        for cp in mine + copies:
            cp.start()

    def finish():
        for j in range(3):
            for i in range(n):
                copy(i, j, True).wait_recv()
        for cp in copies:
            cp.wait_send()
        for cp in mine:
            cp.wait()

    return begin, finish


def chip_exchange(name, parts):
    n = len(parts)

    def body(*refs):
        begin, finish = _chip_exchange_phases(refs[:n], refs[n:2 * n], refs[2 * n:])
        begin()
        finish()

    any_spec = pl.BlockSpec(memory_space=pl.ANY)
    return pl.pallas_call(
        body, name=name,
        out_shape=[jax.ShapeDtypeStruct(t.shape, t.dtype) for t in parts],
        in_specs=[any_spec] * n, out_specs=[any_spec] * n,
        scratch_shapes=[pltpu.SemaphoreType.DMA((3 * n,)), pltpu.SemaphoreType.DMA((3 * n,)),
                        pltpu.SemaphoreType.DMA((n,))],
    )(*parts)


def pair_sum(name, mine, got, core):
    _, rows, cols = got.shape
    row_bytes = 4 * LANES * (-(-cols // LANES))
    tile = _pick(rows, [t for t in (512, 352, 256, 128, 64, 32, 16, 8)
                        if t * row_bytes <= ADAMW_BLOCK_BYTES])

    def body(core_ref, a_ref, b_ref, o_ref):
        o_ref[...] = (a_ref[...].astype(F32) + b_ref[...].astype(F32)).astype(o_ref.dtype)

    spec = pl.BlockSpec((4, tile, cols), lambda i, core_ref: (0, i, 0))
    return pl.pallas_call(
        body, name=name,
        grid_spec=pltpu.PrefetchScalarGridSpec(
            num_scalar_prefetch=1, grid=(rows // tile,),
            in_specs=[pl.BlockSpec((4, None, tile, cols), lambda i, core_ref: (0, core_ref[0], i, 0)),
                      spec],
            out_specs=spec),
        out_shape=jax.ShapeDtypeStruct(got.shape, got.dtype), compiler_params=_cparams(1),
    )(core, mine, got)


ADAMW_BLOCK_BYTES = 3 * 512 * 1024


def adamw(name, grad, w, m, v):
    rows, cols = w.shape
    stacked = grad.ndim == 3
    row_bytes = 4 * LANES * (-(-cols // LANES))
    tile = _pick(rows, [t for t in (512, 352, 256, 128, 64, 32, 16, 8)
                        if t * row_bytes <= ADAMW_BLOCK_BYTES])

    def body(g_ref, w_ref, m_ref, v_ref, go_ref, d_ref, mo_ref, vo_ref):
        if stacked:
            g = g_ref[0].astype(F32)
            for s in range(1, grad.shape[0]):
                g = g + g_ref[s].astype(F32)
        else:
            g = g_ref[...]
        m = ADAM_B1 * m_ref[...] + (1.0 - ADAM_B1) * g
        v = ADAM_B2 * v_ref[...] + (1.0 - ADAM_B2) * jnp.square(g)
        m_hat = m / (1.0 - ADAM_B1 ** ADAM_STEP)
        v_hat = v / (1.0 - ADAM_B2 ** ADAM_STEP)
        go_ref[...] = g
        d_ref[...] = -ADAM_LR * (m_hat / (jnp.sqrt(v_hat) + ADAM_EPS) + ADAM_WD * w_ref[...])
        mo_ref[...] = m
        vo_ref[...] = v

    flat = pl.BlockSpec((tile, cols), lambda i: (i, 0))
    g_spec = pl.BlockSpec((grad.shape[0], tile, cols), lambda i: (0, i, 0)) if stacked else flat
    return pl.pallas_call(
        body, name=name, grid=(rows // tile,),
        in_specs=[g_spec, flat, flat, flat], out_specs=[flat] * 4,
        out_shape=[jax.ShapeDtypeStruct((rows, cols), F32)] * 4,
        compiler_params=_cparams(1),
    )(grad, w, m, v)


def silu_rows(name, x):
    def body(x_ref, o_ref):
        o_ref[...] = _silu(x_ref[...])

    return pl.pallas_call(body, name=name, out_shape=jax.ShapeDtypeStruct(x.shape, F32))(x)


BIG = (("ffn1_wg", (D, D_FF // N_DEV), 1), ("ffn1_wu", (D, D_FF // N_DEV), 1),
       ("ffn1_wd", (D_FF // N_DEV, D), 0), ("w_in", (D, IN_COLS // N_DEV), 1),
       ("conv_w", (DN_CONV, 3 * D // N_DEV), 1), ("w_a", (D // N_DEV, D), 0),
       ("w_b", (DA_HEADS * DA_DIM, D // N_DEV), 1), ("w_o", (D // N_DEV, D), 0),
       ("ffn2_wg", (D, D_FF // N_DEV), 1), ("ffn2_wu", (D, D_FF // N_DEV), 1),
       ("ffn2_wd", (D_FF // N_DEV, D), 0))
SMALL = (("ada_b", (DEPTH, N_ADA * D)), ("ln_ffn1", (DEPTH, D)), ("ln_mix", (DEPTH, D)),
         ("ln_ffn2", (DEPTH, D)), ("a_log", (DEPTH, DN_HEADS)), ("dt_bias", (DEPTH, DN_HEADS)),
         ("dn_norm", (DEPTH, DN_DIM)), ("final_norm", (D,)))
SMALL_ROWS = 32


def _pack(arrays, rows):
    flat = jnp.concatenate([a.reshape(-1) for a in arrays])
    return jnp.pad(flat, (0, rows * D - flat.shape[0])).reshape(rows, D)


def _unpack(buf, shapes):
    flat = buf.reshape(-1)
    out, off = [], 0
    for shp in shapes:
        n = int(np.prod(shp))
        out.append(flat[off:off + n].reshape(shp))
        off += n
    return out


def _full_weights(gathered, entries):
    full = {}
    for (name, (a, b), axis), t in zip(entries, gathered):
        if axis == 1:
            full[name] = t.transpose(1, 0, 2).reshape(a, N_DEV * b)
        else:
            full[name] = t.reshape(N_DEV * a, b)
    return full


_B0, _DQ0, _GA0 = 4096, 4112, 6416
_N_WIDE = 6 * D


def _reorder_in_proj(w):
    pad = jnp.zeros((w.shape[0], IN_COLS_PAD - IN_COLS), w.dtype)
    return jnp.concatenate([w[:, :_B0], w[:, _GA0:], w[:, _DQ0:_GA0], w[:, _B0:_DQ0], pad], axis=1)


def _ffn(tag, h, ln, shift, scale, gate, w_gu, w_d):
    n = norm_mod(tag + "_norm", h, ln, shift, scale)
    a = gate_up_act(tag + "_gu", n, w_gu)
    f = linear(tag + "_down", a, w_d, F32)
    return residual(tag + "_res", h, f, gate, 0.5)


N_PRE = 5


def _mods(mod, l):
    return [mod[l, i * D:(i + 1) * D][None] for i in range(N_ADA)]


def _layer_pre(l, h, gathered, small, mod):
    w = _full_weights(gathered, BIG[:N_PRE])
    tag = f"l{l}"
    sh1, sc1, gt1, sh2, sc2 = _mods(mod, l)[:5]
    w_gu1 = _interleave_gate_up(w["ffn1_wg"], w["ffn1_wu"])
    h = _ffn(tag + "_ffn1", h, small["ln_ffn1"][l][None], sh1, sc1, gt1, w_gu1, w["ffn1_wd"])
    u = norm_mod(tag + "_mixnorm", h, small["ln_mix"][l][None], sh2, sc2)
    w_in = _reorder_in_proj(w["w_in"])
    (q_pre, k_pre, v_pre, z, gate_a, gate_b) = linear_split(
        tag + "_mix_in", u, w_in[:, :_N_WIDE], D, (1,) * 6, (BF16,) * 6)
    (da_q, da_k, da_v) = linear_split(
        tag + "_mix_in_da", u, w_in[:, _N_WIDE:_N_WIDE + 3 * DA_HEADS * DA_DIM], DA_HEADS * DA_DIM,
        (1,) * 3, (BF16,) * 3)
    ab = linear(tag + "_mix_in_ab", u, w_in[:, _N_WIDE + 3 * DA_HEADS * DA_DIM:], F32)
    cw = w["conv_w"].astype(F32).reshape(DN_CONV, 1, 3 * D)
    q, k, v = [conv_heads(f"{tag}_mix_conv_{m}", t, cw[:, :, i * D:(i + 1) * D], m)
               for i, (m, t) in enumerate(zip("qkv", (q_pre, k_pre, v_pre)))]
    pad = lambda t: jnp.pad(t, (DN_HEADS, ab.shape[1] - 2 * DN_HEADS))[None]
    gb = decay_beta(tag + "_mix_decay", ab, pad(small["a_log"][l]), pad(small["dt_bias"][l]))
    return (q, k, v, gb[:, :LANES]), (h, z, da_q, da_k, da_v, gate_a, gate_b)


def _layer_post(l, o, carry, gathered, small, mod):
    w = _full_weights(gathered, BIG[N_PRE:])
    tag = f"l{l}"
    h, z, da_q, da_k, da_v, gate_a, gate_b = carry
    gt2, sh3, sc3, gt3 = _mods(mod, l)[5:]
    o_a = gated_head_norm(tag + "_mix_gnorm", o, z, small["dn_norm"][l][None])
    y_a = linear(tag + "_mix_wa", o_a, w["w_a"], BF16)

    outs, lses = [], []
    for r in DA_DILATIONS:
        o_r, lse_r = dilated_attention(f"{tag}_mix_attn{r}",
                                       *[_sub_order(t, r) for t in (da_q, da_k, da_v)], r)
        outs.append(_seq_order(o_r, r))
        lses.append(_seq_order(lse_r, r))
    o_b = combine_patterns(tag + "_mix_comb", outs, lses)
    y_b = linear(tag + "_mix_wb", o_b, w["w_b"], BF16)

    merged = merge_gates(tag + "_mix_merge", gate_a, gate_b, y_a, y_b)
    m = linear(tag + "_mix_wo", merged, w["w_o"], F32)
    h = residual(tag + "_mixres", h, m, gt2, 1.0)
    w_gu2 = _interleave_gate_up(w["ffn2_wg"], w["ffn2_wu"])
    return _ffn(tag + "_ffn2", h, small["ln_ffn2"][l][None], sh3, sc3, gt3, w_gu2, w["ffn2_wd"])


def _layer(l, h, gathered, small, mod):
    (q, k, v, gb), carry = _layer_pre(l, h, gathered[:N_PRE], small, mod)
    o = delta_rule(f"l{l}_mix_delta", q, k, v, gb)
    return _layer_post(l, o, carry, gathered[N_PRE:], small, mod)


def _head(h, small, target):
    rows = loss_rows("loss", h, small["final_norm"][None], target)
    return 0.5 * jnp.sum(rows)


def _local_loss(x, gathered, small, mod, target):
    h = x
    for l in range(DEPTH):
        h = _layer(l, h, gathered[l], small, mod)
    return _head(h, small, target)


def kernel(x, c, ada_w, ada_b, ln_ffn1, ln_mix, ln_ffn2, ffn1_wg, ffn1_wu, ffn1_wd, w_in, conv_w, a_log, dt_bias, dn_norm, w_a, w_b, w_o, ffn2_wg, ffn2_wu, ffn2_wd, final_norm, loss_target, m_ada_w, m_ada_b, m_ln_ffn1, m_ln_mix, m_ln_ffn2, m_ffn1_wg, m_ffn1_wu, m_ffn1_wd, m_w_in, m_conv_w, m_a_log, m_dt_bias, m_dn_norm, m_w_a, m_w_b, m_w_o, m_ffn2_wg, m_ffn2_wu, m_ffn2_wd, m_final_norm, v_ada_w, v_ada_b, v_ln_ffn1, v_ln_mix, v_ln_ffn2, v_ffn1_wg, v_ffn1_wu, v_ffn1_wd, v_w_in, v_conv_w, v_a_log, v_dt_bias, v_dn_norm, v_w_a, v_w_b, v_w_o, v_ffn2_wg, v_ffn2_wu, v_ffn2_wd, v_final_norm):
    args = dict(locals())
    big_names = [n for n, _, _ in BIG]
    small_names = [n for n, _ in SMALL]
    me = 4 * lax.axis_index("x") + 2 * lax.axis_index("y") + lax.axis_index("c")
    cols = N_ADA * D // N_DEV

    c_all = all_gather("gather_c", jnp.pad(silu_rows("silu_c", c), ((0, 7), (0, 0))))[:, 0]
    mod_cols = jnp.stack([matmul(f"ada{l}", c_all, ada_w[l], "nn", F32) for l in range(DEPTH)])
    mod_cols = mod_cols + lax.dynamic_slice_in_dim(ada_b, me * cols, cols, axis=1)[:, None, :]
    mod_all = all_gather("gather_mod", mod_cols.reshape(DEPTH * N_DEV, cols))
    mod_all = mod_all.reshape(N_DEV, DEPTH, N_DEV, cols)
    mod = lax.dynamic_index_in_dim(mod_all, me, axis=2, keepdims=False)
    mod = mod.transpose(1, 0, 2).reshape(DEPTH, N_ADA * D)

    shards = [[args[n][l].astype(BF16) for n in big_names] for l in range(DEPTH)]
    pre_names, post_names = big_names[:N_PRE], big_names[N_PRE:]
    small = {n: args[n] for n in small_names if n != "ada_b"}
    target = loss_target[0]

    w0_pre = all_gather_many("gather_w0", shards[0][:N_PRE])
    pre0, vjp_pre0 = jax.vjp(functools.partial(_layer_pre, 0), x[0], w0_pre, small, mod)
    (q0, k0, v0, gb0), carry0 = pre0
    o0, states0, got = delta_fwd("l0_mix_delta", q0, k0, v0, gb0,
                                 gather=shards[0][N_PRE:] + shards[1][:N_PRE])
    w0_post, w1_pre = got[:len(post_names)], got[len(post_names):]
    h1, vjp_post0 = jax.vjp(functools.partial(_layer_post, 0), o0, carry0, w0_post, small, mod)
    pre1, vjp_pre1 = jax.vjp(functools.partial(_layer_pre, 1), h1, w1_pre, small, mod)
    (q1, k1, v1, gb1), carry1 = pre1
    o1, states1, w1_post = delta_fwd("l1_mix_delta", q1, k1, v1, gb1, gather=shards[1][N_PRE:])
    loss, vjp_post1 = jax.vjp(
        lambda o, carry, w, sm, md: _head(_layer_post(1, o, carry, w, sm, md), sm, target),
        o1, carry1, w1_post, small, mod)

    my_core = lax.axis_index("c").astype(jnp.int32).reshape(1)

    def pair_sums(tag, names, grads):
        from_sibling = sibling_exchange(f"pair_grads_{tag}", grads)
        return [pair_sum(f"pair_sum_{tag}_{n}", t.reshape(4, 2, -1, t.shape[-1]), got_n, my_core)
                for n, t, got_n in zip(names, grads, from_sibling)]

    do1, dcarry1, dw1_post, dsmall_a, dmod_a = vjp_post1(jnp.ones((), F32))
    ddelta1, landed1_post = delta_bwd("l1_mix_delta", q1, k1, v1, gb1, states1, do1,
                                      exchange=pair_sums("l1_post", post_names, list(dw1_post)))
    dh1, dw1_pre, dsmall_b, dmod_b = vjp_pre1((ddelta1, dcarry1))
    do0, dcarry0, dw0_post, dsmall_c, dmod_c = vjp_post0(dh1)
    ddelta0, landed_mid = delta_bwd(
        "l0_mix_delta", q0, k0, v0, gb0, states0, do0,
        exchange=pair_sums("mid", pre_names + post_names, list(dw1_pre) + list(dw0_post)))
    dx, dw0_pre, dsmall_d, dmod_d = vjp_pre0((ddelta0, dcarry0))
    landed0_pre = chip_exchange("scatter_grads0", pair_sums("l0_pre", pre_names, list(dw0_pre)))
    landed0 = list(landed0_pre) + list(landed_mid[N_PRE:])
    landed1 = list(landed_mid[:N_PRE]) + list(landed1_post)
    dsmall = {n: dsmall_a[n] + dsmall_b[n] + dsmall_c[n] + dsmall_d[n] for n in dsmall_a}
    dmod = dmod_a + dmod_b + dmod_c + dmod_d

    part = _pack([dmod] + [dsmall[n] for n in small_names[1:]], SMALL_ROWS)
    parts = all_gather("gather_small", part)
    sm_out = adamw("adamw_small", parts, _pack([args[n] for n in small_names], SMALL_ROWS),
                   _pack([args["m_" + n] for n in small_names], SMALL_ROWS),
                   _pack([args["v_" + n] for n in small_names], SMALL_ROWS))

    dmod_all = parts.reshape(N_DEV, -1)[:, :DEPTH * N_ADA * D].reshape(N_DEV, DEPTH, N_ADA * D)
    dmod_mine = lax.dynamic_slice_in_dim(dmod_all, me * cols, cols, axis=2)
    g_ada = jnp.stack([matmul(f"ada{l}_dw", c_all, dmod_mine[:, l], "tn", F32) for l in range(DEPTH)])
    flat2 = lambda t: t.reshape(-1, t.shape[-1])
    ada_out = adamw("adamw_ada_w", flat2(g_ada), flat2(ada_w), flat2(m_ada_w), flat2(v_ada_w))

    big_out = {}
    for n, t0, t1 in zip(big_names, landed0, landed1):
        both = jnp.concatenate([t0, t1], axis=1)
        big_out[n] = adamw("adamw_" + n, both, flat2(args[n]), flat2(args["m_" + n]), flat2(args["v_" + n]))

    small_shapes = [shp for _, shp in SMALL]
    names = ["ada_w", "ada_b", "ln_ffn1", "ln_mix", "ln_ffn2", "ffn1_wg", "ffn1_wu", "ffn1_wd", "w_in",
             "conv_w", "a_log", "dt_bias", "dn_norm", "w_a", "w_b", "w_o", "ffn2_wg", "ffn2_wu",
             "ffn2_wd", "final_norm"]
    outs = [lax.psum(loss, ("x", "y", "c")), dx[None]]
    for kind in range(4):
        table = {n: big_out[n][kind].reshape(args[n].shape) for n in big_names}
        table.update(zip(small_names, _unpack(sm_out[kind], small_shapes)))
        table["ada_w"] = ada_out[kind].reshape(ada_w.shape)
        outs += [table[n] for n in names]
    return tuple(outs)
```

```python
import functools
import math

import numpy as np
import jax
import jax.numpy as jnp
from jax import lax
from jax.experimental import pallas as pl
from jax.experimental.pallas import tpu as pltpu

F32 = jnp.float32
BF16 = jnp.bfloat16

D = 1024
SEQ = 4096
DEPTH = 2
N_DEV = 8
DN_HEADS = 8
DN_DIM = 128
DN_CHUNK = 64
DN_CONV = 4
DA_HEADS = 12
DA_DIM = 64
DA_BLOCK = 128
DA_DILATIONS = (1, 4, 16)
ALIBI_MAX_EXP = 8.0
D_FF = 2816
N_ADA = 9
NORM_EPS = 1e-6
IN_COLS = 8464
IN_COLS_PAD = 8704
ADAM_LR, ADAM_B1, ADAM_B2, ADAM_EPS, ADAM_WD, ADAM_STEP = 0.001, 0.9, 0.999, 1e-08, 0.01, 10
NEG = -1e30

VMEM_LIMIT = 56 * 1024 * 1024
LANES = 128

MESH = pl.DeviceIdType.MESH


def _cparams(n_grid):
    return pltpu.CompilerParams(dimension_semantics=("arbitrary",) * n_grid,
                                vmem_limit_bytes=VMEM_LIMIT)


def blockwise(name, f, grid, ins, outs):
    n_in, n_out = len(ins), len(outs)
    diff = [i for i, (_, _, kind) in enumerate(ins) if kind != "const"]

    def apply(*vals):
        res = f(*vals)
        return tuple(r.astype(dt) for r, (_, dt, _, _) in zip(res, outs))

    def fwd_call(*arrays):
        def body(*refs):
            res = apply(*[r[...] for r in refs[:n_in]])
            for r, v in zip(refs[n_in:], res):
                r[...] = v

        return pl.pallas_call(
            body, name=name + "_fwd", grid=grid,
            in_specs=[pl.BlockSpec(b, im) for (b, im, _) in ins],
            out_specs=[pl.BlockSpec(b, im) for (_, _, b, im) in outs],
            out_shape=[jax.ShapeDtypeStruct(s, dt) for (s, dt, _, _) in outs],
            compiler_params=_cparams(len(grid)),
        )(*arrays)

    def bwd_call(arrays, cts):
        def body(*refs):
            in_refs, ct_refs = refs[:n_in], refs[n_in:n_in + n_out]
            g_refs = refs[n_in + n_out:]
            vals = [r[...] for r in in_refs]

            def fd(*dvals):
                full = list(vals)
                for i, v in zip(diff, dvals):
                    full[i] = v
                return apply(*full)

            _, vjp = jax.vjp(fd, *[vals[i] for i in diff])
            grads = vjp(tuple(r[...] for r in ct_refs))
            first = functools.reduce(jnp.logical_and,
                                     [pl.program_id(a) == 0 for a in range(len(grid))])
            for g_ref, g, i in zip(g_refs, grads, diff):
                if ins[i][2] == "acc":
                    @pl.when(first)
                    def _(g_ref=g_ref):
                        g_ref[...] = jnp.zeros_like(g_ref)
                    g_ref[...] += g.astype(F32)
                else:
                    g_ref[...] = g.astype(g_ref.dtype)

        g_shapes = [jax.ShapeDtypeStruct(arrays[i].shape,
                                         F32 if ins[i][2] == "acc" else arrays[i].dtype)
                    for i in diff]
        return pl.pallas_call(
            body, name=name + "_bwd", grid=grid,
            in_specs=([pl.BlockSpec(b, im) for (b, im, _) in ins]
                      + [pl.BlockSpec(b, im) for (_, _, b, im) in outs]),
            out_specs=[pl.BlockSpec(ins[i][0], ins[i][1]) for i in diff],
            out_shape=g_shapes,
            compiler_params=_cparams(len(grid)),
        )(*arrays, *cts)

    @jax.custom_vjp
    def op(*arrays):
        return tuple(fwd_call(*arrays))

    def op_fwd(*arrays):
        return tuple(fwd_call(*arrays)), arrays

    def op_bwd(arrays, cts):
        grads = bwd_call(arrays, cts)
        full = [None] * n_in
        for i, g in zip(diff, grads):
            full[i] = g.astype(arrays[i].dtype)
        return tuple(full)

    op.defvjp(op_fwd, op_bwd)
    op.backward = bwd_call
    return op


def _pick(n, cands):
    for c in cands:
        if n % c == 0:
            return c
    return n


def matmul(name, a, b, form, out_dtype):
    if form == "nn":
        (m, k), (_, n) = a.shape, b.shape
    elif form == "nt":
        (m, k), (n, _) = a.shape, b.shape
    else:
        (k, m), (_, n) = a.shape, b.shape
    tm = _pick(m, (1408, 1024, 512, 256, 128, 8))
    tn = _pick(n, (1408, 1024, 512, 384, 256, 128))
    tk = _pick(k, (1024, 1408, 768, 512, 384, 256, 128, 8))
    nk = k // tk
    a_spec = (pl.BlockSpec((tk, tm), lambda i, j, kk: (kk, i)) if form == "tn"
              else pl.BlockSpec((tm, tk), lambda i, j, kk: (i, kk)))
    b_spec = (pl.BlockSpec((tn, tk), lambda i, j, kk: (j, kk)) if form == "nt"
              else pl.BlockSpec((tk, tn), lambda i, j, kk: (kk, j)))
    dims = {"nn": (((1,), (0,)), ((), ())), "nt": (((1,), (1,)), ((), ())),
            "tn": (((0,), (0,)), ((), ()))}[form]

    def body(a_ref, b_ref, o_ref, acc_ref):
        kk = pl.program_id(2)
        part = lax.dot_general(a_ref[...].astype(BF16), b_ref[...].astype(BF16), dims,
                               preferred_element_type=F32)
        if nk == 1:
            o_ref[...] = part.astype(o_ref.dtype)
            return

        @pl.when(kk == 0)
        def _():
            acc_ref[...] = part

        @pl.when(kk > 0)
        def _():
            acc_ref[...] += part

        @pl.when(kk == nk - 1)
        def _():
            o_ref[...] = acc_ref[...].astype(o_ref.dtype)

    return pl.pallas_call(
        body, name=name, grid=(m // tm, n // tn, nk),
        in_specs=[a_spec, b_spec],
        out_specs=pl.BlockSpec((tm, tn), lambda i, j, kk: (i, j)),
        out_shape=jax.ShapeDtypeStruct((m, n), out_dtype),
        scratch_shapes=[pltpu.VMEM((tm, tn), F32)],
        compiler_params=_cparams(3),
    )(a, b)


def linear(name, x, w, out_dtype):
    @jax.custom_vjp
    def op(x, w):
        return matmul(name + "_y", x, w, "nn", out_dtype)

    def op_fwd(x, w):
        return op(x, w), (x, w)

    def op_bwd(res, dy):
        x, w = res
        dx = matmul(name + "_dx", dy, w, "nt", x.dtype)
        dw = matmul(name + "_dw", x, dy, "tn", w.dtype)
        return dx, dw

    op.defvjp(op_fwd, op_bwd)
    return op(x, w)


def linear_split(name, x, w, tile, seg_tiles, out_dtypes):
    SPLIT_TILE = tile
    m, k = x.shape
    n = w.shape[1]
    nt = n // SPLIT_TILE
    starts = [sum(seg_tiles[:s]) for s in range(len(seg_tiles))]
    assert sum(seg_tiles) == nt
    tm = _pick(m, (1024, 512, 256, 128, 8))

    def inside(j, s):
        return jnp.logical_and(j >= starts[s], j < starts[s] + seg_tiles[s])

    def local(j, s):
        return jnp.clip(j - starts[s], 0, seg_tiles[s] - 1)

    def fwd_call(x, w):
        def body(x_ref, w_ref, *o_refs):
            j = pl.program_id(1)
            y = jnp.dot(x_ref[...].astype(BF16), w_ref[...].astype(BF16), preferred_element_type=F32)
            for s, o_ref in enumerate(o_refs):
                @pl.when(inside(j, s))
                def _(o_ref=o_ref):
                    o_ref[...] = y.astype(o_ref.dtype)

        return pl.pallas_call(
            body, name=name + "_y", grid=(m // tm, nt),
            in_specs=[pl.BlockSpec((tm, k), lambda i, j: (i, 0)),
                      pl.BlockSpec((k, SPLIT_TILE), lambda i, j: (0, j))],
            out_specs=[pl.BlockSpec((tm, SPLIT_TILE), lambda i, j, s=s: (i, local(j, s)))
                       for s in range(len(seg_tiles))],
            out_shape=[jax.ShapeDtypeStruct((m, t * SPLIT_TILE), dt)
                       for t, dt in zip(seg_tiles, out_dtypes)],
            compiler_params=_cparams(2),
        )(x, w)

    def dx_call(dys, w):
        def body(*refs):
            dy_refs, w_ref, o_ref, acc_ref = refs[:-3], refs[-3], refs[-2], refs[-1]
            j = pl.program_id(1)
            for s, dy_ref in enumerate(dy_refs):
                @pl.when(inside(j, s))
                def _(dy_ref=dy_ref):
                    part = lax.dot_general(dy_ref[...].astype(BF16), w_ref[...].astype(BF16),
                                           (((1,), (1,)), ((), ())), preferred_element_type=F32)

                    @pl.when(j == 0)
                    def _():
                        acc_ref[...] = part

                    @pl.when(j > 0)
                    def _():
                        acc_ref[...] += part

            @pl.when(j == nt - 1)
            def _():
                o_ref[...] = acc_ref[...].astype(o_ref.dtype)

        return pl.pallas_call(
            body, name=name + "_dx", grid=(m // tm, nt),
            in_specs=[pl.BlockSpec((tm, SPLIT_TILE), lambda i, j, s=s: (i, local(j, s)))
                      for s in range(len(seg_tiles))]
            + [pl.BlockSpec((k, SPLIT_TILE), lambda i, j: (0, j))],
            out_specs=pl.BlockSpec((tm, k), lambda i, j: (i, 0)),
            out_shape=jax.ShapeDtypeStruct((m, k), x.dtype),
            scratch_shapes=[pltpu.VMEM((tm, k), F32)],
            compiler_params=_cparams(2),
        )(*dys, w)

    def dw_call(x, dys):
        nm = m // tm

        def body(*refs):
            x_ref, dy_refs, o_ref, acc_ref = refs[0], refs[1:-2], refs[-2], refs[-1]
            j, kk = pl.program_id(0), pl.program_id(1)
            for s, dy_ref in enumerate(dy_refs):
                @pl.when(inside(j, s))
                def _(dy_ref=dy_ref):
                    rows = x_ref[pl.ds(pl.multiple_of(kk * tm, tm), tm), :]
                    part = lax.dot_general(rows.astype(BF16), dy_ref[...].astype(BF16),
                                           (((0,), (0,)), ((), ())), preferred_element_type=F32)

                    @pl.when(kk == 0)
                    def _():
                        acc_ref[...] = part

                    @pl.when(kk > 0)
                    def _():
                        acc_ref[...] += part

            @pl.when(kk == nm - 1)
            def _():
                o_ref[...] = acc_ref[...].astype(o_ref.dtype)

        return pl.pallas_call(
            body, name=name + "_dw", grid=(nt, nm),
            in_specs=[pl.BlockSpec((m, k), lambda j, kk: (0, 0))]
            + [pl.BlockSpec((tm, SPLIT_TILE),
                            lambda j, kk, s=s: (jnp.where(inside(j, s), kk, 0), local(j, s)))
               for s in range(len(seg_tiles))],
            out_specs=pl.BlockSpec((k, SPLIT_TILE), lambda j, kk: (0, j)),
            out_shape=jax.ShapeDtypeStruct((k, n), w.dtype),
            scratch_shapes=[pltpu.VMEM((k, SPLIT_TILE), F32)],
            compiler_params=_cparams(2),
        )(x, *dys)

    @jax.custom_vjp
    def op(x, w):
        return tuple(fwd_call(x, w))

    def op_fwd(x, w):
        return op(x, w), (x, w)

    def op_bwd(res, dys):
        x, w = res
        return dx_call(dys, w), dw_call(x, dys)

    op.defvjp(op_fwd, op_bwd)
    return op(x, w)


@jax.custom_vjp
def _sigmoid(x):
    return 1.0 / (1.0 + jnp.exp(-x))


def _sigmoid_fwd(x):
    s = _sigmoid(x)
    return s, s


def _sigmoid_bwd(s, ct):
    return (ct * (s * (1.0 - s)),)


_sigmoid.defvjp(_sigmoid_fwd, _sigmoid_bwd)


@jax.custom_vjp
def _silu(x):
    return x * _sigmoid(x)


def _silu_fwd(x):
    s = _sigmoid(x)
    return x * s, (x, s)


def _silu_bwd(res, ct):
    x, s = res
    return (ct * (s * (1.0 + x * (1.0 - s))),)


_silu.defvjp(_silu_fwd, _silu_bwd)


def _softplus(x):
    return jnp.maximum(x, 0.0) + jnp.log(1.0 + jnp.exp(-jnp.abs(x)))


def _rms(x):
    return x * lax.rsqrt(jnp.mean(x * x, axis=-1, keepdims=True) + NORM_EPS)


ROW_TILE = 512


def _row(i):
    return (i, 0)


def _fixed(*_):
    return (0, 0)


def norm_mod(name, h, ln, shift, scale):
    s, d = h.shape

    def f(h, ln, sh, sc):
        return ((_rms(h) * ln) * (1.0 + sc) + sh,)

    op = blockwise(name, f, (s // ROW_TILE,),
                   [((ROW_TILE, d), _row, "tile")] + [((1, d), _fixed, "acc")] * 3,
                   [((s, d), BF16, (ROW_TILE, d), _row)])
    return op(h, ln, shift, scale)[0]


FF_TILE = 1408


def _interleave_gate_up(wg, wu):
    parts = []
    for j in range(wg.shape[1] // FF_TILE):
        parts += [wg[:, j * FF_TILE:(j + 1) * FF_TILE], wu[:, j * FF_TILE:(j + 1) * FF_TILE]]
    return jnp.concatenate(parts, axis=1)


def _swiglu(gu):
    g, u = gu[:, :FF_TILE].astype(F32), gu[:, FF_TILE:].astype(F32)
    return _silu(g) * u


def _swiglu_op(name, s, ff):
    return blockwise(name, lambda gu: (_swiglu(gu),), (s // ROW_TILE, ff // FF_TILE),
                     [((ROW_TILE, 2 * FF_TILE), lambda i, j: (i, j), "tile")],
                     [((s, ff), BF16, (ROW_TILE, FF_TILE), lambda i, j: (i, j))])


def gate_up_act(name, x, w):
    s, k = x.shape
    ff = w.shape[1] // 2
    tm = _pick(s, (512, 256, 128, 8))

    def fwd_call(x, w):
        def body(x_ref, w_ref, gu_ref, a_ref):
            gu = jnp.dot(x_ref[...], w_ref[...], preferred_element_type=F32).astype(BF16)
            gu_ref[...] = gu
            a_ref[...] = _swiglu(gu).astype(BF16)

        return pl.pallas_call(
            body, name=name + "_y", grid=(ff // FF_TILE, s // tm),
            in_specs=[pl.BlockSpec((tm, k), lambda j, i: (i, 0)),
                      pl.BlockSpec((k, 2 * FF_TILE), lambda j, i: (0, j))],
            out_specs=[pl.BlockSpec((tm, 2 * FF_TILE), lambda j, i: (i, j)),
                       pl.BlockSpec((tm, FF_TILE), lambda j, i: (i, j))],
            out_shape=[jax.ShapeDtypeStruct((s, 2 * ff), BF16), jax.ShapeDtypeStruct((s, ff), BF16)],
            compiler_params=_cparams(2),
        )(x, w)

    @jax.custom_vjp
    def op(x, w):
        return fwd_call(x, w)[1]

    def op_fwd(x, w):
        gu, a = fwd_call(x, w)
        return a, (x, w, gu)

    def op_bwd(res, da):
        x, w, gu = res
        dgu = _swiglu_op(name + "_act", s, ff).backward((gu,), (da,))[0]
        return (matmul(name + "_dx", dgu, w, "nt", x.dtype), matmul(name + "_dw", x, dgu, "tn", w.dtype))

    op.defvjp(op_fwd, op_bwd)
    return op(x, w)


def residual(name, h, y, gate, weight):
    s, d = h.shape

    def f(h, y, gate):
        return (h + (weight * gate) * y,)

    op = blockwise(name, f, (s // ROW_TILE,),
                   [((ROW_TILE, d), _row, "tile"), ((ROW_TILE, d), _row, "tile"),
                    ((1, d), _fixed, "acc")],
                   [((s, d), F32, (ROW_TILE, d), _row)])
    branch = blockwise(name + "_branch", lambda y, gate: ((weight * gate) * y,), (s // ROW_TILE,),
                       [((ROW_TILE, d), _row, "tile"), ((1, d), _fixed, "acc")],
                       [((s, d), F32, (ROW_TILE, d), _row)])

    @jax.custom_vjp
    def res(h, y, gate):
        return op(h, y, gate)[0]

    def res_fwd(h, y, gate):
        return res(h, y, gate), (y, gate)

    def res_bwd(saved, ct):
        y, gate = saved
        dy, dgate = branch.backward((y, gate), (ct,))
        return ct, dy.astype(y.dtype), dgate.astype(gate.dtype)

    res.defvjp(res_fwd, res_bwd)
    return res(h, y, gate)


def merge_gates(name, ga, gb, ya, yb):
    s, d = ya.shape

    def f(ga, gb, ya, yb):
        return (_sigmoid(ga.astype(F32)) * ya + _sigmoid(gb.astype(F32)) * yb,)

    op = blockwise(name, f, (s // ROW_TILE,), [((ROW_TILE, d), _row, "tile")] * 4,
                   [((s, d), BF16, (ROW_TILE, d), _row)])
    return op(ga, gb, ya, yb)[0]


def loss_rows(name, h, g, target):
    s, d = h.shape

    def f(h, g, t):
        err = _rms(h) * g - t
        return (jnp.mean(err * err, axis=-1, keepdims=True),)

    op = blockwise(name, f, (s // ROW_TILE,),
                   [((ROW_TILE, d), _row, "tile"), ((1, d), _fixed, "acc"),
                    ((ROW_TILE, d), _row, "const")],
                   [((s, 1), F32, (ROW_TILE, 1), _row)])
    return op(h, g, target)[0]


def decay_beta(name, ab, a_log_pad, dt_bias_pad):
    s, n = ab.shape

    def f(ab, a_log, dt_bias):
        lane = lax.broadcasted_iota(jnp.int32, ab.shape, 1)
        beta = _sigmoid(ab)
        g = -jnp.exp(a_log) * _softplus(ab + dt_bias)
        return (jnp.where(lane < DN_HEADS, beta, jnp.where(lane < 2 * DN_HEADS, g, 0.0)),)

    op = blockwise(name, f, (s // ROW_TILE,),
                   [((ROW_TILE, n), _row, "tile"), ((1, n), _fixed, "acc"), ((1, n), _fixed, "acc")],
                   [((s, n), F32, (ROW_TILE, n), _row)])
    return op(ab, a_log_pad, dt_bias_pad)[0]


def _shift_rows(x, k):
    n = x.shape[0]

    @jax.custom_vjp
    def shift(x):
        row = lax.broadcasted_iota(jnp.int32, x.shape, 0)
        return jnp.where(row >= k, pltpu.roll(x, k, 0), 0.0)

    def shift_fwd(x):
        return shift(x), None

    def shift_bwd(_, g):
        row = lax.broadcasted_iota(jnp.int32, g.shape, 0)
        return (jnp.where(row < n - k, pltpu.roll(g, n - k, 0), 0.0),)

    shift.defvjp(shift_fwd, shift_bwd)
    return shift(x)


def conv_heads(name, x, w, mode):
    s, width = x.shape
    nh = width // LANES

    def f(x, w):
        x = x.astype(F32)
        y = w[DN_CONV - 1] * x
        for j in range(DN_CONV - 1):
            y = y + w[j] * _shift_rows(x, DN_CONV - 1 - j)
        y = _silu(y)
        if mode != "v":
            y = y * lax.rsqrt(jnp.sum(y * y, axis=-1, keepdims=True) + NORM_EPS)
        if mode == "q":
            y = y * (DN_DIM ** -0.5)
        return (y[None],)

    op = blockwise(name, f, (nh,),
                   [((s, LANES), lambda j: (0, j), "tile"),
                    ((DN_CONV, 1, LANES), lambda j: (0, 0, j), "tile")],
                   [((nh, s, LANES), F32, (1, s, LANES), lambda j: (j, 0, 0))])
    return op(x, w)[0]


def gated_head_norm(name, o, z, w):
    nh, s, dh = o.shape

    def f(o, z, w):
        return (_rms(o[0]) * w * _silu(z.astype(F32)),)

    op = blockwise(name, f, (s // ROW_TILE, nh),
                   [((1, ROW_TILE, dh), lambda i, h: (h, i, 0), "tile"),
                    ((ROW_TILE, dh), lambda i, h: (i, h), "tile"),
                    ((1, dh), lambda i, h: (0, 0), "acc")],
                   [((s, nh * dh), BF16, (ROW_TILE, dh), lambda i, h: (i, h))])
    return op(o, z, w)[0]


def _mm(a, b, ca, cb):
    return _bmm(a[None], b[None], ca + 1, cb + 1)[0]


def dilated_attention(name, q, k, v, dilation):
    s, width = q.shape
    nblk = s // DA_BLOCK
    per_sub = nblk // dilation
    slopes = [dilation * 2.0 ** (-ALIBI_MAX_EXP * (h + 1) / DA_HEADS) for h in range(DA_HEADS)]

    def f(q, kp, kc, vp, vc):
        first = (pl.program_id(0) % per_sub) == 0
        qi = lax.broadcasted_iota(jnp.int32, (DA_BLOCK, 2 * DA_BLOCK), 0)
        ki = lax.broadcasted_iota(jnp.int32, (DA_BLOCK, 2 * DA_BLOCK), 1)
        steps = (qi + DA_BLOCK - ki).astype(F32)
        lowest = qi + first.astype(jnp.int32) * (DA_BLOCK - qi)
        valid = jnp.logical_and(ki >= lowest, ki <= qi + DA_BLOCK)
        top = lax.broadcasted_iota(jnp.int32, (DA_BLOCK, LANES), 1) < DA_DIM
        o_parts, lse_parts = [], []
        for pair in range(width // LANES):
            cols = slice(pair * LANES, (pair + 1) * LANES)
            q2 = jnp.concatenate([jnp.where(top, q[:, cols], 0.0), jnp.where(top, 0.0, q[:, cols])], axis=0)
            k2 = jnp.concatenate([kp[:, cols], kc[:, cols]], axis=0)
            v2 = jnp.concatenate([vp[:, cols], vc[:, cols]], axis=0)
            bias = jnp.concatenate([jnp.where(valid, -slopes[2 * pair + half] * steps, NEG)
                                    for half in range(2)], axis=0)
            sc = _mm(q2, k2, 1, 1) * (DA_DIM ** -0.5) + bias
            mx = jnp.max(sc, axis=-1, keepdims=True)
            p = jnp.exp(sc - mx)
            l = jnp.sum(p, axis=-1, keepdims=True)
            o2 = _mm(p / l, v2, 1, 0)
            lse2 = mx + jnp.log(l)
            o_parts.append(jnp.where(top, o2[:DA_BLOCK], o2[DA_BLOCK:]))
            lse_parts.append(jnp.where(top, lse2[:DA_BLOCK], lse2[DA_BLOCK:]))
        return jnp.concatenate(o_parts, axis=1), jnp.concatenate(lse_parts, axis=1)

    blk = (DA_BLOCK, width)
    cur_map = lambda j: (j, 0)
    prev_map = lambda j: (jnp.maximum(j - 1, 0), 0)
    op = blockwise(
        name, f, (nblk,),
        [(blk, cur_map, "tile"), (blk, prev_map, "tile"), (blk, cur_map, "tile"),
         (blk, prev_map, "tile"), (blk, cur_map, "tile")],
        [((s, width), BF16, blk, cur_map), ((s, width), F32, blk, cur_map)])

    @jax.custom_vjp
    def attn(q, k, v):
        return op(q, k, k, v, v)

    def attn_fwd(q, k, v):
        return attn(q, k, v), (q, k, v)

    def attn_bwd(res, cts):
        q, k, v = res
        _, vjp = jax.vjp(op, q, k, k, v, v)
        dq, dkp, dkc, dvp, dvc = vjp(cts)
        fill = lambda t: t.at[s - DA_BLOCK:].set(0.0)
        return dq, dkc + fill(dkp), dvc + fill(dvp)

    attn.defvjp(attn_fwd, attn_bwd)
    return attn(q, k, v)


def combine_patterns(name, outs, lses):
    s, width = outs[0].shape
    n = len(outs)

    def f(*vals):
        o, lse = vals[:n], vals[n:]
        mx = functools.reduce(jnp.maximum, lse)
        e = [jnp.exp(t - mx) for t in lse]
        return (sum(ei * oi for ei, oi in zip(e, o)) / sum(e),)

    tile = ROW_TILE // 2
    op = blockwise(name, f, (s // tile,), [((tile, width), _row, "tile")] * (2 * n),
                   [((s, width), BF16, (tile, width), _row)])
    return op(*outs, *lses)[0]


def _sub_order(t, r):
    if r == 1:
        return t
    s, c = t.shape
    return t.reshape(s // r, r, c).transpose(1, 0, 2).reshape(s, c)


def _seq_order(t, r):
    if r == 1:
        return t
    s, c = t.shape
    return t.reshape(r, s // r, c).transpose(1, 0, 2).reshape(s, c)


def _raw_bmm(a, b, ca, cb):
    return lax.dot_general(a.astype(BF16), b.astype(BF16), (((ca,), (cb,)), ((0,), (0,))),
                           preferred_element_type=F32)


def _split(a):
    hi = a.astype(BF16)
    return hi, (a - hi.astype(F32)).astype(BF16)


def _passes_bmm(a, b, ca, cb, passes):
    if passes == 1:
        return _raw_bmm(a, b, ca, cb)
    (a_hi, a_lo), (b_hi, b_lo) = _split(a), _split(b)
    return _raw_bmm(a_hi, b_hi, ca, cb) + (_raw_bmm(a_hi, b_lo, ca, cb) + _raw_bmm(a_lo, b_hi, ca, cb))


def _bmm(a, b, ca, cb, passes=1):
    fa, fb = 3 - ca, 3 - cb

    @jax.custom_vjp
    def mm(a, b):
        return _passes_bmm(a, b, ca, cb, passes)

    def mm_fwd(a, b):
        return mm(a, b), (a, b)

    def mm_bwd(res, ct):
        a, b = res
        da = (_passes_bmm(ct, b, 2, fb, passes) if ca == 2
              else _passes_bmm(b, ct, fb, 2, passes))
        db = (_passes_bmm(a, ct, fa, 1, passes) if cb == 1
              else _passes_bmm(ct, a, 1, fa, passes))
        return da.astype(a.dtype), db.astype(b.dtype)

    mm.defvjp(mm_fwd, mm_bwd)
    return mm(a, b)


def _delta_chunk(q, k, v, gcol, grow, bcol, state):
    c = q.shape[1]
    ii = lax.broadcasted_iota(jnp.int32, (1, c, c), 1)
    jj = lax.broadcasted_iota(jnp.int32, (1, c, c), 2)
    incl, strict = ii >= jj, ii > jj
    gc_col = jnp.sum(jnp.where(incl, grow, 0.0), axis=2, keepdims=True)
    gc_row = jnp.sum(jnp.where(ii <= jj, gcol, 0.0), axis=1, keepdims=True)
    decay = jnp.where(incl, jnp.exp(jnp.where(incl, gc_col - gc_row, 0.0)), 0.0)
    kb, vb = k * bcol, v * bcol
    m = jnp.where(strict, _bmm(kb, k, 2, 2) * decay, 0.0)
    eye = (ii == jj).astype(F32)
    p = -m
    inv = eye + p
    for _ in range(int(math.log2(c)) - 1):
        p = _bmm(p, p, 2, 1, 3)
        inv = inv + _bmm(inv, p, 2, 1, 3)
    e_col = jnp.exp(gc_col)
    u = _bmm(inv, vb, 2, 1)
    w = _bmm(inv, kb * e_col, 2, 1)
    qk = _bmm(q, k, 2, 2) * decay
    v_new = u - _bmm(w, state, 2, 1)
    o = _bmm(q * e_col, state, 2, 1) + _bmm(qk, v_new, 2, 1)
    g_last = jnp.sum(grow, axis=2, keepdims=True)
    new_state = state * jnp.exp(g_last) + _bmm(k * jnp.exp(g_last - gc_col), v_new, 1, 1)
    return o, new_state


def _delta_chunk_packed(q, k, v, gb, state):
    nh, c = q.shape[0], q.shape[1]
    lane = lax.broadcasted_iota(jnp.int32, gb.shape, 1)
    eye = (lax.broadcasted_iota(jnp.int32, (c, c), 0) == lax.broadcasted_iota(jnp.int32, (c, c), 1))
    column = lambda l: jnp.sum(jnp.where(lane == l, gb, 0.0), axis=1, keepdims=True)
    heads = lambda parts: jnp.concatenate([t[None] for t in parts], axis=0)
    bcol = heads([column(h) for h in range(nh)])
    gcols = [column(nh + h) for h in range(nh)]
    gcol = heads(gcols)
    grow = heads([jnp.sum(jnp.where(eye, g, 0.0), axis=0, keepdims=True) for g in gcols])
    return _delta_chunk(q, k, v, gcol, grow, bcol, state)


def _delta_specs(nh, s, dh, rev):
    c, n = DN_CHUNK, s // DN_CHUNK
    t = (lambda i: n - 1 - i) if rev else (lambda i: i)
    seq = pl.BlockSpec((nh, c, dh), lambda i: (0, t(i), 0))
    gate = pl.BlockSpec((c, LANES), lambda i: (t(i), 0))
    st = pl.BlockSpec((nh, 1, dh, dh), lambda i: (0, t(i), 0, 0))
    return seq, gate, st


def delta_fwd(name, q, k, v, gb, gather=None):
    nh, s, dh = q.shape
    n = s // DN_CHUNK
    seq, gate, st = _delta_specs(nh, s, dh, False)
    extra = list(gather or [])
    ne = len(extra)

    def body(*refs):
        q_ref, k_ref, v_ref, gb_ref = refs[:4]
        x_refs = refs[4:4 + ne]
        o_ref, st_ref = refs[4 + ne:6 + ne]
        out_refs = refs[6 + ne:6 + 2 * ne]
        state = refs[6 + 2 * ne]
        i = pl.program_id(0)
        if ne:
            begin, finish = _gather_phases(x_refs, out_refs, refs[7 + 2 * ne:])
            pl.when(i == 0)(begin)

        @pl.when(i == 0)
        def _():
            state[...] = jnp.zeros_like(state)

        st_ref[:, 0] = state[...]
        o, new_state = _delta_chunk_packed(q_ref[...], k_ref[...], v_ref[...], gb_ref[...], state[...])
        o_ref[...] = o
        state[...] = new_state
        if ne:
            pl.when(i == n - 1)(finish)

    any_spec = pl.BlockSpec(memory_space=pl.ANY)
    out = pl.pallas_call(
        body, name=name + "_fwd", grid=(n,),
        in_specs=[seq, seq, seq, gate] + [any_spec] * ne,
        out_specs=[seq, st] + [any_spec] * ne,
        out_shape=[jax.ShapeDtypeStruct((nh, s, dh), F32),
                   jax.ShapeDtypeStruct((nh, n, dh, dh), F32)]
        + [jax.ShapeDtypeStruct((N_DEV,) + t.shape, t.dtype) for t in extra],
        scratch_shapes=[pltpu.VMEM((nh, dh, dh), F32)]
        + ([pltpu.SemaphoreType.DMA((7 * ne,)), pltpu.SemaphoreType.DMA((7 * ne,)),
            pltpu.SemaphoreType.DMA((ne,))] if ne else []),
        compiler_params=_cparams(1),
    )(q, k, v, gb, *extra)
    return out[0], out[1], list(out[2:])


def delta_bwd(name, q, k, v, gb, states, do, exchange=None):
    nh, s, dh = q.shape
    n = s // DN_CHUNK
    seq, gate, st = _delta_specs(nh, s, dh, True)
    extra = list(exchange or [])
    ne = len(extra)

    def body(*refs):
        q_ref, k_ref, v_ref, gb_ref, st_ref, do_ref = refs[:6]
        p_refs = refs[6:6 + ne]
        dq_ref, dk_ref, dv_ref, dgb_ref = refs[6 + ne:10 + ne]
        land_refs = refs[10 + ne:10 + 2 * ne]
        dstate = refs[10 + 2 * ne]
        i = pl.program_id(0)
        if ne:
            begin, finish = _chip_exchange_phases(p_refs, land_refs, refs[11 + 2 * ne:])
            pl.when(i == 0)(begin)

        @pl.when(i == 0)
        def _():
            dstate[...] = jnp.zeros_like(dstate)

        _, vjp = jax.vjp(_delta_chunk_packed, q_ref[...], k_ref[...], v_ref[...], gb_ref[...],
                         st_ref[:, 0])
        dq, dk, dv, dgb, dst = vjp((do_ref[...], dstate[...]))
        dq_ref[...] = dq
        dk_ref[...] = dk
        dv_ref[...] = dv
        dgb_ref[...] = dgb
        dstate[...] = dst
        if ne:
            pl.when(i == n - 1)(finish)

    any_spec = pl.BlockSpec(memory_space=pl.ANY)
    out = pl.pallas_call(
        body, name=name + "_bwd", grid=(n,),
        in_specs=[seq, seq, seq, gate, st, seq] + [any_spec] * ne,
        out_specs=[seq, seq, seq, gate] + [any_spec] * ne,
        out_shape=[jax.ShapeDtypeStruct((nh, s, dh), F32)] * 3
        + [jax.ShapeDtypeStruct((s, LANES), F32)]
        + [jax.ShapeDtypeStruct(t.shape, t.dtype) for t in extra],
        scratch_shapes=[pltpu.VMEM((nh, dh, dh), F32)]
        + ([pltpu.SemaphoreType.DMA((3 * ne,)), pltpu.SemaphoreType.DMA((3 * ne,)),
            pltpu.SemaphoreType.DMA((ne,))] if ne else []),
        compiler_params=_cparams(1),
    )(q, k, v, gb, states, do, *extra)
    return tuple(out[:4]), list(out[4:])


def delta_rule(name, q, k, v, gb):
    @jax.custom_vjp
    def op(q, k, v, gb):
        return delta_fwd(name, q, k, v, gb)[0]

    def op_fwd(q, k, v, gb):
        o, states, _ = delta_fwd(name, q, k, v, gb)
        return o, (q, k, v, gb, states)

    def op_bwd(res, do):
        return delta_bwd(name, *res, do)[0]

    op.defvjp(op_fwd, op_bwd)
    return op(q, k, v, gb)


def _my_place():
    return lax.axis_index("x"), lax.axis_index("y"), lax.axis_index("c")


def all_gather(name, shard):
    return all_gather_many(name, [shard])[0]


def _gather_phases(x_refs, out_refs, sems):
    n = len(x_refs)
    send_sems, recv_sems, local_sems = sems
    x, y, c = _my_place()
    me, sibling = (x, y, c), (x, y, 1 - c)
    chips = [(1 - x, y), (x, 1 - y), (1 - x, 1 - y)]

    def copy(i, k, block, to, own=False):
        px, py, pc = block
        slot = out_refs[i].at[4 * px + 2 * py + pc]
        return pltpu.make_async_remote_copy(
            src_ref=x_refs[i] if own else slot, dst_ref=slot,
            send_sem=send_sems.at[7 * i + k], recv_sem=recv_sems.at[7 * i + k],
            device_id=to, device_id_type=MESH)

    mine = [pltpu.make_async_copy(x_refs[i], out_refs[i].at[4 * x + 2 * y + c], local_sems.at[i])
            for i in range(n)]
    first = []
    for i in range(n):
        first.append(copy(i, 0, me, sibling, own=True))
        first += [copy(i, 1 + j, me, (*chip, c), own=True) for j, chip in enumerate(chips)]

    def begin():
        for cp in mine + first:
            cp.start()

    def finish():
        passed = []
        for j, chip in enumerate(chips):
            for i in range(n):
                copy(i, 1 + j, (*chip, c), me).wait_recv()
                passed.append(copy(i, 4 + j, (*chip, c), sibling))
                passed[-1].start()
        for i in range(n):
            copy(i, 0, sibling, me).wait_recv()
        for j, chip in enumerate(chips):
            for i in range(n):
                copy(i, 4 + j, (*chip, 1 - c), me).wait_recv()
        for cp in first + passed:
            cp.wait_send()
        for cp in mine:
            cp.wait()

    return begin, finish


def all_gather_many(name, shards):
    n = len(shards)

    def body(*refs):
        begin, finish = _gather_phases(refs[:n], refs[n:2 * n], refs[2 * n:])
        begin()
        finish()

    any_spec = pl.BlockSpec(memory_space=pl.ANY)
    return pl.pallas_call(
        body, name=name,
        out_shape=[jax.ShapeDtypeStruct((N_DEV,) + t.shape, t.dtype) for t in shards],
        in_specs=[any_spec] * n, out_specs=[any_spec] * n,
        scratch_shapes=[pltpu.SemaphoreType.DMA((7 * n,)), pltpu.SemaphoreType.DMA((7 * n,)),
                        pltpu.SemaphoreType.DMA((n,))],
    )(*shards)


def sibling_exchange(name, parts):
    n = len(parts)

    def body(*refs):
        p_refs, out_refs = refs[:n], refs[n:2 * n]
        send_sems, recv_sems = refs[2 * n:]
        x, y, c = _my_place()
        copies = []
        for i in range(n):
            for chip in range(4):
                copies.append(pltpu.make_async_remote_copy(
                    src_ref=p_refs[i].at[2 * chip + (1 - c)], dst_ref=out_refs[i].at[chip],
                    send_sem=send_sems.at[4 * i + chip], recv_sem=recv_sems.at[4 * i + chip],
                    device_id=(x, y, 1 - c), device_id_type=MESH))
        for cp in copies:
            cp.start()
        for cp in copies:
            cp.wait_recv()
        for cp in copies:
            cp.wait_send()

    any_spec = pl.BlockSpec(memory_space=pl.ANY)
    return pl.pallas_call(
        body, name=name,
        out_shape=[jax.ShapeDtypeStruct((4,) + t.shape[1:], t.dtype) for t in parts],
        in_specs=[any_spec] * n, out_specs=[any_spec] * n,
        scratch_shapes=[pltpu.SemaphoreType.DMA((4 * n,)), pltpu.SemaphoreType.DMA((4 * n,))],
    )(*parts)


def _chip_exchange_phases(p_refs, out_refs, sems):
    n = len(p_refs)
    send_sems, recv_sems, local_sems = sems
    x, y, c = _my_place()
    me = 2 * x + y
    peers = [(1 - x, y), (x, 1 - y), (1 - x, 1 - y)]

    def copy(i, j, landing):
        px, py = peers[j]
        return pltpu.make_async_remote_copy(
            src_ref=p_refs[i].at[2 * px + py],
            dst_ref=out_refs[i].at[(2 * px + py) if landing else me],
            send_sem=send_sems.at[3 * i + j], recv_sem=recv_sems.at[3 * i + j],
            device_id=(px, py, c), device_id_type=MESH)

    mine = [pltpu.make_async_copy(p_refs[i].at[me], out_refs[i].at[me], local_sems.at[i])
            for i in range(n)]
    copies = [copy(i, j, False) for j in range(3) for i in range(n)]

    def begin():
        for cp in mine + copies:
            cp.start()

    def finish():
        for j in range(3):
            for i in range(n):
                copy(i, j, True).wait_recv()
        for cp in copies:
            cp.wait_send()
        for cp in mine:
            cp.wait()

    return begin, finish


def chip_exchange(name, parts):
    n = len(parts)

    def body(*refs):
        begin, finish = _chip_exchange_phases(refs[:n], refs[n:2 * n], refs[2 * n:])
        begin()
        finish()

    any_spec = pl.BlockSpec(memory_space=pl.ANY)
    return pl.pallas_call(
        body, name=name,
        out_shape=[jax.ShapeDtypeStruct(t.shape, t.dtype) for t in parts],
        in_specs=[any_spec] * n, out_specs=[any_spec] * n,
        scratch_shapes=[pltpu.SemaphoreType.DMA((3 * n,)), pltpu.SemaphoreType.DMA((3 * n,)),
                        pltpu.SemaphoreType.DMA((n,))],
    )(*parts)


def pair_sum(name, mine, got, core):
    _, rows, cols = got.shape
    row_bytes = 4 * LANES * (-(-cols // LANES))
    tile = _pick(rows, [t for t in (512, 352, 256, 128, 64, 32, 16, 8)
                        if t * row_bytes <= ADAMW_BLOCK_BYTES])

    def body(core_ref, a_ref, b_ref, o_ref):
        o_ref[...] = (a_ref[...].astype(F32) + b_ref[...].astype(F32)).astype(o_ref.dtype)

    spec = pl.BlockSpec((4, tile, cols), lambda i, core_ref: (0, i, 0))
    return pl.pallas_call(
        body, name=name,
        grid_spec=pltpu.PrefetchScalarGridSpec(
            num_scalar_prefetch=1, grid=(rows // tile,),
            in_specs=[pl.BlockSpec((4, None, tile, cols), lambda i, core_ref: (0, core_ref[0], i, 0)),
                      spec],
            out_specs=spec),
        out_shape=jax.ShapeDtypeStruct(got.shape, got.dtype), compiler_params=_cparams(1),
    )(core, mine, got)


ADAMW_BLOCK_BYTES = 3 * 512 * 1024


def adamw(name, grad, w, m, v):
    rows, cols = w.shape
    stacked = grad.ndim == 3
    row_bytes = 4 * LANES * (-(-cols // LANES))
    tile = _pick(rows, [t for t in (512, 352, 256, 128, 64, 32, 16, 8)
                        if t * row_bytes <= ADAMW_BLOCK_BYTES])

    def body(g_ref, w_ref, m_ref, v_ref, go_ref, d_ref, mo_ref, vo_ref):
        if stacked:
            g = g_ref[0].astype(F32)
            for s in range(1, grad.shape[0]):
                g = g + g_ref[s].astype(F32)
        else:
            g = g_ref[...]
        m = ADAM_B1 * m_ref[...] + (1.0 - ADAM_B1) * g
        v = ADAM_B2 * v_ref[...] + (1.0 - ADAM_B2) * jnp.square(g)
        m_hat = m / (1.0 - ADAM_B1 ** ADAM_STEP)
        v_hat = v / (1.0 - ADAM_B2 ** ADAM_STEP)
        go_ref[...] = g
        d_ref[...] = -ADAM_LR * (m_hat / (jnp.sqrt(v_hat) + ADAM_EPS) + ADAM_WD * w_ref[...])
        mo_ref[...] = m
        vo_ref[...] = v

    flat = pl.BlockSpec((tile, cols), lambda i: (i, 0))
    g_spec = pl.BlockSpec((grad.shape[0], tile, cols), lambda i: (0, i, 0)) if stacked else flat
    return pl.pallas_call(
        body, name=name, grid=(rows // tile,),
        in_specs=[g_spec, flat, flat, flat], out_specs=[flat] * 4,
        out_shape=[jax.ShapeDtypeStruct((rows, cols), F32)] * 4,
        compiler_params=_cparams(1),
    )(grad, w, m, v)


def silu_rows(name, x):
    def body(x_ref, o_ref):
        o_ref[...] = _silu(x_ref[...])

    return pl.pallas_call(body, name=name, out_shape=jax.ShapeDtypeStruct(x.shape, F32))(x)


BIG = (("ffn1_wg", (D, D_FF // N_DEV), 1), ("ffn1_wu", (D, D_FF // N_DEV), 1),
       ("ffn1_wd", (D_FF // N_DEV, D), 0), ("w_in", (D, IN_COLS // N_DEV), 1),
       ("conv_w", (DN_CONV, 3 * D // N_DEV), 1), ("w_a", (D // N_DEV, D), 0),
       ("w_b", (DA_HEADS * DA_DIM, D // N_DEV), 1), ("w_o", (D // N_DEV, D), 0),
       ("ffn2_wg", (D, D_FF // N_DEV), 1), ("ffn2_wu", (D, D_FF // N_DEV), 1),
       ("ffn2_wd", (D_FF // N_DEV, D), 0))
SMALL = (("ada_b", (DEPTH, N_ADA * D)), ("ln_ffn1", (DEPTH, D)), ("ln_mix", (DEPTH, D)),
         ("ln_ffn2", (DEPTH, D)), ("a_log", (DEPTH, DN_HEADS)), ("dt_bias", (DEPTH, DN_HEADS)),
         ("dn_norm", (DEPTH, DN_DIM)), ("final_norm", (D,)))
SMALL_ROWS = 32


def _pack(arrays, rows):
    flat = jnp.concatenate([a.reshape(-1) for a in arrays])
    return jnp.pad(flat, (0, rows * D - flat.shape[0])).reshape(rows, D)


def _unpack(buf, shapes):
    flat = buf.reshape(-1)
    out, off = [], 0
    for shp in shapes:
        n = int(np.prod(shp))
        out.append(flat[off:off + n].reshape(shp))
        off += n
    return out


def _full_weights(gathered, entries):
    full = {}
    for (name, (a, b), axis), t in zip(entries, gathered):
        if axis == 1:
            full[name] = t.transpose(1, 0, 2).reshape(a, N_DEV * b)
        else:
            full[name] = t.reshape(N_DEV * a, b)
    return full


_B0, _DQ0, _GA0 = 4096, 4112, 6416
_N_WIDE = 6 * D


def _reorder_in_proj(w):
    pad = jnp.zeros((w.shape[0], IN_COLS_PAD - IN_COLS), w.dtype)
    return jnp.concatenate([w[:, :_B0], w[:, _GA0:], w[:, _DQ0:_GA0], w[:, _B0:_DQ0], pad], axis=1)


def _ffn(tag, h, ln, shift, scale, gate, w_gu, w_d):
    n = norm_mod(tag + "_norm", h, ln, shift, scale)
    a = gate_up_act(tag + "_gu", n, w_gu)
    f = linear(tag + "_down", a, w_d, F32)
    return residual(tag + "_res", h, f, gate, 0.5)


N_PRE = 5


def _mods(mod, l):
    return [mod[l, i * D:(i + 1) * D][None] for i in range(N_ADA)]


def _layer_pre(l, h, gathered, small, mod):
    w = _full_weights(gathered, BIG[:N_PRE])
    tag = f"l{l}"
    sh1, sc1, gt1, sh2, sc2 = _mods(mod, l)[:5]
    w_gu1 = _interleave_gate_up(w["ffn1_wg"], w["ffn1_wu"])
    h = _ffn(tag + "_ffn1", h, small["ln_ffn1"][l][None], sh1, sc1, gt1, w_gu1, w["ffn1_wd"])
    u = norm_mod(tag + "_mixnorm", h, small["ln_mix"][l][None], sh2, sc2)
    w_in = _reorder_in_proj(w["w_in"])
    (q_pre, k_pre, v_pre, z, gate_a, gate_b) = linear_split(
        tag + "_mix_in", u, w_in[:, :_N_WIDE], D, (1,) * 6, (BF16,) * 6)
    (da_q, da_k, da_v) = linear_split(
        tag + "_mix_in_da", u, w_in[:, _N_WIDE:_N_WIDE + 3 * DA_HEADS * DA_DIM], DA_HEADS * DA_DIM,
        (1,) * 3, (BF16,) * 3)
    ab = linear(tag + "_mix_in_ab", u, w_in[:, _N_WIDE + 3 * DA_HEADS * DA_DIM:], F32)
    cw = w["conv_w"].astype(F32).reshape(DN_CONV, 1, 3 * D)
    q, k, v = [conv_heads(f"{tag}_mix_conv_{m}", t, cw[:, :, i * D:(i + 1) * D], m)
               for i, (m, t) in enumerate(zip("qkv", (q_pre, k_pre, v_pre)))]
    pad = lambda t: jnp.pad(t, (DN_HEADS, ab.shape[1] - 2 * DN_HEADS))[None]
    gb = decay_beta(tag + "_mix_decay", ab, pad(small["a_log"][l]), pad(small["dt_bias"][l]))
    return (q, k, v, gb[:, :LANES]), (h, z, da_q, da_k, da_v, gate_a, gate_b)


def _layer_post(l, o, carry, gathered, small, mod):
    w = _full_weights(gathered, BIG[N_PRE:])
    tag = f"l{l}"
    h, z, da_q, da_k, da_v, gate_a, gate_b = carry
    gt2, sh3, sc3, gt3 = _mods(mod, l)[5:]
    o_a = gated_head_norm(tag + "_mix_gnorm", o, z, small["dn_norm"][l][None])
    y_a = linear(tag + "_mix_wa", o_a, w["w_a"], BF16)

    outs, lses = [], []
    for r in DA_DILATIONS:
        o_r, lse_r = dilated_attention(f"{tag}_mix_attn{r}",
                                       *[_sub_order(t, r) for t in (da_q, da_k, da_v)], r)
        outs.append(_seq_order(o_r, r))
        lses.append(_seq_order(lse_r, r))
    o_b = combine_patterns(tag + "_mix_comb", outs, lses)
    y_b = linear(tag + "_mix_wb", o_b, w["w_b"], BF16)

    merged = merge_gates(tag + "_mix_merge", gate_a, gate_b, y_a, y_b)
    m = linear(tag + "_mix_wo", merged, w["w_o"], F32)
    h = residual(tag + "_mixres", h, m, gt2, 1.0)
    w_gu2 = _interleave_gate_up(w["ffn2_wg"], w["ffn2_wu"])
    return _ffn(tag + "_ffn2", h, small["ln_ffn2"][l][None], sh3, sc3, gt3, w_gu2, w["ffn2_wd"])


def _layer(l, h, gathered, small, mod):
    (q, k, v, gb), carry = _layer_pre(l, h, gathered[:N_PRE], small, mod)
    o = delta_rule(f"l{l}_mix_delta", q, k, v, gb)
    return _layer_post(l, o, carry, gathered[N_PRE:], small, mod)


def _head(h, small, target):
    rows = loss_rows("loss", h, small["final_norm"][None], target)
    return 0.5 * jnp.sum(rows)


def _local_loss(x, gathered, small, mod, target):
    h = x
    for l in range(DEPTH):
        h = _layer(l, h, gathered[l], small, mod)
    return _head(h, small, target)


def kernel(x, c, ada_w, ada_b, ln_ffn1, ln_mix, ln_ffn2, ffn1_wg, ffn1_wu, ffn1_wd, w_in, conv_w, a_log, dt_bias, dn_norm, w_a, w_b, w_o, ffn2_wg, ffn2_wu, ffn2_wd, final_norm, loss_target, m_ada_w, m_ada_b, m_ln_ffn1, m_ln_mix, m_ln_ffn2, m_ffn1_wg, m_ffn1_wu, m_ffn1_wd, m_w_in, m_conv_w, m_a_log, m_dt_bias, m_dn_norm, m_w_a, m_w_b, m_w_o, m_ffn2_wg, m_ffn2_wu, m_ffn2_wd, m_final_norm, v_ada_w, v_ada_b, v_ln_ffn1, v_ln_mix, v_ln_ffn2, v_ffn1_wg, v_ffn1_wu, v_ffn1_wd, v_w_in, v_conv_w, v_a_log, v_dt_bias, v_dn_norm, v_w_a, v_w_b, v_w_o, v_ffn2_wg, v_ffn2_wu, v_ffn2_wd, v_final_norm):
    args = dict(locals())
    big_names = [n for n, _, _ in BIG]
    small_names = [n for n, _ in SMALL]
    me = 4 * lax.axis_index("x") + 2 * lax.axis_index("y") + lax.axis_index("c")
    cols = N_ADA * D // N_DEV

    c_all = all_gather("gather_c", jnp.pad(silu_rows("silu_c", c), ((0, 7), (0, 0))))[:, 0]
    mod_cols = jnp.stack([matmul(f"ada{l}", c_all, ada_w[l], "nn", F32) for l in range(DEPTH)])
    mod_cols = mod_cols + lax.dynamic_slice_in_dim(ada_b, me * cols, cols, axis=1)[:, None, :]
    mod_all = all_gather("gather_mod", mod_cols.reshape(DEPTH * N_DEV, cols))
    mod_all = mod_all.reshape(N_DEV, DEPTH, N_DEV, cols)
    mod = lax.dynamic_index_in_dim(mod_all, me, axis=2, keepdims=False)
    mod = mod.transpose(1, 0, 2).reshape(DEPTH, N_ADA * D)

    shards = [[args[n][l].astype(BF16) for n in big_names] for l in range(DEPTH)]
    pre_names, post_names = big_names[:N_PRE], big_names[N_PRE:]
    small = {n: args[n] for n in small_names if n != "ada_b"}
    target = loss_target[0]

    w0_pre = all_gather_many("gather_w0", shards[0][:N_PRE])
    pre0, vjp_pre0 = jax.vjp(functools.partial(_layer_pre, 0), x[0], w0_pre, small, mod)
    (q0, k0, v0, gb0), carry0 = pre0
    o0, states0, got = delta_fwd("l0_mix_delta", q0, k0, v0, gb0,
                                 gather=shards[0][N_PRE:] + shards[1][:N_PRE])
    w0_post, w1_pre = got[:len(post_names)], got[len(post_names):]
    h1, vjp_post0 = jax.vjp(functools.partial(_layer_post, 0), o0, carry0, w0_post, small, mod)
    pre1, vjp_pre1 = jax.vjp(functools.partial(_layer_pre, 1), h1, w1_pre, small, mod)
    (q1, k1, v1, gb1), carry1 = pre1
    o1, states1, w1_post = delta_fwd("l1_mix_delta", q1, k1, v1, gb1, gather=shards[1][N_PRE:])
    loss, vjp_post1 = jax.vjp(
        lambda o, carry, w, sm, md: _head(_layer_post(1, o, carry, w, sm, md), sm, target),
        o1, carry1, w1_post, small, mod)

    my_core = lax.axis_index("c").astype(jnp.int32).reshape(1)

    def pair_sums(tag, names, grads):
        from_sibling = sibling_exchange(f"pair_grads_{tag}", grads)
        return [pair_sum(f"pair_sum_{tag}_{n}", t.reshape(4, 2, -1, t.shape[-1]), got_n, my_core)
                for n, t, got_n in zip(names, grads, from_sibling)]

    do1, dcarry1, dw1_post, dsmall_a, dmod_a = vjp_post1(jnp.ones((), F32))
    ddelta1, landed1_post = delta_bwd("l1_mix_delta", q1, k1, v1, gb1, states1, do1,
                                      exchange=pair_sums("l1_post", post_names, list(dw1_post)))
    dh1, dw1_pre, dsmall_b, dmod_b = vjp_pre1((ddelta1, dcarry1))
    do0, dcarry0, dw0_post, dsmall_c, dmod_c = vjp_post0(dh1)
    ddelta0, landed_mid = delta_bwd(
        "l0_mix_delta", q0, k0, v0, gb0, states0, do0,
        exchange=pair_sums("mid", pre_names + post_names, list(dw1_pre) + list(dw0_post)))
    dx, dw0_pre, dsmall_d, dmod_d = vjp_pre0((ddelta0, dcarry0))
    landed0_pre = chip_exchange("scatter_grads0", pair_sums("l0_pre", pre_names, list(dw0_pre)))
    landed0 = list(landed0_pre) + list(landed_mid[N_PRE:])
    landed1 = list(landed_mid[:N_PRE]) + list(landed1_post)
    dsmall = {n: dsmall_a[n] + dsmall_b[n] + dsmall_c[n] + dsmall_d[n] for n in dsmall_a}
    dmod = dmod_a + dmod_b + dmod_c + dmod_d

    part = _pack([dmod] + [dsmall[n] for n in small_names[1:]], SMALL_ROWS)
    parts = all_gather("gather_small", part)
    sm_out = adamw("adamw_small", parts, _pack([args[n] for n in small_names], SMALL_ROWS),
                   _pack([args["m_" + n] for n in small_names], SMALL_ROWS),
                   _pack([args["v_" + n] for n in small_names], SMALL_ROWS))

    dmod_all = parts.reshape(N_DEV, -1)[:, :DEPTH * N_ADA * D].reshape(N_DEV, DEPTH, N_ADA * D)
    dmod_mine = lax.dynamic_slice_in_dim(dmod_all, me * cols, cols, axis=2)
    g_ada = jnp.stack([matmul(f"ada{l}_dw", c_all, dmod_mine[:, l], "tn", F32) for l in range(DEPTH)])
    flat2 = lambda t: t.reshape(-1, t.shape[-1])
    ada_out = adamw("adamw_ada_w", flat2(g_ada), flat2(ada_w), flat2(m_ada_w), flat2(v_ada_w))

    big_out = {}
    for n, t0, t1 in zip(big_names, landed0, landed1):
        both = jnp.concatenate([t0, t1], axis=1)
        big_out[n] = adamw("adamw_" + n, both, flat2(args[n]), flat2(args["m_" + n]), flat2(args["v_" + n]))

    small_shapes = [shp for _, shp in SMALL]
    names = ["ada_w", "ada_b", "ln_ffn1", "ln_mix", "ln_ffn2", "ffn1_wg", "ffn1_wu", "ffn1_wd", "w_in",
             "conv_w", "a_log", "dt_bias", "dn_norm", "w_a", "w_b", "w_o", "ffn2_wg", "ffn2_wu",
             "ffn2_wd", "final_norm"]
    outs = [lax.psum(loss, ("x", "y", "c")), dx[None]]
    for kind in range(4):
        table = {n: big_out[n][kind].reshape(args[n].shape) for n in big_names}
        table.update(zip(small_names, _unpack(sm_out[kind], small_shapes)))
        table["ada_w"] = ada_out[kind].reshape(ada_w.shape)
        outs += [table[n] for n in names]
    return tuple(outs)
```
